```python
import jax, jax.numpy as jnp
from jax import lax
import numpy as np

D_MODEL = 1024
BATCH = 8
SEQ = 2048
DEPTH = 2

N_HEADS_ATT = 8
HEAD_DIM_ATT = 64
ATT_WIDTH = N_HEADS_ATT * HEAD_DIM_ATT
MOBA_BLOCK = 256
MOBA_TOPK = 3
MOBA_QCHUNK = 16
N_HEADS_RW = 8
HEAD_DIM_RW = 64
RW_WIDTH = N_HEADS_RW * HEAD_DIM_RW
DECAY_LORA = 64
ICLR_LORA = 64
VMIX_LORA = 32
RW_SHIFT_WIDTH = 3 * RW_WIDTH + DECAY_LORA + ICLR_LORA
IN_WIDTHS = (ATT_WIDTH, ATT_WIDTH, ATT_WIDTH, ATT_WIDTH, RW_SHIFT_WIDTH, RW_WIDTH, D_MODEL, D_MODEL)
D_IN = 4 * ATT_WIDTH + RW_SHIFT_WIDTH + RW_WIDTH + 2 * D_MODEL
RMS_EPS = 1e-6
GN_EPS = 64e-5
L2_EPS = 1e-12
NEG_INF = -1e30

kernel_name = "hybrid_moba_rwkv7_gated_block"


def split_cols(p, widths):
    idx = np.cumsum(np.array(widths))[:-1].tolist()
    return jnp.split(p, idx, axis=-1)


def rmsnorm(x, g):
    xf = x.astype(jnp.float32)
    y = xf * lax.rsqrt(jnp.mean(xf * xf, axis=-1, keepdims=True) + RMS_EPS)
    return (y * g.astype(jnp.float32)).astype(x.dtype)


def token_shift_mix(y, mu):
    prev = jnp.pad(y[:, :-1], ((0, 0), (1, 0), (0, 0)))
    return y + (prev - y) * mu


def alibi_slopes(n_heads):
    return 2.0 ** (-8.0 * jnp.arange(1, n_heads + 1, dtype=jnp.float32) / n_heads)


def moba_attention(q, k, v):
    B, S, H, Dh = q.shape
    nb = -(-S // MOBA_BLOCK)
    Sp = nb * MOBA_BLOCK
    nc = Sp // MOBA_QCHUNK
    kk = min(MOBA_TOPK, nb)

    def prep(t):
        return jnp.pad(jnp.transpose(t, (0, 2, 1, 3)), ((0, 0), (0, 0), (0, Sp - S), (0, 0)))

    q = prep(q) * (Dh ** -0.5)
    k = prep(k)
    v = prep(v)
    kb = k.reshape(B, H, nb, MOBA_BLOCK, Dh)
    vb = v.reshape(B, H, nb, MOBA_BLOCK, Dh)

    kmean = jnp.mean(kb.astype(jnp.float32), axis=3)
    gate = jnp.einsum('bhsd,bhnd->bhsn', q.astype(jnp.float32), kmean)
    qblk = jnp.arange(Sp) // MOBA_BLOCK
    past = jnp.arange(nb)[None, :] < qblk[:, None]
    gate = jnp.where(past, gate, NEG_INF)
    _, sel = lax.top_k(gate, kk)

    slopes = alibi_slopes(H)
    bi = jnp.arange(B)[:, None, None, None]
    hi = jnp.arange(H)[None, :, None, None]
    kpos_in = jnp.arange(MOBA_BLOCK)

    def chunk(args):
        qc, selc, start = args
        blk = start // MOBA_BLOCK
        qpos = (start + jnp.arange(MOBA_QCHUNK)).astype(jnp.float32)
        ksel = kb[bi, hi, selc]
        vsel = vb[bi, hi, selc]
        spos = (selc[..., None] * MOBA_BLOCK + kpos_in).astype(jnp.float32)
        ls = jnp.einsum('bhqd,bhqjkd->bhqjk', qc, ksel).astype(jnp.float32)
        ls = ls - slopes[None, :, None, None, None] * (qpos[None, None, :, None, None] - spos)
        ls = jnp.where((selc < blk)[..., None], ls, NEG_INF)
        kown = lax.dynamic_index_in_dim(kb, blk, axis=2, keepdims=False)
        vown = lax.dynamic_index_in_dim(vb, blk, axis=2, keepdims=False)
        opos = (blk * MOBA_BLOCK + kpos_in).astype(jnp.float32)
        dist = qpos[:, None] - opos[None, :]
        lo = jnp.einsum('bhqd,bhkd->bhqk', qc, kown).astype(jnp.float32)
        lo = lo - slopes[None, :, None, None] * dist
        lo = jnp.where(dist >= 0, lo, NEG_INF)
        logits = jnp.concatenate([ls.reshape(B, H, MOBA_QCHUNK, kk * MOBA_BLOCK), lo], axis=-1)
        p = jax.nn.softmax(logits, axis=-1).astype(vb.dtype)
        psel = p[..., :kk * MOBA_BLOCK].reshape(B, H, MOBA_QCHUNK, kk, MOBA_BLOCK)
        pown = p[..., kk * MOBA_BLOCK:]
        return (jnp.einsum('bhqjk,bhqjkd->bhqd', psel, vsel)
                + jnp.einsum('bhqk,bhkd->bhqd', pown, vown))

    qcs = jnp.moveaxis(q.reshape(B, H, nc, MOBA_QCHUNK, Dh), 2, 0)
    sels = jnp.moveaxis(sel.reshape(B, H, nc, MOBA_QCHUNK, kk), 2, 0)
    starts = jnp.arange(nc) * MOBA_QCHUNK
    out = lax.map(chunk, (qcs, sels, starts))
    out = jnp.moveaxis(out, 0, 2).reshape(B, H, Sp, Dh)[:, :, :S]
    return jnp.transpose(out, (0, 2, 1, 3)).reshape(B, S, H * Dh)


def rwkv7_scan(r, w, k, v, kk, a):
    B, S, H, N = r.shape

    def step(state, inp):
        r_t, w_t, k_t, v_t, kk_t, a_t = inp
        sa = jnp.einsum('bhvk,bhk->bhv', state, -kk_t)
        state = (state * w_t[:, :, None, :]
                 + sa[..., None] * (kk_t * a_t)[:, :, None, :]
                 + v_t[..., None] * k_t[:, :, None, :])
        return state, jnp.einsum('bhvk,bhk->bhv', state, r_t)

    xs = tuple(jnp.moveaxis(t.astype(jnp.float32), 1, 0) for t in (r, w, k, v, kk, a))
    state0 = jnp.zeros((B, H, N, N), jnp.float32)
    _, out = lax.scan(step, state0, xs)
    return jnp.moveaxis(out, 0, 1)


def head_groupnorm(o, g, b):
    B, S, H, N = o.shape
    mu = jnp.mean(o, axis=-1, keepdims=True)
    var = jnp.mean(jnp.square(o - mu), axis=-1, keepdims=True)
    on = ((o - mu) * lax.rsqrt(var + GN_EPS)).reshape(B, S, H * N)
    return on * g.astype(jnp.float32) + b.astype(jnp.float32)


def setup_inputs(seed: int = 0) -> dict:
    key = jax.random.key(seed)
    ks = jax.random.split(key, 24)
    f32 = jnp.float32
    L, D, Lv = DEPTH, D_MODEL, DEPTH - 1

    def nrm(k, shape, scale):
        return jax.random.normal(k, shape, f32) * scale

    return {
        "x": nrm(ks[0], (BATCH, SEQ, D), 1.0),
        "norm_pre": 1.0 + nrm(ks[1], (L, D), 0.05),
        "norm_post": 1.0 + nrm(ks[2], (L, D), 0.05),
        "w_in": nrm(ks[3], (L, D, D_IN), D ** -0.5),
        "rw_mu": jax.random.uniform(ks[4], (L, RW_SHIFT_WIDTH), f32, 0.0, 1.0),
        "rw_w0": jax.random.uniform(ks[5], (L, RW_WIDTH), f32, -6.0, 0.5),
        "rw_w_up": nrm(ks[6], (L, DECAY_LORA, RW_WIDTH), 0.1 * DECAY_LORA ** -0.5),
        "rw_a0": nrm(ks[7], (L, RW_WIDTH), 0.5),
        "rw_a_up": nrm(ks[8], (L, ICLR_LORA, RW_WIDTH), 0.5 * ICLR_LORA ** -0.5),
        "rw_k_k": 0.85 + nrm(ks[9], (L, RW_WIDTH), 0.05),
        "rw_k_a": 1.0 + nrm(ks[10], (L, RW_WIDTH), 0.05),
        "rw_r_k": nrm(ks[11], (L, RW_WIDTH), 0.1),
        "rw_ln_g": 1.0 + nrm(ks[12], (L, RW_WIDTH), 0.05),
        "rw_ln_b": nrm(ks[13], (L, RW_WIDTH), 0.02),
        "rw_vmix_down": nrm(ks[14], (Lv, D, VMIX_LORA), D ** -0.5),
        "rw_vmix_mu": jax.random.uniform(ks[15], (Lv, VMIX_LORA), f32, 0.0, 1.0),
        "rw_vmix_up": nrm(ks[16], (Lv, VMIX_LORA, RW_WIDTH), 0.5 * VMIX_LORA ** -0.5),
        "rw_vmix0": nrm(ks[17], (Lv, RW_WIDTH), 0.5),
        "w_up_att": nrm(ks[18], (L, ATT_WIDTH, D), ATT_WIDTH ** -0.5),
        "w_up_rw": nrm(ks[19], (L, RW_WIDTH, D), RW_WIDTH ** -0.5),
        "w_out": nrm(ks[20], (L, D, D), D ** -0.5),
    }


def reference(x, norm_pre, norm_post, w_in, rw_mu, rw_w0, rw_w_up, rw_a0, rw_a_up, rw_k_k, rw_k_a,
              rw_r_k, rw_ln_g, rw_ln_b, rw_vmix_down, rw_vmix_mu, rw_vmix_up, rw_vmix0,
              w_up_att, w_up_rw, w_out):
    B, S, _ = x.shape
    v_first = None
    for l in range(DEPTH):
        h = rmsnorm(x, norm_pre[l])
        proj = jnp.einsum('bsd,de->bse', h, w_in[l])
        aq, ak, av, az, rw_stream, rz, g_att, g_rw = split_cols(proj, IN_WIDTHS)

        heads_a = lambda t: t.reshape(B, S, N_HEADS_ATT, HEAD_DIM_ATT)
        ya = moba_attention(heads_a(aq), heads_a(ak), heads_a(av)).astype(x.dtype)
        ya = ya * jax.nn.silu(az)

        rs, ksr, vs, wd, ad = split_cols(token_shift_mix(rw_stream, rw_mu[l]),
                                         (RW_WIDTH, RW_WIDTH, RW_WIDTH, DECAY_LORA, ICLR_LORA))
        w_log = -jax.nn.softplus(-(rw_w0[l] + jnp.tanh(wd) @ rw_w_up[l])) - 0.5
        decay = jnp.exp(-jnp.exp(w_log.astype(jnp.float32)))
        a = jax.nn.sigmoid(rw_a0[l] + ad @ rw_a_up[l])
        if l == 0:
            v_first = vs
            vr = vs
        else:
            vd = token_shift_mix(h @ rw_vmix_down[l - 1], rw_vmix_mu[l - 1])
            vr = vs + (v_first - vs) * jax.nn.sigmoid(rw_vmix0[l - 1] + vd @ rw_vmix_up[l - 1])
        heads_b = lambda t: t.reshape(B, S, N_HEADS_RW, HEAD_DIM_RW)
        kkf = heads_b(ksr * rw_k_k[l]).astype(jnp.float32)
        kkf = kkf / jnp.maximum(jnp.sqrt(jnp.sum(kkf * kkf, axis=-1, keepdims=True)), L2_EPS)
        kmod = ksr * (1.0 + (a - 1.0) * rw_k_a[l])
        o = rwkv7_scan(heads_b(rs), heads_b(decay), heads_b(kmod), heads_b(vr), kkf, heads_b(a))
        o = head_groupnorm(o, rw_ln_g[l], rw_ln_b[l])
        bonus = (jnp.sum(heads_b((rs * kmod * rw_r_k[l]).astype(jnp.float32)), axis=-1, keepdims=True)
                 * heads_b(vr.astype(jnp.float32))).reshape(B, S, RW_WIDTH)
        yb = (o + bonus).astype(x.dtype) * jax.nn.silu(rz)

        u = (jax.nn.sigmoid(g_att) * (ya @ w_up_att[l])
             + jax.nn.sigmoid(g_rw) * (yb @ w_up_rw[l]))
        y = u @ w_out[l]
        x = x + rmsnorm(y, norm_post[l])
    return x
```

```python
import functools

import jax
import jax.numpy as jnp
from jax import lax
from jax.experimental import pallas as pl
from jax.experimental.pallas import tpu as pltpu

F32 = jnp.float32
BF16 = jnp.bfloat16

D_MODEL = 1024
N_HEADS = 8
HEAD_DIM = 64
WIDTH = N_HEADS * HEAD_DIM
MOBA_BLOCK = 256
MOBA_TOPK = 3
DECAY_LORA = 64
ICLR_LORA = 64
VMIX_LORA = 32
RMS_EPS = 1e-6
GN_EPS = 64e-5
L2_EPS = 1e-12
NEG_INF = -1e30

LANES = 128
PAIR = 2 * HEAD_DIM
N_PAIRS = WIDTH // PAIR
CHUNK = 64

COL_Q, COL_K, COL_V, COL_AZ = 0, 512, 1024, 1536
COL_R, COL_RK, COL_RV, COL_RZ = 2048, 2560, 3072, 3584
COL_GATT, COL_GRW = 4096, 5120
COL_LORA = 6144
COL_VD = 6272
PROJ_W = 6400

VMEM_LIMIT = 48 * 1024 * 1024


def _sigmoid(x):
    return 1.0 / (1.0 + jnp.exp(-x))


def _mm(a, b):
    return jnp.dot(a.astype(BF16), b.astype(BF16), preferred_element_type=F32)


def _mm_nt(a, b):
    return lax.dot_general(a.astype(BF16), b.astype(BF16), (((1,), (1,)), ((), ())),
                           preferred_element_type=F32)


def _mm_tn(a, b):
    return lax.dot_general(a.astype(BF16), b.astype(BF16), (((0,), (0,)), ((), ())),
                           preferred_element_type=F32)


def _inproj_kernel(x_ref, g_ref, w_ref, o_ref, h_ref):
    @pl.when(pl.program_id(1) == 0)
    def _():
        x = x_ref[...]
        ms = jnp.mean(x * x, axis=-1, keepdims=True)
        h_ref[...] = (x * lax.rsqrt(ms + RMS_EPS) * g_ref[...]).astype(BF16)

    o_ref[...] = jnp.dot(h_ref[...], w_ref[...], preferred_element_type=F32)


def _inproj(x2d, g, w_cat, *, tm=1024, tn=1280):
    t, d = x2d.shape
    n = w_cat.shape[1]
    return pl.pallas_call(
        _inproj_kernel,
        grid=(t // tm, n // tn),
        in_specs=[
            pl.BlockSpec((tm, d), lambda i, j: (i, 0)),
            pl.BlockSpec((1, d), lambda i, j: (0, 0)),
            pl.BlockSpec((d, tn), lambda i, j: (0, j)),
        ],
        out_specs=pl.BlockSpec((tm, tn), lambda i, j: (i, j)),
        out_shape=jax.ShapeDtypeStruct((t, n), F32),
        scratch_shapes=[pltpu.VMEM((tm, d), BF16)],
        compiler_params=pltpu.CompilerParams(
            dimension_semantics=("parallel", "arbitrary"), vmem_limit_bytes=VMEM_LIMIT),
        name="inproj",
    )(x2d, g.reshape(1, d), w_cat)


def _moba_kernel(q_ref, k_ref, v_ref, o_ref, *, n_blocks):
    hp = pl.program_id(1)
    blk = MOBA_BLOCK
    row = lax.broadcasted_iota(jnp.int32, (blk, blk), 0)
    col = lax.broadcasted_iota(jnp.int32, (blk, blk), 1)
    d0 = (row - col).astype(F32)
    causal = row >= col
    blk_iota = lax.broadcasted_iota(jnp.int32, (blk, n_blocks), 1)
    mean_row = lax.broadcasted_iota(jnp.int32, (n_blocks, HEAD_DIM), 0)

    for h in range(2):
        sl = slice(h * HEAD_DIM, (h + 1) * HEAD_DIM)
        head = (hp * 2 + h + 1).astype(F32)
        slope = jnp.exp2(jnp.zeros((1, 1), F32) - (8.0 / N_HEADS) * head)

        kmean = jnp.zeros((n_blocks, HEAD_DIM), F32)
        for n in range(n_blocks):
            kb = k_ref[n * blk:(n + 1) * blk, sl]
            kmean = jnp.where(mean_row == n, jnp.mean(kb, axis=0, keepdims=True), kmean)

        def q_block(i, carry, sl=sl, slope=slope, kmean=kmean):
            r0 = pl.multiple_of(i * blk, blk)
            q = q_ref[pl.ds(r0, blk), sl] * (HEAD_DIM ** -0.5)
            qb = q.astype(BF16)

            gate = lax.dot_general(q, kmean, (((1,), (1,)), ((), ())),
                                   precision=lax.Precision.HIGHEST, preferred_element_type=F32)
            past = blk_iota < i
            gm = jnp.where(past, gate, NEG_INF)
            rank = jnp.zeros((blk, n_blocks), F32)
            for m in range(n_blocks):
                gcol = gm[:, m:m + 1]
                beats = (gcol > gm) | ((gcol == gm) & (blk_iota > m))
                rank = rank + jnp.where(beats, 1.0, 0.0)
            sel = jnp.where(past & (rank < MOBA_TOPK), 1.0, 0.0)

            ki = k_ref[pl.ds(r0, blk), sl].astype(BF16)
            vi = v_ref[pl.ds(r0, blk), sl].astype(BF16)
            s = _mm_nt(qb, ki) - slope * d0
            s = jnp.where(causal, s, NEG_INF)
            m_run = jnp.max(s, axis=-1, keepdims=True)
            p = jnp.exp(s - m_run)
            l_run = jnp.sum(p, axis=-1, keepdims=True)
            acc = _mm(p, vi)

            def past_block(j, c):
                m_run, l_run, acc = c
                c0 = pl.multiple_of(j * blk, blk)
                kj = k_ref[pl.ds(c0, blk), sl].astype(BF16)
                vj = v_ref[pl.ds(c0, blk), sl].astype(BF16)
                selj = jnp.sum(jnp.where(blk_iota == j, sel, 0.0), axis=-1, keepdims=True) > 0.5
                dist = d0 + ((i - j) * blk).astype(F32)
                s = _mm_nt(qb, kj) - slope * dist
                s = jnp.where(selj, s, NEG_INF)
                m_new = jnp.maximum(m_run, jnp.max(s, axis=-1, keepdims=True))
                alpha = jnp.exp(m_run - m_new)
                p = jnp.exp(s - m_new)
                l_new = alpha * l_run + jnp.sum(p, axis=-1, keepdims=True)
                acc_new = alpha * acc + _mm(p, vj)
                return m_new, l_new, acc_new

            m_run, l_run, acc = lax.fori_loop(0, i, past_block, (m_run, l_run, acc))
            o_ref[pl.ds(r0, blk), sl] = acc / l_run
            return carry

        lax.fori_loop(0, n_blocks, q_block, 0)


def _moba(proj):
    b, s, _ = proj.shape
    n_blocks = s // MOBA_BLOCK
    qo, ko, vo = COL_Q // PAIR, COL_K // PAIR, COL_V // PAIR
    return pl.pallas_call(
        functools.partial(_moba_kernel, n_blocks=n_blocks),
        grid=(b, N_PAIRS),
        in_specs=[
            pl.BlockSpec((None, s, PAIR), lambda i, p: (i, 0, qo + p)),
            pl.BlockSpec((None, s, PAIR), lambda i, p: (i, 0, ko + p)),
            pl.BlockSpec((None, s, PAIR), lambda i, p: (i, 0, vo + p)),
        ],
        out_specs=pl.BlockSpec((None, s, PAIR), lambda i, p: (i, 0, p)),
        out_shape=jax.ShapeDtypeStruct((b, s, WIDTH), F32),
        compiler_params=pltpu.CompilerParams(
            dimension_semantics=("parallel", "parallel"), vmem_limit_bytes=VMEM_LIMIT),
        name="moba",
    )(proj, proj, proj)


def _rwkv_chunk(r, k, v, na, nb, lw, state, consts):
    ltri, head0, strict, incl, eye = consts
    cum = jnp.dot(ltri, lw, precision=lax.Precision.HIGHEST, preferred_element_type=F32)
    tot = cum[CHUNK - 1:CHUNK, :]
    e_pos = jnp.exp(cum)
    e_neg = jnp.exp(-cum)
    e_prev = jnp.exp(cum - lw)
    e_rel = jnp.exp(tot - cum)

    def stack(x):
        return jnp.concatenate([jnp.where(head0, x, 0.0), jnp.where(head0, 0.0, x)], axis=0)

    def dup(x):
        return jnp.concatenate([x, x], axis=0)

    lhs = jnp.concatenate([stack(na * e_prev), stack(r * e_pos)], axis=0).astype(BF16)
    rhs = jnp.concatenate([dup(nb * e_neg), dup(k * e_neg)], axis=0).astype(BF16)
    g = _mm_nt(lhs, rhs)
    n2 = 2 * CHUNK
    a_ab = jnp.where(strict, g[:n2, :n2], 0.0)
    a_ak = jnp.where(strict, g[:n2, n2:], 0.0)
    a_rb = jnp.where(incl, g[n2:, :n2], 0.0)
    a_rk = jnp.where(incl, g[n2:, n2:], 0.0)

    t_inv = eye + a_ab
    pw = a_ab
    for _ in range(CHUNK.bit_length() - 2):
        pw = _mm(pw, pw)
        t_inv = t_inv + _mm(t_inv, pw)

    vs = stack(v)
    xo = _mm_nt(lhs, state)
    av = _mm(jnp.concatenate([a_ak, a_rk], axis=0), vs)
    us = _mm(t_inv, xo[:n2] + av[:n2])
    os_ = xo[n2:] + av[n2:] + _mm(a_rb, us)
    out = os_[:CHUNK] + os_[CHUNK:]
    new_state = state * jnp.exp(tot) + _mm_tn(
        jnp.concatenate([us, vs], axis=0),
        jnp.concatenate([stack(nb * e_rel), stack(k * e_rel)], axis=0))
    return out, new_state


def _rwkv_kernel(*refs, has_vmix, ts):
    if has_vmix:
        (r_ref, k_ref, v_ref, rz_ref, lora_ref, vd_ref, vfirst_ref,
         mu_rkv_ref, mu_lora_ref, w0a0_ref, lora_w_ref, kk_ref, ka_ref, rk_ref, lng_ref, lnb_ref,
         vmix_mu_ref, vmix_up_ref, vmix0_ref,
         yb_ref, carry_rkv, carry_lora, carry_vd, state_ref) = refs
    else:
        (r_ref, k_ref, v_ref, rz_ref, lora_ref,
         mu_rkv_ref, mu_lora_ref, w0a0_ref, lora_w_ref, kk_ref, ka_ref, rk_ref, lng_ref, lnb_ref,
         yb_ref, vfirst_ref, carry_rkv, carry_lora, state_ref) = refs
        carry_vd = None

    @pl.when(pl.program_id(1) == 0)
    def _():
        carry_rkv[...] = jnp.zeros_like(carry_rkv)
        carry_lora[...] = jnp.zeros_like(carry_lora)
        if carry_vd is not None:
            carry_vd[...] = jnp.zeros_like(carry_vd)
        state_ref[...] = jnp.zeros_like(state_ref)

    def shift(y, carry_ref, row_idx, mu):
        first = lax.broadcasted_iota(jnp.int32, y.shape, 0) == 0
        prev = jnp.where(first, carry_ref[row_idx:row_idx + 1, :], pltpu.roll(y, 1, 0))
        carry_ref[row_idx:row_idx + 1, :] = y[ts - 1:ts, :]
        return y + (prev - y) * mu

    mu_rkv = mu_rkv_ref[...]
    rs = shift(r_ref[...], carry_rkv, 0, mu_rkv[0:1, :])
    ks = shift(k_ref[...], carry_rkv, 1, mu_rkv[1:2, :])
    vs = shift(v_ref[...], carry_rkv, 2, mu_rkv[2:3, :])
    lo = shift(lora_ref[...], carry_lora, 0, mu_lora_ref[...])

    lane = lax.broadcasted_iota(jnp.int32, (ts, LANES), 1)
    z = jnp.where(lane < DECAY_LORA, jnp.tanh(lo), lo)
    wa = w0a0_ref[...] + _mm(z, lora_w_ref[...])
    zz = -wa[:, :WIDTH]
    w_log = -(jnp.maximum(zz, 0.0) + jnp.log(1.0 + jnp.exp(-jnp.abs(zz)))) - 0.5
    lw = -jnp.exp(w_log)
    a = _sigmoid(wa[:, WIDTH:])

    if has_vmix:
        vd = shift(vd_ref[...], carry_vd, 0, vmix_mu_ref[...])
        mix = _sigmoid(vmix0_ref[...] + _mm(vd, vmix_up_ref[...]))
        vr = vs + (vfirst_ref[...] - vs) * mix
    else:
        vfirst_ref[...] = vs
        vr = vs

    kx = ks * kk_ref[...]
    kmod = ks * (1.0 + (a - 1.0) * ka_ref[...])
    rkr = rs * kmod * rk_ref[...]
    rz = rz_ref[...]
    ln_g = lng_ref[...]
    ln_b = lnb_ref[...]

    head0_t = lane < HEAD_DIM

    def headsum(x):
        s0 = jnp.sum(jnp.where(head0_t, x, 0.0), axis=-1, keepdims=True)
        s1 = jnp.sum(jnp.where(head0_t, 0.0, x), axis=-1, keepdims=True)
        return jnp.where(head0_t, s0, s1)

    n2 = 2 * CHUNK
    crow = lax.broadcasted_iota(jnp.int32, (CHUNK, CHUNK), 0)
    ccol = lax.broadcasted_iota(jnp.int32, (CHUNK, CHUNK), 1)
    ltri = jnp.where(ccol <= crow, 1.0, 0.0).astype(F32)
    head0 = lax.broadcasted_iota(jnp.int32, (CHUNK, PAIR), 1) < HEAD_DIM
    srow = lax.broadcasted_iota(jnp.int32, (n2, n2), 0)
    scol = lax.broadcasted_iota(jnp.int32, (n2, n2), 1)
    same_head = (srow // CHUNK) == (scol // CHUNK)
    strict = same_head & (scol < srow)
    incl = same_head & (scol <= srow)
    eye = jnp.where(srow == scol, 1.0, 0.0).astype(F32)
    consts = (ltri, head0, strict, incl, eye)

    for p in range(N_PAIRS):
        sl = slice(p * PAIR, (p + 1) * PAIR)
        kxp = kx[:, sl]
        kkp = kxp / jnp.maximum(jnp.sqrt(headsum(kxp * kxp)), L2_EPS)
        a_p = a[:, sl]
        vr_p = vr[:, sl]
        bonus = headsum(rkr[:, sl]) * vr_p
        na = -kkp
        nb = kkp * a_p
        r_p = rs[:, sl]
        k_p = kmod[:, sl]
        lw_p = lw[:, sl]

        state = state_ref[p]
        outs = []
        for c in range(ts // CHUNK):
            rc = slice(c * CHUNK, (c + 1) * CHUNK)
            o_c, state = _rwkv_chunk(r_p[rc], k_p[rc], vr_p[rc], na[rc], nb[rc], lw_p[rc], state, consts)
            outs.append(o_c)
        state_ref[p] = state
        o_p = jnp.concatenate(outs, axis=0) if len(outs) > 1 else outs[0]

        mu = headsum(o_p) * (1.0 / HEAD_DIM)
        d = o_p - mu
        var = headsum(d * d) * (1.0 / HEAD_DIM)
        on = d * lax.rsqrt(var + GN_EPS) * ln_g[:, sl] + ln_b[:, sl]
        rz_p = rz[:, sl]
        yb_ref[:, sl] = (on + bonus) * (rz_p * _sigmoid(rz_p))


def _rwkv(proj, vfirst, p, *, ts=128):
    b, s, _ = proj.shape
    has_vmix = vfirst is not None
    row = lambda width, col: pl.BlockSpec((None, ts, width), lambda i, t: (i, t, col // width))
    const = lambda shape: pl.BlockSpec(shape, lambda i, t: (0,) * len(shape))
    act = pl.BlockSpec((None, ts, WIDTH), lambda i, t: (i, t, 0))

    in_specs = [row(WIDTH, COL_R), row(WIDTH, COL_RK), row(WIDTH, COL_RV), row(WIDTH, COL_RZ),
                row(LANES, COL_LORA)]
    args = [proj, proj, proj, proj, proj]
    if has_vmix:
        in_specs += [row(LANES, COL_VD), act]
        args += [proj, vfirst]
    in_specs += [const((3, WIDTH)), const((1, LANES)), const((1, 2 * WIDTH)), const((LANES, 2 * WIDTH))]
    args += [p["mu_rkv"], p["mu_lora"], p["w0a0"], p["lora_w"]]
    for name in ("k_k", "k_a", "r_k", "ln_g", "ln_b"):
        in_specs.append(const((1, WIDTH)))
        args.append(p[name])
    if has_vmix:
        in_specs += [const((1, LANES)), const((LANES, WIDTH)), const((1, WIDTH))]
        args += [p["vmix_mu"], p["vmix_up"], p["vmix0"]]

    out_shape = [jax.ShapeDtypeStruct((b, s, WIDTH), F32)]
    out_specs = [act]
    scratch = [pltpu.VMEM((8, WIDTH), F32), pltpu.VMEM((8, LANES), F32)]
    if has_vmix:
        scratch.append(pltpu.VMEM((8, LANES), F32))
    else:
        out_shape.append(jax.ShapeDtypeStruct((b, s, WIDTH), F32))
        out_specs.append(act)
    scratch.append(pltpu.VMEM((N_PAIRS, PAIR, PAIR), F32))

    res = pl.pallas_call(
        functools.partial(_rwkv_kernel, has_vmix=has_vmix, ts=ts),
        grid=(b, s // ts),
        in_specs=in_specs,
        out_specs=out_specs,
        out_shape=out_shape,
        scratch_shapes=scratch,
        compiler_params=pltpu.CompilerParams(
            dimension_semantics=("parallel", "arbitrary"), vmem_limit_bytes=VMEM_LIMIT),
        name="rwkv_vmix" if has_vmix else "rwkv",
    )(*args)
    return (res[0], vfirst) if has_vmix else (res[0], res[1])


def _outproj_kernel(ya_ref, az_ref, yb_ref, gatt_ref, grw_ref, x_ref,
                    wua_ref, wub_ref, wout_ref, g_ref, o_ref):
    az = az_ref[...]
    ya = ya_ref[...] * (az * _sigmoid(az))
    u = (_sigmoid(gatt_ref[...]) * _mm(ya, wua_ref[...])
         + _sigmoid(grw_ref[...]) * _mm(yb_ref[...], wub_ref[...]))
    y = _mm(u, wout_ref[...])
    ms = jnp.mean(y * y, axis=-1, keepdims=True)
    o_ref[...] = x_ref[...] + y * lax.rsqrt(ms + RMS_EPS) * g_ref[...]


def _outproj(ya, proj2d, yb, x2d, w_up_att, w_up_rw, w_out, g, *, tm=512):
    t, d = x2d.shape
    row = lambda width, col: pl.BlockSpec((tm, width), lambda i: (i, col // width))
    const = lambda shape: pl.BlockSpec(shape, lambda i: (0, 0))
    return pl.pallas_call(
        _outproj_kernel,
        grid=(t // tm,),
        in_specs=[row(WIDTH, 0), row(WIDTH, COL_AZ), row(WIDTH, 0), row(d, COL_GATT), row(d, COL_GRW),
                  row(d, 0), const((WIDTH, d)), const((WIDTH, d)), const((d, d)), const((1, d))],
        out_specs=row(d, 0),
        out_shape=jax.ShapeDtypeStruct((t, d), F32),
        compiler_params=pltpu.CompilerParams(
            dimension_semantics=("parallel",), vmem_limit_bytes=VMEM_LIMIT),
        name="outproj",
    )(ya, proj2d, yb, proj2d, proj2d, x2d, w_up_att, w_up_rw, w_out, g.reshape(1, d))


def _layer_params(l, w_in, rw_mu, rw_w0, rw_w_up, rw_a0, rw_a_up, rw_k_k, rw_k_a, rw_r_k,
                  rw_ln_g, rw_ln_b, rw_vmix_down, rw_vmix_mu, rw_vmix_up, rw_vmix0):
    w = w_in[l]
    d = w.shape[0]
    att = w[:, :4 * WIDTH]
    o = 4 * WIDTH
    rkv = w[:, o:o + 3 * WIDTH]
    lora = w[:, o + 3 * WIDTH:o + 3 * WIDTH + DECAY_LORA + ICLR_LORA]
    o = o + 3 * WIDTH + DECAY_LORA + ICLR_LORA
    rz = w[:, o:o + WIDTH]
    gates = w[:, o + WIDTH:]
    vd = jnp.zeros((d, LANES), F32)
    if l > 0:
        vd = vd.at[:, :VMIX_LORA].set(rw_vmix_down[l - 1])
    w_cat = jnp.concatenate([att, rkv, rz, gates, lora, vd], axis=1).astype(BF16)

    mu = rw_mu[l]
    zeros = jnp.zeros((DECAY_LORA, WIDTH), F32)
    lora_w = jnp.concatenate([jnp.concatenate([rw_w_up[l], zeros], axis=1),
                              jnp.concatenate([zeros, rw_a_up[l]], axis=1)], axis=0).astype(BF16)
    p = {
        "w_cat": w_cat,
        "mu_rkv": mu[:3 * WIDTH].reshape(3, WIDTH),
        "mu_lora": mu[3 * WIDTH:].reshape(1, LANES),
        "w0a0": jnp.concatenate([rw_w0[l], rw_a0[l]]).reshape(1, 2 * WIDTH),
        "lora_w": lora_w,
        "k_k": rw_k_k[l].reshape(1, WIDTH), "k_a": rw_k_a[l].reshape(1, WIDTH),
        "r_k": rw_r_k[l].reshape(1, WIDTH),
        "ln_g": rw_ln_g[l].reshape(1, WIDTH), "ln_b": rw_ln_b[l].reshape(1, WIDTH),
    }
    if l > 0:
        p["vmix_mu"] = jnp.zeros((1, LANES), F32).at[0, :VMIX_LORA].set(rw_vmix_mu[l - 1])
        p["vmix_up"] = jnp.zeros((LANES, WIDTH), F32).at[:VMIX_LORA].set(rw_vmix_up[l - 1]).astype(BF16)
        p["vmix0"] = rw_vmix0[l - 1].reshape(1, WIDTH)
    return p


def kernel(x, norm_pre, norm_post, w_in, rw_mu, rw_w0, rw_w_up, rw_a0, rw_a_up, rw_k_k, rw_k_a, rw_r_k, rw_ln_g, rw_ln_b, rw_vmix_down, rw_vmix_mu, rw_vmix_up, rw_vmix0, w_up_att, w_up_rw, w_out):
    b, s, d = x.shape
    assert d == D_MODEL and s % MOBA_BLOCK == 0
    depth = w_in.shape[0]
    x2d = x.reshape(b * s, d)
    vfirst = None
    for l in range(depth):
        p = _layer_params(l, w_in, rw_mu, rw_w0, rw_w_up, rw_a0, rw_a_up, rw_k_k, rw_k_a, rw_r_k,
                          rw_ln_g, rw_ln_b, rw_vmix_down, rw_vmix_mu, rw_vmix_up, rw_vmix0)
        proj2d = _inproj(x2d, norm_pre[l], p["w_cat"])
        proj = proj2d.reshape(b, s, PROJ_W)
        ya = _moba(proj)
        yb, vfirst = _rwkv(proj, vfirst, p)
        x2d = _outproj(ya.reshape(b * s, WIDTH), proj2d, yb.reshape(b * s, WIDTH), x2d,
                       w_up_att[l].astype(BF16), w_up_rw[l].astype(BF16), w_out[l].astype(BF16),
                       norm_post[l])
    return x2d.reshape(b, s, d)
```

```python
import functools

import jax
import jax.numpy as jnp
from jax import lax
from jax.experimental import pallas as pl
from jax.experimental.pallas import tpu as pltpu

F32 = jnp.float32
BF16 = jnp.bfloat16

D_MODEL = 1024
N_HEADS = 8
HEAD_DIM = 64
WIDTH = N_HEADS * HEAD_DIM
MOBA_BLOCK = 256
MOBA_TOPK = 3
DECAY_LORA = 64
ICLR_LORA = 64
VMIX_LORA = 32
RMS_EPS = 1e-6
GN_EPS = 64e-5
L2_EPS = 1e-12
NEG_INF = -1e30

LANES = 128
PAIR = 2 * HEAD_DIM
N_PAIRS = WIDTH // PAIR
CHUNK = 64

COL_Q, COL_K, COL_V, COL_AZ = 0, 512, 1024, 1536
COL_R, COL_RK, COL_RV, COL_RZ = 2048, 2560, 3072, 3584
COL_GATT, COL_GRW = 4096, 5120
COL_LORA = 6144
COL_VD = 6272
PROJ_W = 6400

VMEM_LIMIT = 48 * 1024 * 1024


def _sigmoid(x):
    return 1.0 / (1.0 + jnp.exp(-x))


def _mm(a, b):
    return jnp.dot(a.astype(BF16), b.astype(BF16), preferred_element_type=F32)


def _mm_nt(a, b):
    return lax.dot_general(a.astype(BF16), b.astype(BF16), (((1,), (1,)), ((), ())),
                           preferred_element_type=F32)


def _mm_tn(a, b):
    return lax.dot_general(a.astype(BF16), b.astype(BF16), (((0,), (0,)), ((), ())),
                           preferred_element_type=F32)


def _inproj_kernel(x_ref, g_ref, w_ref, o_ref, h_ref):
    @pl.when(pl.program_id(1) == 0)
    def _():
        x = x_ref[...]
        ms = jnp.mean(x * x, axis=-1, keepdims=True)
        h_ref[...] = (x * lax.rsqrt(ms + RMS_EPS) * g_ref[...]).astype(BF16)

    o_ref[...] = jnp.dot(h_ref[...], w_ref[...], preferred_element_type=F32)


def _inproj(x2d, g, w_cat, *, tm=1024, tn=1280):
    t, d = x2d.shape
    n = w_cat.shape[1]
    return pl.pallas_call(
        _inproj_kernel,
        grid=(t // tm, n // tn),
        in_specs=[
            pl.BlockSpec((tm, d), lambda i, j: (i, 0)),
            pl.BlockSpec((1, d), lambda i, j: (0, 0)),
            pl.BlockSpec((d, tn), lambda i, j: (0, j)),
        ],
        out_specs=pl.BlockSpec((tm, tn), lambda i, j: (i, j)),
        out_shape=jax.ShapeDtypeStruct((t, n), F32),
        scratch_shapes=[pltpu.VMEM((tm, d), BF16)],
        compiler_params=pltpu.CompilerParams(
            dimension_semantics=("parallel", "arbitrary"), vmem_limit_bytes=VMEM_LIMIT),
        name="inproj",
    )(x2d, g.reshape(1, d), w_cat)


def _moba_kernel(q_ref, k_ref, v_ref, o_ref, kb_ref, vt_ref, bias_ref, s_ref, p_ref, *, n_blocks):
    hp = pl.program_id(1)
    blk = MOBA_BLOCK
    s_len = n_blocks * blk
    log2e = 1.4426950408889634
    q_scale = (HEAD_DIM ** -0.5) * log2e

    kb_ref[...] = k_ref[...].astype(BF16)
    vt_ref[...] = v_ref[...].T.astype(BF16)

    mean_row = lax.broadcasted_iota(jnp.int32, (n_blocks, PAIR), 0)
    kmean = jnp.zeros((n_blocks, PAIR), F32)
    for n in range(n_blocks):
        kmean = jnp.where(mean_row == n,
                          jnp.mean(k_ref[n * blk:(n + 1) * blk, :], axis=0, keepdims=True), kmean)

    lane = lax.broadcasted_iota(jnp.int32, (1, PAIR), 1)
    head_lanes = (lane < HEAD_DIM, lane >= HEAD_DIM)
    nrow = lax.broadcasted_iota(jnp.int32, (n_blocks, s_len), 0)
    qblk = lax.broadcasted_iota(jnp.int32, (n_blocks, s_len), 1) // blk
    past = nrow < qblk
    krow = lax.broadcasted_iota(jnp.int32, (blk, blk), 0)
    qcol = lax.broadcasted_iota(jnp.int32, (blk, blk), 1)
    dist0 = (qcol - krow).astype(F32)
    causal = qcol >= krow

    alibi0 = []
    for h in range(2):
        head = (hp * 2 + h + 1).astype(F32)
        slope2 = jnp.exp2(jnp.zeros((1, 1), F32) - (8.0 / N_HEADS) * head) * log2e
        alibi0.append(slope2 * dist0)

        gate = lax.dot_general(jnp.where(head_lanes[h], kmean, 0.0), q_ref[...], (((1,), (1,)), ((), ())),
                               precision=lax.Precision.HIGHEST, preferred_element_type=F32)
        gm = jnp.where(past, gate, NEG_INF)
        rank = jnp.zeros((n_blocks, s_len), F32)
        for m in range(n_blocks):
            gcol = gm[m:m + 1, :]
            beats = (gcol > gm) | ((gcol == gm) & (nrow > m))
            rank = rank + jnp.where(beats, 1.0, 0.0)
        sel = past & (rank < MOBA_TOPK)
        bias_ref[h] = jnp.where(sel, -slope2 * ((qblk - nrow) * blk).astype(F32), NEG_INF)

    for i in range(n_blocks):
        rows = slice(i * blk, (i + 1) * blk)
        n_k = (i + 1) * blk
        q_i = q_ref[rows, :] * q_scale
        acc = []
        for h in range(2):
            qm = jnp.where(head_lanes[h], q_i, 0.0).astype(BF16)
            m_run = None
            for j in range(i + 1):
                st = lax.dot_general(kb_ref[j * blk:(j + 1) * blk, :], qm, (((1,), (1,)), ((), ())),
                                     preferred_element_type=F32)
                if j == i:
                    st = jnp.where(causal, st - alibi0[h], NEG_INF)
                else:
                    st = st - alibi0[h] + bias_ref[h, j:j + 1, rows]
                s_ref[h, j * blk:(j + 1) * blk, :] = st
                m_j = jnp.max(st, axis=0, keepdims=True)
                m_run = m_j if m_run is None else jnp.maximum(m_run, m_j)
            p = jnp.exp2(s_ref[h, :n_k, :] - m_run)
            l_sum = jnp.sum(p, axis=0, keepdims=True)
            p_ref[h, :n_k, :] = p.astype(BF16)
            pv = jnp.dot(vt_ref[:, :n_k], p_ref[h, :n_k, :], preferred_element_type=F32)
            acc.append(pv * (1.0 / l_sum))
        out_t = jnp.concatenate([acc[0][:HEAD_DIM], acc[1][HEAD_DIM:]], axis=0)
        o_ref[rows, :] = out_t.T


def _moba(proj):
    b, s, _ = proj.shape
    n_blocks = s // MOBA_BLOCK
    qo, ko, vo = COL_Q // PAIR, COL_K // PAIR, COL_V // PAIR
    return pl.pallas_call(
        functools.partial(_moba_kernel, n_blocks=n_blocks),
        grid=(b, N_PAIRS),
        in_specs=[
            pl.BlockSpec((None, s, PAIR), lambda i, p: (i, 0, qo + p)),
            pl.BlockSpec((None, s, PAIR), lambda i, p: (i, 0, ko + p)),
            pl.BlockSpec((None, s, PAIR), lambda i, p: (i, 0, vo + p)),
        ],
        out_specs=pl.BlockSpec((None, s, PAIR), lambda i, p: (i, 0, p)),
        out_shape=jax.ShapeDtypeStruct((b, s, WIDTH), F32),
        scratch_shapes=[
            pltpu.VMEM((s, PAIR), BF16),
            pltpu.VMEM((PAIR, s), BF16),
            pltpu.VMEM((2, n_blocks, s), F32),
            pltpu.VMEM((2, s, MOBA_BLOCK), F32),
            pltpu.VMEM((2, s, MOBA_BLOCK), BF16),
        ],
        compiler_params=pltpu.CompilerParams(
            dimension_semantics=("parallel", "parallel"), vmem_limit_bytes=VMEM_LIMIT),
        name="moba",
    )(proj, proj, proj)


def _rwkv_chunk(r, k, v, na, nb, lw, state, consts):
    ltri, head0, strict, incl, eye = consts
    cum = jnp.dot(ltri, lw, precision=lax.Precision.HIGHEST, preferred_element_type=F32)
    tot = cum[CHUNK - 1:CHUNK, :]
    e_pos = jnp.exp(cum)
    e_neg = jnp.exp(-cum)
    e_prev = jnp.exp(cum - lw)
    e_rel = jnp.exp(tot - cum)

    def stack(x):
        return jnp.concatenate([jnp.where(head0, x, 0.0), jnp.where(head0, 0.0, x)], axis=0)

    def dup(x):
        return jnp.concatenate([x, x], axis=0)

    lhs = jnp.concatenate([stack(na * e_prev), stack(r * e_pos)], axis=0).astype(BF16)
    rhs = jnp.concatenate([dup(nb * e_neg), dup(k * e_neg)], axis=0).astype(BF16)
    g = _mm_nt(lhs, rhs)
    n2 = 2 * CHUNK
    a_ab = jnp.where(strict, g[:n2, :n2], 0.0)
    a_ak = jnp.where(strict, g[:n2, n2:], 0.0)
    a_rb = jnp.where(incl, g[n2:, :n2], 0.0)
    a_rk = jnp.where(incl, g[n2:, n2:], 0.0)

    t_inv = eye + a_ab
    pw = a_ab
    for _ in range(CHUNK.bit_length() - 2):
        pw = _mm(pw, pw)
        t_inv = t_inv + _mm(t_inv, pw)

    vs = stack(v)
    xo = _mm_nt(lhs, state)
    av = _mm(jnp.concatenate([a_ak, a_rk], axis=0), vs)
    us = _mm(t_inv, xo[:n2] + av[:n2])
    os_ = xo[n2:] + av[n2:] + _mm(a_rb, us)
    out = os_[:CHUNK] + os_[CHUNK:]
    new_state = state * jnp.exp(tot) + _mm_tn(
        jnp.concatenate([us, vs], axis=0),
        jnp.concatenate([stack(nb * e_rel), stack(k * e_rel)], axis=0))
    return out, new_state


def _rwkv_kernel(*refs, has_vmix, ts):
    if has_vmix:
        (r_ref, k_ref, v_ref, rz_ref, lora_ref, vd_ref, vfirst_ref,
         mu_rkv_ref, mu_lora_ref, w0a0_ref, lora_w_ref, kk_ref, ka_ref, rk_ref, lng_ref, lnb_ref,
         vmix_mu_ref, vmix_up_ref, vmix0_ref,
         yb_ref, carry_rkv, carry_lora, carry_vd, state_ref) = refs
    else:
        (r_ref, k_ref, v_ref, rz_ref, lora_ref,
         mu_rkv_ref, mu_lora_ref, w0a0_ref, lora_w_ref, kk_ref, ka_ref, rk_ref, lng_ref, lnb_ref,
         yb_ref, vfirst_ref, carry_rkv, carry_lora, state_ref) = refs
        carry_vd = None

    @pl.when(pl.program_id(1) == 0)
    def _():
        carry_rkv[...] = jnp.zeros_like(carry_rkv)
        carry_lora[...] = jnp.zeros_like(carry_lora)
        if carry_vd is not None:
            carry_vd[...] = jnp.zeros_like(carry_vd)
        state_ref[...] = jnp.zeros_like(state_ref)

    def shift(y, carry_ref, row_idx, mu):
        first = lax.broadcasted_iota(jnp.int32, y.shape, 0) == 0
        prev = jnp.where(first, carry_ref[row_idx:row_idx + 1, :], pltpu.roll(y, 1, 0))
        carry_ref[row_idx:row_idx + 1, :] = y[ts - 1:ts, :]
        return y + (prev - y) * mu

    mu_rkv = mu_rkv_ref[...]
    rs = shift(r_ref[...], carry_rkv, 0, mu_rkv[0:1, :])
    ks = shift(k_ref[...], carry_rkv, 1, mu_rkv[1:2, :])
    vs = shift(v_ref[...], carry_rkv, 2, mu_rkv[2:3, :])
    lo = shift(lora_ref[...], carry_lora, 0, mu_lora_ref[...])

    lane = lax.broadcasted_iota(jnp.int32, (ts, LANES), 1)
    z = jnp.where(lane < DECAY_LORA, jnp.tanh(lo), lo)
    wa = w0a0_ref[...] + _mm(z, lora_w_ref[...])
    zz = -wa[:, :WIDTH]
    w_log = -(jnp.maximum(zz, 0.0) + jnp.log(1.0 + jnp.exp(-jnp.abs(zz)))) - 0.5
    lw = -jnp.exp(w_log)
    a = _sigmoid(wa[:, WIDTH:])

    if has_vmix:
        vd = shift(vd_ref[...], carry_vd, 0, vmix_mu_ref[...])
        mix = _sigmoid(vmix0_ref[...] + _mm(vd, vmix_up_ref[...]))
        vr = vs + (vfirst_ref[...] - vs) * mix
    else:
        vfirst_ref[...] = vs
        vr = vs

    kx = ks * kk_ref[...]
    kmod = ks * (1.0 + (a - 1.0) * ka_ref[...])
    rkr = rs * kmod * rk_ref[...]
    rz = rz_ref[...]
    ln_g = lng_ref[...]
    ln_b = lnb_ref[...]

    head0_t = lane < HEAD_DIM

    def headsum(x):
        s0 = jnp.sum(jnp.where(head0_t, x, 0.0), axis=-1, keepdims=True)
        s1 = jnp.sum(jnp.where(head0_t, 0.0, x), axis=-1, keepdims=True)
        return jnp.where(head0_t, s0, s1)

    n2 = 2 * CHUNK
    crow = lax.broadcasted_iota(jnp.int32, (CHUNK, CHUNK), 0)
    ccol = lax.broadcasted_iota(jnp.int32, (CHUNK, CHUNK), 1)
    ltri = jnp.where(ccol <= crow, 1.0, 0.0).astype(F32)
    head0 = lax.broadcasted_iota(jnp.int32, (CHUNK, PAIR), 1) < HEAD_DIM
    srow = lax.broadcasted_iota(jnp.int32, (n2, n2), 0)
    scol = lax.broadcasted_iota(jnp.int32, (n2, n2), 1)
    same_head = (srow // CHUNK) == (scol // CHUNK)
    strict = same_head & (scol < srow)
    incl = same_head & (scol <= srow)
    eye = jnp.where(srow == scol, 1.0, 0.0).astype(F32)
    consts = (ltri, head0, strict, incl, eye)

    for p in range(N_PAIRS):
        sl = slice(p * PAIR, (p + 1) * PAIR)
        kxp = kx[:, sl]
        kkp = kxp / jnp.maximum(jnp.sqrt(headsum(kxp * kxp)), L2_EPS)
        a_p = a[:, sl]
        vr_p = vr[:, sl]
        bonus = headsum(rkr[:, sl]) * vr_p
        na = -kkp
        nb = kkp * a_p
        r_p = rs[:, sl]
        k_p = kmod[:, sl]
        lw_p = lw[:, sl]

        state = state_ref[p]
        outs = []
        for c in range(ts // CHUNK):
            rc = slice(c * CHUNK, (c + 1) * CHUNK)
            o_c, state = _rwkv_chunk(r_p[rc], k_p[rc], vr_p[rc], na[rc], nb[rc], lw_p[rc], state, consts)
            outs.append(o_c)
        state_ref[p] = state
        o_p = jnp.concatenate(outs, axis=0) if len(outs) > 1 else outs[0]

        mu = headsum(o_p) * (1.0 / HEAD_DIM)
        d = o_p - mu
        var = headsum(d * d) * (1.0 / HEAD_DIM)
        on = d * lax.rsqrt(var + GN_EPS) * ln_g[:, sl] + ln_b[:, sl]
        rz_p = rz[:, sl]
        yb_ref[:, sl] = (on + bonus) * (rz_p * _sigmoid(rz_p))


def _rwkv(proj, vfirst, p, *, ts=128):
    b, s, _ = proj.shape
    has_vmix = vfirst is not None
    row = lambda width, col: pl.BlockSpec((None, ts, width), lambda i, t: (i, t, col // width))
    const = lambda shape: pl.BlockSpec(shape, lambda i, t: (0,) * len(shape))
    act = pl.BlockSpec((None, ts, WIDTH), lambda i, t: (i, t, 0))

    in_specs = [row(WIDTH, COL_R), row(WIDTH, COL_RK), row(WIDTH, COL_RV), row(WIDTH, COL_RZ),
                row(LANES, COL_LORA)]
    args = [proj, proj, proj, proj, proj]
    if has_vmix:
        in_specs += [row(LANES, COL_VD), act]
        args += [proj, vfirst]
    in_specs += [const((3, WIDTH)), const((1, LANES)), const((1, 2 * WIDTH)), const((LANES, 2 * WIDTH))]
    args += [p["mu_rkv"], p["mu_lora"], p["w0a0"], p["lora_w"]]
    for name in ("k_k", "k_a", "r_k", "ln_g", "ln_b"):
        in_specs.append(const((1, WIDTH)))
        args.append(p[name])
    if has_vmix:
        in_specs += [const((1, LANES)), const((LANES, WIDTH)), const((1, WIDTH))]
        args += [p["vmix_mu"], p["vmix_up"], p["vmix0"]]

    out_shape = [jax.ShapeDtypeStruct((b, s, WIDTH), F32)]
    out_specs = [act]
    scratch = [pltpu.VMEM((8, WIDTH), F32), pltpu.VMEM((8, LANES), F32)]
    if has_vmix:
        scratch.append(pltpu.VMEM((8, LANES), F32))
    else:
        out_shape.append(jax.ShapeDtypeStruct((b, s, WIDTH), F32))
        out_specs.append(act)
    scratch.append(pltpu.VMEM((N_PAIRS, PAIR, PAIR), F32))

    res = pl.pallas_call(
        functools.partial(_rwkv_kernel, has_vmix=has_vmix, ts=ts),
        grid=(b, s // ts),
        in_specs=in_specs,
        out_specs=out_specs,
        out_shape=out_shape,
        scratch_shapes=scratch,
        compiler_params=pltpu.CompilerParams(
            dimension_semantics=("parallel", "arbitrary"), vmem_limit_bytes=VMEM_LIMIT),
        name="rwkv_vmix" if has_vmix else "rwkv",
    )(*args)
    return (res[0], vfirst) if has_vmix else (res[0], res[1])


def _outproj_kernel(ya_ref, az_ref, yb_ref, gatt_ref, grw_ref, x_ref,
                    wua_ref, wub_ref, wout_ref, g_ref, o_ref):
    az = az_ref[...]
    ya = ya_ref[...] * (az * _sigmoid(az))
    u = (_sigmoid(gatt_ref[...]) * _mm(ya, wua_ref[...])
         + _sigmoid(grw_ref[...]) * _mm(yb_ref[...], wub_ref[...]))
    y = _mm(u, wout_ref[...])
    ms = jnp.mean(y * y, axis=-1, keepdims=True)
    o_ref[...] = x_ref[...] + y * lax.rsqrt(ms + RMS_EPS) * g_ref[...]


def _outproj(ya, proj2d, yb, x2d, w_up_att, w_up_rw, w_out, g, *, tm=512):
    t, d = x2d.shape
    row = lambda width, col: pl.BlockSpec((tm, width), lambda i: (i, col // width))
    const = lambda shape: pl.BlockSpec(shape, lambda i: (0, 0))
    return pl.pallas_call(
        _outproj_kernel,
        grid=(t // tm,),
        in_specs=[row(WIDTH, 0), row(WIDTH, COL_AZ), row(WIDTH, 0), row(d, COL_GATT), row(d, COL_GRW),
                  row(d, 0), const((WIDTH, d)), const((WIDTH, d)), const((d, d)), const((1, d))],
        out_specs=row(d, 0),
        out_shape=jax.ShapeDtypeStruct((t, d), F32),
        compiler_params=pltpu.CompilerParams(
            dimension_semantics=("parallel",), vmem_limit_bytes=VMEM_LIMIT),
        name="outproj",
    )(ya, proj2d, yb, proj2d, proj2d, x2d, w_up_att, w_up_rw, w_out, g.reshape(1, d))


def _layer_params(l, w_in, rw_mu, rw_w0, rw_w_up, rw_a0, rw_a_up, rw_k_k, rw_k_a, rw_r_k,
                  rw_ln_g, rw_ln_b, rw_vmix_down, rw_vmix_mu, rw_vmix_up, rw_vmix0):
    w = w_in[l]
    d = w.shape[0]
    att = w[:, :4 * WIDTH]
    o = 4 * WIDTH
    rkv = w[:, o:o + 3 * WIDTH]
    lora = w[:, o + 3 * WIDTH:o + 3 * WIDTH + DECAY_LORA + ICLR_LORA]
    o = o + 3 * WIDTH + DECAY_LORA + ICLR_LORA
    rz = w[:, o:o + WIDTH]
    gates = w[:, o + WIDTH:]
    vd = jnp.zeros((d, LANES), F32)
    if l > 0:
        vd = vd.at[:, :VMIX_LORA].set(rw_vmix_down[l - 1])
    w_cat = jnp.concatenate([att, rkv, rz, gates, lora, vd], axis=1).astype(BF16)

    mu = rw_mu[l]
    zeros = jnp.zeros((DECAY_LORA, WIDTH), F32)
    lora_w = jnp.concatenate([jnp.concatenate([rw_w_up[l], zeros], axis=1),
                              jnp.concatenate([zeros, rw_a_up[l]], axis=1)], axis=0).astype(BF16)
    p = {
        "w_cat": w_cat,
        "mu_rkv": mu[:3 * WIDTH].reshape(3, WIDTH),
        "mu_lora": mu[3 * WIDTH:].reshape(1, LANES),
        "w0a0": jnp.concatenate([rw_w0[l], rw_a0[l]]).reshape(1, 2 * WIDTH),
        "lora_w": lora_w,
        "k_k": rw_k_k[l].reshape(1, WIDTH), "k_a": rw_k_a[l].reshape(1, WIDTH),
        "r_k": rw_r_k[l].reshape(1, WIDTH),
        "ln_g": rw_ln_g[l].reshape(1, WIDTH), "ln_b": rw_ln_b[l].reshape(1, WIDTH),
    }
    if l > 0:
        p["vmix_mu"] = jnp.zeros((1, LANES), F32).at[0, :VMIX_LORA].set(rw_vmix_mu[l - 1])
        p["vmix_up"] = jnp.zeros((LANES, WIDTH), F32).at[:VMIX_LORA].set(rw_vmix_up[l - 1]).astype(BF16)
        p["vmix0"] = rw_vmix0[l - 1].reshape(1, WIDTH)
    return p


def kernel(x, norm_pre, norm_post, w_in, rw_mu, rw_w0, rw_w_up, rw_a0, rw_a_up, rw_k_k, rw_k_a, rw_r_k, rw_ln_g, rw_ln_b, rw_vmix_down, rw_vmix_mu, rw_vmix_up, rw_vmix0, w_up_att, w_up_rw, w_out):
    b, s, d = x.shape
    assert d == D_MODEL and s % MOBA_BLOCK == 0
    depth = w_in.shape[0]
    x2d = x.reshape(b * s, d)
    vfirst = None
    for l in range(depth):
        p = _layer_params(l, w_in, rw_mu, rw_w0, rw_w_up, rw_a0, rw_a_up, rw_k_k, rw_k_a, rw_r_k,
                          rw_ln_g, rw_ln_b, rw_vmix_down, rw_vmix_mu, rw_vmix_up, rw_vmix0)
        proj2d = _inproj(x2d, norm_pre[l], p["w_cat"])
        proj = proj2d.reshape(b, s, PROJ_W)
        ya = _moba(proj)
        yb, vfirst = _rwkv(proj, vfirst, p)
        x2d = _outproj(ya.reshape(b * s, WIDTH), proj2d, yb.reshape(b * s, WIDTH), x2d,
                       w_up_att[l].astype(BF16), w_up_rw[l].astype(BF16), w_out[l].astype(BF16),
                       norm_post[l])
    return x2d.reshape(b, s, d)
```

```python
import functools

import jax
import jax.numpy as jnp
from jax import lax
from jax.experimental import pallas as pl
from jax.experimental.pallas import tpu as pltpu

F32 = jnp.float32
BF16 = jnp.bfloat16

D_MODEL = 1024
N_HEADS = 8
HEAD_DIM = 64
WIDTH = N_HEADS * HEAD_DIM
MOBA_BLOCK = 256
MOBA_TOPK = 3
DECAY_LORA = 64
ICLR_LORA = 64
VMIX_LORA = 32
RMS_EPS = 1e-6
GN_EPS = 64e-5
L2_EPS = 1e-12
NEG_INF = -1e30

LANES = 128
PAIR = 2 * HEAD_DIM
N_PAIRS = WIDTH // PAIR
CHUNK = 64

COL_Q, COL_K, COL_V, COL_AZ = 0, 512, 1024, 1536
COL_R, COL_RK, COL_RV, COL_RZ = 2048, 2560, 3072, 3584
COL_GATT, COL_GRW = 4096, 5120
COL_LORA = 6144
COL_VD = 6272
PROJ_W = 6400

VMEM_LIMIT = 48 * 1024 * 1024


def _sigmoid(x):
    return 1.0 / (1.0 + jnp.exp(-x))


def _mm(a, b):
    return jnp.dot(a.astype(BF16), b.astype(BF16), preferred_element_type=F32)


def _mm_nt(a, b):
    return lax.dot_general(a.astype(BF16), b.astype(BF16), (((1,), (1,)), ((), ())),
                           preferred_element_type=F32)


def _mm_tn(a, b):
    return lax.dot_general(a.astype(BF16), b.astype(BF16), (((0,), (0,)), ((), ())),
                           preferred_element_type=F32)


def _inproj_kernel(x_ref, g_ref, w_ref, o_ref, h_ref):
    @pl.when(pl.program_id(1) == 0)
    def _():
        x = x_ref[...]
        ms = jnp.mean(x * x, axis=-1, keepdims=True)
        h_ref[...] = (x * lax.rsqrt(ms + RMS_EPS) * g_ref[...]).astype(BF16)

    o_ref[...] = jnp.dot(h_ref[...], w_ref[...], preferred_element_type=F32)


def _inproj(x2d, g, w_cat, *, tm=1024, tn=1280):
    t, d = x2d.shape
    n = w_cat.shape[1]
    return pl.pallas_call(
        _inproj_kernel,
        grid=(t // tm, n // tn),
        in_specs=[
            pl.BlockSpec((tm, d), lambda i, j: (i, 0)),
            pl.BlockSpec((1, d), lambda i, j: (0, 0)),
            pl.BlockSpec((d, tn), lambda i, j: (0, j)),
        ],
        out_specs=pl.BlockSpec((tm, tn), lambda i, j: (i, j)),
        out_shape=jax.ShapeDtypeStruct((t, n), F32),
        scratch_shapes=[pltpu.VMEM((tm, d), BF16)],
        compiler_params=pltpu.CompilerParams(
            dimension_semantics=("parallel", "arbitrary"), vmem_limit_bytes=VMEM_LIMIT),
        name="inproj",
    )(x2d, g.reshape(1, d), w_cat)


def _moba_kernel(q_ref, k_ref, v_ref, o_ref, kb_ref, vt_ref, bias_ref, s_ref, p_ref, *, n_blocks):
    hp = pl.program_id(1)
    blk = MOBA_BLOCK
    s_len = n_blocks * blk
    log2e = 1.4426950408889634
    q_scale = (HEAD_DIM ** -0.5) * log2e

    kb_ref[...] = k_ref[...].astype(BF16)
    vt_ref[...] = v_ref[...].T.astype(BF16)

    mean_row = lax.broadcasted_iota(jnp.int32, (n_blocks, PAIR), 0)
    kmean = jnp.zeros((n_blocks, PAIR), F32)
    for n in range(n_blocks):
        kmean = jnp.where(mean_row == n,
                          jnp.mean(k_ref[n * blk:(n + 1) * blk, :], axis=0, keepdims=True), kmean)

    lane = lax.broadcasted_iota(jnp.int32, (1, PAIR), 1)
    head_lanes = (lane < HEAD_DIM, lane >= HEAD_DIM)
    nrow = lax.broadcasted_iota(jnp.int32, (n_blocks, s_len), 0)
    qblk = lax.broadcasted_iota(jnp.int32, (n_blocks, s_len), 1) // blk
    past = nrow < qblk
    krow = lax.broadcasted_iota(jnp.int32, (blk, blk), 0)
    qcol = lax.broadcasted_iota(jnp.int32, (blk, blk), 1)
    dist0 = (qcol - krow).astype(F32)
    causal = qcol >= krow

    alibi0 = []
    for h in range(2):
        head = lax.convert_element_type(hp * 2 + h + 1, F32)
        slope2 = jnp.exp2(jnp.zeros((1, 1), F32) - (8.0 / N_HEADS) * head) * log2e
        alibi0.append(slope2 * dist0)

        gate = lax.dot_general(jnp.where(head_lanes[h], kmean, 0.0), q_ref[...], (((1,), (1,)), ((), ())),
                               precision=lax.Precision.HIGHEST, preferred_element_type=F32)
        gm = jnp.where(past, gate, NEG_INF)
        rank = jnp.zeros((n_blocks, s_len), F32)
        for m in range(n_blocks):
            gcol = gm[m:m + 1, :]
            beats = (gcol > gm) | ((gcol == gm) & (nrow > m))
            rank = rank + jnp.where(beats, 1.0, 0.0)
        sel = past & (rank < MOBA_TOPK)
        bias_ref[h] = jnp.where(sel, -slope2 * ((qblk - nrow) * blk).astype(F32), NEG_INF)

    for i in range(n_blocks):
        rows = slice(i * blk, (i + 1) * blk)
        n_k = (i + 1) * blk
        q_i = q_ref[rows, :] * q_scale
        acc = []
        for h in range(2):
            qm = jnp.where(head_lanes[h], q_i, 0.0).astype(BF16)
            m_run = None
            for j in range(i + 1):
                st = lax.dot_general(kb_ref[j * blk:(j + 1) * blk, :], qm, (((1,), (1,)), ((), ())),
                                     preferred_element_type=F32)
                if j == i:
                    st = jnp.where(causal, st - alibi0[h], NEG_INF)
                else:
                    st = st - alibi0[h] + bias_ref[h, j:j + 1, rows]
                s_ref[h, j * blk:(j + 1) * blk, :] = st
                m_j = jnp.max(st, axis=0, keepdims=True)
                m_run = m_j if m_run is None else jnp.maximum(m_run, m_j)
            p = jnp.exp2(s_ref[h, :n_k, :] - m_run)
            l_sum = jnp.sum(p, axis=0, keepdims=True)
            p_ref[h, :n_k, :] = p.astype(BF16)
            pv = jnp.dot(vt_ref[:, :n_k], p_ref[h, :n_k, :], preferred_element_type=F32)
            acc.append(pv * (1.0 / l_sum))
        out_t = jnp.concatenate([acc[0][:HEAD_DIM], acc[1][HEAD_DIM:]], axis=0)
        o_ref[rows, :] = out_t.T


def _moba(proj):
    b, s, _ = proj.shape
    n_blocks = s // MOBA_BLOCK
    qo, ko, vo = COL_Q // PAIR, COL_K // PAIR, COL_V // PAIR
    return pl.pallas_call(
        functools.partial(_moba_kernel, n_blocks=n_blocks),
        grid=(b, N_PAIRS),
        in_specs=[
            pl.BlockSpec((None, s, PAIR), lambda i, p: (i, 0, qo + p)),
            pl.BlockSpec((None, s, PAIR), lambda i, p: (i, 0, ko + p)),
            pl.BlockSpec((None, s, PAIR), lambda i, p: (i, 0, vo + p)),
        ],
        out_specs=pl.BlockSpec((None, s, PAIR), lambda i, p: (i, 0, p)),
        out_shape=jax.ShapeDtypeStruct((b, s, WIDTH), F32),
        scratch_shapes=[
            pltpu.VMEM((s, PAIR), BF16),
            pltpu.VMEM((PAIR, s), BF16),
            pltpu.VMEM((2, n_blocks, s), F32),
            pltpu.VMEM((2, s, MOBA_BLOCK), F32),
            pltpu.VMEM((2, s, MOBA_BLOCK), BF16),
        ],
        compiler_params=pltpu.CompilerParams(
            dimension_semantics=("parallel", "parallel"), vmem_limit_bytes=VMEM_LIMIT),
        name="moba",
    )(proj, proj, proj)


def _rwkv_scan(rs, kmod, vr, na, nb, lw, state_ref, ts):
    n_chunks = ts // CHUNK
    n2 = 2 * CHUNK
    trow = lax.broadcasted_iota(jnp.int32, (ts, ts), 0)
    tcol = lax.broadcasted_iota(jnp.int32, (ts, ts), 1)
    ltri = jnp.where((trow // CHUNK == tcol // CHUNK) & (tcol <= trow), 1.0, 0.0).astype(F32)
    head0 = lax.broadcasted_iota(jnp.int32, (CHUNK, PAIR), 1) < HEAD_DIM
    srow = lax.broadcasted_iota(jnp.int32, (n2, n2), 0)
    scol = lax.broadcasted_iota(jnp.int32, (n2, n2), 1)
    same_head = (srow // CHUNK) == (scol // CHUNK)
    strict = same_head & (scol < srow)
    incl = same_head & (scol <= srow)
    eye = jnp.where(srow == scol, 1.0, 0.0).astype(F32)

    def stack(x):
        return jnp.concatenate([jnp.where(head0, x, 0.0), jnp.where(head0, 0.0, x)], axis=0)

    def dup(x):
        return jnp.concatenate([x, x], axis=0)

    cum = jnp.dot(ltri, lw, precision=lax.Precision.HIGHEST, preferred_element_type=F32)
    a_t = na * jnp.exp(cum - lw)
    r_t = rs * jnp.exp(cum)
    e_neg = jnp.exp(-cum)
    b_t = nb * e_neg
    k_t = kmod * e_neg

    pcs = [(p, c) for c in range(n_chunks) for p in range(N_PAIRS)]
    lhs_a, lhs_r, rhs, b_h, k_h, v_s, gam = {}, {}, {}, {}, {}, {}, {}
    for c in range(n_chunks):
        rc = slice(c * CHUNK, (c + 1) * CHUNK)
        tot = cum[(c + 1) * CHUNK - 1:(c + 1) * CHUNK, :]
        e_rel = jnp.exp(tot - cum[rc])
        bh_c = nb[rc] * e_rel
        kh_c = kmod[rc] * e_rel
        gam_c = jnp.exp(tot)
        for p in range(N_PAIRS):
            sl = slice(p * PAIR, (p + 1) * PAIR)
            pc = (p, c)
            lhs_a[pc] = stack(a_t[rc, sl]).astype(BF16)
            lhs_r[pc] = stack(r_t[rc, sl])
            rhs[pc] = jnp.concatenate([dup(b_t[rc, sl]), dup(k_t[rc, sl])], axis=0).astype(BF16)
            b_h[pc] = stack(bh_c[:, sl]).astype(BF16)
            k_h[pc] = stack(kh_c[:, sl])
            v_s[pc] = stack(vr[rc, sl]).astype(BF16)
            gam[pc] = gam_c[:, sl]

    a_ab, a_ak, a_rb, a_rk = {}, {}, {}, {}
    for pc in pcs:
        g = _mm_nt(jnp.concatenate([lhs_a[pc], lhs_r[pc].astype(BF16)], axis=0), rhs[pc])
        a_ab[pc] = jnp.where(strict, g[:n2, :n2], 0.0)
        a_ak[pc] = jnp.where(strict, g[:n2, n2:], 0.0).astype(BF16)
        a_rb[pc] = jnp.where(incl, g[n2:, :n2], 0.0).astype(BF16)
        a_rk[pc] = jnp.where(incl, g[n2:, n2:], 0.0)

    t_inv = {pc: eye + a_ab[pc] for pc in pcs}
    pw = {pc: a_ab[pc].astype(BF16) for pc in pcs}
    for _ in range(CHUNK.bit_length() - 2):
        pw = {pc: _mm(pw[pc], pw[pc]).astype(BF16) for pc in pcs}
        t_inv = {pc: t_inv[pc] + _mm(t_inv[pc], pw[pc]) for pc in pcs}
    t_inv = {pc: t_inv[pc].astype(BF16) for pc in pcs}

    z = {pc: _mm_tn(t_inv[pc], b_h[pc]).astype(BF16) for pc in pcs}
    w = {pc: _mm(a_rb[pc], t_inv[pc]).astype(BF16) for pc in pcs}
    m_t = {pc: _mm_tn(lhs_a[pc], z[pc]).astype(BF16) for pc in pcs}
    y = {pc: (_mm_tn(a_ak[pc], z[pc]) + k_h[pc]).astype(BF16) for pc in pcs}
    q_p = {pc: (lhs_r[pc] + _mm(w[pc], lhs_a[pc])).astype(BF16) for pc in pcs}
    p_p = {pc: (_mm(w[pc], a_ak[pc]) + a_rk[pc]).astype(BF16) for pc in pcs}
    g_s = {pc: _mm_tn(v_s[pc], y[pc]) for pc in pcs}
    o_loc = {pc: _mm(p_p[pc], v_s[pc]) for pc in pcs}

    states = [state_ref[p] for p in range(N_PAIRS)]
    outs = [[] for _ in range(N_PAIRS)]
    for c in range(n_chunks):
        for p in range(N_PAIRS):
            pc = (p, c)
            sb = states[p].astype(BF16)
            os_ = _mm_nt(q_p[pc], sb) + o_loc[pc]
            outs[p].append(os_[:CHUNK] + os_[CHUNK:])
            states[p] = states[p] * gam[pc] + _mm(sb, m_t[pc]) + g_s[pc]
    for p in range(N_PAIRS):
        state_ref[p] = states[p]
    return [jnp.concatenate(o, axis=0) if len(o) > 1 else o[0] for o in outs]


def _rwkv_kernel(*refs, has_vmix, ts):
    if has_vmix:
        (r_ref, k_ref, v_ref, rz_ref, lora_ref, vd_ref, vfirst_ref,
         mu_rkv_ref, mu_lora_ref, w0a0_ref, lora_w_ref, kk_ref, ka_ref, rk_ref, lng_ref, lnb_ref,
         vmix_mu_ref, vmix_up_ref, vmix0_ref,
         yb_ref, carry_rkv, carry_lora, carry_vd, state_ref) = refs
    else:
        (r_ref, k_ref, v_ref, rz_ref, lora_ref,
         mu_rkv_ref, mu_lora_ref, w0a0_ref, lora_w_ref, kk_ref, ka_ref, rk_ref, lng_ref, lnb_ref,
         yb_ref, vfirst_ref, carry_rkv, carry_lora, state_ref) = refs
        carry_vd = None

    @pl.when(pl.program_id(1) == 0)
    def _():
        carry_rkv[...] = jnp.zeros_like(carry_rkv)
        carry_lora[...] = jnp.zeros_like(carry_lora)
        if carry_vd is not None:
            carry_vd[...] = jnp.zeros_like(carry_vd)
        state_ref[...] = jnp.zeros_like(state_ref)

    def shift(y, carry_ref, row_idx, mu):
        first = lax.broadcasted_iota(jnp.int32, y.shape, 0) == 0
        prev = jnp.where(first, carry_ref[row_idx:row_idx + 1, :], pltpu.roll(y, 1, 0))
        carry_ref[row_idx:row_idx + 1, :] = y[ts - 1:ts, :]
        return y + (prev - y) * mu

    mu_rkv = mu_rkv_ref[...]
    rs = shift(r_ref[...], carry_rkv, 0, mu_rkv[0:1, :])
    ks = shift(k_ref[...], carry_rkv, 1, mu_rkv[1:2, :])
    vs = shift(v_ref[...], carry_rkv, 2, mu_rkv[2:3, :])
    lo = shift(lora_ref[...], carry_lora, 0, mu_lora_ref[...])

    lane = lax.broadcasted_iota(jnp.int32, (ts, LANES), 1)
    z = jnp.where(lane < DECAY_LORA, jnp.tanh(lo), lo)
    wa = w0a0_ref[...] + _mm(z, lora_w_ref[...])
    zz = -wa[:, :WIDTH]
    w_log = -(jnp.maximum(zz, 0.0) + jnp.log(1.0 + jnp.exp(-jnp.abs(zz)))) - 0.5
    lw = -jnp.exp(w_log)
    a = _sigmoid(wa[:, WIDTH:])

    if has_vmix:
        vd = shift(vd_ref[...], carry_vd, 0, vmix_mu_ref[...])
        mix = _sigmoid(vmix0_ref[...] + _mm(vd, vmix_up_ref[...]))
        vr = vs + (vfirst_ref[...] - vs) * mix
    else:
        vfirst_ref[...] = vs
        vr = vs

    kx = ks * kk_ref[...]
    kmod = ks * (1.0 + (a - 1.0) * ka_ref[...])
    rkr = rs * kmod * rk_ref[...]
    rz = rz_ref[...]
    ln_g = lng_ref[...]
    ln_b = lnb_ref[...]

    head0_t = lane < HEAD_DIM

    def headsum(x):
        s0 = jnp.sum(jnp.where(head0_t, x, 0.0), axis=-1, keepdims=True)
        s1 = jnp.sum(jnp.where(head0_t, 0.0, x), axis=-1, keepdims=True)
        return jnp.where(head0_t, s0, s1)

    kk_all = []
    for p in range(N_PAIRS):
        kxp = kx[:, p * PAIR:(p + 1) * PAIR]
        kk_all.append(kxp / jnp.maximum(jnp.sqrt(headsum(kxp * kxp)), L2_EPS))
    kk = jnp.concatenate(kk_all, axis=1)

    o_pairs = _rwkv_scan(rs, kmod, vr, -kk, kk * a, lw, state_ref, ts)

    for p in range(N_PAIRS):
        sl = slice(p * PAIR, (p + 1) * PAIR)
        o_p = o_pairs[p]
        bonus = headsum(rkr[:, sl]) * vr[:, sl]
        mu = headsum(o_p) * (1.0 / HEAD_DIM)
        d = o_p - mu
        var = headsum(d * d) * (1.0 / HEAD_DIM)
        on = d * lax.rsqrt(var + GN_EPS) * ln_g[:, sl] + ln_b[:, sl]
        rz_p = rz[:, sl]
        yb_ref[:, sl] = (on + bonus) * (rz_p * _sigmoid(rz_p))


def _rwkv(proj, vfirst, p, *, ts=128):
    b, s, _ = proj.shape
    has_vmix = vfirst is not None
    row = lambda width, col: pl.BlockSpec((None, ts, width), lambda i, t: (i, t, col // width))
    const = lambda shape: pl.BlockSpec(shape, lambda i, t: (0,) * len(shape))
    act = pl.BlockSpec((None, ts, WIDTH), lambda i, t: (i, t, 0))

    in_specs = [row(WIDTH, COL_R), row(WIDTH, COL_RK), row(WIDTH, COL_RV), row(WIDTH, COL_RZ),
                row(LANES, COL_LORA)]
    args = [proj, proj, proj, proj, proj]
    if has_vmix:
        in_specs += [row(LANES, COL_VD), act]
        args += [proj, vfirst]
    in_specs += [const((3, WIDTH)), const((1, LANES)), const((1, 2 * WIDTH)), const((LANES, 2 * WIDTH))]
    args += [p["mu_rkv"], p["mu_lora"], p["w0a0"], p["lora_w"]]
    for name in ("k_k", "k_a", "r_k", "ln_g", "ln_b"):
        in_specs.append(const((1, WIDTH)))
        args.append(p[name])
    if has_vmix:
        in_specs += [const((1, LANES)), const((LANES, WIDTH)), const((1, WIDTH))]
        args += [p["vmix_mu"], p["vmix_up"], p["vmix0"]]

    out_shape = [jax.ShapeDtypeStruct((b, s, WIDTH), F32)]
    out_specs = [act]
    scratch = [pltpu.VMEM((8, WIDTH), F32), pltpu.VMEM((8, LANES), F32)]
    if has_vmix:
        scratch.append(pltpu.VMEM((8, LANES), F32))
    else:
        out_shape.append(jax.ShapeDtypeStruct((b, s, WIDTH), F32))
        out_specs.append(act)
    scratch.append(pltpu.VMEM((N_PAIRS, PAIR, PAIR), F32))

    res = pl.pallas_call(
        functools.partial(_rwkv_kernel, has_vmix=has_vmix, ts=ts),
        grid=(b, s // ts),
        in_specs=in_specs,
        out_specs=out_specs,
        out_shape=out_shape,
        scratch_shapes=scratch,
        compiler_params=pltpu.CompilerParams(
            dimension_semantics=("parallel", "arbitrary"), vmem_limit_bytes=VMEM_LIMIT),
        name="rwkv_vmix" if has_vmix else "rwkv",
    )(*args)
    return (res[0], vfirst) if has_vmix else (res[0], res[1])


def _outproj_kernel(ya_ref, az_ref, yb_ref, gatt_ref, grw_ref, x_ref,
                    wua_ref, wub_ref, wout_ref, g_ref, o_ref):
    az = az_ref[...]
    ya = ya_ref[...] * (az * _sigmoid(az))
    u = (_sigmoid(gatt_ref[...]) * _mm(ya, wua_ref[...])
         + _sigmoid(grw_ref[...]) * _mm(yb_ref[...], wub_ref[...]))
    y = _mm(u, wout_ref[...])
    ms = jnp.mean(y * y, axis=-1, keepdims=True)
    o_ref[...] = x_ref[...] + y * lax.rsqrt(ms + RMS_EPS) * g_ref[...]


def _outproj(ya, proj2d, yb, x2d, w_up_att, w_up_rw, w_out, g, *, tm=512):
    t, d = x2d.shape
    row = lambda width, col: pl.BlockSpec((tm, width), lambda i: (i, col // width))
    const = lambda shape: pl.BlockSpec(shape, lambda i: (0, 0))
    return pl.pallas_call(
        _outproj_kernel,
        grid=(t // tm,),
        in_specs=[row(WIDTH, 0), row(WIDTH, COL_AZ), row(WIDTH, 0), row(d, COL_GATT), row(d, COL_GRW),
                  row(d, 0), const((WIDTH, d)), const((WIDTH, d)), const((d, d)), const((1, d))],
        out_specs=row(d, 0),
        out_shape=jax.ShapeDtypeStruct((t, d), F32),
        compiler_params=pltpu.CompilerParams(
            dimension_semantics=("parallel",), vmem_limit_bytes=VMEM_LIMIT),
        name="outproj",
    )(ya, proj2d, yb, proj2d, proj2d, x2d, w_up_att, w_up_rw, w_out, g.reshape(1, d))


def _layer_params(l, w_in, rw_mu, rw_w0, rw_w_up, rw_a0, rw_a_up, rw_k_k, rw_k_a, rw_r_k,
                  rw_ln_g, rw_ln_b, rw_vmix_down, rw_vmix_mu, rw_vmix_up, rw_vmix0):
    w = w_in[l]
    d = w.shape[0]
    att = w[:, :4 * WIDTH]
    o = 4 * WIDTH
    rkv = w[:, o:o + 3 * WIDTH]
    lora = w[:, o + 3 * WIDTH:o + 3 * WIDTH + DECAY_LORA + ICLR_LORA]
    o = o + 3 * WIDTH + DECAY_LORA + ICLR_LORA
    rz = w[:, o:o + WIDTH]
    gates = w[:, o + WIDTH:]
    vd = jnp.zeros((d, LANES), F32)
    if l > 0:
        vd = vd.at[:, :VMIX_LORA].set(rw_vmix_down[l - 1])
    w_cat = jnp.concatenate([att, rkv, rz, gates, lora, vd], axis=1).astype(BF16)

    mu = rw_mu[l]
    zeros = jnp.zeros((DECAY_LORA, WIDTH), F32)
    lora_w = jnp.concatenate([jnp.concatenate([rw_w_up[l], zeros], axis=1),
                              jnp.concatenate([zeros, rw_a_up[l]], axis=1)], axis=0).astype(BF16)
    p = {
        "w_cat": w_cat,
        "mu_rkv": mu[:3 * WIDTH].reshape(3, WIDTH),
        "mu_lora": mu[3 * WIDTH:].reshape(1, LANES),
        "w0a0": jnp.concatenate([rw_w0[l], rw_a0[l]]).reshape(1, 2 * WIDTH),
        "lora_w": lora_w,
        "k_k": rw_k_k[l].reshape(1, WIDTH), "k_a": rw_k_a[l].reshape(1, WIDTH),
        "r_k": rw_r_k[l].reshape(1, WIDTH),
        "ln_g": rw_ln_g[l].reshape(1, WIDTH), "ln_b": rw_ln_b[l].reshape(1, WIDTH),
    }
    if l > 0:
        p["vmix_mu"] = jnp.zeros((1, LANES), F32).at[0, :VMIX_LORA].set(rw_vmix_mu[l - 1])
        p["vmix_up"] = jnp.zeros((LANES, WIDTH), F32).at[:VMIX_LORA].set(rw_vmix_up[l - 1]).astype(BF16)
        p["vmix0"] = rw_vmix0[l - 1].reshape(1, WIDTH)
    return p


def kernel(x, norm_pre, norm_post, w_in, rw_mu, rw_w0, rw_w_up, rw_a0, rw_a_up, rw_k_k, rw_k_a, rw_r_k, rw_ln_g, rw_ln_b, rw_vmix_down, rw_vmix_mu, rw_vmix_up, rw_vmix0, w_up_att, w_up_rw, w_out):
    b, s, d = x.shape
    assert d == D_MODEL and s % MOBA_BLOCK == 0
    depth = w_in.shape[0]
    x2d = x.reshape(b * s, d)
    vfirst = None
    for l in range(depth):
        p = _layer_params(l, w_in, rw_mu, rw_w0, rw_w_up, rw_a0, rw_a_up, rw_k_k, rw_k_a, rw_r_k,
                          rw_ln_g, rw_ln_b, rw_vmix_down, rw_vmix_mu, rw_vmix_up, rw_vmix0)
        proj2d = _inproj(x2d, norm_pre[l], p["w_cat"])
        proj = proj2d.reshape(b, s, PROJ_W)
        ya = _moba(proj)
        yb, vfirst = _rwkv(proj, vfirst, p)
        x2d = _outproj(ya.reshape(b * s, WIDTH), proj2d, yb.reshape(b * s, WIDTH), x2d,
                       w_up_att[l].astype(BF16), w_up_rw[l].astype(BF16), w_out[l].astype(BF16),
                       norm_post[l])
    return x2d.reshape(b, s, d)
```

```python
import functools

import jax
import jax.numpy as jnp
from jax import lax
from jax.experimental import pallas as pl
from jax.experimental.pallas import tpu as pltpu

F32 = jnp.float32
BF16 = jnp.bfloat16

D_MODEL = 1024
N_HEADS = 8
HEAD_DIM = 64
WIDTH = N_HEADS * HEAD_DIM
MOBA_BLOCK = 256
MOBA_TOPK = 3
DECAY_LORA = 64
ICLR_LORA = 64
VMIX_LORA = 32
RMS_EPS = 1e-6
GN_EPS = 64e-5
L2_EPS = 1e-12
NEG_INF = -1e30

LANES = 128
PAIR = 2 * HEAD_DIM
N_PAIRS = WIDTH // PAIR
CHUNK = 64

COL_Q, COL_K, COL_V, COL_AZ = 0, 512, 1024, 1536
COL_R, COL_RK, COL_RV, COL_RZ = 2048, 2560, 3072, 3584
COL_GATT, COL_GRW = 4096, 5120
COL_LORA = 6144
COL_VD = 6272
PROJ_W = 6400

VMEM_LIMIT = 48 * 1024 * 1024


def _sigmoid(x):
    return 1.0 / (1.0 + jnp.exp(-x))


def _mm(a, b):
    return jnp.dot(a.astype(BF16), b.astype(BF16), preferred_element_type=F32)


def _mm_nt(a, b):
    return lax.dot_general(a.astype(BF16), b.astype(BF16), (((1,), (1,)), ((), ())),
                           preferred_element_type=F32)


def _mm_tn(a, b):
    return lax.dot_general(a.astype(BF16), b.astype(BF16), (((0,), (0,)), ((), ())),
                           preferred_element_type=F32)


def _inproj_kernel(x_ref, g_ref, w_ref, o_ref, h_ref):
    @pl.when(pl.program_id(1) == 0)
    def _():
        x = x_ref[...]
        ms = jnp.mean(x * x, axis=-1, keepdims=True)
        h_ref[...] = (x * lax.rsqrt(ms + RMS_EPS) * g_ref[...]).astype(BF16)

    o_ref[...] = jnp.dot(h_ref[...], w_ref[...], preferred_element_type=F32)


def _inproj(x2d, g, w_cat, *, tm=1024, tn=1280):
    t, d = x2d.shape
    n = w_cat.shape[1]
    return pl.pallas_call(
        _inproj_kernel,
        grid=(t // tm, n // tn),
        in_specs=[
            pl.BlockSpec((tm, d), lambda i, j: (i, 0)),
            pl.BlockSpec((1, d), lambda i, j: (0, 0)),
            pl.BlockSpec((d, tn), lambda i, j: (0, j)),
        ],
        out_specs=pl.BlockSpec((tm, tn), lambda i, j: (i, j)),
        out_shape=jax.ShapeDtypeStruct((t, n), F32),
        scratch_shapes=[pltpu.VMEM((tm, d), BF16)],
        compiler_params=pltpu.CompilerParams(
            dimension_semantics=("parallel", "arbitrary"), vmem_limit_bytes=VMEM_LIMIT),
        name="inproj",
    )(x2d, g.reshape(1, d), w_cat)


def _moba_kernel(q_ref, k_ref, v_ref, o_ref, kb_ref, vt_ref, bias_ref, s_ref, p_ref, *, n_blocks):
    hp = pl.program_id(1)
    blk = MOBA_BLOCK
    s_len = n_blocks * blk
    log2e = 1.4426950408889634
    q_scale = (HEAD_DIM ** -0.5) * log2e

    kb_ref[...] = k_ref[...].astype(BF16)
    vt_ref[...] = v_ref[...].T.astype(BF16)

    mean_row = lax.broadcasted_iota(jnp.int32, (n_blocks, PAIR), 0)
    kmean = jnp.zeros((n_blocks, PAIR), F32)
    for n in range(n_blocks):
        kmean = jnp.where(mean_row == n,
                          jnp.mean(k_ref[n * blk:(n + 1) * blk, :], axis=0, keepdims=True), kmean)

    lane = lax.broadcasted_iota(jnp.int32, (1, PAIR), 1)
    head_lanes = (lane < HEAD_DIM, lane >= HEAD_DIM)
    nrow = lax.broadcasted_iota(jnp.int32, (n_blocks, s_len), 0)
    qblk = lax.broadcasted_iota(jnp.int32, (n_blocks, s_len), 1) // blk
    past = nrow < qblk
    krow = lax.broadcasted_iota(jnp.int32, (blk, blk), 0)
    qcol = lax.broadcasted_iota(jnp.int32, (blk, blk), 1)
    dist0 = (qcol - krow).astype(F32)
    causal = qcol >= krow

    alibi0 = []
    for h in range(2):
        head = lax.convert_element_type(hp * 2 + h + 1, F32)
        slope2 = jnp.exp2(jnp.zeros((1, 1), F32) - (8.0 / N_HEADS) * head) * log2e
        alibi0.append(slope2 * dist0)

        gate = lax.dot_general(jnp.where(head_lanes[h], kmean, 0.0), q_ref[...], (((1,), (1,)), ((), ())),
                               precision=lax.Precision.HIGHEST, preferred_element_type=F32)
        gm = jnp.where(past, gate, NEG_INF)
        rank = jnp.zeros((n_blocks, s_len), F32)
        for m in range(n_blocks):
            gcol = gm[m:m + 1, :]
            beats = (gcol > gm) | ((gcol == gm) & (nrow > m))
            rank = rank + jnp.where(beats, 1.0, 0.0)
        sel = past & (rank < MOBA_TOPK)
        bias_ref[h] = jnp.where(sel, -slope2 * ((qblk - nrow) * blk).astype(F32), NEG_INF)

    for i in range(n_blocks):
        rows = slice(i * blk, (i + 1) * blk)
        n_k = (i + 1) * blk
        q_i = q_ref[rows, :] * q_scale
        acc = []
        for h in range(2):
            qm = jnp.where(head_lanes[h], q_i, 0.0).astype(BF16)
            m_run = None
            for j in range(i + 1):
                st = lax.dot_general(kb_ref[j * blk:(j + 1) * blk, :], qm, (((1,), (1,)), ((), ())),
                                     preferred_element_type=F32)
                if j == i:
                    st = jnp.where(causal, st - alibi0[h], NEG_INF)
                else:
                    st = st - alibi0[h] + bias_ref[h, j:j + 1, rows]
                s_ref[h, j * blk:(j + 1) * blk, :] = st
                m_j = jnp.max(st, axis=0, keepdims=True)
                m_run = m_j if m_run is None else jnp.maximum(m_run, m_j)
            p = jnp.exp2(s_ref[h, :n_k, :] - m_run)
            l_sum = jnp.sum(p, axis=0, keepdims=True)
            p_ref[h, :n_k, :] = p.astype(BF16)
            pv = jnp.dot(vt_ref[:, :n_k], p_ref[h, :n_k, :], preferred_element_type=F32)
            acc.append(pv * (1.0 / l_sum))
        out_t = jnp.concatenate([acc[0][:HEAD_DIM], acc[1][HEAD_DIM:]], axis=0)
        o_ref[rows, :] = out_t.T


def _moba(proj):
    b, s, _ = proj.shape
    n_blocks = s // MOBA_BLOCK
    qo, ko, vo = COL_Q // PAIR, COL_K // PAIR, COL_V // PAIR
    return pl.pallas_call(
        functools.partial(_moba_kernel, n_blocks=n_blocks),
        grid=(b, N_PAIRS),
        in_specs=[
            pl.BlockSpec((None, s, PAIR), lambda i, p: (i, 0, qo + p)),
            pl.BlockSpec((None, s, PAIR), lambda i, p: (i, 0, ko + p)),
            pl.BlockSpec((None, s, PAIR), lambda i, p: (i, 0, vo + p)),
        ],
        out_specs=pl.BlockSpec((None, s, PAIR), lambda i, p: (i, 0, p)),
        out_shape=jax.ShapeDtypeStruct((b, s, WIDTH), F32),
        scratch_shapes=[
            pltpu.VMEM((s, PAIR), BF16),
            pltpu.VMEM((PAIR, s), BF16),
            pltpu.VMEM((2, n_blocks, s), F32),
            pltpu.VMEM((2, s, MOBA_BLOCK), F32),
            pltpu.VMEM((2, s, MOBA_BLOCK), BF16),
        ],
        compiler_params=pltpu.CompilerParams(
            dimension_semantics=("parallel", "parallel"), vmem_limit_bytes=VMEM_LIMIT),
        name="moba",
    )(proj, proj, proj)


def _rwkv_scan(rs, kmod, vr, na, nb, lw, state_ref, ts):
    n_chunks = ts // CHUNK
    n2 = 2 * CHUNK
    trow = lax.broadcasted_iota(jnp.int32, (ts, ts), 0)
    tcol = lax.broadcasted_iota(jnp.int32, (ts, ts), 1)
    ltri = jnp.where((trow // CHUNK == tcol // CHUNK) & (tcol <= trow), 1.0, 0.0).astype(F32)
    head0 = lax.broadcasted_iota(jnp.int32, (CHUNK, PAIR), 1) < HEAD_DIM
    srow = lax.broadcasted_iota(jnp.int32, (n2, n2), 0)
    scol = lax.broadcasted_iota(jnp.int32, (n2, n2), 1)
    same_head = (srow // CHUNK) == (scol // CHUNK)
    strict = same_head & (scol < srow)
    incl = same_head & (scol <= srow)
    eye = jnp.where(srow == scol, 1.0, 0.0).astype(F32)

    def stack(x):
        return jnp.concatenate([jnp.where(head0, x, 0.0), jnp.where(head0, 0.0, x)], axis=0)

    def dup(x):
        return jnp.concatenate([x, x], axis=0)

    cum = jnp.dot(ltri, lw, precision=lax.Precision.HIGHEST, preferred_element_type=F32)
    a_t = na * jnp.exp(cum - lw)
    r_t = rs * jnp.exp(cum)
    e_neg = jnp.exp(-cum)
    b_t = nb * e_neg
    k_t = kmod * e_neg

    pcs = [(p, c) for c in range(n_chunks) for p in range(N_PAIRS)]
    lhs_a, lhs_r, rhs, b_h, k_h, v_s, gam = {}, {}, {}, {}, {}, {}, {}
    for c in range(n_chunks):
        rc = slice(c * CHUNK, (c + 1) * CHUNK)
        tot = cum[(c + 1) * CHUNK - 1:(c + 1) * CHUNK, :]
        e_rel = jnp.exp(tot - cum[rc])
        bh_c = nb[rc] * e_rel
        kh_c = kmod[rc] * e_rel
        gam_c = jnp.exp(tot)
        for p in range(N_PAIRS):
            sl = slice(p * PAIR, (p + 1) * PAIR)
            pc = (p, c)
            lhs_a[pc] = stack(a_t[rc, sl]).astype(BF16)
            lhs_r[pc] = stack(r_t[rc, sl])
            rhs[pc] = jnp.concatenate([dup(b_t[rc, sl]), dup(k_t[rc, sl])], axis=0).astype(BF16)
            b_h[pc] = stack(bh_c[:, sl]).astype(BF16)
            k_h[pc] = stack(kh_c[:, sl])
            v_s[pc] = stack(vr[rc, sl]).astype(BF16)
            gam[pc] = gam_c[:, sl]

    a_ab, a_ak, a_rb, a_rk = {}, {}, {}, {}
    for pc in pcs:
        g = _mm_nt(jnp.concatenate([lhs_a[pc], lhs_r[pc].astype(BF16)], axis=0), rhs[pc])
        a_ab[pc] = jnp.where(strict, g[:n2, :n2], 0.0)
        a_ak[pc] = jnp.where(strict, g[:n2, n2:], 0.0).astype(BF16)
        a_rb[pc] = jnp.where(incl, g[n2:, :n2], 0.0).astype(BF16)
        a_rk[pc] = jnp.where(incl, g[n2:, n2:], 0.0)

    t_inv = {pc: eye + a_ab[pc] for pc in pcs}
    pw = {pc: a_ab[pc].astype(BF16) for pc in pcs}
    pw = {pc: _mm(pw[pc], pw[pc]).astype(BF16) for pc in pcs}
    for _ in range(CHUNK.bit_length() - 3):
        both = {pc: _mm(jnp.concatenate([pw[pc], t_inv[pc].astype(BF16)], axis=0), pw[pc]) for pc in pcs}
        t_inv = {pc: t_inv[pc] + both[pc][n2:] for pc in pcs}
        pw = {pc: both[pc][:n2].astype(BF16) for pc in pcs}
    t_inv = {pc: (t_inv[pc] + _mm(t_inv[pc], pw[pc])).astype(BF16) for pc in pcs}

    z = {pc: _mm_tn(t_inv[pc], b_h[pc]).astype(BF16) for pc in pcs}
    w = {pc: _mm(a_rb[pc], t_inv[pc]).astype(BF16) for pc in pcs}
    la_ak = {pc: jnp.concatenate([lhs_a[pc], a_ak[pc]], axis=1) for pc in pcs}
    my = {pc: _mm_tn(la_ak[pc], z[pc]) for pc in pcs}
    wq = {pc: _mm(w[pc], la_ak[pc]) for pc in pcs}
    m_t = {pc: my[pc][:n2].astype(BF16) for pc in pcs}
    y = {pc: (my[pc][n2:] + k_h[pc]).astype(BF16) for pc in pcs}
    q_p = {pc: (lhs_r[pc] + wq[pc][:, :n2]).astype(BF16) for pc in pcs}
    p_p = {pc: (wq[pc][:, n2:] + a_rk[pc]).astype(BF16) for pc in pcs}
    g_s = {pc: _mm_tn(v_s[pc], y[pc]) for pc in pcs}
    o_loc = {pc: _mm(p_p[pc], v_s[pc]) for pc in pcs}

    states = [state_ref[p] for p in range(N_PAIRS)]
    outs = [[] for _ in range(N_PAIRS)]
    for c in range(n_chunks):
        for p in range(N_PAIRS):
            pc = (p, c)
            sb = states[p].astype(BF16)
            os_ = _mm_nt(q_p[pc], sb) + o_loc[pc]
            outs[p].append(os_[:CHUNK] + os_[CHUNK:])
            states[p] = states[p] * gam[pc] + _mm(sb, m_t[pc]) + g_s[pc]
    for p in range(N_PAIRS):
        state_ref[p] = states[p]
    return [jnp.concatenate(o, axis=0) if len(o) > 1 else o[0] for o in outs]


def _rwkv_kernel(*refs, has_vmix, ts):
    if has_vmix:
        (r_ref, k_ref, v_ref, rz_ref, lora_ref, vd_ref, vfirst_ref,
         mu_rkv_ref, mu_lora_ref, w0a0_ref, lora_w_ref, kk_ref, ka_ref, rk_ref, lng_ref, lnb_ref,
         vmix_mu_ref, vmix_up_ref, vmix0_ref,
         yb_ref, carry_rkv, carry_lora, carry_vd, state_ref) = refs
    else:
        (r_ref, k_ref, v_ref, rz_ref, lora_ref,
         mu_rkv_ref, mu_lora_ref, w0a0_ref, lora_w_ref, kk_ref, ka_ref, rk_ref, lng_ref, lnb_ref,
         yb_ref, vfirst_ref, carry_rkv, carry_lora, state_ref) = refs
        carry_vd = None

    @pl.when(pl.program_id(1) == 0)
    def _():
        carry_rkv[...] = jnp.zeros_like(carry_rkv)
        carry_lora[...] = jnp.zeros_like(carry_lora)
        if carry_vd is not None:
            carry_vd[...] = jnp.zeros_like(carry_vd)
        state_ref[...] = jnp.zeros_like(state_ref)

    def shift(y, carry_ref, row_idx, mu):
        first = lax.broadcasted_iota(jnp.int32, y.shape, 0) == 0
        prev = jnp.where(first, carry_ref[row_idx:row_idx + 1, :], pltpu.roll(y, 1, 0))
        carry_ref[row_idx:row_idx + 1, :] = y[ts - 1:ts, :]
        return y + (prev - y) * mu

    mu_rkv = mu_rkv_ref[...]
    rs = shift(r_ref[...], carry_rkv, 0, mu_rkv[0:1, :])
    ks = shift(k_ref[...], carry_rkv, 1, mu_rkv[1:2, :])
    vs = shift(v_ref[...], carry_rkv, 2, mu_rkv[2:3, :])
    lo = shift(lora_ref[...], carry_lora, 0, mu_lora_ref[...])

    lane = lax.broadcasted_iota(jnp.int32, (ts, LANES), 1)
    z = jnp.where(lane < DECAY_LORA, jnp.tanh(lo), lo)
    wa = w0a0_ref[...] + _mm(z, lora_w_ref[...])
    zz = -wa[:, :WIDTH]
    w_log = -(jnp.maximum(zz, 0.0) + jnp.log(1.0 + jnp.exp(-jnp.abs(zz)))) - 0.5
    lw = -jnp.exp(w_log)
    a = _sigmoid(wa[:, WIDTH:])

    if has_vmix:
        vd = shift(vd_ref[...], carry_vd, 0, vmix_mu_ref[...])
        mix = _sigmoid(vmix0_ref[...] + _mm(vd, vmix_up_ref[...]))
        vr = vs + (vfirst_ref[...] - vs) * mix
    else:
        vfirst_ref[...] = vs
        vr = vs

    kx = ks * kk_ref[...]
    kmod = ks * (1.0 + (a - 1.0) * ka_ref[...])
    rkr = rs * kmod * rk_ref[...]
    rz = rz_ref[...]
    ln_g = lng_ref[...]
    ln_b = lnb_ref[...]

    head0_t = lane < HEAD_DIM

    def headsum(x):
        s0 = jnp.sum(jnp.where(head0_t, x, 0.0), axis=-1, keepdims=True)
        s1 = jnp.sum(jnp.where(head0_t, 0.0, x), axis=-1, keepdims=True)
        return jnp.where(head0_t, s0, s1)

    kk_all = []
    for p in range(N_PAIRS):
        kxp = kx[:, p * PAIR:(p + 1) * PAIR]
        kk_all.append(kxp * jnp.minimum(lax.rsqrt(headsum(kxp * kxp)), 1.0 / L2_EPS))
    kk = jnp.concatenate(kk_all, axis=1)

    o_pairs = _rwkv_scan(rs, kmod, vr, -kk, kk * a, lw, state_ref, ts)

    for p in range(N_PAIRS):
        sl = slice(p * PAIR, (p + 1) * PAIR)
        o_p = o_pairs[p]
        bonus = headsum(rkr[:, sl]) * vr[:, sl]
        mu = headsum(o_p) * (1.0 / HEAD_DIM)
        d = o_p - mu
        var = headsum(d * d) * (1.0 / HEAD_DIM)
        on = d * lax.rsqrt(var + GN_EPS) * ln_g[:, sl] + ln_b[:, sl]
        rz_p = rz[:, sl]
        yb_ref[:, sl] = (on + bonus) * (rz_p * _sigmoid(rz_p))


def _rwkv(proj, vfirst, p, *, ts=256):
    b, s, _ = proj.shape
    has_vmix = vfirst is not None
    row = lambda width, col: pl.BlockSpec((None, ts, width), lambda i, t: (i, t, col // width))
    const = lambda shape: pl.BlockSpec(shape, lambda i, t: (0,) * len(shape))
    act = pl.BlockSpec((None, ts, WIDTH), lambda i, t: (i, t, 0))

    in_specs = [row(WIDTH, COL_R), row(WIDTH, COL_RK), row(WIDTH, COL_RV), row(WIDTH, COL_RZ),
                row(LANES, COL_LORA)]
    args = [proj, proj, proj, proj, proj]
    if has_vmix:
        in_specs += [row(LANES, COL_VD), act]
        args += [proj, vfirst]
    in_specs += [const((3, WIDTH)), const((1, LANES)), const((1, 2 * WIDTH)), const((LANES, 2 * WIDTH))]
    args += [p["mu_rkv"], p["mu_lora"], p["w0a0"], p["lora_w"]]
    for name in ("k_k", "k_a", "r_k", "ln_g", "ln_b"):
        in_specs.append(const((1, WIDTH)))
        args.append(p[name])
    if has_vmix:
        in_specs += [const((1, LANES)), const((LANES, WIDTH)), const((1, WIDTH))]
        args += [p["vmix_mu"], p["vmix_up"], p["vmix0"]]

    out_shape = [jax.ShapeDtypeStruct((b, s, WIDTH), F32)]
    out_specs = [act]
    scratch = [pltpu.VMEM((8, WIDTH), F32), pltpu.VMEM((8, LANES), F32)]
    if has_vmix:
        scratch.append(pltpu.VMEM((8, LANES), F32))
    else:
        out_shape.append(jax.ShapeDtypeStruct((b, s, WIDTH), F32))
        out_specs.append(act)
    scratch.append(pltpu.VMEM((N_PAIRS, PAIR, PAIR), F32))

    res = pl.pallas_call(
        functools.partial(_rwkv_kernel, has_vmix=has_vmix, ts=ts),
        grid=(b, s // ts),
        in_specs=in_specs,
        out_specs=out_specs,
        out_shape=out_shape,
        scratch_shapes=scratch,
        compiler_params=pltpu.CompilerParams(
            dimension_semantics=("parallel", "arbitrary"), vmem_limit_bytes=VMEM_LIMIT),
        name="rwkv_vmix" if has_vmix else "rwkv",
    )(*args)
    return (res[0], vfirst) if has_vmix else (res[0], res[1])


def _outproj_kernel(ya_ref, az_ref, yb_ref, gatt_ref, grw_ref, x_ref,
                    wua_ref, wub_ref, wout_ref, g_ref, o_ref):
    az = az_ref[...]
    ya = ya_ref[...] * (az * _sigmoid(az))
    u = (_sigmoid(gatt_ref[...]) * _mm(ya, wua_ref[...])
         + _sigmoid(grw_ref[...]) * _mm(yb_ref[...], wub_ref[...]))
    y = _mm(u, wout_ref[...])
    ms = jnp.mean(y * y, axis=-1, keepdims=True)
    o_ref[...] = x_ref[...] + y * lax.rsqrt(ms + RMS_EPS) * g_ref[...]


def _outproj(ya, proj2d, yb, x2d, w_up_att, w_up_rw, w_out, g, *, tm=512):
    t, d = x2d.shape
    row = lambda width, col: pl.BlockSpec((tm, width), lambda i: (i, col // width))
    const = lambda shape: pl.BlockSpec(shape, lambda i: (0, 0))
    return pl.pallas_call(
        _outproj_kernel,
        grid=(t // tm,),
        in_specs=[row(WIDTH, 0), row(WIDTH, COL_AZ), row(WIDTH, 0), row(d, COL_GATT), row(d, COL_GRW),
                  row(d, 0), const((WIDTH, d)), const((WIDTH, d)), const((d, d)), const((1, d))],
        out_specs=row(d, 0),
        out_shape=jax.ShapeDtypeStruct((t, d), F32),
        compiler_params=pltpu.CompilerParams(
            dimension_semantics=("parallel",), vmem_limit_bytes=VMEM_LIMIT),
        name="outproj",
    )(ya, proj2d, yb, proj2d, proj2d, x2d, w_up_att, w_up_rw, w_out, g.reshape(1, d))


def _layer_params(l, w_in, rw_mu, rw_w0, rw_w_up, rw_a0, rw_a_up, rw_k_k, rw_k_a, rw_r_k,
                  rw_ln_g, rw_ln_b, rw_vmix_down, rw_vmix_mu, rw_vmix_up, rw_vmix0):
    w = w_in[l]
    d = w.shape[0]
    att = w[:, :4 * WIDTH]
    o = 4 * WIDTH
    rkv = w[:, o:o + 3 * WIDTH]
    lora = w[:, o + 3 * WIDTH:o + 3 * WIDTH + DECAY_LORA + ICLR_LORA]
    o = o + 3 * WIDTH + DECAY_LORA + ICLR_LORA
    rz = w[:, o:o + WIDTH]
    gates = w[:, o + WIDTH:]
    vd = jnp.zeros((d, LANES), F32)
    if l > 0:
        vd = vd.at[:, :VMIX_LORA].set(rw_vmix_down[l - 1])
    w_cat = jnp.concatenate([att, rkv, rz, gates, lora, vd], axis=1).astype(BF16)

    mu = rw_mu[l]
    zeros = jnp.zeros((DECAY_LORA, WIDTH), F32)
    lora_w = jnp.concatenate([jnp.concatenate([rw_w_up[l], zeros], axis=1),
                              jnp.concatenate([zeros, rw_a_up[l]], axis=1)], axis=0).astype(BF16)
    p = {
        "w_cat": w_cat,
        "mu_rkv": mu[:3 * WIDTH].reshape(3, WIDTH),
        "mu_lora": mu[3 * WIDTH:].reshape(1, LANES),
        "w0a0": jnp.concatenate([rw_w0[l], rw_a0[l]]).reshape(1, 2 * WIDTH),
        "lora_w": lora_w,
        "k_k": rw_k_k[l].reshape(1, WIDTH), "k_a": rw_k_a[l].reshape(1, WIDTH),
        "r_k": rw_r_k[l].reshape(1, WIDTH),
        "ln_g": rw_ln_g[l].reshape(1, WIDTH), "ln_b": rw_ln_b[l].reshape(1, WIDTH),
    }
    if l > 0:
        p["vmix_mu"] = jnp.zeros((1, LANES), F32).at[0, :VMIX_LORA].set(rw_vmix_mu[l - 1])
        p["vmix_up"] = jnp.zeros((LANES, WIDTH), F32).at[:VMIX_LORA].set(rw_vmix_up[l - 1]).astype(BF16)
        p["vmix0"] = rw_vmix0[l - 1].reshape(1, WIDTH)
    return p


def kernel(x, norm_pre, norm_post, w_in, rw_mu, rw_w0, rw_w_up, rw_a0, rw_a_up, rw_k_k, rw_k_a, rw_r_k, rw_ln_g, rw_ln_b, rw_vmix_down, rw_vmix_mu, rw_vmix_up, rw_vmix0, w_up_att, w_up_rw, w_out):
    b, s, d = x.shape
    assert d == D_MODEL and s % MOBA_BLOCK == 0
    depth = w_in.shape[0]
    x2d = x.reshape(b * s, d)
    vfirst = None
    for l in range(depth):
        p = _layer_params(l, w_in, rw_mu, rw_w0, rw_w_up, rw_a0, rw_a_up, rw_k_k, rw_k_a, rw_r_k,
                          rw_ln_g, rw_ln_b, rw_vmix_down, rw_vmix_mu, rw_vmix_up, rw_vmix0)
        proj2d = _inproj(x2d, norm_pre[l], p["w_cat"])
        proj = proj2d.reshape(b, s, PROJ_W)
        ya = _moba(proj)
        yb, vfirst = _rwkv(proj, vfirst, p)
        x2d = _outproj(ya.reshape(b * s, WIDTH), proj2d, yb.reshape(b * s, WIDTH), x2d,
                       w_up_att[l].astype(BF16), w_up_rw[l].astype(BF16), w_out[l].astype(BF16),
                       norm_post[l])
    return x2d.reshape(b, s, d)
```

```python
import functools

import jax
import jax.numpy as jnp
from jax import lax
from jax.experimental import pallas as pl
from jax.experimental.pallas import tpu as pltpu

F32 = jnp.float32
BF16 = jnp.bfloat16
ACT = BF16

D_MODEL = 1024
N_HEADS = 8
HEAD_DIM = 64
WIDTH = N_HEADS * HEAD_DIM
MOBA_BLOCK = 256
MOBA_TOPK = 3
DECAY_LORA = 64
ICLR_LORA = 64
VMIX_LORA = 32
RMS_EPS = 1e-6
GN_EPS = 64e-5
L2_EPS = 1e-12
NEG_INF = -1e30

LANES = 128
PAIR = 2 * HEAD_DIM
N_PAIRS = WIDTH // PAIR
CHUNK = 64

COL_Q, COL_K, COL_V, COL_AZ = 0, 512, 1024, 1536
COL_R, COL_RK, COL_RV, COL_RZ = 2048, 2560, 3072, 3584
COL_GATT, COL_GRW = 4096, 5120
COL_LORA = 6144
COL_VD = 6272
PROJ_W = 6400

VMEM_LIMIT = 48 * 1024 * 1024


def _sigmoid(x):
    return 1.0 / (1.0 + jnp.exp(-x))


def _mm(a, b):
    return jnp.dot(a.astype(BF16), b.astype(BF16), preferred_element_type=F32)


def _mm_nt(a, b):
    return lax.dot_general(a.astype(BF16), b.astype(BF16), (((1,), (1,)), ((), ())),
                           preferred_element_type=F32)


def _mm_tn(a, b):
    return lax.dot_general(a.astype(BF16), b.astype(BF16), (((0,), (0,)), ((), ())),
                           preferred_element_type=F32)


def _inproj_kernel(x_ref, g_ref, w_ref, o_ref, h_ref):
    @pl.when(pl.program_id(1) == 0)
    def _():
        x = x_ref[...]
        ms = jnp.mean(x * x, axis=-1, keepdims=True)
        h_ref[...] = (x * lax.rsqrt(ms + RMS_EPS) * g_ref[...]).astype(BF16)

    o_ref[...] = jnp.dot(h_ref[...], w_ref[...], preferred_element_type=F32).astype(o_ref.dtype)


def _inproj(x2d, g, w_cat, *, tm=1024, tn=1280):
    t, d = x2d.shape
    n = w_cat.shape[1]
    return pl.pallas_call(
        _inproj_kernel,
        grid=(t // tm, n // tn),
        in_specs=[
            pl.BlockSpec((tm, d), lambda i, j: (i, 0)),
            pl.BlockSpec((1, d), lambda i, j: (0, 0)),
            pl.BlockSpec((d, tn), lambda i, j: (0, j)),
        ],
        out_specs=pl.BlockSpec((tm, tn), lambda i, j: (i, j)),
        out_shape=jax.ShapeDtypeStruct((t, n), ACT),
        scratch_shapes=[pltpu.VMEM((tm, d), BF16)],
        compiler_params=pltpu.CompilerParams(
            dimension_semantics=("parallel", "arbitrary"), vmem_limit_bytes=VMEM_LIMIT),
        name="inproj",
    )(x2d, g.reshape(1, d), w_cat)


def _moba_kernel(q_ref, k_ref, v_ref, o_ref, vt_ref, bias_ref, s_ref, p_ref, *, n_blocks):
    hp = pl.program_id(1)
    blk = MOBA_BLOCK
    s_len = n_blocks * blk
    log2e = 1.4426950408889634
    q_scale = (HEAD_DIM ** -0.5) * log2e

    vt_ref[...] = v_ref[...].T

    mean_row = lax.broadcasted_iota(jnp.int32, (n_blocks, PAIR), 0)
    kmean = jnp.zeros((n_blocks, PAIR), F32)
    for n in range(n_blocks):
        kmean = jnp.where(mean_row == n,
                          jnp.mean(k_ref[n * blk:(n + 1) * blk, :].astype(F32), axis=0, keepdims=True), kmean)

    lane = lax.broadcasted_iota(jnp.int32, (1, PAIR), 1)
    head_lanes = (lane < HEAD_DIM, lane >= HEAD_DIM)
    nrow = lax.broadcasted_iota(jnp.int32, (n_blocks, s_len), 0)
    qblk = lax.broadcasted_iota(jnp.int32, (n_blocks, s_len), 1) // blk
    past = nrow < qblk
    krow = lax.broadcasted_iota(jnp.int32, (blk, blk), 0)
    qcol = lax.broadcasted_iota(jnp.int32, (blk, blk), 1)
    dist0 = (qcol - krow).astype(F32)
    causal = qcol >= krow

    km = jnp.concatenate([jnp.where(head_lanes[0], kmean, 0.0), jnp.where(head_lanes[1], kmean, 0.0)], axis=0)
    km_hi = km.astype(BF16)
    km_r = km - km_hi.astype(F32)
    km_mid = km_r.astype(BF16)
    km_lo = (km_r - km_mid.astype(F32)).astype(BF16)
    nt = (((1,), (1,)), ((), ()))
    q_all = q_ref[...]
    gates = (lax.dot_general(km_lo, q_all, nt, preferred_element_type=F32)
             + lax.dot_general(km_mid, q_all, nt, preferred_element_type=F32)
             + lax.dot_general(km_hi, q_all, nt, preferred_element_type=F32))

    alibi0 = []
    for h in range(2):
        head = lax.convert_element_type(hp * 2 + h + 1, F32)
        slope2 = jnp.exp2(jnp.zeros((1, 1), F32) - (8.0 / N_HEADS) * head) * log2e
        alibi0.append(slope2 * dist0)

        gm = jnp.where(past, gates[h * n_blocks:(h + 1) * n_blocks], NEG_INF)
        rank = jnp.zeros((n_blocks, s_len), F32)
        for m in range(n_blocks):
            gcol = gm[m:m + 1, :]
            beats = (gcol > gm) | ((gcol == gm) & (nrow > m))
            rank = rank + jnp.where(beats, 1.0, 0.0)
        sel = past & (rank < MOBA_TOPK)
        bias_ref[h] = jnp.where(sel, -slope2 * ((qblk - nrow) * blk).astype(F32), NEG_INF)

    for i in range(n_blocks):
        rows = slice(i * blk, (i + 1) * blk)
        n_k = (i + 1) * blk
        q_i = q_ref[rows, :].astype(F32) * q_scale
        acc = []
        for h in range(2):
            qm = jnp.where(head_lanes[h], q_i, 0.0).astype(BF16)
            m_run = None
            for j in range(i + 1):
                st = lax.dot_general(k_ref[j * blk:(j + 1) * blk, :], qm, nt,
                                     preferred_element_type=F32)
                if j == i:
                    st = jnp.where(causal, st - alibi0[h], NEG_INF)
                else:
                    st = st - alibi0[h] + bias_ref[h, j:j + 1, rows]
                s_ref[h, j * blk:(j + 1) * blk, :] = st
                m_j = jnp.max(st, axis=0, keepdims=True)
                m_run = m_j if m_run is None else jnp.maximum(m_run, m_j)
            p = jnp.exp2(s_ref[h, :n_k, :] - m_run)
            l_sum = jnp.sum(p, axis=0, keepdims=True)
            p_ref[h, :n_k, :] = p.astype(BF16)
            pv = jnp.dot(vt_ref[:, :n_k], p_ref[h, :n_k, :], preferred_element_type=F32)
            acc.append(pv * (1.0 / l_sum))
        out_t = jnp.concatenate([acc[0][:HEAD_DIM], acc[1][HEAD_DIM:]], axis=0)
        o_ref[rows, :] = out_t.T.astype(o_ref.dtype)


def _moba(proj):
    b, s, _ = proj.shape
    n_blocks = s // MOBA_BLOCK
    qo, ko, vo = COL_Q // PAIR, COL_K // PAIR, COL_V // PAIR
    return pl.pallas_call(
        functools.partial(_moba_kernel, n_blocks=n_blocks),
        grid=(b, N_PAIRS),
        in_specs=[
            pl.BlockSpec((None, s, PAIR), lambda i, p: (i, 0, qo + p)),
            pl.BlockSpec((None, s, PAIR), lambda i, p: (i, 0, ko + p)),
            pl.BlockSpec((None, s, PAIR), lambda i, p: (i, 0, vo + p)),
        ],
        out_specs=pl.BlockSpec((None, s, PAIR), lambda i, p: (i, 0, p)),
        out_shape=jax.ShapeDtypeStruct((b, s, WIDTH), ACT),
        scratch_shapes=[
            pltpu.VMEM((PAIR, s), BF16),
            pltpu.VMEM((2, n_blocks, s), F32),
            pltpu.VMEM((2, s, MOBA_BLOCK), F32),
            pltpu.VMEM((2, s, MOBA_BLOCK), BF16),
        ],
        compiler_params=pltpu.CompilerParams(
            dimension_semantics=("parallel", "parallel"), vmem_limit_bytes=VMEM_LIMIT),
        name="moba",
    )(proj, proj, proj)


def _rwkv_scan(rs, kmod, vr, na, nb, lw, state_ref, ts):
    n_chunks = ts // CHUNK
    n2 = 2 * CHUNK
    trow = lax.broadcasted_iota(jnp.int32, (ts, ts), 0)
    tcol = lax.broadcasted_iota(jnp.int32, (ts, ts), 1)
    ltri = jnp.where((trow // CHUNK == tcol // CHUNK) & (tcol <= trow), 1.0, 0.0).astype(F32)
    head0 = lax.broadcasted_iota(jnp.int32, (CHUNK, PAIR), 1) < HEAD_DIM
    srow = lax.broadcasted_iota(jnp.int32, (n2, n2), 0)
    scol = lax.broadcasted_iota(jnp.int32, (n2, n2), 1)
    same_head = (srow // CHUNK) == (scol // CHUNK)
    strict = same_head & (scol < srow)
    incl = same_head & (scol <= srow)
    eye = jnp.where(srow == scol, 1.0, 0.0).astype(F32)

    def stack(x):
        return jnp.concatenate([jnp.where(head0, x, 0.0), jnp.where(head0, 0.0, x)], axis=0)

    def dup(x):
        return jnp.concatenate([x, x], axis=0)

    cum = jnp.dot(ltri, lw, precision=lax.Precision.HIGHEST, preferred_element_type=F32)
    a_t = na * jnp.exp(cum - lw)
    r_t = rs * jnp.exp(cum)
    e_neg = jnp.exp(-cum)
    b_t = nb * e_neg
    k_t = kmod * e_neg

    pcs = [(p, c) for c in range(n_chunks) for p in range(N_PAIRS)]
    lhs_a, lhs_r, rhs, b_h, k_h, v_s, gam = {}, {}, {}, {}, {}, {}, {}
    for c in range(n_chunks):
        rc = slice(c * CHUNK, (c + 1) * CHUNK)
        tot = cum[(c + 1) * CHUNK - 1:(c + 1) * CHUNK, :]
        e_rel = jnp.exp(tot - cum[rc])
        bh_c = nb[rc] * e_rel
        kh_c = kmod[rc] * e_rel
        gam_c = jnp.exp(tot)
        for p in range(N_PAIRS):
            sl = slice(p * PAIR, (p + 1) * PAIR)
            pc = (p, c)
            lhs_a[pc] = stack(a_t[rc, sl]).astype(BF16)
            lhs_r[pc] = stack(r_t[rc, sl])
            rhs[pc] = jnp.concatenate([dup(b_t[rc, sl]), dup(k_t[rc, sl])], axis=0).astype(BF16)
            b_h[pc] = stack(bh_c[:, sl]).astype(BF16)
            k_h[pc] = stack(kh_c[:, sl])
            v_s[pc] = stack(vr[rc, sl]).astype(BF16)
            gam[pc] = gam_c[:, sl]

    a_ab, a_ak, a_rb, a_rk = {}, {}, {}, {}
    for pc in pcs:
        g = _mm_nt(jnp.concatenate([lhs_a[pc], lhs_r[pc].astype(BF16)], axis=0), rhs[pc])
        a_ab[pc] = jnp.where(strict, g[:n2, :n2], 0.0)
        a_ak[pc] = jnp.where(strict, g[:n2, n2:], 0.0).astype(BF16)
        a_rb[pc] = jnp.where(incl, g[n2:, :n2], 0.0).astype(BF16)
        a_rk[pc] = jnp.where(incl, g[n2:, n2:], 0.0)

    t_inv = {pc: eye + a_ab[pc] for pc in pcs}
    pw = {pc: a_ab[pc].astype(BF16) for pc in pcs}
    pw = {pc: _mm(pw[pc], pw[pc]).astype(BF16) for pc in pcs}
    for _ in range(CHUNK.bit_length() - 3):
        both = {pc: _mm(jnp.concatenate([pw[pc], t_inv[pc].astype(BF16)], axis=0), pw[pc]) for pc in pcs}
        t_inv = {pc: t_inv[pc] + both[pc][n2:] for pc in pcs}
        pw = {pc: both[pc][:n2].astype(BF16) for pc in pcs}
    t_inv = {pc: (t_inv[pc] + _mm(t_inv[pc], pw[pc])).astype(BF16) for pc in pcs}

    z = {pc: _mm_tn(t_inv[pc], b_h[pc]).astype(BF16) for pc in pcs}
    w = {pc: _mm(a_rb[pc], t_inv[pc]).astype(BF16) for pc in pcs}
    la_ak = {pc: jnp.concatenate([lhs_a[pc], a_ak[pc]], axis=1) for pc in pcs}
    my = {pc: _mm_tn(la_ak[pc], z[pc]) for pc in pcs}
    wq = {pc: _mm(w[pc], la_ak[pc]) for pc in pcs}
    m_t = {pc: my[pc][:n2].astype(BF16) for pc in pcs}
    y = {pc: (my[pc][n2:] + k_h[pc]).astype(BF16) for pc in pcs}
    q_p = {pc: (lhs_r[pc] + wq[pc][:, :n2]).astype(BF16) for pc in pcs}
    p_p = {pc: (wq[pc][:, n2:] + a_rk[pc]).astype(BF16) for pc in pcs}
    g_s = {pc: _mm_tn(v_s[pc], y[pc]) for pc in pcs}
    o_loc = {pc: _mm(p_p[pc], v_s[pc]) for pc in pcs}

    states = [state_ref[p] for p in range(N_PAIRS)]
    outs = [[] for _ in range(N_PAIRS)]
    for c in range(n_chunks):
        for p in range(N_PAIRS):
            pc = (p, c)
            sb = states[p].astype(BF16)
            os_ = _mm_nt(q_p[pc], sb) + o_loc[pc]
            outs[p].append(os_[:CHUNK] + os_[CHUNK:])
            states[p] = states[p] * gam[pc] + _mm(sb, m_t[pc]) + g_s[pc]
    for p in range(N_PAIRS):
        state_ref[p] = states[p]
    return [jnp.concatenate(o, axis=0) if len(o) > 1 else o[0] for o in outs]


def _rwkv_kernel(*refs, has_vmix, ts):
    if has_vmix:
        (r_ref, k_ref, v_ref, rz_ref, lora_ref, vd_ref, vfirst_ref,
         mu_rkv_ref, mu_lora_ref, w0a0_ref, lora_w_ref, kk_ref, ka_ref, rk_ref, lng_ref, lnb_ref,
         vmix_mu_ref, vmix_up_ref, vmix0_ref,
         yb_ref, carry_rkv, carry_lora, carry_vd, state_ref) = refs
    else:
        (r_ref, k_ref, v_ref, rz_ref, lora_ref,
         mu_rkv_ref, mu_lora_ref, w0a0_ref, lora_w_ref, kk_ref, ka_ref, rk_ref, lng_ref, lnb_ref,
         yb_ref, vfirst_ref, carry_rkv, carry_lora, state_ref) = refs
        carry_vd = None

    @pl.when(pl.program_id(1) == 0)
    def _():
        carry_rkv[...] = jnp.zeros_like(carry_rkv)
        carry_lora[...] = jnp.zeros_like(carry_lora)
        if carry_vd is not None:
            carry_vd[...] = jnp.zeros_like(carry_vd)
        state_ref[...] = jnp.zeros_like(state_ref)

    def shift(y, carry_ref, row_idx, mu):
        first = lax.broadcasted_iota(jnp.int32, y.shape, 0) == 0
        prev = jnp.where(first, carry_ref[row_idx:row_idx + 1, :], pltpu.roll(y, 1, 0))
        carry_ref[row_idx:row_idx + 1, :] = y[ts - 1:ts, :]
        return y + (prev - y) * mu

    mu_rkv = mu_rkv_ref[...]
    rs = shift(r_ref[...].astype(F32), carry_rkv, 0, mu_rkv[0:1, :])
    ks = shift(k_ref[...].astype(F32), carry_rkv, 1, mu_rkv[1:2, :])
    vs = shift(v_ref[...].astype(F32), carry_rkv, 2, mu_rkv[2:3, :])
    lo = shift(lora_ref[...].astype(F32), carry_lora, 0, mu_lora_ref[...])

    lane = lax.broadcasted_iota(jnp.int32, (ts, LANES), 1)
    z = jnp.where(lane < DECAY_LORA, jnp.tanh(lo), lo)
    wa = w0a0_ref[...] + _mm(z, lora_w_ref[...])
    zz = -wa[:, :WIDTH]
    w_log = -(jnp.maximum(zz, 0.0) + jnp.log(1.0 + jnp.exp(-jnp.abs(zz)))) - 0.5
    lw = -jnp.exp(w_log)
    a = _sigmoid(wa[:, WIDTH:])

    if has_vmix:
        vd = shift(vd_ref[...].astype(F32), carry_vd, 0, vmix_mu_ref[...])
        mix = _sigmoid(vmix0_ref[...] + _mm(vd, vmix_up_ref[...]))
        vr = vs + (vfirst_ref[...] - vs) * mix
    else:
        vfirst_ref[...] = vs
        vr = vs

    kx = ks * kk_ref[...]
    kmod = ks * (1.0 + (a - 1.0) * ka_ref[...])
    rkr = rs * kmod * rk_ref[...]
    rz = rz_ref[...].astype(F32)
    ln_g = lng_ref[...]
    ln_b = lnb_ref[...]

    head0_t = lane < HEAD_DIM

    def headsum(x):
        s0 = jnp.sum(jnp.where(head0_t, x, 0.0), axis=-1, keepdims=True)
        s1 = jnp.sum(jnp.where(head0_t, 0.0, x), axis=-1, keepdims=True)
        return jnp.where(head0_t, s0, s1)

    kk_all = []
    for p in range(N_PAIRS):
        kxp = kx[:, p * PAIR:(p + 1) * PAIR]
        kk_all.append(kxp * jnp.minimum(lax.rsqrt(headsum(kxp * kxp)), 1.0 / L2_EPS))
    kk = jnp.concatenate(kk_all, axis=1)

    o_pairs = _rwkv_scan(rs, kmod, vr, -kk, kk * a, lw, state_ref, ts)

    for p in range(N_PAIRS):
        sl = slice(p * PAIR, (p + 1) * PAIR)
        o_p = o_pairs[p]
        bonus = headsum(rkr[:, sl]) * vr[:, sl]
        mu = headsum(o_p) * (1.0 / HEAD_DIM)
        d = o_p - mu
        var = headsum(d * d) * (1.0 / HEAD_DIM)
        on = d * lax.rsqrt(var + GN_EPS) * ln_g[:, sl] + ln_b[:, sl]
        rz_p = rz[:, sl]
        yb_ref[:, sl] = ((on + bonus) * (rz_p * _sigmoid(rz_p))).astype(yb_ref.dtype)


def _rwkv(proj, vfirst, p, *, ts=256):
    b, s, _ = proj.shape
    has_vmix = vfirst is not None
    row = lambda width, col: pl.BlockSpec((None, ts, width), lambda i, t: (i, t, col // width))
    const = lambda shape: pl.BlockSpec(shape, lambda i, t: (0,) * len(shape))
    act = pl.BlockSpec((None, ts, WIDTH), lambda i, t: (i, t, 0))

    in_specs = [row(WIDTH, COL_R), row(WIDTH, COL_RK), row(WIDTH, COL_RV), row(WIDTH, COL_RZ),
                row(LANES, COL_LORA)]
    args = [proj, proj, proj, proj, proj]
    if has_vmix:
        in_specs += [row(LANES, COL_VD), act]
        args += [proj, vfirst]
    in_specs += [const((3, WIDTH)), const((1, LANES)), const((1, 2 * WIDTH)), const((LANES, 2 * WIDTH))]
    args += [p["mu_rkv"], p["mu_lora"], p["w0a0"], p["lora_w"]]
    for name in ("k_k", "k_a", "r_k", "ln_g", "ln_b"):
        in_specs.append(const((1, WIDTH)))
        args.append(p[name])
    if has_vmix:
        in_specs += [const((1, LANES)), const((LANES, WIDTH)), const((1, WIDTH))]
        args += [p["vmix_mu"], p["vmix_up"], p["vmix0"]]

    out_shape = [jax.ShapeDtypeStruct((b, s, WIDTH), ACT)]
    out_specs = [act]
    scratch = [pltpu.VMEM((8, WIDTH), F32), pltpu.VMEM((8, LANES), F32)]
    if has_vmix:
        scratch.append(pltpu.VMEM((8, LANES), F32))
    else:
        out_shape.append(jax.ShapeDtypeStruct((b, s, WIDTH), F32))
        out_specs.append(act)
    scratch.append(pltpu.VMEM((N_PAIRS, PAIR, PAIR), F32))

    res = pl.pallas_call(
        functools.partial(_rwkv_kernel, has_vmix=has_vmix, ts=ts),
        grid=(b, s // ts),
        in_specs=in_specs,
        out_specs=out_specs,
        out_shape=out_shape,
        scratch_shapes=scratch,
        compiler_params=pltpu.CompilerParams(
            dimension_semantics=("parallel", "arbitrary"), vmem_limit_bytes=VMEM_LIMIT),
        name="rwkv_vmix" if has_vmix else "rwkv",
    )(*args)
    return (res[0], vfirst) if has_vmix else (res[0], res[1])


def _outproj_kernel(ya_ref, az_ref, yb_ref, gatt_ref, grw_ref, x_ref,
                    wua_ref, wub_ref, wout_ref, g_ref, o_ref):
    az = az_ref[...].astype(F32)
    ya = ya_ref[...].astype(F32) * (az * _sigmoid(az))
    u = (_sigmoid(gatt_ref[...].astype(F32)) * _mm(ya, wua_ref[...])
         + _sigmoid(grw_ref[...].astype(F32)) * _mm(yb_ref[...], wub_ref[...]))
    y = _mm(u, wout_ref[...])
    ms = jnp.mean(y * y, axis=-1, keepdims=True)
    o_ref[...] = x_ref[...] + y * lax.rsqrt(ms + RMS_EPS) * g_ref[...]


def _outproj(ya, proj2d, yb, x2d, w_up_att, w_up_rw, w_out, g, *, tm=512):
    t, d = x2d.shape
    row = lambda width, col: pl.BlockSpec((tm, width), lambda i: (i, col // width))
    const = lambda shape: pl.BlockSpec(shape, lambda i: (0, 0))
    return pl.pallas_call(
        _outproj_kernel,
        grid=(t // tm,),
        in_specs=[row(WIDTH, 0), row(WIDTH, COL_AZ), row(WIDTH, 0), row(d, COL_GATT), row(d, COL_GRW),
                  row(d, 0), const((WIDTH, d)), const((WIDTH, d)), const((d, d)), const((1, d))],
        out_specs=row(d, 0),
        out_shape=jax.ShapeDtypeStruct((t, d), F32),
        compiler_params=pltpu.CompilerParams(
            dimension_semantics=("parallel",), vmem_limit_bytes=VMEM_LIMIT),
        name="outproj",
    )(ya, proj2d, yb, proj2d, proj2d, x2d, w_up_att, w_up_rw, w_out, g.reshape(1, d))


def _layer_params(l, w_in, rw_mu, rw_w0, rw_w_up, rw_a0, rw_a_up, rw_k_k, rw_k_a, rw_r_k,
                  rw_ln_g, rw_ln_b, rw_vmix_down, rw_vmix_mu, rw_vmix_up, rw_vmix0):
    w = w_in[l]
    d = w.shape[0]
    att = w[:, :4 * WIDTH]
    o = 4 * WIDTH
    rkv = w[:, o:o + 3 * WIDTH]
    lora = w[:, o + 3 * WIDTH:o + 3 * WIDTH + DECAY_LORA + ICLR_LORA]
    o = o + 3 * WIDTH + DECAY_LORA + ICLR_LORA
    rz = w[:, o:o + WIDTH]
    gates = w[:, o + WIDTH:]
    vd = jnp.zeros((d, LANES), F32)
    if l > 0:
        vd = vd.at[:, :VMIX_LORA].set(rw_vmix_down[l - 1])
    w_cat = jnp.concatenate([att, rkv, rz, gates, lora, vd], axis=1).astype(BF16)

    mu = rw_mu[l]
    zeros = jnp.zeros((DECAY_LORA, WIDTH), F32)
    lora_w = jnp.concatenate([jnp.concatenate([rw_w_up[l], zeros], axis=1),
                              jnp.concatenate([zeros, rw_a_up[l]], axis=1)], axis=0).astype(BF16)
    p = {
        "w_cat": w_cat,
        "mu_rkv": mu[:3 * WIDTH].reshape(3, WIDTH),
        "mu_lora": mu[3 * WIDTH:].reshape(1, LANES),
        "w0a0": jnp.concatenate([rw_w0[l], rw_a0[l]]).reshape(1, 2 * WIDTH),
        "lora_w": lora_w,
        "k_k": rw_k_k[l].reshape(1, WIDTH), "k_a": rw_k_a[l].reshape(1, WIDTH),
        "r_k": rw_r_k[l].reshape(1, WIDTH),
        "ln_g": rw_ln_g[l].reshape(1, WIDTH), "ln_b": rw_ln_b[l].reshape(1, WIDTH),
    }
    if l > 0:
        p["vmix_mu"] = jnp.zeros((1, LANES), F32).at[0, :VMIX_LORA].set(rw_vmix_mu[l - 1])
        p["vmix_up"] = jnp.zeros((LANES, WIDTH), F32).at[:VMIX_LORA].set(rw_vmix_up[l - 1]).astype(BF16)
        p["vmix0"] = rw_vmix0[l - 1].reshape(1, WIDTH)
    return p


def kernel(x, norm_pre, norm_post, w_in, rw_mu, rw_w0, rw_w_up, rw_a0, rw_a_up, rw_k_k, rw_k_a, rw_r_k, rw_ln_g, rw_ln_b, rw_vmix_down, rw_vmix_mu, rw_vmix_up, rw_vmix0, w_up_att, w_up_rw, w_out):
    b, s, d = x.shape
    assert d == D_MODEL and s % MOBA_BLOCK == 0
    depth = w_in.shape[0]
    x2d = x.reshape(b * s, d)
    vfirst = None
    for l in range(depth):
        p = _layer_params(l, w_in, rw_mu, rw_w0, rw_w_up, rw_a0, rw_a_up, rw_k_k, rw_k_a, rw_r_k,
                          rw_ln_g, rw_ln_b, rw_vmix_down, rw_vmix_mu, rw_vmix_up, rw_vmix0)
        proj2d = _inproj(x2d, norm_pre[l], p["w_cat"])
        proj = proj2d.reshape(b, s, PROJ_W)
        ya = _moba(proj)
        yb, vfirst = _rwkv(proj, vfirst, p)
        x2d = _outproj(ya.reshape(b * s, WIDTH), proj2d, yb.reshape(b * s, WIDTH), x2d,
                       w_up_att[l].astype(BF16), w_up_rw[l].astype(BF16), w_out[l].astype(BF16),
                       norm_post[l])
    return x2d.reshape(b, s, d)
```

```python
import functools

import jax
import jax.numpy as jnp
from jax import lax
from jax.experimental import pallas as pl
from jax.experimental.pallas import tpu as pltpu

F32 = jnp.float32
BF16 = jnp.bfloat16
ACT = BF16

D_MODEL = 1024
N_HEADS = 8
HEAD_DIM = 64
WIDTH = N_HEADS * HEAD_DIM
MOBA_BLOCK = 256
MOBA_TOPK = 3
DECAY_LORA = 64
ICLR_LORA = 64
VMIX_LORA = 32
RMS_EPS = 1e-6
GN_EPS = 64e-5
L2_EPS = 1e-12
NEG_INF = -1e30

LANES = 128
PAIR = 2 * HEAD_DIM
N_PAIRS = WIDTH // PAIR
CHUNK = 64

COL_Q, COL_K, COL_V, COL_AZ = 0, 512, 1024, 1536
COL_R, COL_RK, COL_RV, COL_RZ = 2048, 2560, 3072, 3584
COL_GATT, COL_GRW = 4096, 5120
COL_LORA = 6144
COL_VD = 6272
PROJ_W = 6400

VMEM_LIMIT = 48 * 1024 * 1024


def _sigmoid(x):
    return 1.0 / (1.0 + jnp.exp(-x))


def _mm(a, b):
    return jnp.dot(a.astype(BF16), b.astype(BF16), preferred_element_type=F32)


def _mm_nt(a, b):
    return lax.dot_general(a.astype(BF16), b.astype(BF16), (((1,), (1,)), ((), ())),
                           preferred_element_type=F32)


def _mm_tn(a, b):
    return lax.dot_general(a.astype(BF16), b.astype(BF16), (((0,), (0,)), ((), ())),
                           preferred_element_type=F32)


def _inproj_kernel(x_ref, g_ref, w_ref, o_ref, h_ref):
    @pl.when(pl.program_id(1) == 0)
    def _():
        x = x_ref[...]
        ms = jnp.mean(x * x, axis=-1, keepdims=True)
        h_ref[...] = (x * lax.rsqrt(ms + RMS_EPS) * g_ref[...]).astype(BF16)

    o_ref[...] = jnp.dot(h_ref[...], w_ref[...], preferred_element_type=F32).astype(o_ref.dtype)


def _inproj(x2d, g, w_cat, *, tm=1024, tn=1280):
    t, d = x2d.shape
    n = w_cat.shape[1]
    return pl.pallas_call(
        _inproj_kernel,
        grid=(t // tm, n // tn),
        in_specs=[
            pl.BlockSpec((tm, d), lambda i, j: (i, 0)),
            pl.BlockSpec((1, d), lambda i, j: (0, 0)),
            pl.BlockSpec((d, tn), lambda i, j: (0, j)),
        ],
        out_specs=pl.BlockSpec((tm, tn), lambda i, j: (i, j)),
        out_shape=jax.ShapeDtypeStruct((t, n), ACT),
        scratch_shapes=[pltpu.VMEM((tm, d), BF16)],
        compiler_params=pltpu.CompilerParams(
            dimension_semantics=("parallel", "arbitrary"), vmem_limit_bytes=VMEM_LIMIT),
        name="inproj",
    )(x2d, g.reshape(1, d), w_cat)


def _moba_kernel(q_ref, k_ref, v_ref, o_ref, vt_ref, bias_ref, s_ref, p_ref, *, n_blocks):
    hp = pl.program_id(1)
    blk = MOBA_BLOCK
    s_len = n_blocks * blk
    log2e = 1.4426950408889634
    q_scale = (HEAD_DIM ** -0.5) * log2e

    vt_ref[...] = v_ref[...].T

    mean_row = lax.broadcasted_iota(jnp.int32, (n_blocks, PAIR), 0)
    kmean = jnp.zeros((n_blocks, PAIR), F32)
    for n in range(n_blocks):
        kmean = jnp.where(mean_row == n,
                          jnp.mean(k_ref[n * blk:(n + 1) * blk, :].astype(F32), axis=0, keepdims=True), kmean)

    lane = lax.broadcasted_iota(jnp.int32, (1, PAIR), 1)
    head_lanes = (lane < HEAD_DIM, lane >= HEAD_DIM)
    nrow = lax.broadcasted_iota(jnp.int32, (n_blocks, s_len), 0)
    qblk = lax.broadcasted_iota(jnp.int32, (n_blocks, s_len), 1) // blk
    past = nrow < qblk
    krow = lax.broadcasted_iota(jnp.int32, (blk, blk), 0)
    qcol = lax.broadcasted_iota(jnp.int32, (blk, blk), 1)
    dist0 = (qcol - krow).astype(F32)
    causal = qcol >= krow

    km = jnp.concatenate([jnp.where(head_lanes[0], kmean, 0.0), jnp.where(head_lanes[1], kmean, 0.0)], axis=0)
    km_hi = km.astype(BF16)
    km_r = km - km_hi.astype(F32)
    km_mid = km_r.astype(BF16)
    km_lo = (km_r - km_mid.astype(F32)).astype(BF16)
    nt = (((1,), (1,)), ((), ()))
    q_all = q_ref[...]
    gates = (lax.dot_general(km_lo, q_all, nt, preferred_element_type=F32)
             + lax.dot_general(km_mid, q_all, nt, preferred_element_type=F32)
             + lax.dot_general(km_hi, q_all, nt, preferred_element_type=F32))

    alibi0 = []
    for h in range(2):
        head = lax.convert_element_type(hp * 2 + h + 1, F32)
        slope2 = jnp.exp2(jnp.zeros((1, 1), F32) - (8.0 / N_HEADS) * head) * log2e
        alibi0.append(slope2 * dist0)

        gm = jnp.where(past, gates[h * n_blocks:(h + 1) * n_blocks], NEG_INF)
        rank = jnp.zeros((n_blocks, s_len), F32)
        for m in range(n_blocks):
            gcol = gm[m:m + 1, :]
            beats = (gcol > gm) | ((gcol == gm) & (nrow > m))
            rank = rank + jnp.where(beats, 1.0, 0.0)
        sel = past & (rank < MOBA_TOPK)
        bias_ref[h] = jnp.where(sel, -slope2 * ((qblk - nrow) * blk).astype(F32), NEG_INF)

    for i in range(n_blocks):
        rows = slice(i * blk, (i + 1) * blk)
        n_k = (i + 1) * blk
        q_i = q_ref[rows, :].astype(F32) * q_scale
        acc = []
        for h in range(2):
            qm = jnp.where(head_lanes[h], q_i, 0.0).astype(BF16)
            m_run = None
            for j in range(i + 1):
                st = lax.dot_general(k_ref[j * blk:(j + 1) * blk, :], qm, nt,
                                     preferred_element_type=F32)
                if j == i:
                    st = jnp.where(causal, st - alibi0[h], NEG_INF)
                else:
                    st = st - alibi0[h] + bias_ref[h, j:j + 1, rows]
                s_ref[h, j * blk:(j + 1) * blk, :] = st
                m_j = jnp.max(st, axis=0, keepdims=True)
                m_run = m_j if m_run is None else jnp.maximum(m_run, m_j)
            p = jnp.exp2(s_ref[h, :n_k, :] - m_run)
            l_sum = jnp.sum(p, axis=0, keepdims=True)
            p_ref[h, :n_k, :] = p.astype(BF16)
            pv = jnp.dot(vt_ref[:, :n_k], p_ref[h, :n_k, :], preferred_element_type=F32)
            acc.append(pv * (1.0 / l_sum))
        out_t = jnp.concatenate([acc[0][:HEAD_DIM], acc[1][HEAD_DIM:]], axis=0)
        o_ref[rows, :] = out_t.T.astype(o_ref.dtype)


def _moba(proj):
    b, s, _ = proj.shape
    n_blocks = s // MOBA_BLOCK
    qo, ko, vo = COL_Q // PAIR, COL_K // PAIR, COL_V // PAIR
    return pl.pallas_call(
        functools.partial(_moba_kernel, n_blocks=n_blocks),
        grid=(b, N_PAIRS),
        in_specs=[
            pl.BlockSpec((None, s, PAIR), lambda i, p: (i, 0, qo + p)),
            pl.BlockSpec((None, s, PAIR), lambda i, p: (i, 0, ko + p)),
            pl.BlockSpec((None, s, PAIR), lambda i, p: (i, 0, vo + p)),
        ],
        out_specs=pl.BlockSpec((None, s, PAIR), lambda i, p: (i, 0, p)),
        out_shape=jax.ShapeDtypeStruct((b, s, WIDTH), ACT),
        scratch_shapes=[
            pltpu.VMEM((PAIR, s), BF16),
            pltpu.VMEM((2, n_blocks, s), F32),
            pltpu.VMEM((2, s, MOBA_BLOCK), F32),
            pltpu.VMEM((2, s, MOBA_BLOCK), BF16),
        ],
        compiler_params=pltpu.CompilerParams(
            dimension_semantics=("parallel", "parallel"), vmem_limit_bytes=VMEM_LIMIT),
        name="moba",
    )(proj, proj, proj)


def _rwkv_scan(rs, kmod, vr, na, nb, lw, state_ref, ts):
    n_chunks = ts // CHUNK
    n2 = 2 * CHUNK
    trow = lax.broadcasted_iota(jnp.int32, (ts, ts), 0)
    tcol = lax.broadcasted_iota(jnp.int32, (ts, ts), 1)
    ltri = jnp.where((trow // CHUNK == tcol // CHUNK) & (tcol <= trow), 1.0, 0.0).astype(BF16)
    head0 = lax.broadcasted_iota(jnp.int32, (CHUNK, PAIR), 1) < HEAD_DIM
    wrow = lax.broadcasted_iota(jnp.int32, (CHUNK, PAIR), 0)
    wcol = lax.broadcasted_iota(jnp.int32, (CHUNK, PAIR), 1) % CHUNK
    strict = wcol < wrow
    incl = wcol <= wrow
    eye = jnp.where(wcol == wrow, 1.0, 0.0).astype(F32)
    srow = lax.broadcasted_iota(jnp.int32, (n2, n2), 0)
    scol = lax.broadcasted_iota(jnp.int32, (n2, n2), 1)
    same_head = (srow // CHUNK) == (scol // CHUNK)

    def stack(x):
        zero = jnp.zeros_like(x)
        return jnp.concatenate([jnp.where(head0, x, zero), jnp.where(head0, zero, x)], axis=0)

    lw_hi = lw.astype(BF16)
    lw_r = lw - lw_hi.astype(F32)
    lw_mid = lw_r.astype(BF16)
    lw_lo = (lw_r - lw_mid.astype(F32)).astype(BF16)
    cum = (jnp.dot(ltri, lw_lo, preferred_element_type=F32) + jnp.dot(ltri, lw_mid, preferred_element_type=F32)
           + jnp.dot(ltri, lw_hi, preferred_element_type=F32))
    a_t = na * jnp.exp(cum - lw)
    r_t = rs * jnp.exp(cum)
    e_neg = jnp.exp(-cum)
    b_t = nb * e_neg
    k_t = kmod * e_neg

    pcs = [(p, c) for c in range(n_chunks) for p in range(N_PAIRS)]
    lhs_g, rhs_g, a_s, r_n, b_h, k_h, v_s, gam = {}, {}, {}, {}, {}, {}, {}, {}
    for c in range(n_chunks):
        rc = slice(c * CHUNK, (c + 1) * CHUNK)
        tot = cum[(c + 1) * CHUNK - 1:(c + 1) * CHUNK, :]
        e_rel = jnp.exp(tot - cum[rc])
        bh_c = nb[rc] * e_rel
        kh_c = kmod[rc] * e_rel
        gam_c = jnp.exp(tot)
        for p in range(N_PAIRS):
            sl = slice(p * PAIR, (p + 1) * PAIR)
            pc = (p, c)
            a_n = a_t[rc, sl].astype(BF16)
            r_n[pc] = r_t[rc, sl]
            lhs_g[pc] = jnp.concatenate([a_n, r_n[pc].astype(BF16)], axis=0)
            rhs_g[pc] = jnp.concatenate([stack(b_t[rc, sl].astype(BF16)),
                                         stack(k_t[rc, sl].astype(BF16))], axis=0)
            a_s[pc] = stack(a_n)
            b_h[pc] = bh_c[:, sl].astype(BF16)
            k_h[pc] = stack(kh_c[:, sl])
            v_s[pc] = stack(vr[rc, sl].astype(BF16))
            gam[pc] = gam_c[:, sl]

    a_ab, a_ak, a_rb, a_rk = {}, {}, {}, {}
    for pc in pcs:
        g = _mm_nt(lhs_g[pc], rhs_g[pc])
        a_ab[pc] = jnp.where(strict, g[:CHUNK, :n2], 0.0)
        a_ak[pc] = jnp.where(strict, g[:CHUNK, n2:], 0.0).astype(BF16)
        a_rb[pc] = jnp.where(incl, g[CHUNK:, :n2], 0.0).astype(BF16)
        a_rk[pc] = jnp.where(incl, g[CHUNK:, n2:], 0.0)

    t_inv = {pc: eye + a_ab[pc] for pc in pcs}
    pw = {pc: a_ab[pc].astype(BF16) for pc in pcs}
    pw = {pc: _mm(pw[pc], stack(pw[pc])).astype(BF16) for pc in pcs}
    for _ in range(CHUNK.bit_length() - 3):
        both = {pc: _mm(jnp.concatenate([pw[pc], t_inv[pc].astype(BF16)], axis=0), stack(pw[pc])) for pc in pcs}
        t_inv = {pc: t_inv[pc] + both[pc][CHUNK:] for pc in pcs}
        pw = {pc: both[pc][:CHUNK].astype(BF16) for pc in pcs}
    t_inv = {pc: (t_inv[pc] + _mm(t_inv[pc], stack(pw[pc]))).astype(BF16) for pc in pcs}

    z = {pc: jnp.where(same_head, _mm_tn(t_inv[pc], b_h[pc]), 0.0).astype(BF16) for pc in pcs}
    w = {pc: _mm(a_rb[pc], stack(t_inv[pc])).astype(BF16) for pc in pcs}
    la_ak = {pc: jnp.concatenate([a_s[pc], stack(a_ak[pc])], axis=1) for pc in pcs}
    my = {pc: _mm_tn(la_ak[pc], z[pc]) for pc in pcs}
    wq = {pc: _mm(w[pc], la_ak[pc]) for pc in pcs}
    m_t = {pc: my[pc][:n2].astype(BF16) for pc in pcs}
    y = {pc: (my[pc][n2:] + k_h[pc]).astype(BF16) for pc in pcs}
    q_p = {pc: (r_n[pc] + wq[pc][:, :n2]).astype(BF16) for pc in pcs}
    p_p = {pc: (wq[pc][:, n2:] + a_rk[pc]).astype(BF16) for pc in pcs}
    g_s = {pc: _mm_tn(v_s[pc], y[pc]) for pc in pcs}
    o_loc = {pc: _mm(p_p[pc], v_s[pc]) for pc in pcs}

    states = [state_ref[p] for p in range(N_PAIRS)]
    outs = [[] for _ in range(N_PAIRS)]
    for c in range(n_chunks):
        for p in range(N_PAIRS):
            pc = (p, c)
            sb = states[p].astype(BF16)
            outs[p].append(_mm_nt(q_p[pc], sb) + o_loc[pc])
            states[p] = states[p] * gam[pc] + _mm(sb, m_t[pc]) + g_s[pc]
    for p in range(N_PAIRS):
        state_ref[p] = states[p]
    return [jnp.concatenate(o, axis=0) if len(o) > 1 else o[0] for o in outs]


def _rwkv_kernel(*refs, has_vmix, ts):
    if has_vmix:
        (r_ref, k_ref, v_ref, rz_ref, lora_ref, vd_ref, vfirst_ref,
         mu_rkv_ref, mu_lora_ref, w0a0_ref, lora_w_ref, kk_ref, ka_ref, rk_ref, lng_ref, lnb_ref,
         vmix_mu_ref, vmix_up_ref, vmix0_ref,
         yb_ref, carry_rkv, carry_lora, carry_vd, state_ref) = refs
    else:
        (r_ref, k_ref, v_ref, rz_ref, lora_ref,
         mu_rkv_ref, mu_lora_ref, w0a0_ref, lora_w_ref, kk_ref, ka_ref, rk_ref, lng_ref, lnb_ref,
         yb_ref, vfirst_ref, carry_rkv, carry_lora, state_ref) = refs
        carry_vd = None

    @pl.when(pl.program_id(1) == 0)
    def _():
        carry_rkv[...] = jnp.zeros_like(carry_rkv)
        carry_lora[...] = jnp.zeros_like(carry_lora)
        if carry_vd is not None:
            carry_vd[...] = jnp.zeros_like(carry_vd)
        state_ref[...] = jnp.zeros_like(state_ref)

    sub_row = lax.broadcasted_iota(jnp.int32, (ts, ts), 0)
    sub_col = lax.broadcasted_iota(jnp.int32, (ts, ts), 1)
    shift_mat = jnp.where(sub_row == sub_col + 1, 1.0, 0.0).astype(r_ref.dtype)
    top_row = lax.broadcasted_iota(jnp.int32, (8, 1), 0) == 0

    def shift(y_act, carry_ref, row_idx, mu):
        y = y_act.astype(F32)
        prev = jnp.dot(shift_mat, y_act, preferred_element_type=F32)
        head = jnp.where(top_row, carry_ref[row_idx:row_idx + 1, :], prev[:8])
        prev = jnp.concatenate([head, prev[8:]], axis=0)
        carry_ref[row_idx:row_idx + 1, :] = y[ts - 1:ts, :]
        return y + (prev - y) * mu

    mu_rkv = mu_rkv_ref[...]
    rs = shift(r_ref[...], carry_rkv, 0, mu_rkv[0:1, :])
    ks = shift(k_ref[...], carry_rkv, 1, mu_rkv[1:2, :])
    vs = shift(v_ref[...], carry_rkv, 2, mu_rkv[2:3, :])
    lo = shift(lora_ref[...], carry_lora, 0, mu_lora_ref[...])

    lane = lax.broadcasted_iota(jnp.int32, (ts, LANES), 1)
    z = jnp.where(lane < DECAY_LORA, jnp.tanh(lo), lo)
    wa = w0a0_ref[...] + _mm(z, lora_w_ref[...])
    lw = (-0.6065306597126334) * _sigmoid(wa[:, :WIDTH])
    a = _sigmoid(wa[:, WIDTH:])

    if has_vmix:
        vd = shift(vd_ref[...], carry_vd, 0, vmix_mu_ref[...])
        mix = _sigmoid(vmix0_ref[...] + _mm(vd, vmix_up_ref[...]))
        vr = vs + (vfirst_ref[...] - vs) * mix
    else:
        vfirst_ref[...] = vs
        vr = vs

    kx = ks * kk_ref[...]
    kmod = ks * (1.0 + (a - 1.0) * ka_ref[...])
    rkr = rs * kmod * rk_ref[...]
    rz = rz_ref[...].astype(F32)
    ln_g = lng_ref[...]
    ln_b = lnb_ref[...]

    head0_t = lane < HEAD_DIM

    def headsum(x):
        s0 = jnp.sum(jnp.where(head0_t, x, 0.0), axis=-1, keepdims=True)
        s1 = jnp.sum(jnp.where(head0_t, 0.0, x), axis=-1, keepdims=True)
        return jnp.where(head0_t, s0, s1)

    kk_all = []
    for p in range(N_PAIRS):
        kxp = kx[:, p * PAIR:(p + 1) * PAIR]
        kk_all.append(kxp * jnp.minimum(lax.rsqrt(headsum(kxp * kxp)), 1.0 / L2_EPS))
    kk = jnp.concatenate(kk_all, axis=1)

    o_pairs = _rwkv_scan(rs, kmod, vr, -kk, kk * a, lw, state_ref, ts)

    for p in range(N_PAIRS):
        sl = slice(p * PAIR, (p + 1) * PAIR)
        o_p = o_pairs[p]
        bonus = headsum(rkr[:, sl]) * vr[:, sl]
        mu = headsum(o_p) * (1.0 / HEAD_DIM)
        d = o_p - mu
        var = headsum(d * d) * (1.0 / HEAD_DIM)
        on = d * lax.rsqrt(var + GN_EPS) * ln_g[:, sl] + ln_b[:, sl]
        rz_p = rz[:, sl]
        yb_ref[:, sl] = ((on + bonus) * (rz_p * _sigmoid(rz_p))).astype(yb_ref.dtype)


def _rwkv(proj, vfirst, p, *, ts=256):
    b, s, _ = proj.shape
    has_vmix = vfirst is not None
    row = lambda width, col: pl.BlockSpec((None, ts, width), lambda i, t: (i, t, col // width))
    const = lambda shape: pl.BlockSpec(shape, lambda i, t: (0,) * len(shape))
    act = pl.BlockSpec((None, ts, WIDTH), lambda i, t: (i, t, 0))

    in_specs = [row(WIDTH, COL_R), row(WIDTH, COL_RK), row(WIDTH, COL_RV), row(WIDTH, COL_RZ),
                row(LANES, COL_LORA)]
    args = [proj, proj, proj, proj, proj]
    if has_vmix:
        in_specs += [row(LANES, COL_VD), act]
        args += [proj, vfirst]
    in_specs += [const((3, WIDTH)), const((1, LANES)), const((1, 2 * WIDTH)), const((LANES, 2 * WIDTH))]
    args += [p["mu_rkv"], p["mu_lora"], p["w0a0"], p["lora_w"]]
    for name in ("k_k", "k_a", "r_k", "ln_g", "ln_b"):
        in_specs.append(const((1, WIDTH)))
        args.append(p[name])
    if has_vmix:
        in_specs += [const((1, LANES)), const((LANES, WIDTH)), const((1, WIDTH))]
        args += [p["vmix_mu"], p["vmix_up"], p["vmix0"]]

    out_shape = [jax.ShapeDtypeStruct((b, s, WIDTH), ACT)]
    out_specs = [act]
    scratch = [pltpu.VMEM((8, WIDTH), F32), pltpu.VMEM((8, LANES), F32)]
    if has_vmix:
        scratch.append(pltpu.VMEM((8, LANES), F32))
    else:
        out_shape.append(jax.ShapeDtypeStruct((b, s, WIDTH), F32))
        out_specs.append(act)
    scratch.append(pltpu.VMEM((N_PAIRS, PAIR, PAIR), F32))

    res = pl.pallas_call(
        functools.partial(_rwkv_kernel, has_vmix=has_vmix, ts=ts),
        grid=(b, s // ts),
        in_specs=in_specs,
        out_specs=out_specs,
        out_shape=out_shape,
        scratch_shapes=scratch,
        compiler_params=pltpu.CompilerParams(
            dimension_semantics=("parallel", "arbitrary"), vmem_limit_bytes=VMEM_LIMIT),
        name="rwkv_vmix" if has_vmix else "rwkv",
    )(*args)
    return (res[0], vfirst) if has_vmix else (res[0], res[1])


def _outproj_kernel(ya_ref, az_ref, yb_ref, gatt_ref, grw_ref, x_ref,
                    wua_ref, wub_ref, wout_ref, g_ref, o_ref):
    az = az_ref[...].astype(F32)
    ya = ya_ref[...].astype(F32) * (az * _sigmoid(az))
    u = (_sigmoid(gatt_ref[...].astype(F32)) * _mm(ya, wua_ref[...])
         + _sigmoid(grw_ref[...].astype(F32)) * _mm(yb_ref[...], wub_ref[...]))
    y = _mm(u, wout_ref[...])
    ms = jnp.mean(y * y, axis=-1, keepdims=True)
    o_ref[...] = x_ref[...] + y * lax.rsqrt(ms + RMS_EPS) * g_ref[...]


def _outproj(ya, proj2d, yb, x2d, w_up_att, w_up_rw, w_out, g, *, tm=512):
    t, d = x2d.shape
    row = lambda width, col: pl.BlockSpec((tm, width), lambda i: (i, col // width))
    const = lambda shape: pl.BlockSpec(shape, lambda i: (0, 0))
    return pl.pallas_call(
        _outproj_kernel,
        grid=(t // tm,),
        in_specs=[row(WIDTH, 0), row(WIDTH, COL_AZ), row(WIDTH, 0), row(d, COL_GATT), row(d, COL_GRW),
                  row(d, 0), const((WIDTH, d)), const((WIDTH, d)), const((d, d)), const((1, d))],
        out_specs=row(d, 0),
        out_shape=jax.ShapeDtypeStruct((t, d), F32),
        compiler_params=pltpu.CompilerParams(
            dimension_semantics=("parallel",), vmem_limit_bytes=VMEM_LIMIT),
        name="outproj",
    )(ya, proj2d, yb, proj2d, proj2d, x2d, w_up_att, w_up_rw, w_out, g.reshape(1, d))


def _layer_params(l, w_in, rw_mu, rw_w0, rw_w_up, rw_a0, rw_a_up, rw_k_k, rw_k_a, rw_r_k,
                  rw_ln_g, rw_ln_b, rw_vmix_down, rw_vmix_mu, rw_vmix_up, rw_vmix0):
    w = w_in[l]
    d = w.shape[0]
    att = w[:, :4 * WIDTH]
    o = 4 * WIDTH
    rkv = w[:, o:o + 3 * WIDTH]
    lora = w[:, o + 3 * WIDTH:o + 3 * WIDTH + DECAY_LORA + ICLR_LORA]
    o = o + 3 * WIDTH + DECAY_LORA + ICLR_LORA
    rz = w[:, o:o + WIDTH]
    gates = w[:, o + WIDTH:]
    vd = jnp.zeros((d, LANES), F32)
    if l > 0:
        vd = vd.at[:, :VMIX_LORA].set(rw_vmix_down[l - 1])
    w_cat = jnp.concatenate([att, rkv, rz, gates, lora, vd], axis=1).astype(BF16)

    mu = rw_mu[l]
    zeros = jnp.zeros((DECAY_LORA, WIDTH), F32)
    lora_w = jnp.concatenate([jnp.concatenate([rw_w_up[l], zeros], axis=1),
                              jnp.concatenate([zeros, rw_a_up[l]], axis=1)], axis=0).astype(BF16)
    p = {
        "w_cat": w_cat,
        "mu_rkv": mu[:3 * WIDTH].reshape(3, WIDTH),
        "mu_lora": mu[3 * WIDTH:].reshape(1, LANES),
        "w0a0": jnp.concatenate([rw_w0[l], rw_a0[l]]).reshape(1, 2 * WIDTH),
        "lora_w": lora_w,
        "k_k": rw_k_k[l].reshape(1, WIDTH), "k_a": rw_k_a[l].reshape(1, WIDTH),
        "r_k": rw_r_k[l].reshape(1, WIDTH),
        "ln_g": rw_ln_g[l].reshape(1, WIDTH), "ln_b": rw_ln_b[l].reshape(1, WIDTH),
    }
    if l > 0:
        p["vmix_mu"] = jnp.zeros((1, LANES), F32).at[0, :VMIX_LORA].set(rw_vmix_mu[l - 1])
        p["vmix_up"] = jnp.zeros((LANES, WIDTH), F32).at[:VMIX_LORA].set(rw_vmix_up[l - 1]).astype(BF16)
        p["vmix0"] = rw_vmix0[l - 1].reshape(1, WIDTH)
    return p


def kernel(x, norm_pre, norm_post, w_in, rw_mu, rw_w0, rw_w_up, rw_a0, rw_a_up, rw_k_k, rw_k_a, rw_r_k, rw_ln_g, rw_ln_b, rw_vmix_down, rw_vmix_mu, rw_vmix_up, rw_vmix0, w_up_att, w_up_rw, w_out):
    b, s, d = x.shape
    assert d == D_MODEL and s % MOBA_BLOCK == 0
    depth = w_in.shape[0]
    x2d = x.reshape(b * s, d)
    vfirst = None
    for l in range(depth):
        p = _layer_params(l, w_in, rw_mu, rw_w0, rw_w_up, rw_a0, rw_a_up, rw_k_k, rw_k_a, rw_r_k,
                          rw_ln_g, rw_ln_b, rw_vmix_down, rw_vmix_mu, rw_vmix_up, rw_vmix0)
        proj2d = _inproj(x2d, norm_pre[l], p["w_cat"])
        proj = proj2d.reshape(b, s, PROJ_W)
        ya = _moba(proj)
        yb, vfirst = _rwkv(proj, vfirst, p)
        x2d = _outproj(ya.reshape(b * s, WIDTH), proj2d, yb.reshape(b * s, WIDTH), x2d,
                       w_up_att[l].astype(BF16), w_up_rw[l].astype(BF16), w_out[l].astype(BF16),
                       norm_post[l])
    return x2d.reshape(b, s, d)
```

```python
import functools

import jax
import jax.numpy as jnp
from jax import lax
from jax.experimental import pallas as pl
from jax.experimental.pallas import tpu as pltpu

F32 = jnp.float32
BF16 = jnp.bfloat16
ACT = BF16

D_MODEL = 1024
N_HEADS = 8
HEAD_DIM = 64
WIDTH = N_HEADS * HEAD_DIM
MOBA_BLOCK = 256
MOBA_TOPK = 3
DECAY_LORA = 64
ICLR_LORA = 64
VMIX_LORA = 32
RMS_EPS = 1e-6
GN_EPS = 64e-5
L2_EPS = 1e-12
NEG_INF = -1e30

LANES = 128
PAIR = 2 * HEAD_DIM
N_PAIRS = WIDTH // PAIR
CHUNK = 64

COL_Q, COL_K, COL_V, COL_AZ = 0, 512, 1024, 1536
COL_R, COL_RK, COL_RV, COL_RZ = 2048, 2560, 3072, 3584
COL_GATT, COL_GRW = 4096, 5120
COL_LORA = 6144
COL_VD = 6272
PROJ_W = 6400

VMEM_LIMIT = 48 * 1024 * 1024


def _sigmoid(x):
    return 1.0 / (1.0 + jnp.exp(-x))


def _mm(a, b):
    return jnp.dot(a.astype(BF16), b.astype(BF16), preferred_element_type=F32)


def _mm_nt(a, b):
    return lax.dot_general(a.astype(BF16), b.astype(BF16), (((1,), (1,)), ((), ())),
                           preferred_element_type=F32)


def _mm_tn(a, b):
    return lax.dot_general(a.astype(BF16), b.astype(BF16), (((0,), (0,)), ((), ())),
                           preferred_element_type=F32)


def _inproj_kernel(x_ref, g_ref, w_ref, o_ref, h_ref):
    @pl.when(pl.program_id(1) == 0)
    def _():
        x = x_ref[...]
        ms = jnp.mean(x * x, axis=-1, keepdims=True)
        h_ref[...] = (x * lax.rsqrt(ms + RMS_EPS) * g_ref[...]).astype(BF16)

    o_ref[...] = jnp.dot(h_ref[...], w_ref[...], preferred_element_type=F32).astype(o_ref.dtype)


def _inproj(x2d, g, w_cat, *, tm=1024, tn=1280):
    t, d = x2d.shape
    n = w_cat.shape[1]
    return pl.pallas_call(
        _inproj_kernel,
        grid=(t // tm, n // tn),
        in_specs=[
            pl.BlockSpec((tm, d), lambda i, j: (i, 0)),
            pl.BlockSpec((1, d), lambda i, j: (0, 0)),
            pl.BlockSpec((d, tn), lambda i, j: (0, j)),
        ],
        out_specs=pl.BlockSpec((tm, tn), lambda i, j: (i, j)),
        out_shape=jax.ShapeDtypeStruct((t, n), ACT),
        scratch_shapes=[pltpu.VMEM((tm, d), BF16)],
        compiler_params=pltpu.CompilerParams(
            dimension_semantics=("parallel", "arbitrary"), vmem_limit_bytes=VMEM_LIMIT),
        name="inproj",
    )(x2d, g.reshape(1, d), w_cat)


def _moba_kernel(q_ref, k_ref, v_ref, o_ref, vt_ref, qx_ref, kx_ref, s_ref, p_ref, *, n_blocks):
    hp = pl.program_id(1)
    blk = MOBA_BLOCK
    s_len = n_blocks * blk
    log2e = 1.4426950408889634
    q_scale = (HEAD_DIM ** -0.5) * log2e
    nt = (((1,), (1,)), ((), ()))
    sub = 32

    vt_ref[...] = v_ref[...].T

    mean_row = lax.broadcasted_iota(jnp.int32, (n_blocks, PAIR), 0)
    kmean = jnp.zeros((n_blocks, PAIR), F32)
    for n in range(n_blocks):
        kmean = jnp.where(mean_row == n,
                          jnp.mean(k_ref[n * blk:(n + 1) * blk, :].astype(F32), axis=0, keepdims=True), kmean)

    lane = lax.broadcasted_iota(jnp.int32, (1, PAIR), 1)
    head_lanes = (lane < HEAD_DIM, lane >= HEAD_DIM)
    nrow = lax.broadcasted_iota(jnp.int32, (n_blocks, s_len), 0)
    pos_i = lax.broadcasted_iota(jnp.int32, (n_blocks, s_len), 1)
    pblk = pos_i // blk
    pos = pos_i.astype(F32)
    past = nrow < pblk
    krow = lax.broadcasted_iota(jnp.int32, (blk, blk), 0)
    qcol = lax.broadcasted_iota(jnp.int32, (blk, blk), 1)
    causal = qcol >= krow

    km = jnp.concatenate([jnp.where(head_lanes[0], kmean, 0.0), jnp.where(head_lanes[1], kmean, 0.0)], axis=0)
    km_hi = km.astype(BF16)
    km_r = km - km_hi.astype(F32)
    km_mid = km_r.astype(BF16)
    km_lo = (km_r - km_mid.astype(F32)).astype(BF16)
    q_all = q_ref[...]
    gates = (lax.dot_general(km_lo, q_all, nt, preferred_element_type=F32)
             + lax.dot_general(km_mid, q_all, nt, preferred_element_type=F32)
             + lax.dot_general(km_hi, q_all, nt, preferred_element_type=F32))

    def split3(x):
        hi = x.astype(BF16).astype(F32)
        mid = (x - hi).astype(BF16).astype(F32)
        return hi, mid, x - hi - mid

    def rows8(r0, r1, r2, r3, r4, r5):
        out = jnp.zeros((n_blocks, s_len), F32)
        for idx, r in enumerate((r0, r1, r2, r3, r4, r5)):
            out = jnp.where(nrow == idx, r, out)
        return out

    ones = jnp.ones((n_blocks, s_len), F32)
    for h in range(2):
        head = lax.convert_element_type(hp * 2 + h + 1, F32)
        slope2 = jnp.exp2(jnp.zeros((1, 1), F32) - (8.0 / N_HEADS) * head) * log2e

        gm = jnp.where(past, gates[h * n_blocks:(h + 1) * n_blocks], NEG_INF)
        rank = jnp.zeros((n_blocks, s_len), F32)
        for m in range(n_blocks):
            gcol = gm[m:m + 1, :]
            beats = (gcol > gm) | ((gcol == gm) & (nrow > m))
            rank = rank + jnp.where(beats, 1.0, 0.0)
        visible = (past & (rank < MOBA_TOPK)) | (nrow == pblk)

        q_hi, q_mid, q_lo = split3(-slope2 * pos)
        k_hi, k_mid, k_lo = split3(slope2 * pos)
        q_extra = jnp.concatenate([rows8(q_hi, q_mid, q_lo, ones, ones, ones),
                                   jnp.where(visible, 0.0, NEG_INF)], axis=0)
        k_extra = jnp.concatenate([rows8(ones, ones, ones, k_hi, k_mid, k_lo),
                                   jnp.where(nrow == pblk, 1.0, 0.0)], axis=0)
        lead = jnp.zeros(((1 - h) * HEAD_DIM, s_len), F32)
        tail = jnp.zeros((PAIR - 2 * n_blocks - (1 - h) * HEAD_DIM, s_len), F32)
        q_full = jnp.concatenate([lead, q_extra, tail], axis=0) if h == 0 else jnp.concatenate([q_extra, tail], axis=0)
        k_full = jnp.concatenate([lead, k_extra, tail], axis=0) if h == 0 else jnp.concatenate([k_extra, tail], axis=0)
        qx_ref[h] = q_full.T.astype(BF16)
        kx_ref[h] = jnp.where(head_lanes[h], k_ref[...], k_full.T.astype(BF16))

    units = [(i, h) for i in range(n_blocks) for h in range(2)]
    st = {}

    def score_tile(u, j):
        i, h = u
        d = st[u]
        s_tile = lax.dot_general(kx_ref[h, j * blk:(j + 1) * blk, :], d["qm"], nt,
                                 preferred_element_type=F32)
        for c in range(0, blk, sub):
            sc = s_tile[c:c + sub]
            if j == i:
                sc = jnp.where(causal[c:c + sub], sc, NEG_INF)
            s_ref[d["slot"], j * blk + c:j * blk + c + sub, :] = sc
            for r in range(0, sub, 8):
                d["m8"] = sc[r:r + 8] if d["m8"] is None else jnp.maximum(d["m8"], sc[r:r + 8])

    def softmax_tile(u, j):
        d = st[u]
        for c in range(j * blk, (j + 1) * blk, sub):
            pc = jnp.exp2(s_ref[d["slot"], c:c + sub, :] - d["m"])
            for r in range(0, sub, 8):
                d["l8"] = d["l8"] + pc[r:r + 8]
            p_ref[d["slot"], c:c + sub, :] = pc.astype(BF16)

    def finish(u):
        i, h = u
        d = st[u]
        n_k = (i + 1) * blk
        l_sum = jnp.sum(d["l8"], axis=0, keepdims=True)
        pv = jnp.dot(vt_ref[h * HEAD_DIM:(h + 1) * HEAD_DIM, :n_k], p_ref[d["slot"], :n_k, :],
                     preferred_element_type=F32)
        d["out"] = pv * (1.0 / l_sum)
        if h == 1:
            out_t = jnp.concatenate([st[(i, 0)]["out"], d["out"]], axis=0)
            o_ref[i * blk:(i + 1) * blk, :] = out_t.T.astype(o_ref.dtype)

    prev = None
    for idx, u in enumerate(units):
        i, h = u
        rows = slice(i * blk, (i + 1) * blk)
        if h == 0:
            q_i = (q_ref[rows, :].astype(F32) * q_scale).astype(BF16)
        st[u] = {"qm": jnp.where(head_lanes[h], q_i, qx_ref[h, rows, :]), "slot": idx % 2,
                 "m8": None, "l8": jnp.zeros((8, blk), F32)}
        n_prev = prev[0] + 1 if prev is not None else 0
        for step in range(max(i + 1, n_prev)):
            if step < i + 1:
                score_tile(u, step)
            if step < n_prev:
                softmax_tile(prev, step)
        if prev is not None:
            finish(prev)
        st[u]["m"] = jnp.max(st[u]["m8"], axis=0, keepdims=True)
        prev = u
    for step in range(prev[0] + 1):
        softmax_tile(prev, step)
    finish(prev)


def _moba(proj):
    b, s, _ = proj.shape
    n_blocks = s // MOBA_BLOCK
    qo, ko, vo = COL_Q // PAIR, COL_K // PAIR, COL_V // PAIR
    return pl.pallas_call(
        functools.partial(_moba_kernel, n_blocks=n_blocks),
        grid=(b, N_PAIRS),
        in_specs=[
            pl.BlockSpec((None, s, PAIR), lambda i, p: (i, 0, qo + p)),
            pl.BlockSpec((None, s, PAIR), lambda i, p: (i, 0, ko + p)),
            pl.BlockSpec((None, s, PAIR), lambda i, p: (i, 0, vo + p)),
        ],
        out_specs=pl.BlockSpec((None, s, PAIR), lambda i, p: (i, 0, p)),
        out_shape=jax.ShapeDtypeStruct((b, s, WIDTH), ACT),
        scratch_shapes=[
            pltpu.VMEM((PAIR, s), BF16),
            pltpu.VMEM((2, s, PAIR), BF16),
            pltpu.VMEM((2, s, PAIR), BF16),
            pltpu.VMEM((2, s, MOBA_BLOCK), F32),
            pltpu.VMEM((2, s, MOBA_BLOCK), BF16),
        ],
        compiler_params=pltpu.CompilerParams(
            dimension_semantics=("parallel", "parallel"), vmem_limit_bytes=VMEM_LIMIT),
        name="moba",
    )(proj, proj, proj)


def _rwkv_scan(rs, kmod, vr, na, nb, lw, state_ref, ts):
    n_chunks = ts // CHUNK
    n2 = 2 * CHUNK
    trow = lax.broadcasted_iota(jnp.int32, (ts, ts), 0)
    tcol = lax.broadcasted_iota(jnp.int32, (ts, ts), 1)
    ltri = jnp.where((trow // CHUNK == tcol // CHUNK) & (tcol <= trow), 1.0, 0.0).astype(BF16)
    head0 = lax.broadcasted_iota(jnp.int32, (CHUNK, PAIR), 1) < HEAD_DIM
    wrow = lax.broadcasted_iota(jnp.int32, (CHUNK, PAIR), 0)
    wcol = lax.broadcasted_iota(jnp.int32, (CHUNK, PAIR), 1) % CHUNK
    strict = wcol < wrow
    incl = wcol <= wrow
    eye = jnp.where(wcol == wrow, 1.0, 0.0).astype(F32)
    srow = lax.broadcasted_iota(jnp.int32, (n2, n2), 0)
    scol = lax.broadcasted_iota(jnp.int32, (n2, n2), 1)
    same_head = (srow // CHUNK) == (scol // CHUNK)

    def stack(x):
        zero = jnp.zeros_like(x)
        return jnp.concatenate([jnp.where(head0, x, zero), jnp.where(head0, zero, x)], axis=0)

    lw_hi = lw.astype(BF16)
    lw_r = lw - lw_hi.astype(F32)
    lw_mid = lw_r.astype(BF16)
    lw_lo = (lw_r - lw_mid.astype(F32)).astype(BF16)
    cum = (jnp.dot(ltri, lw_lo, preferred_element_type=F32) + jnp.dot(ltri, lw_mid, preferred_element_type=F32)
           + jnp.dot(ltri, lw_hi, preferred_element_type=F32))
    a_t = na * jnp.exp(cum - lw)
    r_t = rs * jnp.exp(cum)
    e_neg = jnp.exp(-cum)
    b_t = nb * e_neg
    k_t = kmod * e_neg

    pcs = [(p, c) for c in range(n_chunks) for p in range(N_PAIRS)]
    lhs_g, rhs_g, a_s, r_n, b_h, k_h, v_s, gam = {}, {}, {}, {}, {}, {}, {}, {}
    for c in range(n_chunks):
        rc = slice(c * CHUNK, (c + 1) * CHUNK)
        tot = cum[(c + 1) * CHUNK - 1:(c + 1) * CHUNK, :]
        e_rel = jnp.exp(tot - cum[rc])
        bh_c = nb[rc] * e_rel
        kh_c = kmod[rc] * e_rel
        gam_c = jnp.exp(tot)
        for p in range(N_PAIRS):
            sl = slice(p * PAIR, (p + 1) * PAIR)
            pc = (p, c)
            a_n = a_t[rc, sl].astype(BF16)
            r_n[pc] = r_t[rc, sl]
            lhs_g[pc] = jnp.concatenate([a_n, r_n[pc].astype(BF16)], axis=0)
            rhs_g[pc] = jnp.concatenate([stack(b_t[rc, sl].astype(BF16)),
                                         stack(k_t[rc, sl].astype(BF16))], axis=0)
            a_s[pc] = stack(a_n)
            b_h[pc] = bh_c[:, sl].astype(BF16)
            k_h[pc] = stack(kh_c[:, sl])
            v_s[pc] = stack(vr[rc, sl].astype(BF16))
            gam[pc] = gam_c[:, sl]

    a_ab, a_ak, a_rb, a_rk = {}, {}, {}, {}
    for pc in pcs:
        g = _mm_nt(lhs_g[pc], rhs_g[pc])
        a_ab[pc] = jnp.where(strict, g[:CHUNK, :n2], 0.0)
        a_ak[pc] = jnp.where(strict, g[:CHUNK, n2:], 0.0).astype(BF16)
        a_rb[pc] = jnp.where(incl, g[CHUNK:, :n2], 0.0).astype(BF16)
        a_rk[pc] = jnp.where(incl, g[CHUNK:, n2:], 0.0)

    t_inv = {pc: eye + a_ab[pc] for pc in pcs}
    pw = {pc: a_ab[pc].astype(BF16) for pc in pcs}
    pw = {pc: _mm(pw[pc], stack(pw[pc])).astype(BF16) for pc in pcs}
    for _ in range(CHUNK.bit_length() - 3):
        both = {pc: _mm(jnp.concatenate([pw[pc], t_inv[pc].astype(BF16)], axis=0), stack(pw[pc])) for pc in pcs}
        t_inv = {pc: t_inv[pc] + both[pc][CHUNK:] for pc in pcs}
        pw = {pc: both[pc][:CHUNK].astype(BF16) for pc in pcs}
    t_inv = {pc: (t_inv[pc] + _mm(t_inv[pc], stack(pw[pc]))).astype(BF16) for pc in pcs}

    z = {pc: jnp.where(same_head, _mm_tn(t_inv[pc], b_h[pc]), 0.0).astype(BF16) for pc in pcs}
    w = {pc: _mm(a_rb[pc], stack(t_inv[pc])).astype(BF16) for pc in pcs}
    la_ak = {pc: jnp.concatenate([a_s[pc], stack(a_ak[pc])], axis=1) for pc in pcs}
    my = {pc: _mm_tn(la_ak[pc], z[pc]) for pc in pcs}
    wq = {pc: _mm(w[pc], la_ak[pc]) for pc in pcs}
    m_t = {pc: my[pc][:n2].astype(BF16) for pc in pcs}
    y = {pc: (my[pc][n2:] + k_h[pc]).astype(BF16) for pc in pcs}
    q_p = {pc: (r_n[pc] + wq[pc][:, :n2]).astype(BF16) for pc in pcs}
    p_p = {pc: (wq[pc][:, n2:] + a_rk[pc]).astype(BF16) for pc in pcs}
    g_s = {pc: _mm_tn(v_s[pc], y[pc]) for pc in pcs}
    o_loc = {pc: _mm(p_p[pc], v_s[pc]) for pc in pcs}

    states = [state_ref[p] for p in range(N_PAIRS)]
    outs = [[] for _ in range(N_PAIRS)]
    for c in range(n_chunks):
        for p in range(N_PAIRS):
            pc = (p, c)
            sb = states[p].astype(BF16)
            outs[p].append(_mm_nt(q_p[pc], sb) + o_loc[pc])
            states[p] = states[p] * gam[pc] + _mm(sb, m_t[pc]) + g_s[pc]
    for p in range(N_PAIRS):
        state_ref[p] = states[p]
    return [jnp.concatenate(o, axis=0) if len(o) > 1 else o[0] for o in outs]


def _rwkv_kernel(*refs, has_vmix, ts):
    if has_vmix:
        (r_ref, k_ref, v_ref, rz_ref, lora_ref, vd_ref, vfirst_ref,
         mu_rkv_ref, mu_lora_ref, w0a0_ref, lora_w_ref, kk_ref, ka_ref, rk_ref, lng_ref, lnb_ref,
         vmix_mu_ref, vmix_up_ref, vmix0_ref,
         yb_ref, carry_rkv, carry_lora, carry_vd, state_ref) = refs
    else:
        (r_ref, k_ref, v_ref, rz_ref, lora_ref,
         mu_rkv_ref, mu_lora_ref, w0a0_ref, lora_w_ref, kk_ref, ka_ref, rk_ref, lng_ref, lnb_ref,
         yb_ref, vfirst_ref, carry_rkv, carry_lora, state_ref) = refs
        carry_vd = None

    @pl.when(pl.program_id(1) == 0)
    def _():
        carry_rkv[...] = jnp.zeros_like(carry_rkv)
        carry_lora[...] = jnp.zeros_like(carry_lora)
        if carry_vd is not None:
            carry_vd[...] = jnp.zeros_like(carry_vd)
        state_ref[...] = jnp.zeros_like(state_ref)

    sub_row = lax.broadcasted_iota(jnp.int32, (ts, ts), 0)
    sub_col = lax.broadcasted_iota(jnp.int32, (ts, ts), 1)
    shift_mat = jnp.where(sub_row == sub_col + 1, 1.0, 0.0).astype(r_ref.dtype)
    top_row = lax.broadcasted_iota(jnp.int32, (8, 1), 0) == 0

    def shift(y_act, carry_ref, row_idx, mu):
        y = y_act.astype(F32)
        prev = jnp.dot(shift_mat, y_act, preferred_element_type=F32)
        head = jnp.where(top_row, carry_ref[row_idx:row_idx + 1, :], prev[:8])
        prev = jnp.concatenate([head, prev[8:]], axis=0)
        carry_ref[row_idx:row_idx + 1, :] = y[ts - 1:ts, :]
        return y + (prev - y) * mu

    mu_rkv = mu_rkv_ref[...]
    rs = shift(r_ref[...], carry_rkv, 0, mu_rkv[0:1, :])
    ks = shift(k_ref[...], carry_rkv, 1, mu_rkv[1:2, :])
    vs = shift(v_ref[...], carry_rkv, 2, mu_rkv[2:3, :])
    lo = shift(lora_ref[...], carry_lora, 0, mu_lora_ref[...])

    lane = lax.broadcasted_iota(jnp.int32, (ts, LANES), 1)
    z = jnp.where(lane < DECAY_LORA, jnp.tanh(lo), lo)
    wa = w0a0_ref[...] + _mm(z, lora_w_ref[...])
    lw = (-0.6065306597126334) * _sigmoid(wa[:, :WIDTH])
    a = _sigmoid(wa[:, WIDTH:])

    if has_vmix:
        vd = shift(vd_ref[...], carry_vd, 0, vmix_mu_ref[...])
        mix = _sigmoid(vmix0_ref[...] + _mm(vd, vmix_up_ref[...]))
        vr = vs + (vfirst_ref[...] - vs) * mix
    else:
        vfirst_ref[...] = vs
        vr = vs

    kx = ks * kk_ref[...]
    kmod = ks * (1.0 + (a - 1.0) * ka_ref[...])
    rkr = rs * kmod * rk_ref[...]
    rz = rz_ref[...].astype(F32)
    ln_g = lng_ref[...]
    ln_b = lnb_ref[...]

    head0_t = lane < HEAD_DIM

    def headsum(x):
        s0 = jnp.sum(jnp.where(head0_t, x, 0.0), axis=-1, keepdims=True)
        s1 = jnp.sum(jnp.where(head0_t, 0.0, x), axis=-1, keepdims=True)
        return jnp.where(head0_t, s0, s1)

    kk_all = []
    for p in range(N_PAIRS):
        kxp = kx[:, p * PAIR:(p + 1) * PAIR]
        kk_all.append(kxp * jnp.minimum(lax.rsqrt(headsum(kxp * kxp)), 1.0 / L2_EPS))
    kk = jnp.concatenate(kk_all, axis=1)

    o_pairs = _rwkv_scan(rs, kmod, vr, -kk, kk * a, lw, state_ref, ts)

    for p in range(N_PAIRS):
        sl = slice(p * PAIR, (p + 1) * PAIR)
        o_p = o_pairs[p]
        bonus = headsum(rkr[:, sl]) * vr[:, sl]
        mu = headsum(o_p) * (1.0 / HEAD_DIM)
        d = o_p - mu
        var = headsum(d * d) * (1.0 / HEAD_DIM)
        on = d * lax.rsqrt(var + GN_EPS) * ln_g[:, sl] + ln_b[:, sl]
        rz_p = rz[:, sl]
        yb_ref[:, sl] = ((on + bonus) * (rz_p * _sigmoid(rz_p))).astype(yb_ref.dtype)


def _rwkv(proj, vfirst, p, *, ts=256):
    b, s, _ = proj.shape
    has_vmix = vfirst is not None
    row = lambda width, col: pl.BlockSpec((None, ts, width), lambda i, t: (i, t, col // width))
    const = lambda shape: pl.BlockSpec(shape, lambda i, t: (0,) * len(shape))
    act = pl.BlockSpec((None, ts, WIDTH), lambda i, t: (i, t, 0))

    in_specs = [row(WIDTH, COL_R), row(WIDTH, COL_RK), row(WIDTH, COL_RV), row(WIDTH, COL_RZ),
                row(LANES, COL_LORA)]
    args = [proj, proj, proj, proj, proj]
    if has_vmix:
        in_specs += [row(LANES, COL_VD), act]
        args += [proj, vfirst]
    in_specs += [const((3, WIDTH)), const((1, LANES)), const((1, 2 * WIDTH)), const((LANES, 2 * WIDTH))]
    args += [p["mu_rkv"], p["mu_lora"], p["w0a0"], p["lora_w"]]
    for name in ("k_k", "k_a", "r_k", "ln_g", "ln_b"):
        in_specs.append(const((1, WIDTH)))
        args.append(p[name])
    if has_vmix:
        in_specs += [const((1, LANES)), const((LANES, WIDTH)), const((1, WIDTH))]
        args += [p["vmix_mu"], p["vmix_up"], p["vmix0"]]

    out_shape = [jax.ShapeDtypeStruct((b, s, WIDTH), ACT)]
    out_specs = [act]
    scratch = [pltpu.VMEM((8, WIDTH), F32), pltpu.VMEM((8, LANES), F32)]
    if has_vmix:
        scratch.append(pltpu.VMEM((8, LANES), F32))
    else:
        out_shape.append(jax.ShapeDtypeStruct((b, s, WIDTH), F32))
        out_specs.append(act)
    scratch.append(pltpu.VMEM((N_PAIRS, PAIR, PAIR), F32))

    res = pl.pallas_call(
        functools.partial(_rwkv_kernel, has_vmix=has_vmix, ts=ts),
        grid=(b, s // ts),
        in_specs=in_specs,
        out_specs=out_specs,
        out_shape=out_shape,
        scratch_shapes=scratch,
        compiler_params=pltpu.CompilerParams(
            dimension_semantics=("parallel", "arbitrary"), vmem_limit_bytes=VMEM_LIMIT),
        name="rwkv_vmix" if has_vmix else "rwkv",
    )(*args)
    return (res[0], vfirst) if has_vmix else (res[0], res[1])


def _outproj_kernel(ya_ref, az_ref, yb_ref, gatt_ref, grw_ref, x_ref,
                    wua_ref, wub_ref, wout_ref, g_ref, o_ref):
    az = az_ref[...].astype(F32)
    ya = ya_ref[...].astype(F32) * (az * _sigmoid(az))
    u = (_sigmoid(gatt_ref[...].astype(F32)) * _mm(ya, wua_ref[...])
         + _sigmoid(grw_ref[...].astype(F32)) * _mm(yb_ref[...], wub_ref[...]))
    y = _mm(u, wout_ref[...])
    ms = jnp.mean(y * y, axis=-1, keepdims=True)
    o_ref[...] = x_ref[...] + y * lax.rsqrt(ms + RMS_EPS) * g_ref[...]


def _outproj(ya, proj2d, yb, x2d, w_up_att, w_up_rw, w_out, g, *, tm=512):
    t, d = x2d.shape
    row = lambda width, col: pl.BlockSpec((tm, width), lambda i: (i, col // width))
    const = lambda shape: pl.BlockSpec(shape, lambda i: (0, 0))
    return pl.pallas_call(
        _outproj_kernel,
        grid=(t // tm,),
        in_specs=[row(WIDTH, 0), row(WIDTH, COL_AZ), row(WIDTH, 0), row(d, COL_GATT), row(d, COL_GRW),
                  row(d, 0), const((WIDTH, d)), const((WIDTH, d)), const((d, d)), const((1, d))],
        out_specs=row(d, 0),
        out_shape=jax.ShapeDtypeStruct((t, d), F32),
        compiler_params=pltpu.CompilerParams(
            dimension_semantics=("parallel",), vmem_limit_bytes=VMEM_LIMIT),
        name="outproj",
    )(ya, proj2d, yb, proj2d, proj2d, x2d, w_up_att, w_up_rw, w_out, g.reshape(1, d))


def _layer_params(l, w_in, rw_mu, rw_w0, rw_w_up, rw_a0, rw_a_up, rw_k_k, rw_k_a, rw_r_k,
                  rw_ln_g, rw_ln_b, rw_vmix_down, rw_vmix_mu, rw_vmix_up, rw_vmix0):
    w = w_in[l]
    d = w.shape[0]
    att = w[:, :4 * WIDTH]
    o = 4 * WIDTH
    rkv = w[:, o:o + 3 * WIDTH]
    lora = w[:, o + 3 * WIDTH:o + 3 * WIDTH + DECAY_LORA + ICLR_LORA]
    o = o + 3 * WIDTH + DECAY_LORA + ICLR_LORA
    rz = w[:, o:o + WIDTH]
    gates = w[:, o + WIDTH:]
    vd = jnp.zeros((d, LANES), F32)
    if l > 0:
        vd = vd.at[:, :VMIX_LORA].set(rw_vmix_down[l - 1])
    w_cat = jnp.concatenate([att, rkv, rz, gates, lora, vd], axis=1).astype(BF16)

    mu = rw_mu[l]
    zeros = jnp.zeros((DECAY_LORA, WIDTH), F32)
    lora_w = jnp.concatenate([jnp.concatenate([rw_w_up[l], zeros], axis=1),
                              jnp.concatenate([zeros, rw_a_up[l]], axis=1)], axis=0).astype(BF16)
    p = {
        "w_cat": w_cat,
        "mu_rkv": mu[:3 * WIDTH].reshape(3, WIDTH),
        "mu_lora": mu[3 * WIDTH:].reshape(1, LANES),
        "w0a0": jnp.concatenate([rw_w0[l], rw_a0[l]]).reshape(1, 2 * WIDTH),
        "lora_w": lora_w,
        "k_k": rw_k_k[l].reshape(1, WIDTH), "k_a": rw_k_a[l].reshape(1, WIDTH),
        "r_k": rw_r_k[l].reshape(1, WIDTH),
        "ln_g": rw_ln_g[l].reshape(1, WIDTH), "ln_b": rw_ln_b[l].reshape(1, WIDTH),
    }
    if l > 0:
        p["vmix_mu"] = jnp.zeros((1, LANES), F32).at[0, :VMIX_LORA].set(rw_vmix_mu[l - 1])
        p["vmix_up"] = jnp.zeros((LANES, WIDTH), F32).at[:VMIX_LORA].set(rw_vmix_up[l - 1]).astype(BF16)
        p["vmix0"] = rw_vmix0[l - 1].reshape(1, WIDTH)
    return p


def kernel(x, norm_pre, norm_post, w_in, rw_mu, rw_w0, rw_w_up, rw_a0, rw_a_up, rw_k_k, rw_k_a, rw_r_k, rw_ln_g, rw_ln_b, rw_vmix_down, rw_vmix_mu, rw_vmix_up, rw_vmix0, w_up_att, w_up_rw, w_out):
    b, s, d = x.shape
    assert d == D_MODEL and s % MOBA_BLOCK == 0
    depth = w_in.shape[0]
    x2d = x.reshape(b * s, d)
    vfirst = None
    for l in range(depth):
        p = _layer_params(l, w_in, rw_mu, rw_w0, rw_w_up, rw_a0, rw_a_up, rw_k_k, rw_k_a, rw_r_k,
                          rw_ln_g, rw_ln_b, rw_vmix_down, rw_vmix_mu, rw_vmix_up, rw_vmix0)
        proj2d = _inproj(x2d, norm_pre[l], p["w_cat"])
        proj = proj2d.reshape(b, s, PROJ_W)
        ya = _moba(proj)
        yb, vfirst = _rwkv(proj, vfirst, p)
        x2d = _outproj(ya.reshape(b * s, WIDTH), proj2d, yb.reshape(b * s, WIDTH), x2d,
                       w_up_att[l].astype(BF16), w_up_rw[l].astype(BF16), w_out[l].astype(BF16),
                       norm_post[l])
    return x2d.reshape(b, s, d)
```

```python
import functools

import jax
import jax.numpy as jnp
from jax import lax
from jax.experimental import pallas as pl
from jax.experimental.pallas import tpu as pltpu

F32 = jnp.float32
BF16 = jnp.bfloat16
ACT = BF16

D_MODEL = 1024
N_HEADS = 8
HEAD_DIM = 64
WIDTH = N_HEADS * HEAD_DIM
MOBA_BLOCK = 256
MOBA_TOPK = 3
DECAY_LORA = 64
ICLR_LORA = 64
VMIX_LORA = 32
RMS_EPS = 1e-6
GN_EPS = 64e-5
L2_EPS = 1e-12
NEG_INF = -1e30

LANES = 128
PAIR = 2 * HEAD_DIM
N_PAIRS = WIDTH // PAIR
CHUNK = 64

COL_Q, COL_K, COL_V, COL_AZ = 0, 512, 1024, 1536
COL_R, COL_RK, COL_RV, COL_RZ = 2048, 2560, 3072, 3584
COL_GATT, COL_GRW = 4096, 5120
COL_LORA = 6144
COL_VD = 6272
PROJ_W = 6400

VMEM_LIMIT = 48 * 1024 * 1024


def _sigmoid(x):
    return 0.5 * jnp.tanh(0.5 * x) + 0.5


def _mm(a, b):
    return jnp.dot(a.astype(BF16), b.astype(BF16), preferred_element_type=F32)


def _mm_nt(a, b):
    return lax.dot_general(a.astype(BF16), b.astype(BF16), (((1,), (1,)), ((), ())),
                           preferred_element_type=F32)


def _mm_tn(a, b):
    return lax.dot_general(a.astype(BF16), b.astype(BF16), (((0,), (0,)), ((), ())),
                           preferred_element_type=F32)


def _inproj_kernel(x_ref, g_ref, w_ref, o_ref, *, tn):
    x = x_ref[...]
    ms = jnp.mean(x * x, axis=-1, keepdims=True)
    h = (x * lax.rsqrt(ms + RMS_EPS) * g_ref[...]).astype(BF16)
    for n0 in range(0, o_ref.shape[1], tn):
        o_ref[:, n0:n0 + tn] = jnp.dot(h, w_ref[:, n0:n0 + tn],
                                       preferred_element_type=F32).astype(o_ref.dtype)


def _inproj(x2d, g, w_cat, *, tm=512, tn=1280):
    t, d = x2d.shape
    n = w_cat.shape[1]
    return pl.pallas_call(
        functools.partial(_inproj_kernel, tn=tn),
        grid=(t // tm,),
        in_specs=[
            pl.BlockSpec((tm, d), lambda i: (i, 0)),
            pl.BlockSpec((1, d), lambda i: (0, 0)),
            pl.BlockSpec((d, n), lambda i: (0, 0), pipeline_mode=pl.Buffered(1)),
        ],
        out_specs=pl.BlockSpec((tm, n), lambda i: (i, 0)),
        out_shape=jax.ShapeDtypeStruct((t, n), ACT),
        compiler_params=pltpu.CompilerParams(
            dimension_semantics=("parallel",), vmem_limit_bytes=VMEM_LIMIT),
        name="inproj",
    )(x2d, g.reshape(1, d), w_cat)


def _moba_kernel(q_ref, k_ref, v_ref, o_ref, vt_ref, qx_ref, kx_ref, s_ref, p_ref, *, n_blocks):
    hp = pl.program_id(1)
    blk = MOBA_BLOCK
    s_len = n_blocks * blk
    log2e = 1.4426950408889634
    q_scale = (HEAD_DIM ** -0.5) * log2e
    nt = (((1,), (1,)), ((), ()))
    sub = 32

    vt_ref[...] = v_ref[...].T

    mean_row = lax.broadcasted_iota(jnp.int32, (n_blocks, PAIR), 0)
    kmean = jnp.zeros((n_blocks, PAIR), F32)
    for n in range(n_blocks):
        kmean = jnp.where(mean_row == n,
                          jnp.mean(k_ref[n * blk:(n + 1) * blk, :].astype(F32), axis=0, keepdims=True), kmean)

    lane = lax.broadcasted_iota(jnp.int32, (1, PAIR), 1)
    head_lanes = (lane < HEAD_DIM, lane >= HEAD_DIM)
    nrow = lax.broadcasted_iota(jnp.int32, (n_blocks, s_len), 0)
    pos_i = lax.broadcasted_iota(jnp.int32, (n_blocks, s_len), 1)
    pblk = pos_i // blk
    pos = pos_i.astype(F32)
    past = nrow < pblk
    krow = lax.broadcasted_iota(jnp.int32, (blk, blk), 0)
    qcol = lax.broadcasted_iota(jnp.int32, (blk, blk), 1)
    causal = qcol >= krow

    km = jnp.concatenate([jnp.where(head_lanes[0], kmean, 0.0), jnp.where(head_lanes[1], kmean, 0.0)], axis=0)
    km_hi = km.astype(BF16)
    km_r = km - km_hi.astype(F32)
    km_mid = km_r.astype(BF16)
    km_lo = (km_r - km_mid.astype(F32)).astype(BF16)
    q_all = q_ref[...]
    gates = (lax.dot_general(km_lo, q_all, nt, preferred_element_type=F32)
             + lax.dot_general(km_mid, q_all, nt, preferred_element_type=F32)
             + lax.dot_general(km_hi, q_all, nt, preferred_element_type=F32))

    def split3(x):
        hi = x.astype(BF16).astype(F32)
        mid = (x - hi).astype(BF16).astype(F32)
        return hi, mid, x - hi - mid

    def rows8(r0, r1, r2, r3, r4, r5):
        out = jnp.zeros((n_blocks, s_len), F32)
        for idx, r in enumerate((r0, r1, r2, r3, r4, r5)):
            out = jnp.where(nrow == idx, r, out)
        return out

    ones = jnp.ones((n_blocks, s_len), F32)
    for h in range(2):
        head = lax.convert_element_type(hp * 2 + h + 1, F32)
        slope2 = jnp.exp2(jnp.zeros((1, 1), F32) - (8.0 / N_HEADS) * head) * log2e

        gm = jnp.where(past, gates[h * n_blocks:(h + 1) * n_blocks], NEG_INF)
        rank = jnp.zeros((n_blocks, s_len), F32)
        for m in range(n_blocks):
            gcol = gm[m:m + 1, :]
            beats = (gcol > gm) | ((gcol == gm) & (nrow > m))
            rank = rank + jnp.where(beats, 1.0, 0.0)
        visible = (past & (rank < MOBA_TOPK)) | (nrow == pblk)

        q_hi, q_mid, q_lo = split3(-slope2 * pos)
        k_hi, k_mid, k_lo = split3(slope2 * pos)
        q_extra = jnp.concatenate([rows8(q_hi, q_mid, q_lo, ones, ones, ones),
                                   jnp.where(visible, 0.0, NEG_INF)], axis=0)
        k_extra = jnp.concatenate([rows8(ones, ones, ones, k_hi, k_mid, k_lo),
                                   jnp.where(nrow == pblk, 1.0, 0.0)], axis=0)
        lead = jnp.zeros(((1 - h) * HEAD_DIM, s_len), F32)
        tail = jnp.zeros((PAIR - 2 * n_blocks - (1 - h) * HEAD_DIM, s_len), F32)
        q_full = jnp.concatenate([lead, q_extra, tail], axis=0) if h == 0 else jnp.concatenate([q_extra, tail], axis=0)
        k_full = jnp.concatenate([lead, k_extra, tail], axis=0) if h == 0 else jnp.concatenate([k_extra, tail], axis=0)
        qx_ref[h] = q_full.T.astype(BF16)
        kx_ref[h] = jnp.where(head_lanes[h], k_ref[...], k_full.T.astype(BF16))

    units = [(i, h) for i in range(n_blocks) for h in range(2)]
    st = {}

    def score_tile(u, j):
        i, h = u
        d = st[u]
        s_tile = lax.dot_general(kx_ref[h, j * blk:(j + 1) * blk, :], d["qm"], nt,
                                 preferred_element_type=F32)
        for c in range(0, blk, sub):
            sc = s_tile[c:c + sub]
            if j == i:
                sc = jnp.where(causal[c:c + sub], sc, NEG_INF)
            s_ref[d["slot"], j * blk + c:j * blk + c + sub, :] = sc
            for r in range(0, sub, 8):
                d["m8"] = sc[r:r + 8] if d["m8"] is None else jnp.maximum(d["m8"], sc[r:r + 8])

    def softmax_tile(u, j):
        d = st[u]
        for c in range(j * blk, (j + 1) * blk, sub):
            pc = jnp.exp2(s_ref[d["slot"], c:c + sub, :] - d["m"])
            for r in range(0, sub, 8):
                d["l8"] = d["l8"] + pc[r:r + 8]
            p_ref[d["slot"], c:c + sub, :] = pc.astype(BF16)

    def finish(u):
        i, h = u
        d = st[u]
        n_k = (i + 1) * blk
        l_sum = jnp.sum(d["l8"], axis=0, keepdims=True)
        pv = jnp.dot(vt_ref[h * HEAD_DIM:(h + 1) * HEAD_DIM, :n_k], p_ref[d["slot"], :n_k, :],
                     preferred_element_type=F32)
        d["out"] = pv * (1.0 / l_sum)
        if h == 1:
            out_t = jnp.concatenate([st[(i, 0)]["out"], d["out"]], axis=0)
            o_ref[i * blk:(i + 1) * blk, :] = out_t.T.astype(o_ref.dtype)

    prev = None
    for idx, u in enumerate(units):
        i, h = u
        rows = slice(i * blk, (i + 1) * blk)
        if h == 0:
            q_i = (q_ref[rows, :].astype(F32) * q_scale).astype(BF16)
        st[u] = {"qm": jnp.where(head_lanes[h], q_i, qx_ref[h, rows, :]), "slot": idx % 2,
                 "m8": None, "l8": jnp.zeros((8, blk), F32)}
        n_prev = prev[0] + 1 if prev is not None else 0
        for step in range(max(i + 1, n_prev)):
            if step < i + 1:
                score_tile(u, step)
            if step < n_prev:
                softmax_tile(prev, step)
        if prev is not None:
            finish(prev)
        st[u]["m"] = jnp.max(st[u]["m8"], axis=0, keepdims=True)
        prev = u
    for step in range(prev[0] + 1):
        softmax_tile(prev, step)
    finish(prev)


def _moba(proj):
    b, s, _ = proj.shape
    n_blocks = s // MOBA_BLOCK
    qo, ko, vo = COL_Q // PAIR, COL_K // PAIR, COL_V // PAIR
    return pl.pallas_call(
        functools.partial(_moba_kernel, n_blocks=n_blocks),
        grid=(b, N_PAIRS),
        in_specs=[
            pl.BlockSpec((None, s, PAIR), lambda i, p: (i, 0, qo + p)),
            pl.BlockSpec((None, s, PAIR), lambda i, p: (i, 0, ko + p)),
            pl.BlockSpec((None, s, PAIR), lambda i, p: (i, 0, vo + p)),
        ],
        out_specs=pl.BlockSpec((None, s, PAIR), lambda i, p: (i, 0, p)),
        out_shape=jax.ShapeDtypeStruct((b, s, WIDTH), ACT),
        scratch_shapes=[
            pltpu.VMEM((PAIR, s), BF16),
            pltpu.VMEM((2, s, PAIR), BF16),
            pltpu.VMEM((2, s, PAIR), BF16),
            pltpu.VMEM((2, s, MOBA_BLOCK), F32),
            pltpu.VMEM((2, s, MOBA_BLOCK), BF16),
        ],
        compiler_params=pltpu.CompilerParams(
            dimension_semantics=("parallel", "parallel"), vmem_limit_bytes=VMEM_LIMIT),
        name="moba",
    )(proj, proj, proj)


def _rwkv_scan(rs, kmod, vr, na, nb, lw, state_ref, ts):
    n_chunks = ts // CHUNK
    n2 = 2 * CHUNK
    trow = lax.broadcasted_iota(jnp.int32, (ts, ts), 0)
    tcol = lax.broadcasted_iota(jnp.int32, (ts, ts), 1)
    ltri = jnp.where((trow // CHUNK == tcol // CHUNK) & (tcol <= trow), 1.0, 0.0).astype(BF16)
    head0 = lax.broadcasted_iota(jnp.int32, (CHUNK, PAIR), 1) < HEAD_DIM
    wrow = lax.broadcasted_iota(jnp.int32, (CHUNK, PAIR), 0)
    wcol = lax.broadcasted_iota(jnp.int32, (CHUNK, PAIR), 1) % CHUNK
    strict = wcol < wrow
    incl = wcol <= wrow
    eye = jnp.where(wcol == wrow, 1.0, 0.0).astype(F32)
    srow = lax.broadcasted_iota(jnp.int32, (n2, n2), 0)
    scol = lax.broadcasted_iota(jnp.int32, (n2, n2), 1)
    same_head = (srow // CHUNK) == (scol // CHUNK)

    def stack(x):
        zero = jnp.zeros_like(x)
        return jnp.concatenate([jnp.where(head0, x, zero), jnp.where(head0, zero, x)], axis=0)

    lw_hi = lw.astype(BF16)
    lw_r = lw - lw_hi.astype(F32)
    lw_mid = lw_r.astype(BF16)
    lw_lo = (lw_r - lw_mid.astype(F32)).astype(BF16)
    cum = (jnp.dot(ltri, lw_lo, preferred_element_type=F32) + jnp.dot(ltri, lw_mid, preferred_element_type=F32)
           + jnp.dot(ltri, lw_hi, preferred_element_type=F32))
    a_t = na * jnp.exp(cum - lw)
    r_t = rs * jnp.exp(cum)
    e_neg = jnp.exp(-cum)
    b_t = nb * e_neg
    k_t = kmod * e_neg

    pcs = [(p, c) for c in range(n_chunks) for p in range(N_PAIRS)]
    lhs_g, rhs_g, a_s, r_n, b_h, k_h, v_s, gam = {}, {}, {}, {}, {}, {}, {}, {}
    for c in range(n_chunks):
        rc = slice(c * CHUNK, (c + 1) * CHUNK)
        tot = cum[(c + 1) * CHUNK - 1:(c + 1) * CHUNK, :]
        e_rel = jnp.exp(tot - cum[rc])
        bh_c = nb[rc] * e_rel
        kh_c = kmod[rc] * e_rel
        gam_c = jnp.exp(tot)
        for p in range(N_PAIRS):
            sl = slice(p * PAIR, (p + 1) * PAIR)
            pc = (p, c)
            a_n = a_t[rc, sl].astype(BF16)
            r_n[pc] = r_t[rc, sl]
            lhs_g[pc] = jnp.concatenate([a_n, r_n[pc].astype(BF16)], axis=0)
            rhs_g[pc] = jnp.concatenate([stack(b_t[rc, sl].astype(BF16)),
                                         stack(k_t[rc, sl].astype(BF16))], axis=0)
            a_s[pc] = stack(a_n)
            b_h[pc] = bh_c[:, sl].astype(BF16)
            k_h[pc] = stack(kh_c[:, sl])
            v_s[pc] = stack(vr[rc, sl].astype(BF16))
            gam[pc] = gam_c[:, sl]

    a_ab, a_ak, a_rb, a_rk = {}, {}, {}, {}
    for pc in pcs:
        g = _mm_nt(lhs_g[pc], rhs_g[pc])
        a_ab[pc] = jnp.where(strict, g[:CHUNK, :n2], 0.0)
        a_ak[pc] = jnp.where(strict, g[:CHUNK, n2:], 0.0).astype(BF16)
        a_rb[pc] = jnp.where(incl, g[CHUNK:, :n2], 0.0).astype(BF16)
        a_rk[pc] = jnp.where(incl, g[CHUNK:, n2:], 0.0)

    t_inv = {pc: eye + a_ab[pc] for pc in pcs}
    pw = {pc: a_ab[pc].astype(BF16) for pc in pcs}
    pw = {pc: _mm(pw[pc], stack(pw[pc])).astype(BF16) for pc in pcs}
    for _ in range(CHUNK.bit_length() - 3):
        both = {pc: _mm(jnp.concatenate([pw[pc], t_inv[pc].astype(BF16)], axis=0), stack(pw[pc])) for pc in pcs}
        t_inv = {pc: t_inv[pc] + both[pc][CHUNK:] for pc in pcs}
        pw = {pc: both[pc][:CHUNK].astype(BF16) for pc in pcs}
    t_inv = {pc: (t_inv[pc] + _mm(t_inv[pc], stack(pw[pc]))).astype(BF16) for pc in pcs}

    z = {pc: jnp.where(same_head, _mm_tn(t_inv[pc], b_h[pc]), 0.0).astype(BF16) for pc in pcs}
    w = {pc: _mm(a_rb[pc], stack(t_inv[pc])).astype(BF16) for pc in pcs}
    la_ak = {pc: jnp.concatenate([a_s[pc], stack(a_ak[pc])], axis=1) for pc in pcs}
    my = {pc: _mm_tn(la_ak[pc], z[pc]) for pc in pcs}
    wq = {pc: _mm(w[pc], la_ak[pc]) for pc in pcs}
    m_t = {pc: my[pc][:n2].astype(BF16) for pc in pcs}
    y = {pc: (my[pc][n2:] + k_h[pc]).astype(BF16) for pc in pcs}
    q_p = {pc: (r_n[pc] + wq[pc][:, :n2]).astype(BF16) for pc in pcs}
    p_p = {pc: (wq[pc][:, n2:] + a_rk[pc]).astype(BF16) for pc in pcs}
    g_s = {pc: _mm_tn(v_s[pc], y[pc]) for pc in pcs}
    o_loc = {pc: _mm(p_p[pc], v_s[pc]) for pc in pcs}

    states = [state_ref[p] for p in range(N_PAIRS)]
    outs = [[] for _ in range(N_PAIRS)]
    for c in range(n_chunks):
        for p in range(N_PAIRS):
            pc = (p, c)
            sb = states[p].astype(BF16)
            outs[p].append(_mm_nt(q_p[pc], sb) + o_loc[pc])
            states[p] = states[p] * gam[pc] + _mm(sb, m_t[pc]) + g_s[pc]
    for p in range(N_PAIRS):
        state_ref[p] = states[p]
    return [jnp.concatenate(o, axis=0) if len(o) > 1 else o[0] for o in outs]


def _rwkv_kernel(*refs, has_vmix, ts):
    if has_vmix:
        (r_ref, k_ref, v_ref, rz_ref, lora_ref, vd_ref, vfirst_ref,
         mu_rkv_ref, mu_lora_ref, w0a0_ref, lora_w_ref, kk_ref, ka_ref, rk_ref, lng_ref, lnb_ref,
         vmix_mu_ref, vmix_up_ref, vmix0_ref,
         yb_ref, carry_rkv, carry_lora, carry_vd, state_ref) = refs
    else:
        (r_ref, k_ref, v_ref, rz_ref, lora_ref,
         mu_rkv_ref, mu_lora_ref, w0a0_ref, lora_w_ref, kk_ref, ka_ref, rk_ref, lng_ref, lnb_ref,
         yb_ref, vfirst_ref, carry_rkv, carry_lora, state_ref) = refs
        carry_vd = None

    @pl.when(pl.program_id(1) == 0)
    def _():
        carry_rkv[...] = jnp.zeros_like(carry_rkv)
        carry_lora[...] = jnp.zeros_like(carry_lora)
        if carry_vd is not None:
            carry_vd[...] = jnp.zeros_like(carry_vd)
        state_ref[...] = jnp.zeros_like(state_ref)

    sub_row = lax.broadcasted_iota(jnp.int32, (ts, ts), 0)
    sub_col = lax.broadcasted_iota(jnp.int32, (ts, ts), 1)
    diff_mat = (jnp.where(sub_row == sub_col + 1, 1.0, 0.0)
                - jnp.where(sub_row == sub_col, 1.0, 0.0)).astype(r_ref.dtype)
    top_row = lax.broadcasted_iota(jnp.int32, (8, 1), 0) == 0

    def shift(y_act, carry_ref, row_idx, mu):
        y = y_act.astype(F32)
        delta = jnp.dot(diff_mat, y_act, preferred_element_type=F32)
        head = jnp.where(top_row, delta[:8] + carry_ref[row_idx:row_idx + 1, :], delta[:8])
        delta = jnp.concatenate([head, delta[8:]], axis=0)
        carry_ref[row_idx:row_idx + 1, :] = y[ts - 1:ts, :]
        return y + delta * mu

    mu_rkv = mu_rkv_ref[...]
    rs = shift(r_ref[...], carry_rkv, 0, mu_rkv[0:1, :])
    ks = shift(k_ref[...], carry_rkv, 1, mu_rkv[1:2, :])
    vs = shift(v_ref[...], carry_rkv, 2, mu_rkv[2:3, :])
    lo = shift(lora_ref[...], carry_lora, 0, mu_lora_ref[...])

    lane = lax.broadcasted_iota(jnp.int32, (ts, LANES), 1)
    z = jnp.where(lane < DECAY_LORA, jnp.tanh(lo), lo)
    wa = w0a0_ref[...] + _mm(z, lora_w_ref[...])
    lw = (-0.6065306597126334) * _sigmoid(wa[:, :WIDTH])
    a = _sigmoid(wa[:, WIDTH:])

    if has_vmix:
        vd = shift(vd_ref[...], carry_vd, 0, vmix_mu_ref[...])
        mix = _sigmoid(vmix0_ref[...] + _mm(vd, vmix_up_ref[...]))
        vr = vs + (vfirst_ref[...] - vs) * mix
    else:
        vfirst_ref[...] = vs
        vr = vs

    kx = ks * kk_ref[...]
    kmod = ks * (1.0 + (a - 1.0) * ka_ref[...])
    rkr = rs * kmod * rk_ref[...]
    rz = rz_ref[...].astype(F32)
    ln_g = lng_ref[...]
    ln_b = lnb_ref[...]

    head0_t = lane < HEAD_DIM

    def headsum(x):
        s0 = jnp.sum(jnp.where(head0_t, x, 0.0), axis=-1, keepdims=True)
        s1 = jnp.sum(jnp.where(head0_t, 0.0, x), axis=-1, keepdims=True)
        return jnp.where(head0_t, s0, s1)

    kk_all = []
    for p in range(N_PAIRS):
        kxp = kx[:, p * PAIR:(p + 1) * PAIR]
        kk_all.append(kxp * jnp.minimum(lax.rsqrt(headsum(kxp * kxp)), 1.0 / L2_EPS))
    kk = jnp.concatenate(kk_all, axis=1)

    o_pairs = _rwkv_scan(rs, kmod, vr, -kk, kk * a, lw, state_ref, ts)

    for p in range(N_PAIRS):
        sl = slice(p * PAIR, (p + 1) * PAIR)
        o_p = o_pairs[p]
        bonus = headsum(rkr[:, sl]) * vr[:, sl]
        mu = headsum(o_p) * (1.0 / HEAD_DIM)
        d = o_p - mu
        var = headsum(d * d) * (1.0 / HEAD_DIM)
        on = d * lax.rsqrt(var + GN_EPS) * ln_g[:, sl] + ln_b[:, sl]
        rz_p = rz[:, sl]
        yb_ref[:, sl] = ((on + bonus) * (rz_p * _sigmoid(rz_p))).astype(yb_ref.dtype)


def _rwkv(proj, vfirst, p, *, ts=256):
    b, s, _ = proj.shape
    has_vmix = vfirst is not None
    row = lambda width, col: pl.BlockSpec((None, ts, width), lambda i, t: (i, t, col // width))
    const = lambda shape: pl.BlockSpec(shape, lambda i, t: (0,) * len(shape))
    act = pl.BlockSpec((None, ts, WIDTH), lambda i, t: (i, t, 0))

    in_specs = [row(WIDTH, COL_R), row(WIDTH, COL_RK), row(WIDTH, COL_RV), row(WIDTH, COL_RZ),
                row(LANES, COL_LORA)]
    args = [proj, proj, proj, proj, proj]
    if has_vmix:
        in_specs += [row(LANES, COL_VD), act]
        args += [proj, vfirst]
    in_specs += [const((3, WIDTH)), const((1, LANES)), const((1, 2 * WIDTH)), const((LANES, 2 * WIDTH))]
    args += [p["mu_rkv"], p["mu_lora"], p["w0a0"], p["lora_w"]]
    for name in ("k_k", "k_a", "r_k", "ln_g", "ln_b"):
        in_specs.append(const((1, WIDTH)))
        args.append(p[name])
    if has_vmix:
        in_specs += [const((1, LANES)), const((LANES, WIDTH)), const((1, WIDTH))]
        args += [p["vmix_mu"], p["vmix_up"], p["vmix0"]]

    out_shape = [jax.ShapeDtypeStruct((b, s, WIDTH), ACT)]
    out_specs = [act]
    scratch = [pltpu.VMEM((8, WIDTH), F32), pltpu.VMEM((8, LANES), F32)]
    if has_vmix:
        scratch.append(pltpu.VMEM((8, LANES), F32))
    else:
        out_shape.append(jax.ShapeDtypeStruct((b, s, WIDTH), F32))
        out_specs.append(act)
    scratch.append(pltpu.VMEM((N_PAIRS, PAIR, PAIR), F32))

    res = pl.pallas_call(
        functools.partial(_rwkv_kernel, has_vmix=has_vmix, ts=ts),
        grid=(b, s // ts),
        in_specs=in_specs,
        out_specs=out_specs,
        out_shape=out_shape,
        scratch_shapes=scratch,
        compiler_params=pltpu.CompilerParams(
            dimension_semantics=("parallel", "arbitrary"), vmem_limit_bytes=VMEM_LIMIT),
        name="rwkv_vmix" if has_vmix else "rwkv",
    )(*args)
    return (res[0], vfirst) if has_vmix else (res[0], res[1])


def _outproj_kernel(ya_ref, az_ref, yb_ref, gatt_ref, grw_ref, x_ref,
                    wua_ref, wub_ref, wout_ref, g_ref, o_ref):
    az = az_ref[...].astype(F32)
    ya = ya_ref[...].astype(F32) * (az * _sigmoid(az))
    u = (_sigmoid(gatt_ref[...].astype(F32)) * _mm(ya, wua_ref[...])
         + _sigmoid(grw_ref[...].astype(F32)) * _mm(yb_ref[...], wub_ref[...]))
    y = _mm(u, wout_ref[...])
    ms = jnp.mean(y * y, axis=-1, keepdims=True)
    o_ref[...] = x_ref[...] + y * lax.rsqrt(ms + RMS_EPS) * g_ref[...]


def _outproj(ya, proj2d, yb, x2d, w_up_att, w_up_rw, w_out, g, *, tm=512):
    t, d = x2d.shape
    row = lambda width, col: pl.BlockSpec((tm, width), lambda i: (i, col // width))
    const = lambda shape: pl.BlockSpec(shape, lambda i: (0, 0))
    return pl.pallas_call(
        _outproj_kernel,
        grid=(t // tm,),
        in_specs=[row(WIDTH, 0), row(WIDTH, COL_AZ), row(WIDTH, 0), row(d, COL_GATT), row(d, COL_GRW),
                  row(d, 0), const((WIDTH, d)), const((WIDTH, d)), const((d, d)), const((1, d))],
        out_specs=row(d, 0),
        out_shape=jax.ShapeDtypeStruct((t, d), F32),
        compiler_params=pltpu.CompilerParams(
            dimension_semantics=("parallel",), vmem_limit_bytes=VMEM_LIMIT),
        name="outproj",
    )(ya, proj2d, yb, proj2d, proj2d, x2d, w_up_att, w_up_rw, w_out, g.reshape(1, d))


def _layer_params(l, w_in, rw_mu, rw_w0, rw_w_up, rw_a0, rw_a_up, rw_k_k, rw_k_a, rw_r_k,
                  rw_ln_g, rw_ln_b, rw_vmix_down, rw_vmix_mu, rw_vmix_up, rw_vmix0):
    w = w_in[l]
    d = w.shape[0]
    att = w[:, :4 * WIDTH]
    o = 4 * WIDTH
    rkv = w[:, o:o + 3 * WIDTH]
    lora = w[:, o + 3 * WIDTH:o + 3 * WIDTH + DECAY_LORA + ICLR_LORA]
    o = o + 3 * WIDTH + DECAY_LORA + ICLR_LORA
    rz = w[:, o:o + WIDTH]
    gates = w[:, o + WIDTH:]
    vd = jnp.zeros((d, LANES), F32)
    if l > 0:
        vd = vd.at[:, :VMIX_LORA].set(rw_vmix_down[l - 1])
    w_cat = jnp.concatenate([att, rkv, rz, gates, lora, vd], axis=1).astype(BF16)

    mu = rw_mu[l]
    zeros = jnp.zeros((DECAY_LORA, WIDTH), F32)
    lora_w = jnp.concatenate([jnp.concatenate([rw_w_up[l], zeros], axis=1),
                              jnp.concatenate([zeros, rw_a_up[l]], axis=1)], axis=0).astype(BF16)
    p = {
        "w_cat": w_cat,
        "mu_rkv": mu[:3 * WIDTH].reshape(3, WIDTH),
        "mu_lora": mu[3 * WIDTH:].reshape(1, LANES),
        "w0a0": jnp.concatenate([rw_w0[l], rw_a0[l]]).reshape(1, 2 * WIDTH),
        "lora_w": lora_w,
        "k_k": rw_k_k[l].reshape(1, WIDTH), "k_a": rw_k_a[l].reshape(1, WIDTH),
        "r_k": rw_r_k[l].reshape(1, WIDTH),
        "ln_g": rw_ln_g[l].reshape(1, WIDTH), "ln_b": rw_ln_b[l].reshape(1, WIDTH),
    }
    if l > 0:
        p["vmix_mu"] = jnp.zeros((1, LANES), F32).at[0, :VMIX_LORA].set(rw_vmix_mu[l - 1])
        p["vmix_up"] = jnp.zeros((LANES, WIDTH), F32).at[:VMIX_LORA].set(rw_vmix_up[l - 1]).astype(BF16)
        p["vmix0"] = rw_vmix0[l - 1].reshape(1, WIDTH)
    return p


def kernel(x, norm_pre, norm_post, w_in, rw_mu, rw_w0, rw_w_up, rw_a0, rw_a_up, rw_k_k, rw_k_a, rw_r_k, rw_ln_g, rw_ln_b, rw_vmix_down, rw_vmix_mu, rw_vmix_up, rw_vmix0, w_up_att, w_up_rw, w_out):
    b, s, d = x.shape
    assert d == D_MODEL and s % MOBA_BLOCK == 0
    depth = w_in.shape[0]
    x2d = x.reshape(b * s, d)
    vfirst = None
    for l in range(depth):
        p = _layer_params(l, w_in, rw_mu, rw_w0, rw_w_up, rw_a0, rw_a_up, rw_k_k, rw_k_a, rw_r_k,
                          rw_ln_g, rw_ln_b, rw_vmix_down, rw_vmix_mu, rw_vmix_up, rw_vmix0)
        proj2d = _inproj(x2d, norm_pre[l], p["w_cat"])
        proj = proj2d.reshape(b, s, PROJ_W)
        ya = _moba(proj)
        yb, vfirst = _rwkv(proj, vfirst, p)
        x2d = _outproj(ya.reshape(b * s, WIDTH), proj2d, yb.reshape(b * s, WIDTH), x2d,
                       w_up_att[l].astype(BF16), w_up_rw[l].astype(BF16), w_out[l].astype(BF16),
                       norm_post[l])
    return x2d.reshape(b, s, d)
```

```python
import functools

import jax
import jax.numpy as jnp
from jax import lax
from jax.experimental import pallas as pl
from jax.experimental.pallas import tpu as pltpu

F32 = jnp.float32
BF16 = jnp.bfloat16
ACT = BF16

D_MODEL = 1024
N_HEADS = 8
HEAD_DIM = 64
WIDTH = N_HEADS * HEAD_DIM
MOBA_BLOCK = 256
MOBA_TOPK = 3
DECAY_LORA = 64
ICLR_LORA = 64
VMIX_LORA = 32
RMS_EPS = 1e-6
GN_EPS = 64e-5
L2_EPS = 1e-12
NEG_INF = -1e30

LANES = 128
PAIR = 2 * HEAD_DIM
N_PAIRS = WIDTH // PAIR
V_ROWS = HEAD_DIM + 16
CHUNK = 64

COL_Q, COL_K, COL_V, COL_AZ = 0, 512, 1024, 1536
COL_R, COL_RK, COL_RV, COL_RZ = 2048, 2560, 3072, 3584
COL_GATT, COL_GRW = 4096, 5120
COL_LORA = 6144
COL_VD = 6272
PROJ_W = 6400

VMEM_LIMIT = 48 * 1024 * 1024


def _sigmoid(x):
    return 0.5 * jnp.tanh(0.5 * x) + 0.5


def _mm(a, b):
    return jnp.dot(a.astype(BF16), b.astype(BF16), preferred_element_type=F32)


def _mm_nt(a, b):
    return lax.dot_general(a.astype(BF16), b.astype(BF16), (((1,), (1,)), ((), ())),
                           preferred_element_type=F32)


def _mm_tn(a, b):
    return lax.dot_general(a.astype(BF16), b.astype(BF16), (((0,), (0,)), ((), ())),
                           preferred_element_type=F32)


def _inproj_kernel(x_ref, g_ref, w_ref, o_ref, *, tn):
    x = x_ref[...]
    ms = jnp.mean(x * x, axis=-1, keepdims=True)
    h = (x * lax.rsqrt(ms + RMS_EPS) * g_ref[...]).astype(BF16)
    for n0 in range(0, o_ref.shape[1], tn):
        o_ref[:, n0:n0 + tn] = jnp.dot(h, w_ref[:, n0:n0 + tn],
                                       preferred_element_type=F32).astype(o_ref.dtype)


def _inproj(x2d, g, w_cat, *, tm=512, tn=1280):
    t, d = x2d.shape
    n = w_cat.shape[1]
    return pl.pallas_call(
        functools.partial(_inproj_kernel, tn=tn),
        grid=(t // tm,),
        in_specs=[
            pl.BlockSpec((tm, d), lambda i: (i, 0)),
            pl.BlockSpec((1, d), lambda i: (0, 0)),
            pl.BlockSpec((d, n), lambda i: (0, 0), pipeline_mode=pl.Buffered(1)),
        ],
        out_specs=pl.BlockSpec((tm, n), lambda i: (i, 0)),
        out_shape=jax.ShapeDtypeStruct((t, n), ACT),
        compiler_params=pltpu.CompilerParams(
            dimension_semantics=("parallel",), vmem_limit_bytes=VMEM_LIMIT),
        name="inproj",
    )(x2d, g.reshape(1, d), w_cat)


def _moba_kernel(q_ref, k_ref, v_ref, o_ref, vt_ref, qx_ref, kx_ref, s_ref, p_ref, *, n_blocks):
    hp = pl.program_id(1)
    blk = MOBA_BLOCK
    s_len = n_blocks * blk
    log2e = 1.4426950408889634
    q_scale = (HEAD_DIM ** -0.5) * log2e
    nt = (((1,), (1,)), ((), ()))
    sub = 32

    vt = v_ref[...].T
    ones_rows = jnp.where(lax.broadcasted_iota(jnp.int32, (V_ROWS - HEAD_DIM, s_len), 0) == 0, 1.0, 0.0).astype(BF16)
    for h in range(2):
        vt_ref[h] = jnp.concatenate([vt[h * HEAD_DIM:(h + 1) * HEAD_DIM], ones_rows], axis=0)

    mean_row = lax.broadcasted_iota(jnp.int32, (n_blocks, PAIR), 0)
    kmean = jnp.zeros((n_blocks, PAIR), F32)
    for n in range(n_blocks):
        kmean = jnp.where(mean_row == n,
                          jnp.mean(k_ref[n * blk:(n + 1) * blk, :].astype(F32), axis=0, keepdims=True), kmean)

    lane = lax.broadcasted_iota(jnp.int32, (1, PAIR), 1)
    head_lanes = (lane < HEAD_DIM, lane >= HEAD_DIM)
    nrow = lax.broadcasted_iota(jnp.int32, (n_blocks, s_len), 0)
    pos_i = lax.broadcasted_iota(jnp.int32, (n_blocks, s_len), 1)
    pblk = pos_i // blk
    pos = pos_i.astype(F32)
    past = nrow < pblk
    krow = lax.broadcasted_iota(jnp.int32, (blk, blk), 0)
    qcol = lax.broadcasted_iota(jnp.int32, (blk, blk), 1)
    causal = qcol >= krow

    km = jnp.concatenate([jnp.where(head_lanes[0], kmean, 0.0), jnp.where(head_lanes[1], kmean, 0.0)], axis=0)
    km_hi = km.astype(BF16)
    km_r = km - km_hi.astype(F32)
    km_mid = km_r.astype(BF16)
    km_lo = (km_r - km_mid.astype(F32)).astype(BF16)
    q_all = q_ref[...]
    gates = (lax.dot_general(km_lo, q_all, nt, preferred_element_type=F32)
             + lax.dot_general(km_mid, q_all, nt, preferred_element_type=F32)
             + lax.dot_general(km_hi, q_all, nt, preferred_element_type=F32))

    def split3(x):
        hi = x.astype(BF16).astype(F32)
        mid = (x - hi).astype(BF16).astype(F32)
        return hi, mid, x - hi - mid

    def rows8(r0, r1, r2, r3, r4, r5):
        out = jnp.zeros((n_blocks, s_len), F32)
        for idx, r in enumerate((r0, r1, r2, r3, r4, r5)):
            out = jnp.where(nrow == idx, r, out)
        return out

    ones = jnp.ones((n_blocks, s_len), F32)
    for h in range(2):
        head = lax.convert_element_type(hp * 2 + h + 1, F32)
        slope2 = jnp.exp2(jnp.zeros((1, 1), F32) - (8.0 / N_HEADS) * head) * log2e

        gm = jnp.where(past, gates[h * n_blocks:(h + 1) * n_blocks], NEG_INF)
        rank = jnp.zeros((n_blocks, s_len), F32)
        for m in range(n_blocks):
            gcol = gm[m:m + 1, :]
            beats = (gcol > gm) | ((gcol == gm) & (nrow > m))
            rank = rank + jnp.where(beats, 1.0, 0.0)
        visible = (past & (rank < MOBA_TOPK)) | (nrow == pblk)

        q_hi, q_mid, q_lo = split3(-slope2 * pos)
        k_hi, k_mid, k_lo = split3(slope2 * pos)
        q_extra = jnp.concatenate([rows8(q_hi, q_mid, q_lo, ones, ones, ones),
                                   jnp.where(visible, 0.0, NEG_INF)], axis=0)
        k_extra = jnp.concatenate([rows8(ones, ones, ones, k_hi, k_mid, k_lo),
                                   jnp.where(nrow == pblk, 1.0, 0.0)], axis=0)
        lead = jnp.zeros(((1 - h) * HEAD_DIM, s_len), F32)
        tail = jnp.zeros((PAIR - 2 * n_blocks - (1 - h) * HEAD_DIM, s_len), F32)
        q_full = jnp.concatenate([lead, q_extra, tail], axis=0) if h == 0 else jnp.concatenate([q_extra, tail], axis=0)
        k_full = jnp.concatenate([lead, k_extra, tail], axis=0) if h == 0 else jnp.concatenate([k_extra, tail], axis=0)
        qx_ref[h] = q_full.T.astype(BF16)
        kx_ref[h] = jnp.where(head_lanes[h], k_ref[...], k_full.T.astype(BF16))

    units = [(i, h) for i in range(n_blocks) for h in range(2)]
    st = {}

    def score_tile(u, j):
        i, h = u
        d = st[u]
        s_tile = lax.dot_general(kx_ref[h, j * blk:(j + 1) * blk, :], d["qm"], nt,
                                 preferred_element_type=F32)
        for c in range(0, blk, sub):
            sc = s_tile[c:c + sub]
            if j == i:
                sc = jnp.where(causal[c:c + sub], sc, NEG_INF)
            s_ref[d["slot"], j * blk + c:j * blk + c + sub, :] = sc
            for r in range(0, sub, 8):
                d["m8"] = sc[r:r + 8] if d["m8"] is None else jnp.maximum(d["m8"], sc[r:r + 8])

    def softmax_tile(u, j):
        d = st[u]
        for c in range(j * blk, (j + 1) * blk, sub):
            pc = jnp.exp2(s_ref[d["slot"], c:c + sub, :] - d["m"])
            p_ref[d["slot"], c:c + sub, :] = pc.astype(BF16)

    def finish(u):
        i, h = u
        d = st[u]
        n_k = (i + 1) * blk
        pv = jnp.dot(vt_ref[h, :, :n_k], p_ref[d["slot"], :n_k, :],
                     preferred_element_type=F32)
        d["out"] = pv[:HEAD_DIM] * (1.0 / pv[HEAD_DIM:HEAD_DIM + 1])
        if h == 1:
            out_t = jnp.concatenate([st[(i, 0)]["out"], d["out"]], axis=0)
            o_ref[i * blk:(i + 1) * blk, :] = out_t.T.astype(o_ref.dtype)

    prev = None
    for idx, u in enumerate(units):
        i, h = u
        rows = slice(i * blk, (i + 1) * blk)
        if h == 0:
            q_i = (q_ref[rows, :].astype(F32) * q_scale).astype(BF16)
        st[u] = {"qm": jnp.where(head_lanes[h], q_i, qx_ref[h, rows, :]), "slot": idx % 2,
                 "m8": None}
        n_prev = prev[0] + 1 if prev is not None else 0
        for step in range(max(i + 1, n_prev)):
            if step < i + 1:
                score_tile(u, step)
            if step < n_prev:
                softmax_tile(prev, step)
        if prev is not None:
            finish(prev)
        st[u]["m"] = jnp.max(st[u]["m8"], axis=0, keepdims=True)
        prev = u
    for step in range(prev[0] + 1):
        softmax_tile(prev, step)
    finish(prev)


def _moba(proj):
    b, s, _ = proj.shape
    n_blocks = s // MOBA_BLOCK
    qo, ko, vo = COL_Q // PAIR, COL_K // PAIR, COL_V // PAIR
    return pl.pallas_call(
        functools.partial(_moba_kernel, n_blocks=n_blocks),
        grid=(b, N_PAIRS),
        in_specs=[
            pl.BlockSpec((None, s, PAIR), lambda i, p: (i, 0, qo + p)),
            pl.BlockSpec((None, s, PAIR), lambda i, p: (i, 0, ko + p)),
            pl.BlockSpec((None, s, PAIR), lambda i, p: (i, 0, vo + p)),
        ],
        out_specs=pl.BlockSpec((None, s, PAIR), lambda i, p: (i, 0, p)),
        out_shape=jax.ShapeDtypeStruct((b, s, WIDTH), ACT),
        scratch_shapes=[
            pltpu.VMEM((2, V_ROWS, s), BF16),
            pltpu.VMEM((2, s, PAIR), BF16),
            pltpu.VMEM((2, s, PAIR), BF16),
            pltpu.VMEM((2, s, MOBA_BLOCK), F32),
            pltpu.VMEM((2, s, MOBA_BLOCK), BF16),
        ],
        compiler_params=pltpu.CompilerParams(
            dimension_semantics=("parallel", "parallel"), vmem_limit_bytes=VMEM_LIMIT),
        name="moba",
    )(proj, proj, proj)


def _rwkv_scan(rs, kmod, vr, na, nb, lw, state_ref, ts):
    n_chunks = ts // CHUNK
    n2 = 2 * CHUNK
    trow = lax.broadcasted_iota(jnp.int32, (ts, ts), 0)
    tcol = lax.broadcasted_iota(jnp.int32, (ts, ts), 1)
    ltri = jnp.where((trow // CHUNK == tcol // CHUNK) & (tcol <= trow), 1.0, 0.0).astype(BF16)
    head0 = lax.broadcasted_iota(jnp.int32, (CHUNK, PAIR), 1) < HEAD_DIM
    wrow = lax.broadcasted_iota(jnp.int32, (CHUNK, PAIR), 0)
    wcol = lax.broadcasted_iota(jnp.int32, (CHUNK, PAIR), 1) % CHUNK
    strict = wcol < wrow
    incl = wcol <= wrow
    eye = jnp.where(wcol == wrow, 1.0, 0.0).astype(F32)
    srow = lax.broadcasted_iota(jnp.int32, (n2, n2), 0)
    scol = lax.broadcasted_iota(jnp.int32, (n2, n2), 1)
    same_head = (srow // CHUNK) == (scol // CHUNK)

    def stack(x):
        zero = jnp.zeros_like(x)
        return jnp.concatenate([jnp.where(head0, x, zero), jnp.where(head0, zero, x)], axis=0)

    lw_hi = lw.astype(BF16)
    lw_r = lw - lw_hi.astype(F32)
    lw_mid = lw_r.astype(BF16)
    lw_lo = (lw_r - lw_mid.astype(F32)).astype(BF16)
    cum = (jnp.dot(ltri, lw_lo, preferred_element_type=F32) + jnp.dot(ltri, lw_mid, preferred_element_type=F32)
           + jnp.dot(ltri, lw_hi, preferred_element_type=F32))
    a_t = na * jnp.exp(cum - lw)
    r_t = rs * jnp.exp(cum)
    e_neg = jnp.exp(-cum)
    b_t = nb * e_neg
    k_t = kmod * e_neg

    pcs = [(p, c) for c in range(n_chunks) for p in range(N_PAIRS)]
    lhs_g, rhs_g, a_s, r_n, b_h, k_h, v_s, gam = {}, {}, {}, {}, {}, {}, {}, {}
    for c in range(n_chunks):
        rc = slice(c * CHUNK, (c + 1) * CHUNK)
        tot = cum[(c + 1) * CHUNK - 1:(c + 1) * CHUNK, :]
        e_rel = jnp.exp(tot - cum[rc])
        bh_c = nb[rc] * e_rel
        kh_c = kmod[rc] * e_rel
        gam_c = jnp.exp(tot)
        for p in range(N_PAIRS):
            sl = slice(p * PAIR, (p + 1) * PAIR)
            pc = (p, c)
            a_n = a_t[rc, sl].astype(BF16)
            r_n[pc] = r_t[rc, sl]
            lhs_g[pc] = jnp.concatenate([a_n, r_n[pc].astype(BF16)], axis=0)
            rhs_g[pc] = jnp.concatenate([stack(b_t[rc, sl].astype(BF16)),
                                         stack(k_t[rc, sl].astype(BF16))], axis=0)
            a_s[pc] = stack(a_n)
            b_h[pc] = bh_c[:, sl].astype(BF16)
            k_h[pc] = stack(kh_c[:, sl])
            v_s[pc] = stack(vr[rc, sl].astype(BF16))
            gam[pc] = gam_c[:, sl]

    a_ab, a_ak, a_rb, a_rk = {}, {}, {}, {}
    for pc in pcs:
        g = _mm_nt(lhs_g[pc], rhs_g[pc])
        a_ab[pc] = jnp.where(strict, g[:CHUNK, :n2], 0.0)
        a_ak[pc] = jnp.where(strict, g[:CHUNK, n2:], 0.0).astype(BF16)
        a_rb[pc] = jnp.where(incl, g[CHUNK:, :n2], 0.0).astype(BF16)
        a_rk[pc] = jnp.where(incl, g[CHUNK:, n2:], 0.0)

    t_inv = {pc: eye + a_ab[pc] for pc in pcs}
    pw = {pc: a_ab[pc].astype(BF16) for pc in pcs}
    pw = {pc: _mm(pw[pc], stack(pw[pc])).astype(BF16) for pc in pcs}
    for _ in range(CHUNK.bit_length() - 3):
        both = {pc: _mm(jnp.concatenate([pw[pc], t_inv[pc].astype(BF16)], axis=0), stack(pw[pc])) for pc in pcs}
        t_inv = {pc: t_inv[pc] + both[pc][CHUNK:] for pc in pcs}
        pw = {pc: both[pc][:CHUNK].astype(BF16) for pc in pcs}
    t_inv = {pc: (t_inv[pc] + _mm(t_inv[pc], stack(pw[pc]))).astype(BF16) for pc in pcs}

    z = {pc: jnp.where(same_head, _mm_tn(t_inv[pc], b_h[pc]), 0.0).astype(BF16) for pc in pcs}
    w = {pc: _mm(a_rb[pc], stack(t_inv[pc])).astype(BF16) for pc in pcs}
    la_ak = {pc: jnp.concatenate([a_s[pc], stack(a_ak[pc])], axis=1) for pc in pcs}
    my = {pc: _mm_tn(la_ak[pc], z[pc]) for pc in pcs}
    wq = {pc: _mm(w[pc], la_ak[pc]) for pc in pcs}
    m_t = {pc: my[pc][:n2].astype(BF16) for pc in pcs}
    y = {pc: (my[pc][n2:] + k_h[pc]).astype(BF16) for pc in pcs}
    q_p = {pc: (r_n[pc] + wq[pc][:, :n2]).astype(BF16) for pc in pcs}
    p_p = {pc: (wq[pc][:, n2:] + a_rk[pc]).astype(BF16) for pc in pcs}
    g_s = {pc: _mm_tn(v_s[pc], y[pc]) for pc in pcs}
    o_loc = {pc: _mm(p_p[pc], v_s[pc]) for pc in pcs}

    states = [state_ref[p] for p in range(N_PAIRS)]
    outs = [[] for _ in range(N_PAIRS)]
    for c in range(n_chunks):
        for p in range(N_PAIRS):
            pc = (p, c)
            sb = states[p].astype(BF16)
            outs[p].append(_mm_nt(q_p[pc], sb) + o_loc[pc])
            states[p] = states[p] * gam[pc] + _mm(sb, m_t[pc]) + g_s[pc]
    for p in range(N_PAIRS):
        state_ref[p] = states[p]
    return [jnp.concatenate(o, axis=0) if len(o) > 1 else o[0] for o in outs]


def _rwkv_kernel(*refs, has_vmix, ts):
    if has_vmix:
        (r_ref, k_ref, v_ref, rz_ref, lora_ref, vd_ref, vfirst_ref,
         mu_rkv_ref, mu_lora_ref, w0a0_ref, lora_w_ref, kk_ref, ka_ref, rk_ref, lng_ref, lnb_ref,
         vmix_mu_ref, vmix_up_ref, vmix0_ref,
         yb_ref, carry_rkv, carry_lora, carry_vd, state_ref) = refs
    else:
        (r_ref, k_ref, v_ref, rz_ref, lora_ref,
         mu_rkv_ref, mu_lora_ref, w0a0_ref, lora_w_ref, kk_ref, ka_ref, rk_ref, lng_ref, lnb_ref,
         yb_ref, vfirst_ref, carry_rkv, carry_lora, state_ref) = refs
        carry_vd = None

    @pl.when(pl.program_id(1) == 0)
    def _():
        carry_rkv[...] = jnp.zeros_like(carry_rkv)
        carry_lora[...] = jnp.zeros_like(carry_lora)
        if carry_vd is not None:
            carry_vd[...] = jnp.zeros_like(carry_vd)
        state_ref[...] = jnp.zeros_like(state_ref)

    sub_row = lax.broadcasted_iota(jnp.int32, (ts, ts), 0)
    sub_col = lax.broadcasted_iota(jnp.int32, (ts, ts), 1)
    diff_mat = (jnp.where(sub_row == sub_col + 1, 1.0, 0.0)
                - jnp.where(sub_row == sub_col, 1.0, 0.0)).astype(r_ref.dtype)
    top_row = lax.broadcasted_iota(jnp.int32, (8, 1), 0) == 0

    def shift(y_act, carry_ref, row_idx, mu):
        y = y_act.astype(F32)
        delta = jnp.dot(diff_mat, y_act, preferred_element_type=F32)
        head = jnp.where(top_row, delta[:8] + carry_ref[row_idx:row_idx + 1, :], delta[:8])
        delta = jnp.concatenate([head, delta[8:]], axis=0)
        carry_ref[row_idx:row_idx + 1, :] = y[ts - 1:ts, :]
        return y + delta * mu

    mu_rkv = mu_rkv_ref[...]
    rs = shift(r_ref[...], carry_rkv, 0, mu_rkv[0:1, :])
    ks = shift(k_ref[...], carry_rkv, 1, mu_rkv[1:2, :])
    vs = shift(v_ref[...], carry_rkv, 2, mu_rkv[2:3, :])
    lo = shift(lora_ref[...], carry_lora, 0, mu_lora_ref[...])

    lane = lax.broadcasted_iota(jnp.int32, (ts, LANES), 1)
    z = jnp.where(lane < DECAY_LORA, jnp.tanh(lo), lo)
    wa = w0a0_ref[...] + _mm(z, lora_w_ref[...])
    lw = (-0.6065306597126334) * _sigmoid(wa[:, :WIDTH])
    a = _sigmoid(wa[:, WIDTH:])

    if has_vmix:
        vd = shift(vd_ref[...], carry_vd, 0, vmix_mu_ref[...])
        mix = _sigmoid(vmix0_ref[...] + _mm(vd, vmix_up_ref[...]))
        vr = vs + (vfirst_ref[...] - vs) * mix
    else:
        vfirst_ref[...] = vs
        vr = vs

    kx = ks * kk_ref[...]
    kmod = ks * (1.0 + (a - 1.0) * ka_ref[...])
    rkr = rs * kmod * rk_ref[...]
    rz = rz_ref[...].astype(F32)
    ln_g = lng_ref[...]
    ln_b = lnb_ref[...]

    head0_t = lane < HEAD_DIM

    def headsum(x):
        s0 = jnp.sum(jnp.where(head0_t, x, 0.0), axis=-1, keepdims=True)
        s1 = jnp.sum(jnp.where(head0_t, 0.0, x), axis=-1, keepdims=True)
        return jnp.where(head0_t, s0, s1)

    kk_all = []
    for p in range(N_PAIRS):
        kxp = kx[:, p * PAIR:(p + 1) * PAIR]
        kk_all.append(kxp * jnp.minimum(lax.rsqrt(headsum(kxp * kxp)), 1.0 / L2_EPS))
    kk = jnp.concatenate(kk_all, axis=1)

    o_pairs = _rwkv_scan(rs, kmod, vr, -kk, kk * a, lw, state_ref, ts)

    for p in range(N_PAIRS):
        sl = slice(p * PAIR, (p + 1) * PAIR)
        o_p = o_pairs[p]
        bonus = headsum(rkr[:, sl]) * vr[:, sl]
        mu = headsum(o_p) * (1.0 / HEAD_DIM)
        d = o_p - mu
        var = headsum(d * d) * (1.0 / HEAD_DIM)
        on = d * lax.rsqrt(var + GN_EPS) * ln_g[:, sl] + ln_b[:, sl]
        rz_p = rz[:, sl]
        yb_ref[:, sl] = ((on + bonus) * (rz_p * _sigmoid(rz_p))).astype(yb_ref.dtype)


def _rwkv(proj, vfirst, p, *, ts=256):
    b, s, _ = proj.shape
    has_vmix = vfirst is not None
    row = lambda width, col: pl.BlockSpec((None, ts, width), lambda i, t: (i, t, col // width))
    const = lambda shape: pl.BlockSpec(shape, lambda i, t: (0,) * len(shape))
    act = pl.BlockSpec((None, ts, WIDTH), lambda i, t: (i, t, 0))

    in_specs = [row(WIDTH, COL_R), row(WIDTH, COL_RK), row(WIDTH, COL_RV), row(WIDTH, COL_RZ),
                row(LANES, COL_LORA)]
    args = [proj, proj, proj, proj, proj]
    if has_vmix:
        in_specs += [row(LANES, COL_VD), act]
        args += [proj, vfirst]
    in_specs += [const((3, WIDTH)), const((1, LANES)), const((1, 2 * WIDTH)), const((LANES, 2 * WIDTH))]
    args += [p["mu_rkv"], p["mu_lora"], p["w0a0"], p["lora_w"]]
    for name in ("k_k", "k_a", "r_k", "ln_g", "ln_b"):
        in_specs.append(const((1, WIDTH)))
        args.append(p[name])
    if has_vmix:
        in_specs += [const((1, LANES)), const((LANES, WIDTH)), const((1, WIDTH))]
        args += [p["vmix_mu"], p["vmix_up"], p["vmix0"]]

    out_shape = [jax.ShapeDtypeStruct((b, s, WIDTH), ACT)]
    out_specs = [act]
    scratch = [pltpu.VMEM((8, WIDTH), F32), pltpu.VMEM((8, LANES), F32)]
    if has_vmix:
        scratch.append(pltpu.VMEM((8, LANES), F32))
    else:
        out_shape.append(jax.ShapeDtypeStruct((b, s, WIDTH), F32))
        out_specs.append(act)
    scratch.append(pltpu.VMEM((N_PAIRS, PAIR, PAIR), F32))

    res = pl.pallas_call(
        functools.partial(_rwkv_kernel, has_vmix=has_vmix, ts=ts),
        grid=(b, s // ts),
        in_specs=in_specs,
        out_specs=out_specs,
        out_shape=out_shape,
        scratch_shapes=scratch,
        compiler_params=pltpu.CompilerParams(
            dimension_semantics=("parallel", "arbitrary"), vmem_limit_bytes=VMEM_LIMIT),
        name="rwkv_vmix" if has_vmix else "rwkv",
    )(*args)
    return (res[0], vfirst) if has_vmix else (res[0], res[1])


def _outproj_kernel(ya_ref, az_ref, yb_ref, gatt_ref, grw_ref, x_ref,
                    wua_ref, wub_ref, wout_ref, g_ref, o_ref):
    az = az_ref[...].astype(F32)
    ya = ya_ref[...].astype(F32) * (az * _sigmoid(az))
    u = (_sigmoid(gatt_ref[...].astype(F32)) * _mm(ya, wua_ref[...])
         + _sigmoid(grw_ref[...].astype(F32)) * _mm(yb_ref[...], wub_ref[...]))
    y = _mm(u, wout_ref[...])
    ms = jnp.mean(y * y, axis=-1, keepdims=True)
    o_ref[...] = x_ref[...] + y * lax.rsqrt(ms + RMS_EPS) * g_ref[...]


def _outproj(ya, proj2d, yb, x2d, w_up_att, w_up_rw, w_out, g, *, tm=512):
    t, d = x2d.shape
    row = lambda width, col: pl.BlockSpec((tm, width), lambda i: (i, col // width))
    const = lambda shape: pl.BlockSpec(shape, lambda i: (0, 0))
    return pl.pallas_call(
        _outproj_kernel,
        grid=(t // tm,),
        in_specs=[row(WIDTH, 0), row(WIDTH, COL_AZ), row(WIDTH, 0), row(d, COL_GATT), row(d, COL_GRW),
                  row(d, 0), const((WIDTH, d)), const((WIDTH, d)), const((d, d)), const((1, d))],
        out_specs=row(d, 0),
        out_shape=jax.ShapeDtypeStruct((t, d), F32),
        compiler_params=pltpu.CompilerParams(
            dimension_semantics=("parallel",), vmem_limit_bytes=VMEM_LIMIT),
        name="outproj",
    )(ya, proj2d, yb, proj2d, proj2d, x2d, w_up_att, w_up_rw, w_out, g.reshape(1, d))


def _layer_params(l, w_in, rw_mu, rw_w0, rw_w_up, rw_a0, rw_a_up, rw_k_k, rw_k_a, rw_r_k,
                  rw_ln_g, rw_ln_b, rw_vmix_down, rw_vmix_mu, rw_vmix_up, rw_vmix0):
    w = w_in[l]
    d = w.shape[0]
    att = w[:, :4 * WIDTH]
    o = 4 * WIDTH
    rkv = w[:, o:o + 3 * WIDTH]
    lora = w[:, o + 3 * WIDTH:o + 3 * WIDTH + DECAY_LORA + ICLR_LORA]
    o = o + 3 * WIDTH + DECAY_LORA + ICLR_LORA
    rz = w[:, o:o + WIDTH]
    gates = w[:, o + WIDTH:]
    vd = jnp.zeros((d, LANES), F32)
    if l > 0:
        vd = vd.at[:, :VMIX_LORA].set(rw_vmix_down[l - 1])
    w_cat = jnp.concatenate([att, rkv, rz, gates, lora, vd], axis=1).astype(BF16)

    mu = rw_mu[l]
    zeros = jnp.zeros((DECAY_LORA, WIDTH), F32)
    lora_w = jnp.concatenate([jnp.concatenate([rw_w_up[l], zeros], axis=1),
                              jnp.concatenate([zeros, rw_a_up[l]], axis=1)], axis=0).astype(BF16)
    p = {
        "w_cat": w_cat,
        "mu_rkv": mu[:3 * WIDTH].reshape(3, WIDTH),
        "mu_lora": mu[3 * WIDTH:].reshape(1, LANES),
        "w0a0": jnp.concatenate([rw_w0[l], rw_a0[l]]).reshape(1, 2 * WIDTH),
        "lora_w": lora_w,
        "k_k": rw_k_k[l].reshape(1, WIDTH), "k_a": rw_k_a[l].reshape(1, WIDTH),
        "r_k": rw_r_k[l].reshape(1, WIDTH),
        "ln_g": rw_ln_g[l].reshape(1, WIDTH), "ln_b": rw_ln_b[l].reshape(1, WIDTH),
    }
    if l > 0:
        p["vmix_mu"] = jnp.zeros((1, LANES), F32).at[0, :VMIX_LORA].set(rw_vmix_mu[l - 1])
        p["vmix_up"] = jnp.zeros((LANES, WIDTH), F32).at[:VMIX_LORA].set(rw_vmix_up[l - 1]).astype(BF16)
        p["vmix0"] = rw_vmix0[l - 1].reshape(1, WIDTH)
    return p


def kernel(x, norm_pre, norm_post, w_in, rw_mu, rw_w0, rw_w_up, rw_a0, rw_a_up, rw_k_k, rw_k_a, rw_r_k, rw_ln_g, rw_ln_b, rw_vmix_down, rw_vmix_mu, rw_vmix_up, rw_vmix0, w_up_att, w_up_rw, w_out):
    b, s, d = x.shape
    assert d == D_MODEL and s % MOBA_BLOCK == 0
    depth = w_in.shape[0]
    x2d = x.reshape(b * s, d)
    vfirst = None
    for l in range(depth):
        p = _layer_params(l, w_in, rw_mu, rw_w0, rw_w_up, rw_a0, rw_a_up, rw_k_k, rw_k_a, rw_r_k,
                          rw_ln_g, rw_ln_b, rw_vmix_down, rw_vmix_mu, rw_vmix_up, rw_vmix0)
        proj2d = _inproj(x2d, norm_pre[l], p["w_cat"])
        proj = proj2d.reshape(b, s, PROJ_W)
        ya = _moba(proj)
        yb, vfirst = _rwkv(proj, vfirst, p)
        x2d = _outproj(ya.reshape(b * s, WIDTH), proj2d, yb.reshape(b * s, WIDTH), x2d,
                       w_up_att[l].astype(BF16), w_up_rw[l].astype(BF16), w_out[l].astype(BF16),
                       norm_post[l])
    return x2d.reshape(b, s, d)
```

```python
import functools

import jax
import jax.numpy as jnp
from jax import lax
from jax.experimental import pallas as pl
from jax.experimental.pallas import tpu as pltpu

F32 = jnp.float32
BF16 = jnp.bfloat16
ACT = BF16

D_MODEL = 1024
N_HEADS = 8
HEAD_DIM = 64
WIDTH = N_HEADS * HEAD_DIM
MOBA_BLOCK = 256
MOBA_TOPK = 3
DECAY_LORA = 64
ICLR_LORA = 64
VMIX_LORA = 32
RMS_EPS = 1e-6
GN_EPS = 64e-5
L2_EPS = 1e-12
NEG_INF = -1e30

LANES = 128
PAIR = 2 * HEAD_DIM
N_PAIRS = WIDTH // PAIR
V_ROWS = HEAD_DIM + 16
CHUNK = 64
RWKV_SUB = 256

COL_Q, COL_K, COL_V, COL_AZ = 0, 512, 1024, 1536
COL_R, COL_RK, COL_RV, COL_RZ = 2048, 2560, 3072, 3584
COL_GATT, COL_GRW = 4096, 5120
COL_LORA = 6144
COL_VD = 6272
PROJ_W = 6400

VMEM_LIMIT = 48 * 1024 * 1024


def _sigmoid(x):
    return 0.5 * jnp.tanh(0.5 * x) + 0.5


def _mm(a, b):
    return jnp.dot(a.astype(BF16), b.astype(BF16), preferred_element_type=F32)


def _mm_nt(a, b):
    return lax.dot_general(a.astype(BF16), b.astype(BF16), (((1,), (1,)), ((), ())),
                           preferred_element_type=F32)


def _mm_tn(a, b):
    return lax.dot_general(a.astype(BF16), b.astype(BF16), (((0,), (0,)), ((), ())),
                           preferred_element_type=F32)


def _interleave(gens):
    gens = list(gens)
    while gens:
        for g in list(gens):
            try:
                next(g)
            except StopIteration:
                gens.remove(g)


def _inproj_kernel(x_ref, g_ref, w_ref, o_ref, *, tn):
    x = x_ref[...]
    ms = jnp.mean(x * x, axis=-1, keepdims=True)
    h = (x * lax.rsqrt(ms + RMS_EPS) * g_ref[...]).astype(BF16)
    for n0 in range(0, o_ref.shape[1], tn):
        o_ref[:, n0:n0 + tn] = jnp.dot(h, w_ref[:, n0:n0 + tn],
                                       preferred_element_type=F32).astype(o_ref.dtype)


def _inproj(x2d, g, w_cat, *, tm=512, tn=1280):
    t, d = x2d.shape
    n = w_cat.shape[1]
    return pl.pallas_call(
        functools.partial(_inproj_kernel, tn=tn),
        grid=(t // tm,),
        in_specs=[
            pl.BlockSpec((tm, d), lambda i: (i, 0)),
            pl.BlockSpec((1, d), lambda i: (0, 0)),
            pl.BlockSpec((d, n), lambda i: (0, 0), pipeline_mode=pl.Buffered(1)),
        ],
        out_specs=pl.BlockSpec((tm, n), lambda i: (i, 0)),
        out_shape=jax.ShapeDtypeStruct((t, n), ACT),
        compiler_params=pltpu.CompilerParams(
            dimension_semantics=("parallel",), vmem_limit_bytes=VMEM_LIMIT),
        name="inproj",
    )(x2d, g.reshape(1, d), w_cat)


def _moba_kernel(q_ref, k_ref, v_ref, o_ref, vt_ref, qx_ref, kx_ref, s_ref, p_ref, *, n_blocks):
    hp = pl.program_id(1)
    blk = MOBA_BLOCK
    s_len = n_blocks * blk
    log2e = 1.4426950408889634
    q_scale = (HEAD_DIM ** -0.5) * log2e
    nt = (((1,), (1,)), ((), ()))
    sub = 32

    vt = v_ref[...].T
    ones_rows = jnp.where(lax.broadcasted_iota(jnp.int32, (V_ROWS - HEAD_DIM, s_len), 0) == 0, 1.0, 0.0).astype(BF16)
    for h in range(2):
        vt_ref[h] = jnp.concatenate([vt[h * HEAD_DIM:(h + 1) * HEAD_DIM], ones_rows], axis=0)

    mean_row = lax.broadcasted_iota(jnp.int32, (n_blocks, PAIR), 0)
    kmean = jnp.zeros((n_blocks, PAIR), F32)
    for n in range(n_blocks):
        kmean = jnp.where(mean_row == n,
                          jnp.mean(k_ref[n * blk:(n + 1) * blk, :].astype(F32), axis=0, keepdims=True), kmean)

    lane = lax.broadcasted_iota(jnp.int32, (1, PAIR), 1)
    head_lanes = (lane < HEAD_DIM, lane >= HEAD_DIM)
    nrow = lax.broadcasted_iota(jnp.int32, (n_blocks, s_len), 0)
    pos_i = lax.broadcasted_iota(jnp.int32, (n_blocks, s_len), 1)
    pblk = pos_i // blk
    pos = pos_i.astype(F32)
    past = nrow < pblk
    krow = lax.broadcasted_iota(jnp.int32, (blk, blk), 0)
    qcol = lax.broadcasted_iota(jnp.int32, (blk, blk), 1)
    causal = qcol >= krow

    km = jnp.concatenate([jnp.where(head_lanes[0], kmean, 0.0), jnp.where(head_lanes[1], kmean, 0.0)], axis=0)
    km_hi = km.astype(BF16)
    km_r = km - km_hi.astype(F32)
    km_mid = km_r.astype(BF16)
    km_lo = (km_r - km_mid.astype(F32)).astype(BF16)
    q_all = q_ref[...]
    gates = (lax.dot_general(km_lo, q_all, nt, preferred_element_type=F32)
             + lax.dot_general(km_mid, q_all, nt, preferred_element_type=F32)
             + lax.dot_general(km_hi, q_all, nt, preferred_element_type=F32))

    def split3(x):
        hi = x.astype(BF16).astype(F32)
        mid = (x - hi).astype(BF16).astype(F32)
        return hi, mid, x - hi - mid

    def rows8(r0, r1, r2, r3, r4, r5):
        out = jnp.zeros((n_blocks, s_len), F32)
        for idx, r in enumerate((r0, r1, r2, r3, r4, r5)):
            out = jnp.where(nrow == idx, r, out)
        return out

    ones = jnp.ones((n_blocks, s_len), F32)
    for h in range(2):
        head = lax.convert_element_type(hp * 2 + h + 1, F32)
        slope2 = jnp.exp2(jnp.zeros((1, 1), F32) - (8.0 / N_HEADS) * head) * log2e

        gm = jnp.where(past, gates[h * n_blocks:(h + 1) * n_blocks], NEG_INF)
        rank = jnp.zeros((n_blocks, s_len), F32)
        for m in range(n_blocks):
            gcol = gm[m:m + 1, :]
            beats = (gcol > gm) | ((gcol == gm) & (nrow > m))
            rank = rank + jnp.where(beats, 1.0, 0.0)
        visible = (past & (rank < MOBA_TOPK)) | (nrow == pblk)

        q_hi, q_mid, q_lo = split3(-slope2 * pos)
        k_hi, k_mid, k_lo = split3(slope2 * pos)
        q_extra = jnp.concatenate([rows8(q_hi, q_mid, q_lo, ones, ones, ones),
                                   jnp.where(visible, 0.0, NEG_INF)], axis=0)
        k_extra = jnp.concatenate([rows8(ones, ones, ones, k_hi, k_mid, k_lo),
                                   jnp.where(nrow == pblk, 1.0, 0.0)], axis=0)
        lead = jnp.zeros(((1 - h) * HEAD_DIM, s_len), F32)
        tail = jnp.zeros((PAIR - 2 * n_blocks - (1 - h) * HEAD_DIM, s_len), F32)
        q_full = jnp.concatenate([lead, q_extra, tail], axis=0) if h == 0 else jnp.concatenate([q_extra, tail], axis=0)
        k_full = jnp.concatenate([lead, k_extra, tail], axis=0) if h == 0 else jnp.concatenate([k_extra, tail], axis=0)
        qx_ref[h] = q_full.T.astype(BF16)
        kx_ref[h] = jnp.where(head_lanes[h], k_ref[...], k_full.T.astype(BF16))

    units = [(i, h) for i in range(n_blocks) for h in range(2)]
    st = {}

    def score_tile(u, j):
        i, h = u
        d = st[u]
        s_tile = lax.dot_general(kx_ref[h, j * blk:(j + 1) * blk, :], d["qm"], nt,
                                 preferred_element_type=F32)
        for c in range(0, blk, sub):
            sc = s_tile[c:c + sub]
            if j == i:
                sc = jnp.where(causal[c:c + sub], sc, NEG_INF)
            s_ref[d["slot"], j * blk + c:j * blk + c + sub, :] = sc
            for r in range(0, sub, 8):
                d["m8"] = sc[r:r + 8] if d["m8"] is None else jnp.maximum(d["m8"], sc[r:r + 8])

    def softmax_tile(u, j):
        d = st[u]
        for c in range(j * blk, (j + 1) * blk, sub):
            pc = jnp.exp2(s_ref[d["slot"], c:c + sub, :] - d["m"])
            p_ref[d["slot"], c:c + sub, :] = pc.astype(BF16)

    def finish(u):
        i, h = u
        d = st[u]
        n_k = (i + 1) * blk
        pv = jnp.dot(vt_ref[h, :, :n_k], p_ref[d["slot"], :n_k, :],
                     preferred_element_type=F32)
        d["out"] = pv[:HEAD_DIM] * (1.0 / pv[HEAD_DIM:HEAD_DIM + 1])
        if h == 1:
            out_t = jnp.concatenate([st[(i, 0)]["out"], d["out"]], axis=0)
            o_ref[i * blk:(i + 1) * blk, :] = out_t.T.astype(o_ref.dtype)

    prev = None
    for idx, u in enumerate(units):
        i, h = u
        rows = slice(i * blk, (i + 1) * blk)
        if h == 0:
            q_i = (q_ref[rows, :].astype(F32) * q_scale).astype(BF16)
        st[u] = {"qm": jnp.where(head_lanes[h], q_i, qx_ref[h, rows, :]), "slot": idx % 2,
                 "m8": None}
        n_prev = prev[0] + 1 if prev is not None else 0
        for step in range(max(i + 1, n_prev)):
            if step < i + 1:
                score_tile(u, step)
            if step < n_prev:
                softmax_tile(prev, step)
        if prev is not None:
            finish(prev)
        st[u]["m"] = jnp.max(st[u]["m8"], axis=0, keepdims=True)
        prev = u
    for step in range(prev[0] + 1):
        softmax_tile(prev, step)
    finish(prev)


def _moba(proj):
    b, s, _ = proj.shape
    n_blocks = s // MOBA_BLOCK
    assert n_blocks == 8, "the extra-column layout of the MoBA kernel is laid out for 8 key blocks"
    qo, ko, vo = COL_Q // PAIR, COL_K // PAIR, COL_V // PAIR
    return pl.pallas_call(
        functools.partial(_moba_kernel, n_blocks=n_blocks),
        grid=(b, N_PAIRS),
        in_specs=[
            pl.BlockSpec((None, s, PAIR), lambda i, p: (i, 0, qo + p)),
            pl.BlockSpec((None, s, PAIR), lambda i, p: (i, 0, ko + p)),
            pl.BlockSpec((None, s, PAIR), lambda i, p: (i, 0, vo + p)),
        ],
        out_specs=pl.BlockSpec((None, s, PAIR), lambda i, p: (i, 0, p)),
        out_shape=jax.ShapeDtypeStruct((b, s, WIDTH), ACT),
        scratch_shapes=[
            pltpu.VMEM((2, V_ROWS, s), BF16),
            pltpu.VMEM((2, s, PAIR), BF16),
            pltpu.VMEM((2, s, PAIR), BF16),
            pltpu.VMEM((2, s, MOBA_BLOCK), F32),
            pltpu.VMEM((2, s, MOBA_BLOCK), BF16),
        ],
        compiler_params=pltpu.CompilerParams(
            dimension_semantics=("parallel", "parallel"), vmem_limit_bytes=VMEM_LIMIT),
        name="moba",
    )(proj, proj, proj)


def _rwkv_kernel(*refs, has_vmix, ts):
    if has_vmix:
        (r_ref, k_ref, v_ref, rz_ref, lora_ref, vd_ref, vfirst_ref,
         mu_rkv_ref, mu_lora_ref, w0a0_ref, lora_w_ref, kk_ref, ka_ref, rk_ref, lng_ref, lnb_ref,
         vmix_mu_ref, vmix_up_ref, vmix0_ref,
         yb_ref, carry_rkv, carry_lora, carry_vd, state_ref) = refs
    else:
        (r_ref, k_ref, v_ref, rz_ref, lora_ref,
         mu_rkv_ref, mu_lora_ref, w0a0_ref, lora_w_ref, kk_ref, ka_ref, rk_ref, lng_ref, lnb_ref,
         yb_ref, vfirst_ref, carry_rkv, carry_lora, state_ref) = refs
        carry_vd = None

    @pl.when(pl.program_id(1) == 0)
    def _():
        carry_rkv[...] = jnp.zeros_like(carry_rkv)
        carry_lora[...] = jnp.zeros_like(carry_lora)
        if carry_vd is not None:
            carry_vd[...] = jnp.zeros_like(carry_vd)
        state_ref[...] = jnp.zeros_like(state_ref)

    sub = RWKV_SUB
    n_sub = ts // sub
    n_chunks = sub // CHUNK
    n2 = 2 * CHUNK
    pcs = [(p, c) for c in range(n_chunks) for p in range(N_PAIRS)]

    sub_row = lax.broadcasted_iota(jnp.int32, (sub, sub), 0)
    sub_col = lax.broadcasted_iota(jnp.int32, (sub, sub), 1)
    diff_mat = (jnp.where(sub_row == sub_col + 1, 1.0, 0.0)
                - jnp.where(sub_row == sub_col, 1.0, 0.0)).astype(r_ref.dtype)
    ltri = jnp.where((sub_row // CHUNK == sub_col // CHUNK) & (sub_col <= sub_row), 1.0, 0.0).astype(BF16)
    top_row = lax.broadcasted_iota(jnp.int32, (8, 1), 0) == 0
    lane = lax.broadcasted_iota(jnp.int32, (sub, LANES), 1)
    head0_t = lane < HEAD_DIM
    head0 = lax.broadcasted_iota(jnp.int32, (CHUNK, PAIR), 1) < HEAD_DIM
    wrow = lax.broadcasted_iota(jnp.int32, (CHUNK, PAIR), 0)
    wcol = lax.broadcasted_iota(jnp.int32, (CHUNK, PAIR), 1) % CHUNK
    strict = wcol < wrow
    incl = wcol <= wrow
    eye = jnp.where(wcol == wrow, 1.0, 0.0).astype(F32)
    srow = lax.broadcasted_iota(jnp.int32, (n2, n2), 0)
    scol = lax.broadcasted_iota(jnp.int32, (n2, n2), 1)
    same_head = (srow // CHUNK) == (scol // CHUNK)
    mu_rkv = mu_rkv_ref[...]

    def shift(y_act, carry_ref, row_idx, mu):
        y = y_act.astype(F32)
        delta = jnp.dot(diff_mat, y_act, preferred_element_type=F32)
        head = jnp.where(top_row, delta[:8] + carry_ref[row_idx:row_idx + 1, :], delta[:8])
        delta = jnp.concatenate([head, delta[8:]], axis=0)
        carry_ref[row_idx:row_idx + 1, :] = y[sub - 1:sub, :]
        return y + delta * mu

    def headsum(x):
        s0 = jnp.sum(jnp.where(head0_t, x, 0.0), axis=-1, keepdims=True)
        s1 = jnp.sum(jnp.where(head0_t, 0.0, x), axis=-1, keepdims=True)
        return jnp.where(head0_t, s0, s1)

    def stack(x):
        zero = jnp.zeros_like(x)
        return jnp.concatenate([jnp.where(head0, x, zero), jnp.where(head0, zero, x)], axis=0)

    data = [dict() for _ in range(n_sub)]

    def prep(s):
        d = data[s]
        rows = slice(s * sub, (s + 1) * sub)
        rs = shift(r_ref[rows, :], carry_rkv, 0, mu_rkv[0:1, :])
        yield
        ks = shift(k_ref[rows, :], carry_rkv, 1, mu_rkv[1:2, :])
        yield
        vs = shift(v_ref[rows, :], carry_rkv, 2, mu_rkv[2:3, :])
        yield
        lo = shift(lora_ref[rows, :], carry_lora, 0, mu_lora_ref[...])
        z = jnp.where(lane < DECAY_LORA, jnp.tanh(lo), lo)
        wa = w0a0_ref[...] + _mm(z, lora_w_ref[...])
        lw = (-0.6065306597126334) * _sigmoid(wa[:, :WIDTH])
        yield
        a = _sigmoid(wa[:, WIDTH:])
        if has_vmix:
            vd = shift(vd_ref[rows, :], carry_vd, 0, vmix_mu_ref[...])
            mix = _sigmoid(vmix0_ref[...] + _mm(vd, vmix_up_ref[...]))
            vr = vs + (vfirst_ref[rows, :] - vs) * mix
        else:
            vfirst_ref[rows, :] = vs
            vr = vs
        yield
        kx = ks * kk_ref[...]
        kmod = ks * (1.0 + (a - 1.0) * ka_ref[...])
        d["rkr"] = rs * kmod * rk_ref[...]
        d["vr"] = vr
        yield
        kk_all = []
        for p in range(N_PAIRS):
            kxp = kx[:, p * PAIR:(p + 1) * PAIR]
            kk_all.append(kxp * jnp.minimum(lax.rsqrt(headsum(kxp * kxp)), 1.0 / L2_EPS))
            yield
        kk = jnp.concatenate(kk_all, axis=1)
        na = -kk
        nb = kk * a
        lw_hi = lw.astype(BF16)
        lw_r = lw - lw_hi.astype(F32)
        lw_mid = lw_r.astype(BF16)
        lw_lo = (lw_r - lw_mid.astype(F32)).astype(BF16)
        cum = (jnp.dot(ltri, lw_lo, preferred_element_type=F32) + jnp.dot(ltri, lw_mid, preferred_element_type=F32)
               + jnp.dot(ltri, lw_hi, preferred_element_type=F32))
        yield
        a_t = na * jnp.exp(cum - lw)
        r_t = rs * jnp.exp(cum)
        yield
        e_neg = jnp.exp(-cum)
        b_t = nb * e_neg
        k_t = kmod * e_neg
        yield
        lhs_g, rhs_g, a_s, r_n, b_h, k_h, v_s, gam = {}, {}, {}, {}, {}, {}, {}, {}
        for c in range(n_chunks):
            rc = slice(c * CHUNK, (c + 1) * CHUNK)
            tot = cum[(c + 1) * CHUNK - 1:(c + 1) * CHUNK, :]
            e_rel = jnp.exp(tot - cum[rc])
            bh_c = nb[rc] * e_rel
            kh_c = kmod[rc] * e_rel
            gam_c = jnp.exp(tot)
            for p in range(N_PAIRS):
                sl = slice(p * PAIR, (p + 1) * PAIR)
                pc = (p, c)
                a_n = a_t[rc, sl].astype(BF16)
                r_n[pc] = r_t[rc, sl]
                lhs_g[pc] = jnp.concatenate([a_n, r_n[pc].astype(BF16)], axis=0)
                rhs_g[pc] = jnp.concatenate([stack(b_t[rc, sl].astype(BF16)),
                                             stack(k_t[rc, sl].astype(BF16))], axis=0)
                a_s[pc] = stack(a_n)
                b_h[pc] = bh_c[:, sl].astype(BF16)
                k_h[pc] = stack(kh_c[:, sl])
                v_s[pc] = stack(vr[rc, sl].astype(BF16))
                gam[pc] = gam_c[:, sl]
            yield
        d.update(lhs_g=lhs_g, rhs_g=rhs_g, a_s=a_s, r_n=r_n, b_h=b_h, k_h=k_h, v_s=v_s, gam=gam)

    def operators(s):
        d = data[s]
        a_ab, a_ak, a_rb, a_rk = {}, {}, {}, {}
        for pc in pcs:
            g = _mm_nt(d["lhs_g"][pc], d["rhs_g"][pc])
            a_ab[pc] = jnp.where(strict, g[:CHUNK, :n2], 0.0)
            a_ak[pc] = jnp.where(strict, g[:CHUNK, n2:], 0.0).astype(BF16)
            a_rb[pc] = jnp.where(incl, g[CHUNK:, :n2], 0.0).astype(BF16)
            a_rk[pc] = jnp.where(incl, g[CHUNK:, n2:], 0.0)
        yield
        t_inv = {pc: eye + a_ab[pc] for pc in pcs}
        pw = {pc: a_ab[pc].astype(BF16) for pc in pcs}
        pw = {pc: _mm(pw[pc], stack(pw[pc])).astype(BF16) for pc in pcs}
        yield
        for _ in range(CHUNK.bit_length() - 3):
            both = {pc: _mm(jnp.concatenate([pw[pc], t_inv[pc].astype(BF16)], axis=0), stack(pw[pc])) for pc in pcs}
            t_inv = {pc: t_inv[pc] + both[pc][CHUNK:] for pc in pcs}
            pw = {pc: both[pc][:CHUNK].astype(BF16) for pc in pcs}
            yield
        t_inv = {pc: (t_inv[pc] + _mm(t_inv[pc], stack(pw[pc]))).astype(BF16) for pc in pcs}
        yield
        z = {pc: jnp.where(same_head, _mm_tn(t_inv[pc], d["b_h"][pc]), 0.0).astype(BF16) for pc in pcs}
        w = {pc: _mm(a_rb[pc], stack(t_inv[pc])).astype(BF16) for pc in pcs}
        yield
        la_ak = {pc: jnp.concatenate([d["a_s"][pc], stack(a_ak[pc])], axis=1) for pc in pcs}
        my = {pc: _mm_tn(la_ak[pc], z[pc]) for pc in pcs}
        wq = {pc: _mm(w[pc], la_ak[pc]) for pc in pcs}
        yield
        d["m_t"] = {pc: my[pc][:n2].astype(BF16) for pc in pcs}
        y = {pc: (my[pc][n2:] + d["k_h"][pc]).astype(BF16) for pc in pcs}
        d["q_p"] = {pc: (d["r_n"][pc] + wq[pc][:, :n2]).astype(BF16) for pc in pcs}
        p_p = {pc: (wq[pc][:, n2:] + a_rk[pc]).astype(BF16) for pc in pcs}
        yield
        d["g_s"] = {pc: _mm_tn(d["v_s"][pc], y[pc]) for pc in pcs}
        d["o_loc"] = {pc: _mm(p_p[pc], d["v_s"][pc]) for pc in pcs}

    states = [state_ref[p] for p in range(N_PAIRS)]

    def recurrence(s):
        d = data[s]
        outs = [[] for _ in range(N_PAIRS)]
        for c in range(n_chunks):
            for p in range(N_PAIRS):
                pc = (p, c)
                sb = states[p].astype(BF16)
                outs[p].append(_mm_nt(d["q_p"][pc], sb) + d["o_loc"][pc])
                states[p] = states[p] * d["gam"][pc] + _mm(sb, d["m_t"][pc]) + d["g_s"][pc]
        d["o"] = [jnp.concatenate(o, axis=0) if len(o) > 1 else o[0] for o in outs]

    def epilogue(s):
        d = data[s]
        rows = slice(s * sub, (s + 1) * sub)
        for p in range(N_PAIRS):
            sl = slice(p * PAIR, (p + 1) * PAIR)
            o_p = d["o"][p]
            bonus = headsum(d["rkr"][:, sl]) * d["vr"][:, sl]
            mu = headsum(o_p) * (1.0 / HEAD_DIM)
            dev = o_p - mu
            var = headsum(dev * dev) * (1.0 / HEAD_DIM)
            on = dev * lax.rsqrt(var + GN_EPS) * lng_ref[:, sl] + lnb_ref[:, sl]
            rz_p = rz_ref[rows, sl].astype(F32)
            yb_ref[rows, sl] = ((on + bonus) * (rz_p * _sigmoid(rz_p))).astype(yb_ref.dtype)
            yield

    _interleave([prep(0)])
    for s in range(n_sub):
        gens = [operators(s)]
        if s + 1 < n_sub:
            gens.append(prep(s + 1))
        if s > 0:
            gens.append(epilogue(s - 1))
        _interleave(gens)
        recurrence(s)
    _interleave([epilogue(n_sub - 1)])
    for p in range(N_PAIRS):
        state_ref[p] = states[p]


def _rwkv(proj, vfirst, p, *, ts=256):
    b, s, _ = proj.shape
    has_vmix = vfirst is not None
    row = lambda width, col: pl.BlockSpec((None, ts, width), lambda i, t: (i, t, col // width))
    const = lambda shape: pl.BlockSpec(shape, lambda i, t: (0,) * len(shape))
    act = pl.BlockSpec((None, ts, WIDTH), lambda i, t: (i, t, 0))

    in_specs = [row(WIDTH, COL_R), row(WIDTH, COL_RK), row(WIDTH, COL_RV), row(WIDTH, COL_RZ),
                row(LANES, COL_LORA)]
    args = [proj, proj, proj, proj, proj]
    if has_vmix:
        in_specs += [row(LANES, COL_VD), act]
        args += [proj, vfirst]
    in_specs += [const((3, WIDTH)), const((1, LANES)), const((1, 2 * WIDTH)), const((LANES, 2 * WIDTH))]
    args += [p["mu_rkv"], p["mu_lora"], p["w0a0"], p["lora_w"]]
    for name in ("k_k", "k_a", "r_k", "ln_g", "ln_b"):
        in_specs.append(const((1, WIDTH)))
        args.append(p[name])
    if has_vmix:
        in_specs += [const((1, LANES)), const((LANES, WIDTH)), const((1, WIDTH))]
        args += [p["vmix_mu"], p["vmix_up"], p["vmix0"]]

    out_shape = [jax.ShapeDtypeStruct((b, s, WIDTH), ACT)]
    out_specs = [act]
    scratch = [pltpu.VMEM((8, WIDTH), F32), pltpu.VMEM((8, LANES), F32)]
    if has_vmix:
        scratch.append(pltpu.VMEM((8, LANES), F32))
    else:
        out_shape.append(jax.ShapeDtypeStruct((b, s, WIDTH), F32))
        out_specs.append(act)
    scratch.append(pltpu.VMEM((N_PAIRS, PAIR, PAIR), F32))

    res = pl.pallas_call(
        functools.partial(_rwkv_kernel, has_vmix=has_vmix, ts=ts),
        grid=(b, s // ts),
        in_specs=in_specs,
        out_specs=out_specs,
        out_shape=out_shape,
        scratch_shapes=scratch,
        compiler_params=pltpu.CompilerParams(
            dimension_semantics=("parallel", "arbitrary"), vmem_limit_bytes=VMEM_LIMIT),
        name="rwkv_vmix" if has_vmix else "rwkv",
    )(*args)
    return (res[0], vfirst) if has_vmix else (res[0], res[1])


def _outproj_kernel(ya_ref, az_ref, yb_ref, gatt_ref, grw_ref, x_ref,
                    wua_ref, wub_ref, wout_ref, g_ref, o_ref):
    az = az_ref[...].astype(F32)
    ya = ya_ref[...].astype(F32) * (az * _sigmoid(az))
    u = (_sigmoid(gatt_ref[...].astype(F32)) * _mm(ya, wua_ref[...])
         + _sigmoid(grw_ref[...].astype(F32)) * _mm(yb_ref[...], wub_ref[...]))
    y = _mm(u, wout_ref[...])
    ms = jnp.mean(y * y, axis=-1, keepdims=True)
    o_ref[...] = x_ref[...] + y * lax.rsqrt(ms + RMS_EPS) * g_ref[...]


def _outproj(ya, proj2d, yb, x2d, w_up_att, w_up_rw, w_out, g, *, tm=1024):
    t, d = x2d.shape
    row = lambda width, col: pl.BlockSpec((tm, width), lambda i: (i, col // width))
    const = lambda shape: pl.BlockSpec(shape, lambda i: (0, 0))
    return pl.pallas_call(
        _outproj_kernel,
        grid=(t // tm,),
        in_specs=[row(WIDTH, 0), row(WIDTH, COL_AZ), row(WIDTH, 0), row(d, COL_GATT), row(d, COL_GRW),
                  row(d, 0), const((WIDTH, d)), const((WIDTH, d)), const((d, d)), const((1, d))],
        out_specs=row(d, 0),
        out_shape=jax.ShapeDtypeStruct((t, d), F32),
        compiler_params=pltpu.CompilerParams(
            dimension_semantics=("parallel",), vmem_limit_bytes=VMEM_LIMIT),
        name="outproj",
    )(ya, proj2d, yb, proj2d, proj2d, x2d, w_up_att, w_up_rw, w_out, g.reshape(1, d))


def _layer_params(l, w_in, rw_mu, rw_w0, rw_w_up, rw_a0, rw_a_up, rw_k_k, rw_k_a, rw_r_k,
                  rw_ln_g, rw_ln_b, rw_vmix_down, rw_vmix_mu, rw_vmix_up, rw_vmix0):
    w = w_in[l]
    d = w.shape[0]
    att = w[:, :4 * WIDTH]
    o = 4 * WIDTH
    rkv = w[:, o:o + 3 * WIDTH]
    lora = w[:, o + 3 * WIDTH:o + 3 * WIDTH + DECAY_LORA + ICLR_LORA]
    o = o + 3 * WIDTH + DECAY_LORA + ICLR_LORA
    rz = w[:, o:o + WIDTH]
    gates = w[:, o + WIDTH:]
    pad = LANES - VMIX_LORA
    vd = jnp.pad(rw_vmix_down[l - 1], ((0, 0), (0, pad))) if l > 0 else jnp.zeros((d, LANES), F32)
    w_cat = jnp.concatenate([att, rkv, rz, gates, lora, vd], axis=1).astype(BF16)

    mu = rw_mu[l]
    zeros = jnp.zeros((DECAY_LORA, WIDTH), F32)
    lora_w = jnp.concatenate([jnp.concatenate([rw_w_up[l], zeros], axis=1),
                              jnp.concatenate([zeros, rw_a_up[l]], axis=1)], axis=0).astype(BF16)
    p = {
        "w_cat": w_cat,
        "mu_rkv": mu[:3 * WIDTH].reshape(3, WIDTH),
        "mu_lora": mu[3 * WIDTH:].reshape(1, LANES),
        "w0a0": jnp.concatenate([rw_w0[l], rw_a0[l]]).reshape(1, 2 * WIDTH),
        "lora_w": lora_w,
        "k_k": rw_k_k[l].reshape(1, WIDTH), "k_a": rw_k_a[l].reshape(1, WIDTH),
        "r_k": rw_r_k[l].reshape(1, WIDTH),
        "ln_g": rw_ln_g[l].reshape(1, WIDTH), "ln_b": rw_ln_b[l].reshape(1, WIDTH),
    }
    if l > 0:
        p["vmix_mu"] = jnp.pad(rw_vmix_mu[l - 1], (0, pad)).reshape(1, LANES)
        p["vmix_up"] = jnp.pad(rw_vmix_up[l - 1], ((0, pad), (0, 0))).astype(BF16)
        p["vmix0"] = rw_vmix0[l - 1].reshape(1, WIDTH)
    return p


def kernel(x, norm_pre, norm_post, w_in, rw_mu, rw_w0, rw_w_up, rw_a0, rw_a_up, rw_k_k, rw_k_a, rw_r_k, rw_ln_g, rw_ln_b, rw_vmix_down, rw_vmix_mu, rw_vmix_up, rw_vmix0, w_up_att, w_up_rw, w_out):
    b, s, d = x.shape
    assert d == D_MODEL and s % MOBA_BLOCK == 0
    depth = w_in.shape[0]
    x2d = x.reshape(b * s, d)
    vfirst = None
    for l in range(depth):
        p = _layer_params(l, w_in, rw_mu, rw_w0, rw_w_up, rw_a0, rw_a_up, rw_k_k, rw_k_a, rw_r_k,
                          rw_ln_g, rw_ln_b, rw_vmix_down, rw_vmix_mu, rw_vmix_up, rw_vmix0)
        proj2d = _inproj(x2d, norm_pre[l], p["w_cat"])
        proj = proj2d.reshape(b, s, PROJ_W)
        ya = _moba(proj)
        yb, vfirst = _rwkv(proj, vfirst, p)
        x2d = _outproj(ya.reshape(b * s, WIDTH), proj2d, yb.reshape(b * s, WIDTH), x2d,
                       w_up_att[l].astype(BF16), w_up_rw[l].astype(BF16), w_out[l].astype(BF16),
                       norm_post[l])
    return x2d.reshape(b, s, d)
```

```python
import functools

import jax
import jax.numpy as jnp
from jax import lax
from jax.experimental import pallas as pl
from jax.experimental.pallas import tpu as pltpu

F32 = jnp.float32
BF16 = jnp.bfloat16
ACT = BF16

D_MODEL = 1024
N_HEADS = 8
HEAD_DIM = 64
WIDTH = N_HEADS * HEAD_DIM
MOBA_BLOCK = 256
MOBA_TOPK = 3
DECAY_LORA = 64
ICLR_LORA = 64
VMIX_LORA = 32
RMS_EPS = 1e-6
GN_EPS = 64e-5
L2_EPS = 1e-12
NEG_INF = -1e30

LANES = 128
PAIR = 2 * HEAD_DIM
N_PAIRS = WIDTH // PAIR
V_ROWS = HEAD_DIM + 16
CHUNK = 64
RWKV_SUB = 256

COL_Q, COL_K, COL_V, COL_AZ = 0, 512, 1024, 1536
COL_R, COL_RK, COL_RV, COL_RZ = 2048, 2560, 3072, 3584
COL_GATT, COL_GRW = 4096, 5120
COL_LORA = 6144
COL_VD = 6272
PROJ_W = 6400

VMEM_LIMIT = 48 * 1024 * 1024
LOG2E = 1.4426950408889634


def _sigmoid(x):
    return 0.5 * jnp.tanh(0.5 * x) + 0.5


def _mm(a, b):
    return jnp.dot(a.astype(BF16), b.astype(BF16), preferred_element_type=F32)


def _mm_nt(a, b):
    return lax.dot_general(a.astype(BF16), b.astype(BF16), (((1,), (1,)), ((), ())),
                           preferred_element_type=F32)


def _mm_tn(a, b):
    return lax.dot_general(a.astype(BF16), b.astype(BF16), (((0,), (0,)), ((), ())),
                           preferred_element_type=F32)


def _interleave(gens):
    gens = list(gens)
    while gens:
        for g in list(gens):
            try:
                next(g)
            except StopIteration:
                gens.remove(g)


def _inproj_kernel(x_ref, g_ref, w_ref, o_ref, *, tn):
    x = x_ref[...]
    ms = jnp.mean(x * x, axis=-1, keepdims=True)
    h = (x * lax.rsqrt(ms + RMS_EPS) * g_ref[...]).astype(BF16)
    for n0 in range(0, o_ref.shape[1], tn):
        o_ref[:, n0:n0 + tn] = jnp.dot(h, w_ref[:, n0:n0 + tn],
                                       preferred_element_type=F32).astype(o_ref.dtype)


def _inproj(x2d, g, w_cat, *, tm=512, tn=1280):
    t, d = x2d.shape
    n = w_cat.shape[1]
    return pl.pallas_call(
        functools.partial(_inproj_kernel, tn=tn),
        grid=(t // tm,),
        in_specs=[
            pl.BlockSpec((tm, d), lambda i: (i, 0)),
            pl.BlockSpec((1, d), lambda i: (0, 0)),
            pl.BlockSpec((d, n), lambda i: (0, 0), pipeline_mode=pl.Buffered(1)),
        ],
        out_specs=pl.BlockSpec((tm, n), lambda i: (i, 0)),
        out_shape=jax.ShapeDtypeStruct((t, n), ACT),
        compiler_params=pltpu.CompilerParams(
            dimension_semantics=("parallel",), vmem_limit_bytes=VMEM_LIMIT),
        name="inproj",
    )(x2d, g.reshape(1, d), w_cat)


def _moba_kernel(q_ref, k_ref, v_ref, o_ref, vt_ref, qx_ref, kx_ref, s_ref, p_ref, *, n_blocks):
    hp = pl.program_id(1)
    blk = MOBA_BLOCK
    s_len = n_blocks * blk
    q_scale = (HEAD_DIM ** -0.5) * LOG2E
    nt = (((1,), (1,)), ((), ()))
    sub = 32
    tk = MOBA_BLOCK

    vt = v_ref[...].T
    ones_rows = jnp.where(lax.broadcasted_iota(jnp.int32, (V_ROWS - HEAD_DIM, s_len), 0) == 0, 1.0, 0.0).astype(BF16)
    for h in range(2):
        vt_ref[h] = jnp.concatenate([vt[h * HEAD_DIM:(h + 1) * HEAD_DIM], ones_rows], axis=0)

    mean_row = lax.broadcasted_iota(jnp.int32, (n_blocks, PAIR), 0)
    kmean = jnp.zeros((n_blocks, PAIR), F32)
    for n in range(n_blocks):
        kmean = jnp.where(mean_row == n,
                          jnp.mean(k_ref[n * blk:(n + 1) * blk, :].astype(F32), axis=0, keepdims=True), kmean)

    lane = lax.broadcasted_iota(jnp.int32, (1, PAIR), 1)
    head_lanes = (lane < HEAD_DIM, lane >= HEAD_DIM)
    nrow = lax.broadcasted_iota(jnp.int32, (n_blocks, s_len), 0)
    pos_i = lax.broadcasted_iota(jnp.int32, (n_blocks, s_len), 1)
    pblk = pos_i // blk
    pos = pos_i.astype(F32)
    past = nrow < pblk
    krow = lax.broadcasted_iota(jnp.int32, (blk, blk), 0)
    qcol = lax.broadcasted_iota(jnp.int32, (blk, blk), 1)
    causal = qcol >= krow

    km = jnp.concatenate([jnp.where(head_lanes[0], kmean, 0.0), jnp.where(head_lanes[1], kmean, 0.0)], axis=0)
    km_hi = km.astype(BF16)
    km_r = km - km_hi.astype(F32)
    km_mid = km_r.astype(BF16)
    km_lo = (km_r - km_mid.astype(F32)).astype(BF16)
    parts = lax.dot_general(jnp.concatenate([km_lo, km_mid, km_hi], axis=0), q_ref[...], nt,
                            preferred_element_type=F32)
    ng = 2 * n_blocks
    gates = parts[:ng] + parts[ng:2 * ng] + parts[2 * ng:]

    def split3(x):
        hi = x.astype(BF16).astype(F32)
        mid = (x - hi).astype(BF16).astype(F32)
        return hi, mid, x - hi - mid

    def rows8(r0, r1, r2, r3, r4, r5):
        out = jnp.zeros((n_blocks, s_len), F32)
        for idx, r in enumerate((r0, r1, r2, r3, r4, r5)):
            out = jnp.where(nrow == idx, r, out)
        return out

    ones = jnp.ones((n_blocks, s_len), F32)
    for h in range(2):
        head = lax.convert_element_type(hp * 2 + h + 1, F32)
        slope2 = jnp.exp2(jnp.zeros((1, 1), F32) - (8.0 / N_HEADS) * head) * LOG2E

        gm = jnp.where(past, gates[h * n_blocks:(h + 1) * n_blocks], NEG_INF)
        rank = jnp.zeros((n_blocks, s_len), F32)
        for m in range(n_blocks):
            gcol = gm[m:m + 1, :]
            beats = (gcol > gm) | ((gcol == gm) & (nrow > m))
            rank = rank + jnp.where(beats, 1.0, 0.0)
        visible = (past & (rank < MOBA_TOPK)) | (nrow == pblk)

        q_hi, q_mid, q_lo = split3(-slope2 * pos)
        k_hi, k_mid, k_lo = split3(slope2 * pos)
        q_extra = jnp.concatenate([rows8(q_hi, q_mid, q_lo, ones, ones, ones),
                                   jnp.where(visible, 0.0, NEG_INF)], axis=0)
        k_extra = jnp.concatenate([rows8(ones, ones, ones, k_hi, k_mid, k_lo),
                                   jnp.where(nrow == pblk, 1.0, 0.0)], axis=0)
        lead = jnp.zeros(((1 - h) * HEAD_DIM, s_len), F32)
        tail = jnp.zeros((PAIR - 2 * n_blocks - (1 - h) * HEAD_DIM, s_len), F32)
        q_full = jnp.concatenate([lead, q_extra, tail], axis=0) if h == 0 else jnp.concatenate([q_extra, tail], axis=0)
        k_full = jnp.concatenate([lead, k_extra, tail], axis=0) if h == 0 else jnp.concatenate([k_extra, tail], axis=0)
        qx_ref[h] = q_full.T.astype(BF16)
        kx_ref[h] = jnp.where(head_lanes[h], k_ref[...], k_full.T.astype(BF16))

    units = [(i, h) for i in range(n_blocks) for h in range(2)]
    st = {}

    def score_tile(u, t):
        i, h = u
        d = st[u]
        k0 = t * tk
        s_tile = lax.dot_general(kx_ref[h, k0:k0 + tk, :], d["qm"], nt,
                                 preferred_element_type=F32)
        for c in range(0, tk, sub):
            sc = s_tile[c:c + sub]
            if k0 // blk == i:
                r0 = k0 % blk + c
                sc = jnp.where(causal[r0:r0 + sub], sc, NEG_INF)
            s_ref[d["slot"], k0 + c:k0 + c + sub, :] = sc
            for r in range(0, sub, 8):
                d["m8"] = sc[r:r + 8] if d["m8"] is None else jnp.maximum(d["m8"], sc[r:r + 8])

    def softmax_tile(u, t):
        d = st[u]
        for c in range(t * tk, (t + 1) * tk, sub):
            pc = jnp.exp2(s_ref[d["slot"], c:c + sub, :] - d["m"])
            p_ref[d["slot"], c:c + sub, :] = pc.astype(BF16)

    def finish(u):
        i, h = u
        d = st[u]
        n_k = (i + 1) * blk
        pv = jnp.dot(vt_ref[h, :, :n_k], p_ref[d["slot"], :n_k, :],
                     preferred_element_type=F32)
        d["out"] = pv[:HEAD_DIM] * (1.0 / pv[HEAD_DIM:HEAD_DIM + 1])
        if h == 1:
            out_t = jnp.concatenate([st[(i, 0)]["out"], d["out"]], axis=0)
            o_ref[i * blk:(i + 1) * blk, :] = out_t.T.astype(o_ref.dtype)

    prev = None
    for idx, u in enumerate(units):
        i, h = u
        rows = slice(i * blk, (i + 1) * blk)
        if h == 0:
            q_i = (q_ref[rows, :].astype(F32) * q_scale).astype(BF16)
        st[u] = {"qm": jnp.where(head_lanes[h], q_i, qx_ref[h, rows, :]), "slot": idx % 2,
                 "m8": None}
        n_cur = (i + 1) * blk // tk
        n_prev = (prev[0] + 1) * blk // tk if prev is not None else 0
        for step in range(max(n_cur, n_prev)):
            if step < n_cur:
                score_tile(u, step)
            if step < n_prev:
                softmax_tile(prev, step)
        if prev is not None:
            finish(prev)
        st[u]["m"] = jnp.max(st[u]["m8"], axis=0, keepdims=True)
        prev = u
    for step in range((prev[0] + 1) * blk // tk):
        softmax_tile(prev, step)
    finish(prev)


def _moba(proj):
    b, s, _ = proj.shape
    n_blocks = s // MOBA_BLOCK
    assert n_blocks == 8, "the extra-column layout of the MoBA kernel is laid out for 8 key blocks"
    qo, ko, vo = COL_Q // PAIR, COL_K // PAIR, COL_V // PAIR
    return pl.pallas_call(
        functools.partial(_moba_kernel, n_blocks=n_blocks),
        grid=(b, N_PAIRS),
        in_specs=[
            pl.BlockSpec((None, s, PAIR), lambda i, p: (i, 0, qo + p)),
            pl.BlockSpec((None, s, PAIR), lambda i, p: (i, 0, ko + p)),
            pl.BlockSpec((None, s, PAIR), lambda i, p: (i, 0, vo + p)),
        ],
        out_specs=pl.BlockSpec((None, s, PAIR), lambda i, p: (i, 0, p)),
        out_shape=jax.ShapeDtypeStruct((b, s, WIDTH), ACT),
        scratch_shapes=[
            pltpu.VMEM((2, V_ROWS, s), BF16),
            pltpu.VMEM((2, s, PAIR), BF16),
            pltpu.VMEM((2, s, PAIR), BF16),
            pltpu.VMEM((2, s, MOBA_BLOCK), F32),
            pltpu.VMEM((2, s, MOBA_BLOCK), BF16),
        ],
        compiler_params=pltpu.CompilerParams(
            dimension_semantics=("parallel", "parallel"), vmem_limit_bytes=VMEM_LIMIT),
        name="moba",
    )(proj, proj, proj)


def _rwkv_kernel(*refs, has_vmix, ts):
    if has_vmix:
        (r_ref, k_ref, v_ref, rz_ref, lora_ref, vd_ref, vfirst_ref,
         mu_rkv_ref, mu_lora_ref, w0a0_ref, lora_w_ref, kk_ref, ka_ref, rk_ref, lng_ref, lnb_ref,
         vmix_mu_ref, vmix_up_ref, vmix0_ref,
         yb_ref, carry_rkv, carry_lora, carry_vd, state_ref) = refs
    else:
        (r_ref, k_ref, v_ref, rz_ref, lora_ref,
         mu_rkv_ref, mu_lora_ref, w0a0_ref, lora_w_ref, kk_ref, ka_ref, rk_ref, lng_ref, lnb_ref,
         yb_ref, vfirst_ref, carry_rkv, carry_lora, state_ref) = refs
        carry_vd = None

    @pl.when(pl.program_id(1) == 0)
    def _():
        carry_rkv[...] = jnp.zeros_like(carry_rkv)
        carry_lora[...] = jnp.zeros_like(carry_lora)
        if carry_vd is not None:
            carry_vd[...] = jnp.zeros_like(carry_vd)
        state_ref[...] = jnp.zeros_like(state_ref)

    sub = RWKV_SUB
    n_sub = ts // sub
    n_chunks = sub // CHUNK
    n2 = 2 * CHUNK
    pcs = [(p, c) for c in range(n_chunks) for p in range(N_PAIRS)]

    sub_row = lax.broadcasted_iota(jnp.int32, (sub, sub), 0)
    sub_col = lax.broadcasted_iota(jnp.int32, (sub, sub), 1)
    diff_mat = (jnp.where(sub_row == sub_col + 1, 1.0, 0.0)
                - jnp.where(sub_row == sub_col, 1.0, 0.0)).astype(r_ref.dtype)
    ltri = jnp.where((sub_row // CHUNK == sub_col // CHUNK) & (sub_col <= sub_row), 1.0, 0.0).astype(BF16)
    top_row = lax.broadcasted_iota(jnp.int32, (8, 1), 0) == 0
    lane = lax.broadcasted_iota(jnp.int32, (sub, LANES), 1)
    head0_t = lane < HEAD_DIM
    head0 = lax.broadcasted_iota(jnp.int32, (CHUNK, PAIR), 1) < HEAD_DIM
    wrow = lax.broadcasted_iota(jnp.int32, (CHUNK, PAIR), 0)
    wcol = lax.broadcasted_iota(jnp.int32, (CHUNK, PAIR), 1) % CHUNK
    strict = wcol < wrow
    incl = wcol <= wrow
    eye = jnp.where(wcol == wrow, 1.0, 0.0).astype(F32)
    srow = lax.broadcasted_iota(jnp.int32, (n2, n2), 0)
    scol = lax.broadcasted_iota(jnp.int32, (n2, n2), 1)
    same_head = (srow // CHUNK) == (scol // CHUNK)
    mu_rkv = mu_rkv_ref[...]

    def shift(y_act, carry_ref, row_idx, mu):
        y = y_act.astype(F32)
        delta = jnp.dot(diff_mat, y_act, preferred_element_type=F32)
        head = jnp.where(top_row, delta[:8] + carry_ref[row_idx:row_idx + 1, :], delta[:8])
        delta = jnp.concatenate([head, delta[8:]], axis=0)
        carry_ref[row_idx:row_idx + 1, :] = y[sub - 1:sub, :]
        return y + delta * mu

    def headsum(x):
        s0 = jnp.sum(jnp.where(head0_t, x, 0.0), axis=-1, keepdims=True)
        s1 = jnp.sum(jnp.where(head0_t, 0.0, x), axis=-1, keepdims=True)
        return jnp.where(head0_t, s0, s1)

    def stack(x):
        zero = jnp.zeros_like(x)
        return jnp.concatenate([jnp.where(head0, x, zero), jnp.where(head0, zero, x)], axis=0)

    data = [dict() for _ in range(n_sub)]

    def prep(s):
        d = data[s]
        rows = slice(s * sub, (s + 1) * sub)
        rs = shift(r_ref[rows, :], carry_rkv, 0, mu_rkv[0:1, :])
        yield
        ks = shift(k_ref[rows, :], carry_rkv, 1, mu_rkv[1:2, :])
        yield
        vs = shift(v_ref[rows, :], carry_rkv, 2, mu_rkv[2:3, :])
        yield
        lo = shift(lora_ref[rows, :], carry_lora, 0, mu_lora_ref[...])
        z = jnp.where(lane < DECAY_LORA, jnp.tanh(lo), lo)
        wa = w0a0_ref[...] + _mm(z, lora_w_ref[...])
        lw = (-0.6065306597126334 * LOG2E) * _sigmoid(wa[:, :WIDTH])
        yield
        a = _sigmoid(wa[:, WIDTH:])
        if has_vmix:
            vd = shift(vd_ref[rows, :], carry_vd, 0, vmix_mu_ref[...])
            mix = _sigmoid(vmix0_ref[...] + _mm(vd, vmix_up_ref[...]))
            vr = vs + (vfirst_ref[rows, :] - vs) * mix
        else:
            vfirst_ref[rows, :] = vs
            vr = vs
        yield
        kx = ks * kk_ref[...]
        kmod = ks * (1.0 + (a - 1.0) * ka_ref[...])
        d["rkr"] = rs * kmod * rk_ref[...]
        d["vr"] = vr
        yield
        kk_all = []
        for p in range(N_PAIRS):
            kxp = kx[:, p * PAIR:(p + 1) * PAIR]
            kk_all.append(kxp * jnp.minimum(lax.rsqrt(headsum(kxp * kxp)), 1.0 / L2_EPS))
            yield
        kk = jnp.concatenate(kk_all, axis=1)
        na = -kk
        nb = kk * a
        lw_hi = lw.astype(BF16)
        lw_r = lw - lw_hi.astype(F32)
        lw_mid = lw_r.astype(BF16)
        lw_lo = (lw_r - lw_mid.astype(F32)).astype(BF16)
        cum = (jnp.dot(ltri, lw_lo, preferred_element_type=F32) + jnp.dot(ltri, lw_mid, preferred_element_type=F32)
               + jnp.dot(ltri, lw_hi, preferred_element_type=F32))
        yield
        a_t = na * jnp.exp2(cum - lw)
        r_t = rs * jnp.exp2(cum)
        yield
        e_neg = jnp.exp2(-cum)
        b_t = nb * e_neg
        k_t = kmod * e_neg
        yield
        lhs_g, rhs_g, a_s, r_n, b_h, k_h, v_s, gam = {}, {}, {}, {}, {}, {}, {}, {}
        for c in range(n_chunks):
            rc = slice(c * CHUNK, (c + 1) * CHUNK)
            tot = cum[(c + 1) * CHUNK - 1:(c + 1) * CHUNK, :]
            e_rel = jnp.exp2(tot - cum[rc])
            bh_c = nb[rc] * e_rel
            kh_c = kmod[rc] * e_rel
            gam_c = jnp.exp2(tot)
            for p in range(N_PAIRS):
                sl = slice(p * PAIR, (p + 1) * PAIR)
                pc = (p, c)
                a_n = a_t[rc, sl].astype(BF16)
                r_n[pc] = r_t[rc, sl]
                lhs_g[pc] = jnp.concatenate([a_n, r_n[pc].astype(BF16)], axis=0)
                rhs_g[pc] = jnp.concatenate([stack(b_t[rc, sl].astype(BF16)),
                                             stack(k_t[rc, sl].astype(BF16))], axis=0)
                a_s[pc] = stack(a_n)
                b_h[pc] = bh_c[:, sl].astype(BF16)
                k_h[pc] = stack(kh_c[:, sl])
                v_s[pc] = stack(vr[rc, sl].astype(BF16))
                gam[pc] = gam_c[:, sl]
            yield
        d.update(lhs_g=lhs_g, rhs_g=rhs_g, a_s=a_s, r_n=r_n, b_h=b_h, k_h=k_h, v_s=v_s, gam=gam)

    def operators(s):
        d = data[s]
        a_ab, a_ak, a_rb, a_rk = {}, {}, {}, {}
        for pc in pcs:
            g = _mm_nt(d["lhs_g"][pc], d["rhs_g"][pc])
            a_ab[pc] = jnp.where(strict, g[:CHUNK, :n2], 0.0)
            a_ak[pc] = jnp.where(strict, g[:CHUNK, n2:], 0.0).astype(BF16)
            a_rb[pc] = jnp.where(incl, g[CHUNK:, :n2], 0.0).astype(BF16)
            a_rk[pc] = jnp.where(incl, g[CHUNK:, n2:], 0.0)
        yield
        t_inv = {pc: eye + a_ab[pc] for pc in pcs}
        pw = {pc: a_ab[pc].astype(BF16) for pc in pcs}
        pw = {pc: _mm(pw[pc], stack(pw[pc])).astype(BF16) for pc in pcs}
        yield
        for _ in range(CHUNK.bit_length() - 3):
            both = {pc: _mm(jnp.concatenate([pw[pc], t_inv[pc].astype(BF16)], axis=0), stack(pw[pc])) for pc in pcs}
            t_inv = {pc: t_inv[pc] + both[pc][CHUNK:] for pc in pcs}
            pw = {pc: both[pc][:CHUNK].astype(BF16) for pc in pcs}
            yield
        t_inv = {pc: (t_inv[pc] + _mm(t_inv[pc], stack(pw[pc]))).astype(BF16) for pc in pcs}
        yield
        z = {pc: jnp.where(same_head, _mm_tn(t_inv[pc], d["b_h"][pc]), 0.0).astype(BF16) for pc in pcs}
        w = {pc: _mm(a_rb[pc], stack(t_inv[pc])).astype(BF16) for pc in pcs}
        yield
        la_ak = {pc: jnp.concatenate([d["a_s"][pc], stack(a_ak[pc])], axis=1) for pc in pcs}
        my = {pc: _mm_tn(la_ak[pc], z[pc]) for pc in pcs}
        wq = {pc: _mm(w[pc], la_ak[pc]) for pc in pcs}
        yield
        d["m_t"] = {pc: my[pc][:n2].astype(BF16) for pc in pcs}
        y = {pc: (my[pc][n2:] + d["k_h"][pc]).astype(BF16) for pc in pcs}
        d["q_p"] = {pc: (d["r_n"][pc] + wq[pc][:, :n2]).astype(BF16) for pc in pcs}
        p_p = {pc: (wq[pc][:, n2:] + a_rk[pc]).astype(BF16) for pc in pcs}
        yield
        d["g_s"] = {pc: _mm_tn(d["v_s"][pc], y[pc]) for pc in pcs}
        d["o_loc"] = {pc: _mm(p_p[pc], d["v_s"][pc]) for pc in pcs}

    states = [state_ref[p] for p in range(N_PAIRS)]

    def recurrence(s):
        d = data[s]
        outs = [[] for _ in range(N_PAIRS)]
        for c in range(n_chunks):
            for p in range(N_PAIRS):
                pc = (p, c)
                sb = states[p].astype(BF16)
                outs[p].append(_mm_nt(d["q_p"][pc], sb) + d["o_loc"][pc])
                states[p] = states[p] * d["gam"][pc] + _mm(sb, d["m_t"][pc]) + d["g_s"][pc]
        d["o"] = [jnp.concatenate(o, axis=0) if len(o) > 1 else o[0] for o in outs]

    def epilogue(s):
        d = data[s]
        rows = slice(s * sub, (s + 1) * sub)
        for p in range(N_PAIRS):
            sl = slice(p * PAIR, (p + 1) * PAIR)
            o_p = d["o"][p]
            bonus = headsum(d["rkr"][:, sl]) * d["vr"][:, sl]
            mu = headsum(o_p) * (1.0 / HEAD_DIM)
            dev = o_p - mu
            var = headsum(dev * dev) * (1.0 / HEAD_DIM)
            on = dev * lax.rsqrt(var + GN_EPS) * lng_ref[:, sl] + lnb_ref[:, sl]
            rz_p = rz_ref[rows, sl].astype(F32)
            yb_ref[rows, sl] = ((on + bonus) * (rz_p * _sigmoid(rz_p))).astype(yb_ref.dtype)
            yield

    _interleave([prep(0)])
    for s in range(n_sub):
        gens = [operators(s)]
        if s + 1 < n_sub:
            gens.append(prep(s + 1))
        if s > 0:
            gens.append(epilogue(s - 1))
        _interleave(gens)
        recurrence(s)
    _interleave([epilogue(n_sub - 1)])
    for p in range(N_PAIRS):
        state_ref[p] = states[p]


def _rwkv(proj, vfirst, p, *, ts=256):
    b, s, _ = proj.shape
    has_vmix = vfirst is not None
    row = lambda width, col: pl.BlockSpec((None, ts, width), lambda i, t: (i, t, col // width))
    const = lambda shape: pl.BlockSpec(shape, lambda i, t: (0,) * len(shape))
    act = pl.BlockSpec((None, ts, WIDTH), lambda i, t: (i, t, 0))

    in_specs = [row(WIDTH, COL_R), row(WIDTH, COL_RK), row(WIDTH, COL_RV), row(WIDTH, COL_RZ),
                row(LANES, COL_LORA)]
    args = [proj, proj, proj, proj, proj]
    if has_vmix:
        in_specs += [row(LANES, COL_VD), act]
        args += [proj, vfirst]
    in_specs += [const((3, WIDTH)), const((1, LANES)), const((1, 2 * WIDTH)), const((LANES, 2 * WIDTH))]
    args += [p["mu_rkv"], p["mu_lora"], p["w0a0"], p["lora_w"]]
    for name in ("k_k", "k_a", "r_k", "ln_g", "ln_b"):
        in_specs.append(const((1, WIDTH)))
        args.append(p[name])
    if has_vmix:
        in_specs += [const((1, LANES)), const((LANES, WIDTH)), const((1, WIDTH))]
        args += [p["vmix_mu"], p["vmix_up"], p["vmix0"]]

    out_shape = [jax.ShapeDtypeStruct((b, s, WIDTH), ACT)]
    out_specs = [act]
    scratch = [pltpu.VMEM((8, WIDTH), F32), pltpu.VMEM((8, LANES), F32)]
    if has_vmix:
        scratch.append(pltpu.VMEM((8, LANES), F32))
    else:
        out_shape.append(jax.ShapeDtypeStruct((b, s, WIDTH), F32))
        out_specs.append(act)
    scratch.append(pltpu.VMEM((N_PAIRS, PAIR, PAIR), F32))

    res = pl.pallas_call(
        functools.partial(_rwkv_kernel, has_vmix=has_vmix, ts=ts),
        grid=(b, s // ts),
        in_specs=in_specs,
        out_specs=out_specs,
        out_shape=out_shape,
        scratch_shapes=scratch,
        compiler_params=pltpu.CompilerParams(
            dimension_semantics=("parallel", "arbitrary"), vmem_limit_bytes=VMEM_LIMIT),
        name="rwkv_vmix" if has_vmix else "rwkv",
    )(*args)
    return (res[0], vfirst) if has_vmix else (res[0], res[1])


def _outproj_kernel(ya_ref, az_ref, yb_ref, gatt_ref, grw_ref, x_ref,
                    wua_ref, wub_ref, wout_ref, g_ref, o_ref):
    az = az_ref[...].astype(F32)
    ya = ya_ref[...].astype(F32) * (az * _sigmoid(az))
    u = (_sigmoid(gatt_ref[...].astype(F32)) * _mm(ya, wua_ref[...])
         + _sigmoid(grw_ref[...].astype(F32)) * _mm(yb_ref[...], wub_ref[...]))
    y = _mm(u, wout_ref[...])
    ms = jnp.mean(y * y, axis=-1, keepdims=True)
    o_ref[...] = x_ref[...] + y * lax.rsqrt(ms + RMS_EPS) * g_ref[...]


def _outproj(ya, proj2d, yb, x2d, w_up_att, w_up_rw, w_out, g, *, tm=1024):
    t, d = x2d.shape
    row = lambda width, col: pl.BlockSpec((tm, width), lambda i: (i, col // width))
    const = lambda shape: pl.BlockSpec(shape, lambda i: (0, 0))
    return pl.pallas_call(
        _outproj_kernel,
        grid=(t // tm,),
        in_specs=[row(WIDTH, 0), row(WIDTH, COL_AZ), row(WIDTH, 0), row(d, COL_GATT), row(d, COL_GRW),
                  row(d, 0), const((WIDTH, d)), const((WIDTH, d)), const((d, d)), const((1, d))],
        out_specs=row(d, 0),
        out_shape=jax.ShapeDtypeStruct((t, d), F32),
        compiler_params=pltpu.CompilerParams(
            dimension_semantics=("parallel",), vmem_limit_bytes=VMEM_LIMIT),
        name="outproj",
    )(ya, proj2d, yb, proj2d, proj2d, x2d, w_up_att, w_up_rw, w_out, g.reshape(1, d))


def _layer_params(l, w_in, rw_mu, rw_w0, rw_w_up, rw_a0, rw_a_up, rw_k_k, rw_k_a, rw_r_k,
                  rw_ln_g, rw_ln_b, rw_vmix_down, rw_vmix_mu, rw_vmix_up, rw_vmix0):
    w = w_in[l]
    d = w.shape[0]
    att = w[:, :4 * WIDTH]
    o = 4 * WIDTH
    rkv = w[:, o:o + 3 * WIDTH]
    lora = w[:, o + 3 * WIDTH:o + 3 * WIDTH + DECAY_LORA + ICLR_LORA]
    o = o + 3 * WIDTH + DECAY_LORA + ICLR_LORA
    rz = w[:, o:o + WIDTH]
    gates = w[:, o + WIDTH:]
    pad = LANES - VMIX_LORA
    vd = jnp.pad(rw_vmix_down[l - 1], ((0, 0), (0, pad))) if l > 0 else jnp.zeros((d, LANES), F32)
    w_cat = jnp.concatenate([att, rkv, rz, gates, lora, vd], axis=1).astype(BF16)

    mu = rw_mu[l]
    zeros = jnp.zeros((DECAY_LORA, WIDTH), F32)
    lora_w = jnp.concatenate([jnp.concatenate([rw_w_up[l], zeros], axis=1),
                              jnp.concatenate([zeros, rw_a_up[l]], axis=1)], axis=0).astype(BF16)
    p = {
        "w_cat": w_cat,
        "mu_rkv": mu[:3 * WIDTH].reshape(3, WIDTH),
        "mu_lora": mu[3 * WIDTH:].reshape(1, LANES),
        "w0a0": jnp.concatenate([rw_w0[l], rw_a0[l]]).reshape(1, 2 * WIDTH),
        "lora_w": lora_w,
        "k_k": rw_k_k[l].reshape(1, WIDTH), "k_a": rw_k_a[l].reshape(1, WIDTH),
        "r_k": rw_r_k[l].reshape(1, WIDTH),
        "ln_g": rw_ln_g[l].reshape(1, WIDTH), "ln_b": rw_ln_b[l].reshape(1, WIDTH),
    }
    if l > 0:
        p["vmix_mu"] = jnp.pad(rw_vmix_mu[l - 1], (0, pad)).reshape(1, LANES)
        p["vmix_up"] = jnp.pad(rw_vmix_up[l - 1], ((0, pad), (0, 0))).astype(BF16)
        p["vmix0"] = rw_vmix0[l - 1].reshape(1, WIDTH)
    return p


def kernel(x, norm_pre, norm_post, w_in, rw_mu, rw_w0, rw_w_up, rw_a0, rw_a_up, rw_k_k, rw_k_a, rw_r_k, rw_ln_g, rw_ln_b, rw_vmix_down, rw_vmix_mu, rw_vmix_up, rw_vmix0, w_up_att, w_up_rw, w_out):
    b, s, d = x.shape
    assert d == D_MODEL and s % MOBA_BLOCK == 0
    depth = w_in.shape[0]
    x2d = x.reshape(b * s, d)
    vfirst = None
    for l in range(depth):
        p = _layer_params(l, w_in, rw_mu, rw_w0, rw_w_up, rw_a0, rw_a_up, rw_k_k, rw_k_a, rw_r_k,
                          rw_ln_g, rw_ln_b, rw_vmix_down, rw_vmix_mu, rw_vmix_up, rw_vmix0)
        proj2d = _inproj(x2d, norm_pre[l], p["w_cat"])
        proj = proj2d.reshape(b, s, PROJ_W)
        ya = _moba(proj)
        yb, vfirst = _rwkv(proj, vfirst, p)
        x2d = _outproj(ya.reshape(b * s, WIDTH), proj2d, yb.reshape(b * s, WIDTH), x2d,
                       w_up_att[l].astype(BF16), w_up_rw[l].astype(BF16), w_out[l].astype(BF16),
                       norm_post[l])
    return x2d.reshape(b, s, d)
```

```python
import functools

import jax
import jax.numpy as jnp
from jax import lax
from jax.experimental import pallas as pl
from jax.experimental.pallas import tpu as pltpu

F32 = jnp.float32
BF16 = jnp.bfloat16
ACT = BF16

D_MODEL = 1024
N_HEADS = 8
HEAD_DIM = 64
WIDTH = N_HEADS * HEAD_DIM
MOBA_BLOCK = 256
MOBA_TOPK = 3
DECAY_LORA = 64
ICLR_LORA = 64
VMIX_LORA = 32
RMS_EPS = 1e-6
GN_EPS = 64e-5
L2_EPS = 1e-12
NEG_INF = -1e30

LANES = 128
PAIR = 2 * HEAD_DIM
N_PAIRS = WIDTH // PAIR
V_ROWS = HEAD_DIM + 16
CHUNK = 64
RWKV_SUB = 256

COL_Q, COL_K, COL_V, COL_AZ = 0, 512, 1024, 1536
COL_R, COL_RK, COL_RV = 2048, 2560, 3072
COL_LORA = 3584
COL_RZ = 3712
COL_GATT, COL_GRW = 4224, 5248
COL_VD = 6272
PROJ_W = 6400

VMEM_LIMIT = 48 * 1024 * 1024
LOG2E = 1.4426950408889634


def _sigmoid(x):
    return 0.5 * jnp.tanh(0.5 * x) + 0.5


def _mm(a, b):
    return jnp.dot(a.astype(BF16), b.astype(BF16), preferred_element_type=F32)


def _mm_nt(a, b):
    return lax.dot_general(a.astype(BF16), b.astype(BF16), (((1,), (1,)), ((), ())),
                           preferred_element_type=F32)


def _mm_tn(a, b):
    return lax.dot_general(a.astype(BF16), b.astype(BF16), (((0,), (0,)), ((), ())),
                           preferred_element_type=F32)


def _interleave(gens):
    gens = list(gens)
    while gens:
        for g in list(gens):
            try:
                next(g)
            except StopIteration:
                gens.remove(g)


def _inproj_kernel(x_ref, g_ref, w_ref, o_ref, *, tn):
    x = x_ref[...]
    ms = jnp.mean(x * x, axis=-1, keepdims=True)
    h = (x * lax.rsqrt(ms + RMS_EPS) * g_ref[...]).astype(BF16)
    for n0 in range(0, o_ref.shape[1], tn):
        o_ref[:, n0:n0 + tn] = jnp.dot(h, w_ref[:, n0:n0 + tn],
                                       preferred_element_type=F32).astype(o_ref.dtype)


def _inproj(x2d, g, w_cat, *, tm=512, tn=1280):
    t, d = x2d.shape
    n = w_cat.shape[1]
    return pl.pallas_call(
        functools.partial(_inproj_kernel, tn=tn),
        grid=(t // tm,),
        in_specs=[
            pl.BlockSpec((tm, d), lambda i: (i, 0)),
            pl.BlockSpec((1, d), lambda i: (0, 0)),
            pl.BlockSpec((d, n), lambda i: (0, 0), pipeline_mode=pl.Buffered(1)),
        ],
        out_specs=pl.BlockSpec((tm, n), lambda i: (i, 0)),
        out_shape=jax.ShapeDtypeStruct((t, n), ACT),
        compiler_params=pltpu.CompilerParams(
            dimension_semantics=("parallel",), vmem_limit_bytes=VMEM_LIMIT),
        name="inproj",
    )(x2d, g.reshape(1, d), w_cat)


def _moba_kernel(q_ref, k_ref, v_ref, o_ref, vt_ref, qx_ref, kx_ref, s_ref, p_ref, *, n_blocks):
    hp = pl.program_id(1)
    blk = MOBA_BLOCK
    s_len = n_blocks * blk
    q_scale = (HEAD_DIM ** -0.5) * LOG2E
    nt = (((1,), (1,)), ((), ()))
    sub = 32
    tk = MOBA_BLOCK

    vt = v_ref[...].T
    ones_rows = jnp.where(lax.broadcasted_iota(jnp.int32, (V_ROWS - HEAD_DIM, s_len), 0) == 0, 1.0, 0.0).astype(BF16)
    for h in range(2):
        vt_ref[h] = jnp.concatenate([vt[h * HEAD_DIM:(h + 1) * HEAD_DIM], ones_rows], axis=0)

    mean_row = lax.broadcasted_iota(jnp.int32, (n_blocks, PAIR), 0)
    kmean = jnp.zeros((n_blocks, PAIR), F32)
    for n in range(n_blocks):
        kmean = jnp.where(mean_row == n,
                          jnp.mean(k_ref[n * blk:(n + 1) * blk, :].astype(F32), axis=0, keepdims=True), kmean)

    lane = lax.broadcasted_iota(jnp.int32, (1, PAIR), 1)
    head_lanes = (lane < HEAD_DIM, lane >= HEAD_DIM)
    nrow = lax.broadcasted_iota(jnp.int32, (n_blocks, s_len), 0)
    pos_i = lax.broadcasted_iota(jnp.int32, (n_blocks, s_len), 1)
    pblk = pos_i // blk
    pos = pos_i.astype(F32)
    past = nrow < pblk
    krow = lax.broadcasted_iota(jnp.int32, (blk, blk), 0)
    qcol = lax.broadcasted_iota(jnp.int32, (blk, blk), 1)
    causal = qcol >= krow

    km = jnp.concatenate([jnp.where(head_lanes[0], kmean, 0.0), jnp.where(head_lanes[1], kmean, 0.0)], axis=0)
    km_hi = km.astype(BF16)
    km_r = km - km_hi.astype(F32)
    km_mid = km_r.astype(BF16)
    km_lo = (km_r - km_mid.astype(F32)).astype(BF16)
    parts = lax.dot_general(jnp.concatenate([km_lo, km_mid, km_hi], axis=0), q_ref[...], nt,
                            preferred_element_type=F32)
    ng = 2 * n_blocks
    gates = parts[:ng] + parts[ng:2 * ng] + parts[2 * ng:]

    def split3(x):
        hi = x.astype(BF16).astype(F32)
        mid = (x - hi).astype(BF16).astype(F32)
        return hi, mid, x - hi - mid

    def rows8(r0, r1, r2, r3, r4, r5):
        out = jnp.zeros((n_blocks, s_len), F32)
        for idx, r in enumerate((r0, r1, r2, r3, r4, r5)):
            out = jnp.where(nrow == idx, r, out)
        return out

    ones = jnp.ones((n_blocks, s_len), F32)
    for h in range(2):
        head = lax.convert_element_type(hp * 2 + h + 1, F32)
        slope2 = jnp.exp2(jnp.zeros((1, 1), F32) - (8.0 / N_HEADS) * head) * LOG2E

        gm = jnp.where(past, gates[h * n_blocks:(h + 1) * n_blocks], NEG_INF)
        rank = jnp.zeros((n_blocks, s_len), F32)
        for m in range(n_blocks):
            gcol = gm[m:m + 1, :]
            beats = (gcol > gm) | ((gcol == gm) & (nrow > m))
            rank = rank + jnp.where(beats, 1.0, 0.0)
        visible = (past & (rank < MOBA_TOPK)) | (nrow == pblk)

        q_hi, q_mid, q_lo = split3(-slope2 * pos)
        k_hi, k_mid, k_lo = split3(slope2 * pos)
        q_extra = jnp.concatenate([rows8(q_hi, q_mid, q_lo, ones, ones, ones),
                                   jnp.where(visible, 0.0, NEG_INF)], axis=0)
        k_extra = jnp.concatenate([rows8(ones, ones, ones, k_hi, k_mid, k_lo),
                                   jnp.where(nrow == pblk, 1.0, 0.0)], axis=0)
        lead = jnp.zeros(((1 - h) * HEAD_DIM, s_len), F32)
        tail = jnp.zeros((PAIR - 2 * n_blocks - (1 - h) * HEAD_DIM, s_len), F32)
        q_full = jnp.concatenate([lead, q_extra, tail], axis=0) if h == 0 else jnp.concatenate([q_extra, tail], axis=0)
        k_full = jnp.concatenate([lead, k_extra, tail], axis=0) if h == 0 else jnp.concatenate([k_extra, tail], axis=0)
        qx_ref[h] = q_full.T.astype(BF16)
        kx_ref[h] = jnp.where(head_lanes[h], k_ref[...], k_full.T.astype(BF16))

    units = [(i, h) for i in range(n_blocks) for h in range(2)]
    st = {}

    def score_tile(u, t):
        i, h = u
        d = st[u]
        k0 = t * tk
        s_tile = lax.dot_general(kx_ref[h, k0:k0 + tk, :], d["qm"], nt,
                                 preferred_element_type=F32)
        for c in range(0, tk, sub):
            sc = s_tile[c:c + sub]
            if k0 // blk == i:
                r0 = k0 % blk + c
                sc = jnp.where(causal[r0:r0 + sub], sc, NEG_INF)
            s_ref[d["slot"], k0 + c:k0 + c + sub, :] = sc
            for r in range(0, sub, 8):
                d["m8"] = sc[r:r + 8] if d["m8"] is None else jnp.maximum(d["m8"], sc[r:r + 8])

    def softmax_tile(u, t):
        d = st[u]
        for c in range(t * tk, (t + 1) * tk, sub):
            pc = jnp.exp2(s_ref[d["slot"], c:c + sub, :] - d["m"])
            p_ref[d["slot"], c:c + sub, :] = pc.astype(BF16)

    def finish(u):
        i, h = u
        d = st[u]
        n_k = (i + 1) * blk
        pv = jnp.dot(vt_ref[h, :, :n_k], p_ref[d["slot"], :n_k, :],
                     preferred_element_type=F32)
        d["out"] = pv[:HEAD_DIM] * (1.0 / pv[HEAD_DIM:HEAD_DIM + 1])
        if h == 1:
            out_t = jnp.concatenate([st[(i, 0)]["out"], d["out"]], axis=0)
            o_ref[i * blk:(i + 1) * blk, :] = out_t.T.astype(o_ref.dtype)

    prev = None
    for idx, u in enumerate(units):
        i, h = u
        rows = slice(i * blk, (i + 1) * blk)
        if h == 0:
            q_i = (q_ref[rows, :].astype(F32) * q_scale).astype(BF16)
        st[u] = {"qm": jnp.where(head_lanes[h], q_i, qx_ref[h, rows, :]), "slot": idx % 2,
                 "m8": None}
        n_cur = (i + 1) * blk // tk
        n_prev = (prev[0] + 1) * blk // tk if prev is not None else 0
        for step in range(max(n_cur, n_prev)):
            if step < n_cur:
                score_tile(u, step)
            if step < n_prev:
                softmax_tile(prev, step)
        if prev is not None:
            finish(prev)
        st[u]["m"] = jnp.max(st[u]["m8"], axis=0, keepdims=True)
        prev = u
    for step in range((prev[0] + 1) * blk // tk):
        softmax_tile(prev, step)
    finish(prev)


def _moba(proj):
    b, s, _ = proj.shape
    n_blocks = s // MOBA_BLOCK
    assert n_blocks == 8, "the extra-column layout of the MoBA kernel is laid out for 8 key blocks"
    qo, ko, vo = COL_Q // PAIR, COL_K // PAIR, COL_V // PAIR
    return pl.pallas_call(
        functools.partial(_moba_kernel, n_blocks=n_blocks),
        grid=(b, N_PAIRS),
        in_specs=[
            pl.BlockSpec((None, s, PAIR), lambda i, p: (i, 0, qo + p)),
            pl.BlockSpec((None, s, PAIR), lambda i, p: (i, 0, ko + p)),
            pl.BlockSpec((None, s, PAIR), lambda i, p: (i, 0, vo + p)),
        ],
        out_specs=pl.BlockSpec((None, s, PAIR), lambda i, p: (i, 0, p)),
        out_shape=jax.ShapeDtypeStruct((b, s, WIDTH), ACT),
        scratch_shapes=[
            pltpu.VMEM((2, V_ROWS, s), BF16),
            pltpu.VMEM((2, s, PAIR), BF16),
            pltpu.VMEM((2, s, PAIR), BF16),
            pltpu.VMEM((2, s, MOBA_BLOCK), F32),
            pltpu.VMEM((2, s, MOBA_BLOCK), BF16),
        ],
        compiler_params=pltpu.CompilerParams(
            dimension_semantics=("parallel", "parallel"), vmem_limit_bytes=VMEM_LIMIT),
        name="moba",
    )(proj, proj, proj)


def _rwkv_kernel(*refs, has_vmix, ts):
    if has_vmix:
        (r_ref, k_ref, v_ref, rz_ref, lora_ref, vd_ref, vfirst_ref,
         mu_rkv_ref, mu_lora_ref, w0a0_ref, lora_w_ref, kk_ref, ka_ref, rk_ref, lng_ref, lnb_ref,
         vmix_mu_ref, vmix_up_ref, vmix0_ref,
         yb_ref, carry_rkv, carry_lora, carry_vd, state_ref) = refs
    else:
        (r_ref, k_ref, v_ref, rz_ref, lora_ref,
         mu_rkv_ref, mu_lora_ref, w0a0_ref, lora_w_ref, kk_ref, ka_ref, rk_ref, lng_ref, lnb_ref,
         yb_ref, vfirst_ref, carry_rkv, carry_lora, state_ref) = refs
        carry_vd = None

    @pl.when(pl.program_id(1) == 0)
    def _():
        carry_rkv[...] = jnp.zeros_like(carry_rkv)
        carry_lora[...] = jnp.zeros_like(carry_lora)
        if carry_vd is not None:
            carry_vd[...] = jnp.zeros_like(carry_vd)
        state_ref[...] = jnp.zeros_like(state_ref)

    sub = RWKV_SUB
    n_sub = ts // sub
    n_chunks = sub // CHUNK
    n2 = 2 * CHUNK
    pcs = [(p, c) for c in range(n_chunks) for p in range(N_PAIRS)]

    sub_row = lax.broadcasted_iota(jnp.int32, (sub, sub), 0)
    sub_col = lax.broadcasted_iota(jnp.int32, (sub, sub), 1)
    diff_mat = (jnp.where(sub_row == sub_col + 1, 1.0, 0.0)
                - jnp.where(sub_row == sub_col, 1.0, 0.0)).astype(r_ref.dtype)
    ltri = jnp.where((sub_row // CHUNK == sub_col // CHUNK) & (sub_col <= sub_row), 1.0, 0.0).astype(BF16)
    top_row = lax.broadcasted_iota(jnp.int32, (8, 1), 0) == 0
    lane = lax.broadcasted_iota(jnp.int32, (sub, LANES), 1)
    head0_t = lane < HEAD_DIM
    head0 = lax.broadcasted_iota(jnp.int32, (CHUNK, PAIR), 1) < HEAD_DIM
    wrow = lax.broadcasted_iota(jnp.int32, (CHUNK, PAIR), 0)
    wcol = lax.broadcasted_iota(jnp.int32, (CHUNK, PAIR), 1) % CHUNK
    strict = wcol < wrow
    incl = wcol <= wrow
    eye = jnp.where(wcol == wrow, 1.0, 0.0).astype(F32)
    srow = lax.broadcasted_iota(jnp.int32, (n2, n2), 0)
    scol = lax.broadcasted_iota(jnp.int32, (n2, n2), 1)
    same_head = (srow // CHUNK) == (scol // CHUNK)
    mu_rkv = mu_rkv_ref[...]

    def shift(y_act, carry_ref, row_idx, mu):
        y = y_act.astype(F32)
        delta = jnp.dot(diff_mat, y_act, preferred_element_type=F32)
        head = jnp.where(top_row, delta[:8] + carry_ref[row_idx:row_idx + 1, :], delta[:8])
        delta = jnp.concatenate([head, delta[8:]], axis=0)
        carry_ref[row_idx:row_idx + 1, :] = y[sub - 1:sub, :]
        return y + delta * mu

    def headsum(x):
        s0 = jnp.sum(jnp.where(head0_t, x, 0.0), axis=-1, keepdims=True)
        s1 = jnp.sum(jnp.where(head0_t, 0.0, x), axis=-1, keepdims=True)
        return jnp.where(head0_t, s0, s1)

    def stack(x):
        zero = jnp.zeros_like(x)
        return jnp.concatenate([jnp.where(head0, x, zero), jnp.where(head0, zero, x)], axis=0)

    data = [dict() for _ in range(n_sub)]

    def prep(s):
        d = data[s]
        rows = slice(s * sub, (s + 1) * sub)
        rs = shift(r_ref[rows, :], carry_rkv, 0, mu_rkv[0:1, :])
        yield
        ks = shift(k_ref[rows, :], carry_rkv, 1, mu_rkv[1:2, :])
        yield
        vs = shift(v_ref[rows, :], carry_rkv, 2, mu_rkv[2:3, :])
        yield
        lo = shift(lora_ref[rows, :], carry_lora, 0, mu_lora_ref[...])
        z = jnp.where(lane < DECAY_LORA, jnp.tanh(lo), lo)
        wa = w0a0_ref[...] + _mm(z, lora_w_ref[...])
        lw = (-0.6065306597126334 * LOG2E) * _sigmoid(wa[:, :WIDTH])
        yield
        a = _sigmoid(wa[:, WIDTH:])
        if has_vmix:
            vd = shift(vd_ref[rows, :], carry_vd, 0, vmix_mu_ref[...])
            mix = _sigmoid(vmix0_ref[...] + _mm(vd, vmix_up_ref[...]))
            vr = vs + (vfirst_ref[rows, :] - vs) * mix
        else:
            vfirst_ref[rows, :] = vs
            vr = vs
        yield
        kx = ks * kk_ref[...]
        kmod = ks * (1.0 + (a - 1.0) * ka_ref[...])
        d["rkr"] = rs * kmod * rk_ref[...]
        d["vr"] = vr
        yield
        kk_all = []
        for p in range(N_PAIRS):
            kxp = kx[:, p * PAIR:(p + 1) * PAIR]
            kk_all.append(kxp * jnp.minimum(lax.rsqrt(headsum(kxp * kxp)), 1.0 / L2_EPS))
            yield
        kk = jnp.concatenate(kk_all, axis=1)
        na = -kk
        nb = kk * a
        lw_hi = lw.astype(BF16)
        lw_r = lw - lw_hi.astype(F32)
        lw_mid = lw_r.astype(BF16)
        lw_lo = (lw_r - lw_mid.astype(F32)).astype(BF16)
        cum = (jnp.dot(ltri, lw_lo, preferred_element_type=F32) + jnp.dot(ltri, lw_mid, preferred_element_type=F32)
               + jnp.dot(ltri, lw_hi, preferred_element_type=F32))
        yield
        a_t = na * jnp.exp2(cum - lw)
        r_t = rs * jnp.exp2(cum)
        yield
        e_neg = jnp.exp2(-cum)
        b_t = nb * e_neg
        k_t = kmod * e_neg
        yield
        lhs_g, rhs_g, a_s, r_n, b_h, k_h, v_s, gam = {}, {}, {}, {}, {}, {}, {}, {}
        for c in range(n_chunks):
            rc = slice(c * CHUNK, (c + 1) * CHUNK)
            tot = cum[(c + 1) * CHUNK - 1:(c + 1) * CHUNK, :]
            e_rel = jnp.exp2(tot - cum[rc])
            bh_c = nb[rc] * e_rel
            kh_c = kmod[rc] * e_rel
            gam_c = jnp.exp2(tot)
            for p in range(N_PAIRS):
                sl = slice(p * PAIR, (p + 1) * PAIR)
                pc = (p, c)
                a_n = a_t[rc, sl].astype(BF16)
                r_n[pc] = r_t[rc, sl]
                lhs_g[pc] = jnp.concatenate([a_n, r_n[pc].astype(BF16)], axis=0)
                rhs_g[pc] = jnp.concatenate([stack(b_t[rc, sl].astype(BF16)),
                                             stack(k_t[rc, sl].astype(BF16))], axis=0)
                a_s[pc] = stack(a_n)
                b_h[pc] = bh_c[:, sl].astype(BF16)
                k_h[pc] = stack(kh_c[:, sl])
                v_s[pc] = stack(vr[rc, sl].astype(BF16))
                gam[pc] = gam_c[:, sl]
            yield
        d.update(lhs_g=lhs_g, rhs_g=rhs_g, a_s=a_s, r_n=r_n, b_h=b_h, k_h=k_h, v_s=v_s, gam=gam)

    def operators(s):
        d = data[s]
        a_ab, a_ak, a_rb, a_rk = {}, {}, {}, {}
        for pc in pcs:
            g = _mm_nt(d["lhs_g"][pc], d["rhs_g"][pc])
            a_ab[pc] = jnp.where(strict, g[:CHUNK, :n2], 0.0)
            a_ak[pc] = jnp.where(strict, g[:CHUNK, n2:], 0.0).astype(BF16)
            a_rb[pc] = jnp.where(incl, g[CHUNK:, :n2], 0.0).astype(BF16)
            a_rk[pc] = jnp.where(incl, g[CHUNK:, n2:], 0.0)
        yield
        t_inv = {pc: eye + a_ab[pc] for pc in pcs}
        pw = {pc: a_ab[pc].astype(BF16) for pc in pcs}
        pw = {pc: _mm(pw[pc], stack(pw[pc])).astype(BF16) for pc in pcs}
        yield
        for _ in range(CHUNK.bit_length() - 3):
            both = {pc: _mm(jnp.concatenate([pw[pc], t_inv[pc].astype(BF16)], axis=0), stack(pw[pc])) for pc in pcs}
            t_inv = {pc: t_inv[pc] + both[pc][CHUNK:] for pc in pcs}
            pw = {pc: both[pc][:CHUNK].astype(BF16) for pc in pcs}
            yield
        t_inv = {pc: (t_inv[pc] + _mm(t_inv[pc], stack(pw[pc]))).astype(BF16) for pc in pcs}
        yield
        z = {pc: jnp.where(same_head, _mm_tn(t_inv[pc], d["b_h"][pc]), 0.0).astype(BF16) for pc in pcs}
        w = {pc: _mm(a_rb[pc], stack(t_inv[pc])).astype(BF16) for pc in pcs}
        yield
        la_ak = {pc: jnp.concatenate([d["a_s"][pc], stack(a_ak[pc])], axis=1) for pc in pcs}
        my = {pc: _mm_tn(la_ak[pc], z[pc]) for pc in pcs}
        wq = {pc: _mm(w[pc], la_ak[pc]) for pc in pcs}
        yield
        d["m_t"] = {pc: my[pc][:n2].astype(BF16) for pc in pcs}
        y = {pc: (my[pc][n2:] + d["k_h"][pc]).astype(BF16) for pc in pcs}
        d["q_p"] = {pc: (d["r_n"][pc] + wq[pc][:, :n2]).astype(BF16) for pc in pcs}
        p_p = {pc: (wq[pc][:, n2:] + a_rk[pc]).astype(BF16) for pc in pcs}
        yield
        d["g_s"] = {pc: _mm_tn(d["v_s"][pc], y[pc]) for pc in pcs}
        d["o_loc"] = {pc: _mm(p_p[pc], d["v_s"][pc]) for pc in pcs}

    states = [state_ref[p] for p in range(N_PAIRS)]

    def recurrence(s):
        d = data[s]
        outs = [[] for _ in range(N_PAIRS)]
        for c in range(n_chunks):
            for p in range(N_PAIRS):
                pc = (p, c)
                sb = states[p].astype(BF16)
                outs[p].append(_mm_nt(d["q_p"][pc], sb) + d["o_loc"][pc])
                states[p] = states[p] * d["gam"][pc] + _mm(sb, d["m_t"][pc]) + d["g_s"][pc]
        d["o"] = [jnp.concatenate(o, axis=0) if len(o) > 1 else o[0] for o in outs]

    def epilogue(s):
        d = data[s]
        rows = slice(s * sub, (s + 1) * sub)
        for p in range(N_PAIRS):
            sl = slice(p * PAIR, (p + 1) * PAIR)
            o_p = d["o"][p]
            bonus = headsum(d["rkr"][:, sl]) * d["vr"][:, sl]
            mu = headsum(o_p) * (1.0 / HEAD_DIM)
            dev = o_p - mu
            var = headsum(dev * dev) * (1.0 / HEAD_DIM)
            on = dev * lax.rsqrt(var + GN_EPS) * lng_ref[:, sl] + lnb_ref[:, sl]
            rz_p = rz_ref[rows, sl].astype(F32)
            yb_ref[rows, sl] = ((on + bonus) * (rz_p * _sigmoid(rz_p))).astype(yb_ref.dtype)
            yield

    _interleave([prep(0)])
    for s in range(n_sub):
        gens = [operators(s)]
        if s + 1 < n_sub:
            gens.append(prep(s + 1))
        if s > 0:
            gens.append(epilogue(s - 1))
        _interleave(gens)
        recurrence(s)
    _interleave([epilogue(n_sub - 1)])
    for p in range(N_PAIRS):
        state_ref[p] = states[p]


def _rwkv(proj, vfirst, p, *, ts=256):
    b, s, _ = proj.shape
    has_vmix = vfirst is not None
    proj2d = proj.reshape(b * s, PROJ_W)
    row = lambda width, col: pl.BlockSpec((pl.Element(ts), pl.Element(width)),
                                          lambda i, t: (pl.multiple_of(i * s + t * ts, ts), col))
    const = lambda shape: pl.BlockSpec(shape, lambda i, t: (0,) * len(shape))
    act = pl.BlockSpec((None, ts, WIDTH), lambda i, t: (i, t, 0))

    in_specs = [row(WIDTH, COL_R), row(WIDTH, COL_RK), row(WIDTH, COL_RV), row(WIDTH, COL_RZ),
                row(LANES, COL_LORA)]
    args = [proj2d, proj2d, proj2d, proj2d, proj2d]
    if has_vmix:
        in_specs += [row(LANES, COL_VD), act]
        args += [proj2d, vfirst]
    in_specs += [const((3, WIDTH)), const((1, LANES)), const((1, 2 * WIDTH)), const((LANES, 2 * WIDTH))]
    args += [p["mu_rkv"], p["mu_lora"], p["w0a0"], p["lora_w"]]
    for name in ("k_k", "k_a", "r_k", "ln_g", "ln_b"):
        in_specs.append(const((1, WIDTH)))
        args.append(p[name])
    if has_vmix:
        in_specs += [const((1, LANES)), const((LANES, WIDTH)), const((1, WIDTH))]
        args += [p["vmix_mu"], p["vmix_up"], p["vmix0"]]

    out_shape = [jax.ShapeDtypeStruct((b, s, WIDTH), ACT)]
    out_specs = [act]
    scratch = [pltpu.VMEM((8, WIDTH), F32), pltpu.VMEM((8, LANES), F32)]
    if has_vmix:
        scratch.append(pltpu.VMEM((8, LANES), F32))
    else:
        out_shape.append(jax.ShapeDtypeStruct((b, s, WIDTH), F32))
        out_specs.append(act)
    scratch.append(pltpu.VMEM((N_PAIRS, PAIR, PAIR), F32))

    res = pl.pallas_call(
        functools.partial(_rwkv_kernel, has_vmix=has_vmix, ts=ts),
        grid=(b, s // ts),
        in_specs=in_specs,
        out_specs=out_specs,
        out_shape=out_shape,
        scratch_shapes=scratch,
        compiler_params=pltpu.CompilerParams(
            dimension_semantics=("parallel", "arbitrary"), vmem_limit_bytes=VMEM_LIMIT),
        name="rwkv_vmix" if has_vmix else "rwkv",
    )(*args)
    return (res[0], vfirst) if has_vmix else (res[0], res[1])


def _outproj_kernel(ya_ref, az_ref, yb_ref, gatt_ref, grw_ref, x_ref,
                    wua_ref, wub_ref, wout_ref, g_ref, o_ref):
    az = az_ref[...].astype(F32)
    ya = ya_ref[...].astype(F32) * (az * _sigmoid(az))
    u = (_sigmoid(gatt_ref[...].astype(F32)) * _mm(ya, wua_ref[...])
         + _sigmoid(grw_ref[...].astype(F32)) * _mm(yb_ref[...], wub_ref[...]))
    y = _mm(u, wout_ref[...])
    ms = jnp.mean(y * y, axis=-1, keepdims=True)
    o_ref[...] = x_ref[...] + y * lax.rsqrt(ms + RMS_EPS) * g_ref[...]


def _outproj(ya, proj2d, yb, x2d, w_up_att, w_up_rw, w_out, g, *, tm=1024):
    t, d = x2d.shape
    row = lambda width, col: pl.BlockSpec((pl.Element(tm), pl.Element(width)),
                                          lambda i: (pl.multiple_of(i * tm, tm), col))
    const = lambda shape: pl.BlockSpec(shape, lambda i: (0, 0))
    return pl.pallas_call(
        _outproj_kernel,
        grid=(t // tm,),
        in_specs=[row(WIDTH, 0), row(WIDTH, COL_AZ), row(WIDTH, 0), row(d, COL_GATT), row(d, COL_GRW),
                  row(d, 0), const((WIDTH, d)), const((WIDTH, d)), const((d, d)), const((1, d))],
        out_specs=row(d, 0),
        out_shape=jax.ShapeDtypeStruct((t, d), F32),
        compiler_params=pltpu.CompilerParams(
            dimension_semantics=("parallel",), vmem_limit_bytes=VMEM_LIMIT),
        name="outproj",
    )(ya, proj2d, yb, proj2d, proj2d, x2d, w_up_att, w_up_rw, w_out, g.reshape(1, d))


def _layer_params(l, w_in, rw_mu, rw_w0, rw_w_up, rw_a0, rw_a_up, rw_k_k, rw_k_a, rw_r_k,
                  rw_ln_g, rw_ln_b, rw_vmix_down, rw_vmix_mu, rw_vmix_up, rw_vmix0):
    w = w_in[l]
    d = w.shape[0]
    pad = LANES - VMIX_LORA
    vd = jnp.pad(rw_vmix_down[l - 1], ((0, 0), (0, pad))) if l > 0 else jnp.zeros((d, LANES), F32)
    w_cat = jnp.concatenate([w, vd], axis=1).astype(BF16)

    mu = rw_mu[l]
    zeros = jnp.zeros((DECAY_LORA, WIDTH), F32)
    lora_w = jnp.concatenate([jnp.concatenate([rw_w_up[l], zeros], axis=1),
                              jnp.concatenate([zeros, rw_a_up[l]], axis=1)], axis=0).astype(BF16)
    p = {
        "w_cat": w_cat,
        "mu_rkv": mu[:3 * WIDTH].reshape(3, WIDTH),
        "mu_lora": mu[3 * WIDTH:].reshape(1, LANES),
        "w0a0": jnp.concatenate([rw_w0[l], rw_a0[l]]).reshape(1, 2 * WIDTH),
        "lora_w": lora_w,
        "k_k": rw_k_k[l].reshape(1, WIDTH), "k_a": rw_k_a[l].reshape(1, WIDTH),
        "r_k": rw_r_k[l].reshape(1, WIDTH),
        "ln_g": rw_ln_g[l].reshape(1, WIDTH), "ln_b": rw_ln_b[l].reshape(1, WIDTH),
    }
    if l > 0:
        p["vmix_mu"] = jnp.pad(rw_vmix_mu[l - 1], (0, pad)).reshape(1, LANES)
        p["vmix_up"] = jnp.pad(rw_vmix_up[l - 1], ((0, pad), (0, 0))).astype(BF16)
        p["vmix0"] = rw_vmix0[l - 1].reshape(1, WIDTH)
    return p


def kernel(x, norm_pre, norm_post, w_in, rw_mu, rw_w0, rw_w_up, rw_a0, rw_a_up, rw_k_k, rw_k_a, rw_r_k, rw_ln_g, rw_ln_b, rw_vmix_down, rw_vmix_mu, rw_vmix_up, rw_vmix0, w_up_att, w_up_rw, w_out):
    b, s, d = x.shape
    assert d == D_MODEL and s % MOBA_BLOCK == 0
    depth = w_in.shape[0]
    x2d = x.reshape(b * s, d)
    vfirst = None
    for l in range(depth):
        p = _layer_params(l, w_in, rw_mu, rw_w0, rw_w_up, rw_a0, rw_a_up, rw_k_k, rw_k_a, rw_r_k,
                          rw_ln_g, rw_ln_b, rw_vmix_down, rw_vmix_mu, rw_vmix_up, rw_vmix0)
        proj2d = _inproj(x2d, norm_pre[l], p["w_cat"])
        proj = proj2d.reshape(b, s, PROJ_W)
        ya = _moba(proj)
        yb, vfirst = _rwkv(proj, vfirst, p)
        x2d = _outproj(ya.reshape(b * s, WIDTH), proj2d, yb.reshape(b * s, WIDTH), x2d,
                       w_up_att[l].astype(BF16), w_up_rw[l].astype(BF16), w_out[l].astype(BF16),
                       norm_post[l])
    return x2d.reshape(b, s, d)
```

```python
import functools

import jax
import jax.numpy as jnp
from jax import lax
from jax.experimental import pallas as pl
from jax.experimental.pallas import tpu as pltpu

F32 = jnp.float32
BF16 = jnp.bfloat16
ACT = BF16

D_MODEL = 1024
N_HEADS = 8
HEAD_DIM = 64
WIDTH = N_HEADS * HEAD_DIM
MOBA_BLOCK = 256
MOBA_TOPK = 3
DECAY_LORA = 64
ICLR_LORA = 64
VMIX_LORA = 32
RMS_EPS = 1e-6
GN_EPS = 64e-5
L2_EPS = 1e-12
NEG_INF = -1e30

LANES = 128
PAIR = 2 * HEAD_DIM
N_PAIRS = WIDTH // PAIR
V_ROWS = HEAD_DIM + 16
CHUNK = 64
RWKV_SUB = 256

COL_Q, COL_K, COL_V, COL_AZ = 0, 512, 1024, 1536
COL_R, COL_RK, COL_RV = 2048, 2560, 3072
COL_LORA = 3584
COL_RZ = 3712
COL_GATT, COL_GRW = 4224, 5248
COL_VD = 6272
PROJ_W = 6400

VMEM_LIMIT = 48 * 1024 * 1024
LOG2E = 1.4426950408889634


def _sigmoid(x):
    return 0.5 * jnp.tanh(0.5 * x) + 0.5


def _mm(a, b):
    return jnp.dot(a.astype(BF16), b.astype(BF16), preferred_element_type=F32)


def _mm_nt(a, b):
    return lax.dot_general(a.astype(BF16), b.astype(BF16), (((1,), (1,)), ((), ())),
                           preferred_element_type=F32)


def _mm_tn(a, b):
    return lax.dot_general(a.astype(BF16), b.astype(BF16), (((0,), (0,)), ((), ())),
                           preferred_element_type=F32)


def _interleave(gens):
    gens = list(gens)
    while gens:
        for g in list(gens):
            try:
                next(g)
            except StopIteration:
                gens.remove(g)


def _inproj_kernel(x_ref, g_ref, w_ref, wvd_ref, o_ref, *, tn):
    x = x_ref[...]
    ms = jnp.mean(x * x, axis=-1, keepdims=True)
    h = (x * lax.rsqrt(ms + RMS_EPS) * g_ref[...]).astype(BF16)
    n_main = w_ref.shape[1] - wvd_ref.shape[1]
    for n0 in range(0, n_main, tn):
        n1 = min(n0 + tn, n_main)
        o_ref[:, n0:n1] = jnp.dot(h, w_ref[:, n0:n1], preferred_element_type=F32).astype(o_ref.dtype)
    w_tail = jnp.concatenate([w_ref[:, n_main:], wvd_ref[...]], axis=1)
    o_ref[:, n_main:] = jnp.dot(h, w_tail, preferred_element_type=F32).astype(o_ref.dtype)


def _inproj(x2d, g, w_all, layer, w_vd, *, tm=512, tn=1280):
    t, d = x2d.shape
    n_w = w_all.shape[2]
    n = n_w + w_vd.shape[1]
    return pl.pallas_call(
        functools.partial(_inproj_kernel, tn=tn),
        grid=(t // tm,),
        in_specs=[
            pl.BlockSpec((tm, d), lambda i: (i, 0)),
            pl.BlockSpec((1, d), lambda i: (0, 0)),
            pl.BlockSpec((None, d, n_w), lambda i: (layer, 0, 0), pipeline_mode=pl.Buffered(1)),
            pl.BlockSpec((d, w_vd.shape[1]), lambda i: (0, 0), pipeline_mode=pl.Buffered(1)),
        ],
        out_specs=pl.BlockSpec((tm, n), lambda i: (i, 0)),
        out_shape=jax.ShapeDtypeStruct((t, n), ACT),
        compiler_params=pltpu.CompilerParams(
            dimension_semantics=("parallel",), vmem_limit_bytes=VMEM_LIMIT),
        name="inproj",
    )(x2d, g.reshape(1, d), w_all, w_vd)


def _moba_kernel(q_ref, k_ref, v_ref, o_ref, vt_ref, qx_ref, kx_ref, s_ref, p_ref, *, n_blocks):
    hp = pl.program_id(1)
    blk = MOBA_BLOCK
    s_len = n_blocks * blk
    q_scale = (HEAD_DIM ** -0.5) * LOG2E
    nt = (((1,), (1,)), ((), ()))
    sub = 32
    tk = MOBA_BLOCK

    vt = v_ref[...].T
    ones_rows = jnp.where(lax.broadcasted_iota(jnp.int32, (V_ROWS - HEAD_DIM, s_len), 0) == 0, 1.0, 0.0).astype(BF16)
    for h in range(2):
        vt_ref[h] = jnp.concatenate([vt[h * HEAD_DIM:(h + 1) * HEAD_DIM], ones_rows], axis=0)

    mean_row = lax.broadcasted_iota(jnp.int32, (n_blocks, PAIR), 0)
    kmean = jnp.zeros((n_blocks, PAIR), F32)
    for n in range(n_blocks):
        kmean = jnp.where(mean_row == n,
                          jnp.mean(k_ref[n * blk:(n + 1) * blk, :].astype(F32), axis=0, keepdims=True), kmean)

    lane = lax.broadcasted_iota(jnp.int32, (1, PAIR), 1)
    head_lanes = (lane < HEAD_DIM, lane >= HEAD_DIM)
    nrow = lax.broadcasted_iota(jnp.int32, (n_blocks, s_len), 0)
    pos_i = lax.broadcasted_iota(jnp.int32, (n_blocks, s_len), 1)
    pblk = pos_i // blk
    pos = pos_i.astype(F32)
    past = nrow < pblk
    krow = lax.broadcasted_iota(jnp.int32, (blk, blk), 0)
    qcol = lax.broadcasted_iota(jnp.int32, (blk, blk), 1)
    causal = qcol >= krow

    km = jnp.concatenate([jnp.where(head_lanes[0], kmean, 0.0), jnp.where(head_lanes[1], kmean, 0.0)], axis=0)
    km_hi = km.astype(BF16)
    km_r = km - km_hi.astype(F32)
    km_mid = km_r.astype(BF16)
    km_lo = (km_r - km_mid.astype(F32)).astype(BF16)
    parts = lax.dot_general(jnp.concatenate([km_lo, km_mid, km_hi], axis=0), q_ref[...], nt,
                            preferred_element_type=F32)
    ng = 2 * n_blocks
    gates = parts[:ng] + parts[ng:2 * ng] + parts[2 * ng:]

    def split3(x):
        hi = x.astype(BF16).astype(F32)
        mid = (x - hi).astype(BF16).astype(F32)
        return hi, mid, x - hi - mid

    def rows8(r0, r1, r2, r3, r4, r5):
        out = jnp.zeros((n_blocks, s_len), F32)
        for idx, r in enumerate((r0, r1, r2, r3, r4, r5)):
            out = jnp.where(nrow == idx, r, out)
        return out

    ones = jnp.ones((n_blocks, s_len), F32)
    q_extra, k_extra = [], []
    for h in range(2):
        head = lax.convert_element_type(hp * 2 + h + 1, F32)
        slope2 = jnp.exp2(jnp.zeros((1, 1), F32) - (8.0 / N_HEADS) * head) * LOG2E

        gm = jnp.where(past, gates[h * n_blocks:(h + 1) * n_blocks], NEG_INF)
        rank = jnp.zeros((n_blocks, s_len), F32)
        for m in range(n_blocks):
            gcol = gm[m:m + 1, :]
            beats = (gcol > gm) | ((gcol == gm) & (nrow > m))
            rank = rank + jnp.where(beats, 1.0, 0.0)
        visible = (past & (rank < MOBA_TOPK)) | (nrow == pblk)

        q_hi, q_mid, q_lo = split3(-slope2 * pos)
        k_hi, k_mid, k_lo = split3(slope2 * pos)
        q_extra.append(jnp.concatenate([rows8(q_hi, q_mid, q_lo, ones, ones, ones),
                                        jnp.where(visible, 0.0, NEG_INF)], axis=0))
        k_extra.append(jnp.concatenate([rows8(ones, ones, ones, k_hi, k_mid, k_lo),
                                        jnp.where(nrow == pblk, 1.0, 0.0)], axis=0))
    gap = jnp.zeros((HEAD_DIM - 2 * n_blocks, s_len), F32)
    qx_ref[...] = jnp.concatenate([q_extra[1], gap, q_extra[0], gap], axis=0).T.astype(BF16)
    k_x = jnp.concatenate([k_extra[1], gap, k_extra[0], gap], axis=0).T.astype(BF16)
    for h in range(2):
        kx_ref[h] = jnp.where(head_lanes[h], k_ref[...], k_x)

    units = [(i, h) for i in range(n_blocks) for h in range(2)]
    st = {}

    def score_tile(u, t):
        i, h = u
        d = st[u]
        k0 = t * tk
        s_tile = lax.dot_general(kx_ref[h, k0:k0 + tk, :], d["qm"], nt,
                                 preferred_element_type=F32)
        for c in range(0, tk, sub):
            sc = s_tile[c:c + sub]
            if k0 // blk == i:
                r0 = k0 % blk + c
                sc = jnp.where(causal[r0:r0 + sub], sc, NEG_INF)
            s_ref[d["slot"], k0 + c:k0 + c + sub, :] = sc
            for r in range(0, sub, 8):
                d["m8"] = sc[r:r + 8] if d["m8"] is None else jnp.maximum(d["m8"], sc[r:r + 8])

    def softmax_tile(u, t):
        d = st[u]
        for c in range(t * tk, (t + 1) * tk, sub):
            pc = jnp.exp2(s_ref[d["slot"], c:c + sub, :] - d["m"])
            p_ref[d["slot"], c:c + sub, :] = pc.astype(BF16)

    def finish(u):
        i, h = u
        d = st[u]
        n_k = (i + 1) * blk
        pv = jnp.dot(vt_ref[h, :, :n_k], p_ref[d["slot"], :n_k, :],
                     preferred_element_type=F32)
        d["out"] = pv[:HEAD_DIM] * (1.0 / pv[HEAD_DIM:HEAD_DIM + 1])
        if h == 1:
            out_t = jnp.concatenate([st[(i, 0)]["out"], d["out"]], axis=0)
            o_ref[i * blk:(i + 1) * blk, :] = out_t.T.astype(o_ref.dtype)

    prev = None
    for idx, u in enumerate(units):
        i, h = u
        rows = slice(i * blk, (i + 1) * blk)
        if h == 0:
            q_i = (q_ref[rows, :].astype(F32) * q_scale).astype(BF16)
        st[u] = {"qm": jnp.where(head_lanes[h], q_i, qx_ref[rows, :]), "slot": idx % 2,
                 "m8": None}
        n_cur = (i + 1) * blk // tk
        n_prev = (prev[0] + 1) * blk // tk if prev is not None else 0
        for step in range(max(n_cur, n_prev)):
            if step < n_cur:
                score_tile(u, step)
            if step < n_prev:
                softmax_tile(prev, step)
        if prev is not None:
            finish(prev)
        st[u]["m"] = jnp.max(st[u]["m8"], axis=0, keepdims=True)
        prev = u
    for step in range((prev[0] + 1) * blk // tk):
        softmax_tile(prev, step)
    finish(prev)


def _moba(proj):
    b, s, _ = proj.shape
    n_blocks = s // MOBA_BLOCK
    assert n_blocks == 8, "the extra-column layout of the MoBA kernel is laid out for 8 key blocks"
    qo, ko, vo = COL_Q // PAIR, COL_K // PAIR, COL_V // PAIR
    return pl.pallas_call(
        functools.partial(_moba_kernel, n_blocks=n_blocks),
        grid=(b, N_PAIRS),
        in_specs=[
            pl.BlockSpec((None, s, PAIR), lambda i, p: (i, 0, qo + p)),
            pl.BlockSpec((None, s, PAIR), lambda i, p: (i, 0, ko + p)),
            pl.BlockSpec((None, s, PAIR), lambda i, p: (i, 0, vo + p)),
        ],
        out_specs=pl.BlockSpec((None, s, PAIR), lambda i, p: (i, 0, p)),
        out_shape=jax.ShapeDtypeStruct((b, s, WIDTH), ACT),
        scratch_shapes=[
            pltpu.VMEM((2, V_ROWS, s), BF16),
            pltpu.VMEM((s, PAIR), BF16),
            pltpu.VMEM((2, s, PAIR), BF16),
            pltpu.VMEM((2, s, MOBA_BLOCK), F32),
            pltpu.VMEM((2, s, MOBA_BLOCK), BF16),
        ],
        compiler_params=pltpu.CompilerParams(
            dimension_semantics=("parallel", "parallel"), vmem_limit_bytes=VMEM_LIMIT),
        name="moba",
    )(proj, proj, proj)


def _rwkv_kernel(*refs, has_vmix, ts):
    if has_vmix:
        (r_ref, k_ref, v_ref, rz_ref, lora_ref, vd_ref, vfirst_ref,
         mu_rkv_ref, mu_lora_ref, w0a0_ref, lora_w_ref, kk_ref, ka_ref, rk_ref, lng_ref, lnb_ref,
         vmix_mu_ref, vmix_up_ref, vmix0_ref,
         yb_ref, carry_rkv, carry_lora, carry_vd, state_ref) = refs
    else:
        (r_ref, k_ref, v_ref, rz_ref, lora_ref,
         mu_rkv_ref, mu_lora_ref, w0a0_ref, lora_w_ref, kk_ref, ka_ref, rk_ref, lng_ref, lnb_ref,
         yb_ref, vfirst_ref, carry_rkv, carry_lora, state_ref) = refs
        carry_vd = None

    @pl.when(pl.program_id(1) == 0)
    def _():
        carry_rkv[...] = jnp.zeros_like(carry_rkv)
        carry_lora[...] = jnp.zeros_like(carry_lora)
        if carry_vd is not None:
            carry_vd[...] = jnp.zeros_like(carry_vd)
        state_ref[...] = jnp.zeros_like(state_ref)

    sub = RWKV_SUB
    n_sub = ts // sub
    n_chunks = sub // CHUNK
    n2 = 2 * CHUNK
    pcs = [(p, c) for c in range(n_chunks) for p in range(N_PAIRS)]

    sub_row = lax.broadcasted_iota(jnp.int32, (sub, sub), 0)
    sub_col = lax.broadcasted_iota(jnp.int32, (sub, sub), 1)
    diff_mat = (jnp.where(sub_row == sub_col + 1, 1.0, 0.0)
                - jnp.where(sub_row == sub_col, 1.0, 0.0)).astype(r_ref.dtype)
    ltri = jnp.where((sub_row // CHUNK == sub_col // CHUNK) & (sub_col <= sub_row), 1.0, 0.0).astype(BF16)
    top_row = lax.broadcasted_iota(jnp.int32, (8, 1), 0) == 0
    lane = lax.broadcasted_iota(jnp.int32, (sub, LANES), 1)
    head0_t = lane < HEAD_DIM
    head0 = lax.broadcasted_iota(jnp.int32, (CHUNK, PAIR), 1) < HEAD_DIM
    wrow = lax.broadcasted_iota(jnp.int32, (CHUNK, PAIR), 0)
    wcol = lax.broadcasted_iota(jnp.int32, (CHUNK, PAIR), 1) % CHUNK
    strict = wcol < wrow
    incl = wcol <= wrow
    eye = jnp.where(wcol == wrow, 1.0, 0.0).astype(F32)
    srow = lax.broadcasted_iota(jnp.int32, (n2, n2), 0)
    scol = lax.broadcasted_iota(jnp.int32, (n2, n2), 1)
    same_head = (srow // CHUNK) == (scol // CHUNK)
    mu_rkv = mu_rkv_ref[...]

    def shift(y_act, carry_ref, row_idx, mu):
        y = y_act.astype(F32)
        delta = jnp.dot(diff_mat, y_act, preferred_element_type=F32)
        head = jnp.where(top_row, delta[:8] + carry_ref[row_idx:row_idx + 1, :], delta[:8])
        delta = jnp.concatenate([head, delta[8:]], axis=0)
        carry_ref[row_idx:row_idx + 1, :] = y[sub - 1:sub, :]
        return y + delta * mu

    def headsum(x):
        s0 = jnp.sum(jnp.where(head0_t, x, 0.0), axis=-1, keepdims=True)
        s1 = jnp.sum(jnp.where(head0_t, 0.0, x), axis=-1, keepdims=True)
        return jnp.where(head0_t, s0, s1)

    def stack(x):
        zero = jnp.zeros_like(x)
        return jnp.concatenate([jnp.where(head0, x, zero), jnp.where(head0, zero, x)], axis=0)

    data = [dict() for _ in range(n_sub)]

    def prep(s):
        d = data[s]
        rows = slice(s * sub, (s + 1) * sub)
        rs = shift(r_ref[rows, :], carry_rkv, 0, mu_rkv[0:1, :])
        yield
        ks = shift(k_ref[rows, :], carry_rkv, 1, mu_rkv[1:2, :])
        yield
        vs = shift(v_ref[rows, :], carry_rkv, 2, mu_rkv[2:3, :])
        yield
        lo = shift(lora_ref[rows, :], carry_lora, 0, mu_lora_ref[...])
        z = jnp.where(lane < DECAY_LORA, jnp.tanh(lo), lo)
        wa = w0a0_ref[...] + _mm(z, lora_w_ref[...])
        lw = (-0.6065306597126334 * LOG2E) * _sigmoid(wa[:, :WIDTH])
        yield
        a = _sigmoid(wa[:, WIDTH:])
        if has_vmix:
            vd = shift(vd_ref[rows, :], carry_vd, 0, vmix_mu_ref[...])
            mix = _sigmoid(vmix0_ref[...] + _mm(vd, vmix_up_ref[...]))
            vr = vs + (vfirst_ref[rows, :] - vs) * mix
        else:
            vfirst_ref[rows, :] = vs
            vr = vs
        yield
        kx = ks * kk_ref[...]
        kmod = ks * (1.0 + (a - 1.0) * ka_ref[...])
        d["rkr"] = rs * kmod * rk_ref[...]
        d["vr"] = vr
        yield
        kk_all = []
        for p in range(N_PAIRS):
            kxp = kx[:, p * PAIR:(p + 1) * PAIR]
            kk_all.append(kxp * jnp.minimum(lax.rsqrt(headsum(kxp * kxp)), 1.0 / L2_EPS))
            yield
        kk = jnp.concatenate(kk_all, axis=1)
        na = -kk
        nb = kk * a
        lw_hi = lw.astype(BF16)
        lw_r = lw - lw_hi.astype(F32)
        lw_mid = lw_r.astype(BF16)
        lw_lo = (lw_r - lw_mid.astype(F32)).astype(BF16)
        cum = (jnp.dot(ltri, lw_lo, preferred_element_type=F32) + jnp.dot(ltri, lw_mid, preferred_element_type=F32)
               + jnp.dot(ltri, lw_hi, preferred_element_type=F32))
        yield
        a_t = na * jnp.exp2(cum - lw)
        r_t = rs * jnp.exp2(cum)
        yield
        e_neg = jnp.exp2(-cum)
        b_t = nb * e_neg
        k_t = kmod * e_neg
        yield
        lhs_g, rhs_g, a_s, r_n, b_h, k_h, v_s, gam = {}, {}, {}, {}, {}, {}, {}, {}
        for c in range(n_chunks):
            rc = slice(c * CHUNK, (c + 1) * CHUNK)
            tot = cum[(c + 1) * CHUNK - 1:(c + 1) * CHUNK, :]
            e_rel = jnp.exp2(tot - cum[rc])
            bh_c = nb[rc] * e_rel
            kh_c = kmod[rc] * e_rel
            gam_c = jnp.exp2(tot)
            for p in range(N_PAIRS):
                sl = slice(p * PAIR, (p + 1) * PAIR)
                pc = (p, c)
                a_n = a_t[rc, sl].astype(BF16)
                r_n[pc] = r_t[rc, sl]
                lhs_g[pc] = jnp.concatenate([a_n, r_n[pc].astype(BF16)], axis=0)
                rhs_g[pc] = jnp.concatenate([stack(b_t[rc, sl].astype(BF16)),
                                             stack(k_t[rc, sl].astype(BF16))], axis=0)
                a_s[pc] = stack(a_n)
                b_h[pc] = bh_c[:, sl].astype(BF16)
                k_h[pc] = stack(kh_c[:, sl])
                v_s[pc] = stack(vr[rc, sl].astype(BF16))
                gam[pc] = gam_c[:, sl]
            yield
        d.update(lhs_g=lhs_g, rhs_g=rhs_g, a_s=a_s, r_n=r_n, b_h=b_h, k_h=k_h, v_s=v_s, gam=gam)

    def operators(s):
        d = data[s]
        a_ab, a_ak, a_rb, a_rk = {}, {}, {}, {}
        for pc in pcs:
            g = _mm_nt(d["lhs_g"][pc], d["rhs_g"][pc])
            a_ab[pc] = jnp.where(strict, g[:CHUNK, :n2], 0.0)
            a_ak[pc] = jnp.where(strict, g[:CHUNK, n2:], 0.0).astype(BF16)
            a_rb[pc] = jnp.where(incl, g[CHUNK:, :n2], 0.0).astype(BF16)
            a_rk[pc] = jnp.where(incl, g[CHUNK:, n2:], 0.0)
        yield
        t_inv = {pc: eye + a_ab[pc] for pc in pcs}
        pw = {pc: a_ab[pc].astype(BF16) for pc in pcs}
        pw = {pc: _mm(pw[pc], stack(pw[pc])).astype(BF16) for pc in pcs}
        yield
        for _ in range(CHUNK.bit_length() - 3):
            both = {pc: _mm(jnp.concatenate([pw[pc], t_inv[pc].astype(BF16)], axis=0), stack(pw[pc])) for pc in pcs}
            t_inv = {pc: t_inv[pc] + both[pc][CHUNK:] for pc in pcs}
            pw = {pc: both[pc][:CHUNK].astype(BF16) for pc in pcs}
            yield
        t_inv = {pc: (t_inv[pc] + _mm(t_inv[pc], stack(pw[pc]))).astype(BF16) for pc in pcs}
        yield
        z = {pc: jnp.where(same_head, _mm_tn(t_inv[pc], d["b_h"][pc]), 0.0).astype(BF16) for pc in pcs}
        w = {pc: _mm(a_rb[pc], stack(t_inv[pc])).astype(BF16) for pc in pcs}
        yield
        la_ak = {pc: jnp.concatenate([d["a_s"][pc], stack(a_ak[pc])], axis=1) for pc in pcs}
        my = {pc: _mm_tn(la_ak[pc], z[pc]) for pc in pcs}
        wq = {pc: _mm(w[pc], la_ak[pc]) for pc in pcs}
        yield
        d["m_t"] = {pc: my[pc][:n2].astype(BF16) for pc in pcs}
        y = {pc: (my[pc][n2:] + d["k_h"][pc]).astype(BF16) for pc in pcs}
        d["q_p"] = {pc: (d["r_n"][pc] + wq[pc][:, :n2]).astype(BF16) for pc in pcs}
        p_p = {pc: (wq[pc][:, n2:] + a_rk[pc]).astype(BF16) for pc in pcs}
        yield
        d["g_s"] = {pc: _mm_tn(d["v_s"][pc], y[pc]) for pc in pcs}
        d["o_loc"] = {pc: _mm(p_p[pc], d["v_s"][pc]) for pc in pcs}

    states = [state_ref[p] for p in range(N_PAIRS)]

    def recurrence(s):
        d = data[s]
        outs = [[] for _ in range(N_PAIRS)]
        for c in range(n_chunks):
            for p in range(N_PAIRS):
                pc = (p, c)
                sb = states[p].astype(BF16)
                outs[p].append(_mm_nt(d["q_p"][pc], sb) + d["o_loc"][pc])
                states[p] = states[p] * d["gam"][pc] + _mm(sb, d["m_t"][pc]) + d["g_s"][pc]
        d["o"] = [jnp.concatenate(o, axis=0) if len(o) > 1 else o[0] for o in outs]

    def epilogue(s):
        d = data[s]
        rows = slice(s * sub, (s + 1) * sub)
        for p in range(N_PAIRS):
            sl = slice(p * PAIR, (p + 1) * PAIR)
            o_p = d["o"][p]
            bonus = headsum(d["rkr"][:, sl]) * d["vr"][:, sl]
            mu = headsum(o_p) * (1.0 / HEAD_DIM)
            dev = o_p - mu
            var = headsum(dev * dev) * (1.0 / HEAD_DIM)
            on = dev * lax.rsqrt(var + GN_EPS) * lng_ref[:, sl] + lnb_ref[:, sl]
            rz_p = rz_ref[rows, sl].astype(F32)
            yb_ref[rows, sl] = ((on + bonus) * (rz_p * _sigmoid(rz_p))).astype(yb_ref.dtype)
            yield

    _interleave([prep(0)])
    for s in range(n_sub):
        gens = [operators(s)]
        if s + 1 < n_sub:
            gens.append(prep(s + 1))
        if s > 0:
            gens.append(epilogue(s - 1))
        _interleave(gens)
        recurrence(s)
    _interleave([epilogue(n_sub - 1)])
    for p in range(N_PAIRS):
        state_ref[p] = states[p]


def _rwkv(proj, vfirst, p, *, ts=256):
    b, s, _ = proj.shape
    has_vmix = vfirst is not None
    proj2d = proj.reshape(b * s, PROJ_W)
    row = lambda width, col: pl.BlockSpec((pl.Element(ts), pl.Element(width)),
                                          lambda i, t: (pl.multiple_of(i * s + t * ts, ts), col))
    const = lambda shape: pl.BlockSpec(shape, lambda i, t: (0,) * len(shape))
    act = pl.BlockSpec((None, ts, WIDTH), lambda i, t: (i, t, 0))

    in_specs = [row(WIDTH, COL_R), row(WIDTH, COL_RK), row(WIDTH, COL_RV), row(WIDTH, COL_RZ),
                row(LANES, COL_LORA)]
    args = [proj2d, proj2d, proj2d, proj2d, proj2d]
    if has_vmix:
        in_specs += [row(LANES, COL_VD), act]
        args += [proj2d, vfirst]
    in_specs += [const((3, WIDTH)), const((1, LANES)), const((1, 2 * WIDTH)), const((LANES, 2 * WIDTH))]
    args += [p["mu_rkv"], p["mu_lora"], p["w0a0"], p["lora_w"]]
    for name in ("k_k", "k_a", "r_k", "ln_g", "ln_b"):
        in_specs.append(const((1, WIDTH)))
        args.append(p[name])
    if has_vmix:
        in_specs += [const((1, LANES)), const((LANES, WIDTH)), const((1, WIDTH))]
        args += [p["vmix_mu"], p["vmix_up"], p["vmix0"]]

    out_shape = [jax.ShapeDtypeStruct((b, s, WIDTH), ACT)]
    out_specs = [act]
    scratch = [pltpu.VMEM((8, WIDTH), F32), pltpu.VMEM((8, LANES), F32)]
    if has_vmix:
        scratch.append(pltpu.VMEM((8, LANES), F32))
    else:
        out_shape.append(jax.ShapeDtypeStruct((b, s, WIDTH), F32))
        out_specs.append(act)
    scratch.append(pltpu.VMEM((N_PAIRS, PAIR, PAIR), F32))

    res = pl.pallas_call(
        functools.partial(_rwkv_kernel, has_vmix=has_vmix, ts=ts),
        grid=(b, s // ts),
        in_specs=in_specs,
        out_specs=out_specs,
        out_shape=out_shape,
        scratch_shapes=scratch,
        compiler_params=pltpu.CompilerParams(
            dimension_semantics=("parallel", "arbitrary"), vmem_limit_bytes=VMEM_LIMIT),
        name="rwkv_vmix" if has_vmix else "rwkv",
    )(*args)
    return (res[0], vfirst) if has_vmix else (res[0], res[1])


def _outproj_kernel(ya_ref, az_ref, yb_ref, gatt_ref, grw_ref, x_ref,
                    wua_ref, wub_ref, wout_ref, g_ref, o_ref):
    az = az_ref[...].astype(F32)
    ya = ya_ref[...].astype(F32) * (az * _sigmoid(az))
    u = (_sigmoid(gatt_ref[...].astype(F32)) * _mm(ya, wua_ref[...])
         + _sigmoid(grw_ref[...].astype(F32)) * _mm(yb_ref[...], wub_ref[...]))
    y = _mm(u, wout_ref[...])
    ms = jnp.mean(y * y, axis=-1, keepdims=True)
    o_ref[...] = x_ref[...] + y * lax.rsqrt(ms + RMS_EPS) * g_ref[...]


def _outproj(ya, proj2d, yb, x2d, w_up_att, w_up_rw, w_out, g, *, tm=1024):
    t, d = x2d.shape
    row = lambda width, col: pl.BlockSpec((pl.Element(tm), pl.Element(width)),
                                          lambda i: (pl.multiple_of(i * tm, tm), col))
    const = lambda shape: pl.BlockSpec(shape, lambda i: (0, 0))
    return pl.pallas_call(
        _outproj_kernel,
        grid=(t // tm,),
        in_specs=[row(WIDTH, 0), row(WIDTH, COL_AZ), row(WIDTH, 0), row(d, COL_GATT), row(d, COL_GRW),
                  row(d, 0), const((WIDTH, d)), const((WIDTH, d)), const((d, d)), const((1, d))],
        out_specs=row(d, 0),
        out_shape=jax.ShapeDtypeStruct((t, d), F32),
        compiler_params=pltpu.CompilerParams(
            dimension_semantics=("parallel",), vmem_limit_bytes=VMEM_LIMIT),
        name="outproj",
    )(ya, proj2d, yb, proj2d, proj2d, x2d, w_up_att, w_up_rw, w_out, g.reshape(1, d))


def _layer_params(l, d, rw_mu, rw_w0, rw_w_up, rw_a0, rw_a_up, rw_k_k, rw_k_a, rw_r_k,
                  rw_ln_g, rw_ln_b, rw_vmix_down, rw_vmix_mu, rw_vmix_up, rw_vmix0):
    pad = LANES - VMIX_LORA
    vd = jnp.pad(rw_vmix_down[l - 1], ((0, 0), (0, pad))) if l > 0 else jnp.zeros((d, LANES), F32)

    mu = rw_mu[l]
    zeros = jnp.zeros((DECAY_LORA, WIDTH), F32)
    lora_w = jnp.concatenate([jnp.concatenate([rw_w_up[l], zeros], axis=1),
                              jnp.concatenate([zeros, rw_a_up[l]], axis=1)], axis=0).astype(BF16)
    p = {
        "w_vd": vd.astype(BF16),
        "mu_rkv": mu[:3 * WIDTH].reshape(3, WIDTH),
        "mu_lora": mu[3 * WIDTH:].reshape(1, LANES),
        "w0a0": jnp.concatenate([rw_w0[l], rw_a0[l]]).reshape(1, 2 * WIDTH),
        "lora_w": lora_w,
        "k_k": rw_k_k[l].reshape(1, WIDTH), "k_a": rw_k_a[l].reshape(1, WIDTH),
        "r_k": rw_r_k[l].reshape(1, WIDTH),
        "ln_g": rw_ln_g[l].reshape(1, WIDTH), "ln_b": rw_ln_b[l].reshape(1, WIDTH),
    }
    if l > 0:
        p["vmix_mu"] = jnp.pad(rw_vmix_mu[l - 1], (0, pad)).reshape(1, LANES)
        p["vmix_up"] = jnp.pad(rw_vmix_up[l - 1], ((0, pad), (0, 0))).astype(BF16)
        p["vmix0"] = rw_vmix0[l - 1].reshape(1, WIDTH)
    return p


def kernel(x, norm_pre, norm_post, w_in, rw_mu, rw_w0, rw_w_up, rw_a0, rw_a_up, rw_k_k, rw_k_a, rw_r_k, rw_ln_g, rw_ln_b, rw_vmix_down, rw_vmix_mu, rw_vmix_up, rw_vmix0, w_up_att, w_up_rw, w_out):
    b, s, d = x.shape
    assert d == D_MODEL and s % MOBA_BLOCK == 0
    depth = w_in.shape[0]
    x2d = x.reshape(b * s, d)
    w_all = w_in.astype(BF16)
    vfirst = None
    for l in range(depth):
        p = _layer_params(l, d, rw_mu, rw_w0, rw_w_up, rw_a0, rw_a_up, rw_k_k, rw_k_a, rw_r_k,
                          rw_ln_g, rw_ln_b, rw_vmix_down, rw_vmix_mu, rw_vmix_up, rw_vmix0)
        proj2d = _inproj(x2d, norm_pre[l], w_all, l, p["w_vd"])
        proj = proj2d.reshape(b, s, PROJ_W)
        ya = _moba(proj)
        yb, vfirst = _rwkv(proj, vfirst, p)
        x2d = _outproj(ya.reshape(b * s, WIDTH), proj2d, yb.reshape(b * s, WIDTH), x2d,
                       w_up_att[l].astype(BF16), w_up_rw[l].astype(BF16), w_out[l].astype(BF16),
                       norm_post[l])
    return x2d.reshape(b, s, d)
```

```python
import functools

import jax
import jax.numpy as jnp
from jax import lax
from jax.experimental import pallas as pl
from jax.experimental.pallas import tpu as pltpu

F32 = jnp.float32
BF16 = jnp.bfloat16
ACT = BF16

D_MODEL = 1024
N_HEADS = 8
HEAD_DIM = 64
WIDTH = N_HEADS * HEAD_DIM
MOBA_BLOCK = 256
MOBA_TOPK = 3
DECAY_LORA = 64
ICLR_LORA = 64
VMIX_LORA = 32
RMS_EPS = 1e-6
GN_EPS = 64e-5
L2_EPS = 1e-12
NEG_INF = -1e30

LANES = 128
PAIR = 2 * HEAD_DIM
N_PAIRS = WIDTH // PAIR
V_ROWS = HEAD_DIM + 16
CHUNK = 64

COL_Q, COL_K, COL_V, COL_AZ = 0, 512, 1024, 1536
COL_R, COL_RK, COL_RV = 2048, 2560, 3072
COL_LORA = 3584
COL_RZ = 3712
COL_GATT, COL_GRW = 4224, 5248
COL_VD = 6272
PROJ_W = 6400

VMEM_LIMIT = 48 * 1024 * 1024
LOG2E = 1.4426950408889634


def _sigmoid(x):
    return 0.5 * jnp.tanh(0.5 * x) + 0.5


def _mm(a, b):
    return jnp.dot(a.astype(BF16), b.astype(BF16), preferred_element_type=F32)


def _mm_nt(a, b):
    return lax.dot_general(a.astype(BF16), b.astype(BF16), (((1,), (1,)), ((), ())),
                           preferred_element_type=F32)


def _mm_tn(a, b):
    return lax.dot_general(a.astype(BF16), b.astype(BF16), (((0,), (0,)), ((), ())),
                           preferred_element_type=F32)


def _inproj_kernel(x_ref, g_ref, w_ref, wvd_ref, o_ref, *, tn):
    x = x_ref[...]
    ms = jnp.mean(x * x, axis=-1, keepdims=True)
    h = (x * lax.rsqrt(ms + RMS_EPS) * g_ref[...]).astype(BF16)
    n_main = w_ref.shape[1] - wvd_ref.shape[1]
    for n0 in range(0, n_main, tn):
        n1 = min(n0 + tn, n_main)
        o_ref[:, n0:n1] = jnp.dot(h, w_ref[:, n0:n1], preferred_element_type=F32).astype(o_ref.dtype)
    w_tail = jnp.concatenate([w_ref[:, n_main:], wvd_ref[...]], axis=1)
    o_ref[:, n_main:] = jnp.dot(h, w_tail, preferred_element_type=F32).astype(o_ref.dtype)


def _inproj(x2d, g, w_all, layer, w_vd, *, tm=1024, tn=1280):
    t, d = x2d.shape
    n_w = w_all.shape[2]
    n = n_w + w_vd.shape[1]
    return pl.pallas_call(
        functools.partial(_inproj_kernel, tn=tn),
        grid=(t // tm,),
        in_specs=[
            pl.BlockSpec((tm, d), lambda i: (i, 0)),
            pl.BlockSpec((1, d), lambda i: (0, 0)),
            pl.BlockSpec((None, d, n_w), lambda i: (layer, 0, 0), pipeline_mode=pl.Buffered(1)),
            pl.BlockSpec((d, w_vd.shape[1]), lambda i: (0, 0), pipeline_mode=pl.Buffered(1)),
        ],
        out_specs=pl.BlockSpec((tm, n), lambda i: (i, 0)),
        out_shape=jax.ShapeDtypeStruct((t, n), ACT),
        compiler_params=pltpu.CompilerParams(
            dimension_semantics=("parallel",), vmem_limit_bytes=56 * 1024 * 1024),
        name="inproj",
    )(x2d, g.reshape(1, d), w_all, w_vd)


def _moba_kernel(q_ref, k_ref, v_ref, o_ref, vt_ref, qx_ref, kx_ref, s_ref, p_ref, *, n_blocks):
    hp = pl.program_id(1)
    blk = MOBA_BLOCK
    s_len = n_blocks * blk
    q_scale = (HEAD_DIM ** -0.5) * LOG2E
    nt = (((1,), (1,)), ((), ()))
    sub = 32
    tk = MOBA_BLOCK

    vt = v_ref[...].T
    ones_rows = jnp.where(lax.broadcasted_iota(jnp.int32, (V_ROWS - HEAD_DIM, s_len), 0) == 0, 1.0, 0.0).astype(BF16)
    for h in range(2):
        vt_ref[h] = jnp.concatenate([vt[h * HEAD_DIM:(h + 1) * HEAD_DIM], ones_rows], axis=0)

    mean_row = lax.broadcasted_iota(jnp.int32, (n_blocks, PAIR), 0)
    kmean = jnp.zeros((n_blocks, PAIR), F32)
    for n in range(n_blocks):
        kmean = jnp.where(mean_row == n,
                          jnp.mean(k_ref[n * blk:(n + 1) * blk, :].astype(F32), axis=0, keepdims=True), kmean)

    lane = lax.broadcasted_iota(jnp.int32, (1, PAIR), 1)
    head_lanes = (lane < HEAD_DIM, lane >= HEAD_DIM)
    nrow = lax.broadcasted_iota(jnp.int32, (n_blocks, s_len), 0)
    pos_i = lax.broadcasted_iota(jnp.int32, (n_blocks, s_len), 1)
    pblk = pos_i // blk
    pos = pos_i.astype(F32)
    past = nrow < pblk
    krow = lax.broadcasted_iota(jnp.int32, (blk, blk), 0)
    qcol = lax.broadcasted_iota(jnp.int32, (blk, blk), 1)
    causal = qcol >= krow

    km = jnp.concatenate([jnp.where(head_lanes[0], kmean, 0.0), jnp.where(head_lanes[1], kmean, 0.0)], axis=0)
    km_hi = km.astype(BF16)
    km_r = km - km_hi.astype(F32)
    km_mid = km_r.astype(BF16)
    km_lo = (km_r - km_mid.astype(F32)).astype(BF16)
    parts = lax.dot_general(jnp.concatenate([km_lo, km_mid, km_hi], axis=0), q_ref[...], nt,
                            preferred_element_type=F32)
    ng = 2 * n_blocks
    gates = parts[:ng] + parts[ng:2 * ng] + parts[2 * ng:]

    def split3(x):
        hi = x.astype(BF16).astype(F32)
        mid = (x - hi).astype(BF16).astype(F32)
        return hi, mid, x - hi - mid

    def rows8(r0, r1, r2, r3, r4, r5):
        out = jnp.zeros((n_blocks, s_len), F32)
        for idx, r in enumerate((r0, r1, r2, r3, r4, r5)):
            out = jnp.where(nrow == idx, r, out)
        return out

    ones = jnp.ones((n_blocks, s_len), F32)
    q_extra, k_extra = [], []
    for h in range(2):
        head = lax.convert_element_type(hp * 2 + h + 1, F32)
        slope2 = jnp.exp2(jnp.zeros((1, 1), F32) - (8.0 / N_HEADS) * head) * LOG2E

        gm = jnp.where(past, gates[h * n_blocks:(h + 1) * n_blocks], NEG_INF)
        rank = jnp.zeros((n_blocks, s_len), F32)
        for m in range(n_blocks):
            gcol = gm[m:m + 1, :]
            beats = (gcol > gm) | ((gcol == gm) & (nrow > m))
            rank = rank + jnp.where(beats, 1.0, 0.0)
        visible = (past & (rank < MOBA_TOPK)) | (nrow == pblk)

        q_hi, q_mid, q_lo = split3(-slope2 * pos)
        k_hi, k_mid, k_lo = split3(slope2 * pos)
        q_extra.append(jnp.concatenate([rows8(q_hi, q_mid, q_lo, ones, ones, ones),
                                        jnp.where(visible, 0.0, NEG_INF)], axis=0))
        k_extra.append(jnp.concatenate([rows8(ones, ones, ones, k_hi, k_mid, k_lo),
                                        jnp.where(nrow == pblk, 1.0, 0.0)], axis=0))
    gap = jnp.zeros((HEAD_DIM - 2 * n_blocks, s_len), F32)
    qx_ref[...] = jnp.concatenate([q_extra[1], gap, q_extra[0], gap], axis=0).T.astype(BF16)
    k_x = jnp.concatenate([k_extra[1], gap, k_extra[0], gap], axis=0).T.astype(BF16)
    for h in range(2):
        kx_ref[h] = jnp.where(head_lanes[h], k_ref[...], k_x)

    units = [(i, h) for i in range(n_blocks) for h in range(2)]
    st = {}

    def score_tile(u, t):
        i, h = u
        d = st[u]
        k0 = t * tk
        s_tile = lax.dot_general(kx_ref[h, k0:k0 + tk, :], d["qm"], nt,
                                 preferred_element_type=F32)
        for c in range(0, tk, sub):
            sc = s_tile[c:c + sub]
            if k0 // blk == i:
                r0 = k0 % blk + c
                sc = jnp.where(causal[r0:r0 + sub], sc, NEG_INF)
            s_ref[d["slot"], k0 + c:k0 + c + sub, :] = sc
            for r in range(0, sub, 8):
                d["m8"] = sc[r:r + 8] if d["m8"] is None else jnp.maximum(d["m8"], sc[r:r + 8])

    def softmax_tile(u, t):
        d = st[u]
        for c in range(t * tk, (t + 1) * tk, sub):
            pc = jnp.exp2(s_ref[d["slot"], c:c + sub, :] - d["m"])
            p_ref[d["slot"], c:c + sub, :] = pc.astype(BF16)

    def finish(u):
        i, h = u
        d = st[u]
        n_k = (i + 1) * blk
        pv = jnp.dot(vt_ref[h, :, :n_k], p_ref[d["slot"], :n_k, :],
                     preferred_element_type=F32)
        d["out"] = pv[:HEAD_DIM] * (1.0 / pv[HEAD_DIM:HEAD_DIM + 1])
        if h == 1:
            out_t = jnp.concatenate([st[(i, 0)]["out"], d["out"]], axis=0)
            o_ref[i * blk:(i + 1) * blk, :] = out_t.T.astype(o_ref.dtype)

    prev = None
    for idx, u in enumerate(units):
        i, h = u
        rows = slice(i * blk, (i + 1) * blk)
        if h == 0:
            q_i = (q_ref[rows, :].astype(F32) * q_scale).astype(BF16)
        st[u] = {"qm": jnp.where(head_lanes[h], q_i, qx_ref[rows, :]), "slot": idx % 2,
                 "m8": None}
        n_cur = (i + 1) * blk // tk
        n_prev = (prev[0] + 1) * blk // tk if prev is not None else 0
        for step in range(max(n_cur, n_prev)):
            if step < n_cur:
                score_tile(u, step)
            if step < n_prev:
                softmax_tile(prev, step)
        if prev is not None:
            finish(prev)
        st[u]["m"] = jnp.max(st[u]["m8"], axis=0, keepdims=True)
        prev = u
    for step in range((prev[0] + 1) * blk // tk):
        softmax_tile(prev, step)
    finish(prev)


def _moba(proj):
    b, s, _ = proj.shape
    n_blocks = s // MOBA_BLOCK
    assert n_blocks == 8, "the extra-column layout of the MoBA kernel is laid out for 8 key blocks"
    qo, ko, vo = COL_Q // PAIR, COL_K // PAIR, COL_V // PAIR
    return pl.pallas_call(
        functools.partial(_moba_kernel, n_blocks=n_blocks),
        grid=(b, N_PAIRS),
        in_specs=[
            pl.BlockSpec((None, s, PAIR), lambda i, p: (i, 0, qo + p)),
            pl.BlockSpec((None, s, PAIR), lambda i, p: (i, 0, ko + p)),
            pl.BlockSpec((None, s, PAIR), lambda i, p: (i, 0, vo + p)),
        ],
        out_specs=pl.BlockSpec((None, s, PAIR), lambda i, p: (i, 0, p)),
        out_shape=jax.ShapeDtypeStruct((b, s, WIDTH), ACT),
        scratch_shapes=[
            pltpu.VMEM((2, V_ROWS, s), BF16),
            pltpu.VMEM((s, PAIR), BF16),
            pltpu.VMEM((2, s, PAIR), BF16),
            pltpu.VMEM((2, s, MOBA_BLOCK), F32),
            pltpu.VMEM((2, s, MOBA_BLOCK), BF16),
        ],
        compiler_params=pltpu.CompilerParams(
            dimension_semantics=("parallel", "parallel"), vmem_limit_bytes=VMEM_LIMIT),
        name="moba",
    )(proj, proj, proj)


def _rwkv_kernel(*refs, has_vmix, ts):
    if has_vmix:
        (r_ref, k_ref, v_ref, rz_ref, lora_ref, vd_ref, vfirst_ref,
         mu_rkv_ref, mu_lora_ref, w0a0_ref, lora_w_ref, kk_ref, ka_ref, rk_ref, lng_ref, lnb_ref,
         vmix_mu_ref, vmix_up_ref, vmix0_ref,
         yb_ref, carry_rkv, carry_lora, carry_vd, state_ref) = refs
    else:
        (r_ref, k_ref, v_ref, rz_ref, lora_ref,
         mu_rkv_ref, mu_lora_ref, w0a0_ref, lora_w_ref, kk_ref, ka_ref, rk_ref, lng_ref, lnb_ref,
         yb_ref, vfirst_ref, carry_rkv, carry_lora, state_ref) = refs
        carry_vd = None

    @pl.when(pl.program_id(1) == 0)
    def _():
        carry_rkv[...] = jnp.zeros_like(carry_rkv)
        carry_lora[...] = jnp.zeros_like(carry_lora)
        if carry_vd is not None:
            carry_vd[...] = jnp.zeros_like(carry_vd)
        state_ref[...] = jnp.zeros_like(state_ref)

    sub = ts
    n_chunks = sub // CHUNK
    n2 = 2 * CHUNK
    pcs = [(p, c) for c in range(n_chunks) for p in range(N_PAIRS)]

    sub_row = lax.broadcasted_iota(jnp.int32, (sub, sub), 0)
    sub_col = lax.broadcasted_iota(jnp.int32, (sub, sub), 1)
    diff_mat = (jnp.where(sub_row == sub_col + 1, 1.0, 0.0)
                - jnp.where(sub_row == sub_col, 1.0, 0.0)).astype(r_ref.dtype)
    ltri = jnp.where((sub_row // CHUNK == sub_col // CHUNK) & (sub_col <= sub_row), 1.0, 0.0).astype(BF16)
    top_row = lax.broadcasted_iota(jnp.int32, (8, 1), 0) == 0
    lane = lax.broadcasted_iota(jnp.int32, (sub, LANES), 1)
    head0_t = lane < HEAD_DIM
    head0 = lax.broadcasted_iota(jnp.int32, (CHUNK, PAIR), 1) < HEAD_DIM
    wrow = lax.broadcasted_iota(jnp.int32, (CHUNK, PAIR), 0)
    wcol = lax.broadcasted_iota(jnp.int32, (CHUNK, PAIR), 1) % CHUNK
    strict = wcol < wrow
    incl = wcol <= wrow
    eye = jnp.where(wcol == wrow, 1.0, 0.0).astype(F32)
    srow = lax.broadcasted_iota(jnp.int32, (n2, n2), 0)
    scol = lax.broadcasted_iota(jnp.int32, (n2, n2), 1)
    same_head = (srow // CHUNK) == (scol // CHUNK)
    mu_rkv = mu_rkv_ref[...]

    def shift(y_act, carry_ref, row_idx, mu):
        y = y_act.astype(F32)
        delta = jnp.dot(diff_mat, y_act, preferred_element_type=F32)
        head = jnp.where(top_row, delta[:8] + carry_ref[row_idx:row_idx + 1, :], delta[:8])
        delta = jnp.concatenate([head, delta[8:]], axis=0)
        carry_ref[row_idx:row_idx + 1, :] = y[sub - 1:sub, :]
        return y + delta * mu

    def headsum(x):
        s0 = jnp.sum(jnp.where(head0_t, x, 0.0), axis=-1, keepdims=True)
        s1 = jnp.sum(jnp.where(head0_t, 0.0, x), axis=-1, keepdims=True)
        return jnp.where(head0_t, s0, s1)

    def stack(x):
        zero = jnp.zeros_like(x)
        return jnp.concatenate([jnp.where(head0, x, zero), jnp.where(head0, zero, x)], axis=0)

    d = {}

    def prep():
        rs = shift(r_ref[...], carry_rkv, 0, mu_rkv[0:1, :])
        ks = shift(k_ref[...], carry_rkv, 1, mu_rkv[1:2, :])
        vs = shift(v_ref[...], carry_rkv, 2, mu_rkv[2:3, :])
        lo = shift(lora_ref[...], carry_lora, 0, mu_lora_ref[...])
        z = jnp.where(lane < DECAY_LORA, jnp.tanh(lo), lo)
        wa = w0a0_ref[...] + _mm(z, lora_w_ref[...])
        lw = (-0.6065306597126334 * LOG2E) * _sigmoid(wa[:, :WIDTH])
        a = _sigmoid(wa[:, WIDTH:])
        if has_vmix:
            vd = shift(vd_ref[...], carry_vd, 0, vmix_mu_ref[...])
            mix = _sigmoid(vmix0_ref[...] + _mm(vd, vmix_up_ref[...]))
            vr = vs + (vfirst_ref[...] - vs) * mix
        else:
            vfirst_ref[...] = vs
            vr = vs
        kx = ks * kk_ref[...]
        kmod = ks * (1.0 + (a - 1.0) * ka_ref[...])
        d["rkr"] = rs * kmod * rk_ref[...]
        d["vr"] = vr
        kk_all = []
        for p in range(N_PAIRS):
            kxp = kx[:, p * PAIR:(p + 1) * PAIR]
            kk_all.append(kxp * jnp.minimum(lax.rsqrt(headsum(kxp * kxp)), 1.0 / L2_EPS))
        kk = jnp.concatenate(kk_all, axis=1)
        na = -kk
        nb = kk * a
        lw_hi = lw.astype(BF16)
        lw_r = lw - lw_hi.astype(F32)
        lw_mid = lw_r.astype(BF16)
        lw_lo = (lw_r - lw_mid.astype(F32)).astype(BF16)
        cum = (jnp.dot(ltri, lw_lo, preferred_element_type=F32) + jnp.dot(ltri, lw_mid, preferred_element_type=F32)
               + jnp.dot(ltri, lw_hi, preferred_element_type=F32))
        a_t = na * jnp.exp2(cum - lw)
        r_t = rs * jnp.exp2(cum)
        e_neg = jnp.exp2(-cum)
        b_t = nb * e_neg
        k_t = kmod * e_neg
        lhs_g, rhs_g, a_s, r_n, b_h, k_h, v_s, gam = {}, {}, {}, {}, {}, {}, {}, {}
        for c in range(n_chunks):
            rc = slice(c * CHUNK, (c + 1) * CHUNK)
            tot = cum[(c + 1) * CHUNK - 1:(c + 1) * CHUNK, :]
            e_rel = jnp.exp2(tot - cum[rc])
            bh_c = nb[rc] * e_rel
            kh_c = kmod[rc] * e_rel
            gam_c = jnp.exp2(tot)
            for p in range(N_PAIRS):
                sl = slice(p * PAIR, (p + 1) * PAIR)
                pc = (p, c)
                a_n = a_t[rc, sl].astype(BF16)
                r_n[pc] = r_t[rc, sl]
                lhs_g[pc] = jnp.concatenate([a_n, r_n[pc].astype(BF16)], axis=0)
                rhs_g[pc] = jnp.concatenate([stack(b_t[rc, sl].astype(BF16)),
                                             stack(k_t[rc, sl].astype(BF16))], axis=0)
                a_s[pc] = stack(a_n)
                b_h[pc] = bh_c[:, sl].astype(BF16)
                k_h[pc] = stack(kh_c[:, sl])
                v_s[pc] = stack(vr[rc, sl].astype(BF16))
                gam[pc] = gam_c[:, sl]
        d.update(lhs_g=lhs_g, rhs_g=rhs_g, a_s=a_s, r_n=r_n, b_h=b_h, k_h=k_h, v_s=v_s, gam=gam)

    def operators():
        a_ab, a_ak, a_rb, a_rk = {}, {}, {}, {}
        for pc in pcs:
            g = _mm_nt(d["lhs_g"][pc], d["rhs_g"][pc])
            a_ab[pc] = jnp.where(strict, g[:CHUNK, :n2], 0.0)
            a_ak[pc] = jnp.where(strict, g[:CHUNK, n2:], 0.0).astype(BF16)
            a_rb[pc] = jnp.where(incl, g[CHUNK:, :n2], 0.0).astype(BF16)
            a_rk[pc] = jnp.where(incl, g[CHUNK:, n2:], 0.0)
        t_inv = {pc: eye + a_ab[pc] for pc in pcs}
        pw = {pc: a_ab[pc].astype(BF16) for pc in pcs}
        pw = {pc: _mm(pw[pc], stack(pw[pc])).astype(BF16) for pc in pcs}
        for _ in range(CHUNK.bit_length() - 3):
            both = {pc: _mm(jnp.concatenate([pw[pc], t_inv[pc].astype(BF16)], axis=0), stack(pw[pc])) for pc in pcs}
            t_inv = {pc: t_inv[pc] + both[pc][CHUNK:] for pc in pcs}
            pw = {pc: both[pc][:CHUNK].astype(BF16) for pc in pcs}
        t_inv = {pc: (t_inv[pc] + _mm(t_inv[pc], stack(pw[pc]))).astype(BF16) for pc in pcs}
        z = {pc: jnp.where(same_head, _mm_tn(t_inv[pc], d["b_h"][pc]), 0.0).astype(BF16) for pc in pcs}
        w = {pc: _mm(a_rb[pc], stack(t_inv[pc])).astype(BF16) for pc in pcs}
        la_ak = {pc: jnp.concatenate([d["a_s"][pc], stack(a_ak[pc])], axis=1) for pc in pcs}
        my = {pc: _mm_tn(la_ak[pc], z[pc]) for pc in pcs}
        wq = {pc: _mm(w[pc], la_ak[pc]) for pc in pcs}
        d["m_t"] = {pc: my[pc][:n2].astype(BF16) for pc in pcs}
        y = {pc: (my[pc][n2:] + d["k_h"][pc]).astype(BF16) for pc in pcs}
        d["q_p"] = {pc: (d["r_n"][pc] + wq[pc][:, :n2]).astype(BF16) for pc in pcs}
        p_p = {pc: (wq[pc][:, n2:] + a_rk[pc]).astype(BF16) for pc in pcs}
        d["g_s"] = {pc: _mm_tn(d["v_s"][pc], y[pc]) for pc in pcs}
        d["o_loc"] = {pc: _mm(p_p[pc], d["v_s"][pc]) for pc in pcs}

    def recurrence():
        states = [state_ref[p] for p in range(N_PAIRS)]
        outs = [[] for _ in range(N_PAIRS)]
        for c in range(n_chunks):
            for p in range(N_PAIRS):
                pc = (p, c)
                sb = states[p].astype(BF16)
                outs[p].append(_mm_nt(d["q_p"][pc], sb) + d["o_loc"][pc])
                states[p] = states[p] * d["gam"][pc] + _mm(sb, d["m_t"][pc]) + d["g_s"][pc]
        for p in range(N_PAIRS):
            state_ref[p] = states[p]
        d["o"] = [jnp.concatenate(o, axis=0) for o in outs]

    def epilogue():
        for p in range(N_PAIRS):
            sl = slice(p * PAIR, (p + 1) * PAIR)
            o_p = d["o"][p]
            bonus = headsum(d["rkr"][:, sl]) * d["vr"][:, sl]
            mu = headsum(o_p) * (1.0 / HEAD_DIM)
            dev = o_p - mu
            var = headsum(dev * dev) * (1.0 / HEAD_DIM)
            on = dev * lax.rsqrt(var + GN_EPS) * lng_ref[:, sl] + lnb_ref[:, sl]
            rz_p = rz_ref[:, sl].astype(F32)
            yb_ref[:, sl] = ((on + bonus) * (rz_p * _sigmoid(rz_p))).astype(yb_ref.dtype)

    prep()
    operators()
    recurrence()
    epilogue()


def _rwkv(proj, vfirst, p, *, ts=256):
    b, s, _ = proj.shape
    has_vmix = vfirst is not None
    proj2d = proj.reshape(b * s, PROJ_W)
    row = lambda width, col: pl.BlockSpec((pl.Element(ts), pl.Element(width)),
                                          lambda i, t: (pl.multiple_of(i * s + t * ts, ts), col))
    const = lambda shape: pl.BlockSpec(shape, lambda i, t: (0,) * len(shape))
    act = pl.BlockSpec((None, ts, WIDTH), lambda i, t: (i, t, 0))

    in_specs = [row(WIDTH, COL_R), row(WIDTH, COL_RK), row(WIDTH, COL_RV), row(WIDTH, COL_RZ),
                row(LANES, COL_LORA)]
    args = [proj2d, proj2d, proj2d, proj2d, proj2d]
    if has_vmix:
        in_specs += [row(LANES, COL_VD), act]
        args += [proj2d, vfirst]
    in_specs += [const((3, WIDTH)), const((1, LANES)), const((1, 2 * WIDTH)), const((LANES, 2 * WIDTH))]
    args += [p["mu_rkv"], p["mu_lora"], p["w0a0"], p["lora_w"]]
    for name in ("k_k", "k_a", "r_k", "ln_g", "ln_b"):
        in_specs.append(const((1, WIDTH)))
        args.append(p[name])
    if has_vmix:
        in_specs += [const((1, LANES)), const((LANES, WIDTH)), const((1, WIDTH))]
        args += [p["vmix_mu"], p["vmix_up"], p["vmix0"]]

    out_shape = [jax.ShapeDtypeStruct((b, s, WIDTH), ACT)]
    out_specs = [act]
    scratch = [pltpu.VMEM((8, WIDTH), F32), pltpu.VMEM((8, LANES), F32)]
    if has_vmix:
        scratch.append(pltpu.VMEM((8, LANES), F32))
    else:
        out_shape.append(jax.ShapeDtypeStruct((b, s, WIDTH), F32))
        out_specs.append(act)
    scratch.append(pltpu.VMEM((N_PAIRS, PAIR, PAIR), F32))

    res = pl.pallas_call(
        functools.partial(_rwkv_kernel, has_vmix=has_vmix, ts=ts),
        grid=(b, s // ts),
        in_specs=in_specs,
        out_specs=out_specs,
        out_shape=out_shape,
        scratch_shapes=scratch,
        compiler_params=pltpu.CompilerParams(
            dimension_semantics=("parallel", "arbitrary"), vmem_limit_bytes=VMEM_LIMIT),
        name="rwkv_vmix" if has_vmix else "rwkv",
    )(*args)
    return (res[0], vfirst) if has_vmix else (res[0], res[1])


def _outproj_kernel(ya_ref, az_ref, yb_ref, gatt_ref, grw_ref, x_ref,
                    wua_ref, wub_ref, wout_ref, g_ref, o_ref):
    az = az_ref[...].astype(F32)
    ya = ya_ref[...].astype(F32) * (az * _sigmoid(az))
    u = (_sigmoid(gatt_ref[...].astype(F32)) * _mm(ya, wua_ref[...])
         + _sigmoid(grw_ref[...].astype(F32)) * _mm(yb_ref[...], wub_ref[...]))
    y = _mm(u, wout_ref[...])
    ms = jnp.mean(y * y, axis=-1, keepdims=True)
    o_ref[...] = x_ref[...] + y * lax.rsqrt(ms + RMS_EPS) * g_ref[...]


def _outproj(ya, proj2d, yb, x2d, w_up_att, w_up_rw, w_out, g, *, tm=1024):
    t, d = x2d.shape
    row = lambda width, col: pl.BlockSpec((pl.Element(tm), pl.Element(width)),
                                          lambda i: (pl.multiple_of(i * tm, tm), col))
    const = lambda shape: pl.BlockSpec(shape, lambda i: (0, 0))
    return pl.pallas_call(
        _outproj_kernel,
        grid=(t // tm,),
        in_specs=[row(WIDTH, 0), row(WIDTH, COL_AZ), row(WIDTH, 0), row(d, COL_GATT), row(d, COL_GRW),
                  row(d, 0), const((WIDTH, d)), const((WIDTH, d)), const((d, d)), const((1, d))],
        out_specs=row(d, 0),
        out_shape=jax.ShapeDtypeStruct((t, d), F32),
        compiler_params=pltpu.CompilerParams(
            dimension_semantics=("parallel",), vmem_limit_bytes=VMEM_LIMIT),
        name="outproj",
    )(ya, proj2d, yb, proj2d, proj2d, x2d, w_up_att, w_up_rw, w_out, g.reshape(1, d))


def _layer_params(l, d, rw_mu, rw_w0, rw_w_up, rw_a0, rw_a_up, rw_k_k, rw_k_a, rw_r_k,
                  rw_ln_g, rw_ln_b, rw_vmix_down, rw_vmix_mu, rw_vmix_up, rw_vmix0):
    pad = LANES - VMIX_LORA
    vd = jnp.pad(rw_vmix_down[l - 1], ((0, 0), (0, pad))) if l > 0 else jnp.zeros((d, LANES), F32)

    mu = rw_mu[l]
    zeros = jnp.zeros((DECAY_LORA, WIDTH), F32)
    lora_w = jnp.concatenate([jnp.concatenate([rw_w_up[l], zeros], axis=1),
                              jnp.concatenate([zeros, rw_a_up[l]], axis=1)], axis=0).astype(BF16)
    p = {
        "w_vd": vd.astype(BF16),
        "mu_rkv": mu[:3 * WIDTH].reshape(3, WIDTH),
        "mu_lora": mu[3 * WIDTH:].reshape(1, LANES),
        "w0a0": jnp.concatenate([rw_w0[l], rw_a0[l]]).reshape(1, 2 * WIDTH),
        "lora_w": lora_w,
        "k_k": rw_k_k[l].reshape(1, WIDTH), "k_a": rw_k_a[l].reshape(1, WIDTH),
        "r_k": rw_r_k[l].reshape(1, WIDTH),
        "ln_g": rw_ln_g[l].reshape(1, WIDTH), "ln_b": rw_ln_b[l].reshape(1, WIDTH),
    }
    if l > 0:
        p["vmix_mu"] = jnp.pad(rw_vmix_mu[l - 1], (0, pad)).reshape(1, LANES)
        p["vmix_up"] = jnp.pad(rw_vmix_up[l - 1], ((0, pad), (0, 0))).astype(BF16)
        p["vmix0"] = rw_vmix0[l - 1].reshape(1, WIDTH)
    return p


def kernel(x, norm_pre, norm_post, w_in, rw_mu, rw_w0, rw_w_up, rw_a0, rw_a_up, rw_k_k, rw_k_a, rw_r_k, rw_ln_g, rw_ln_b, rw_vmix_down, rw_vmix_mu, rw_vmix_up, rw_vmix0, w_up_att, w_up_rw, w_out):
    b, s, d = x.shape
    assert d == D_MODEL and s % MOBA_BLOCK == 0
    depth = w_in.shape[0]
    x2d = x.reshape(b * s, d)
    w_all = w_in.astype(BF16)
    vfirst = None
    for l in range(depth):
        p = _layer_params(l, d, rw_mu, rw_w0, rw_w_up, rw_a0, rw_a_up, rw_k_k, rw_k_a, rw_r_k,
                          rw_ln_g, rw_ln_b, rw_vmix_down, rw_vmix_mu, rw_vmix_up, rw_vmix0)
        proj2d = _inproj(x2d, norm_pre[l], w_all, l, p["w_vd"])
        proj = proj2d.reshape(b, s, PROJ_W)
        ya = _moba(proj)
        yb, vfirst = _rwkv(proj, vfirst, p)
        x2d = _outproj(ya.reshape(b * s, WIDTH), proj2d, yb.reshape(b * s, WIDTH), x2d,
                       w_up_att[l].astype(BF16), w_up_rw[l].astype(BF16), w_out[l].astype(BF16),
                       norm_post[l])
    return x2d.reshape(b, s, d)
```

```python
import functools

import jax
import jax.numpy as jnp
from jax import lax
from jax.experimental import pallas as pl
from jax.experimental.pallas import tpu as pltpu

F32 = jnp.float32
BF16 = jnp.bfloat16
ACT = BF16

D_MODEL = 1024
N_HEADS = 8
HEAD_DIM = 64
WIDTH = N_HEADS * HEAD_DIM
MOBA_BLOCK = 256
MOBA_TOPK = 3
DECAY_LORA = 64
ICLR_LORA = 64
VMIX_LORA = 32
RMS_EPS = 1e-6
GN_EPS = 64e-5
L2_EPS = 1e-12
NEG_INF = -1e30

LANES = 128
PAIR = 2 * HEAD_DIM
N_PAIRS = WIDTH // PAIR
V_ROWS = HEAD_DIM + 16
MOBA_PAIRS_PER_STEP = 4
CHUNK = 64

COL_Q, COL_K, COL_V, COL_AZ = 0, 512, 1024, 1536
COL_R, COL_RK, COL_RV = 2048, 2560, 3072
COL_LORA = 3584
COL_RZ = 3712
COL_GATT, COL_GRW = 4224, 5248
COL_VD = 6272
PROJ_W = 6400

VMEM_LIMIT = 48 * 1024 * 1024
LOG2E = 1.4426950408889634


def _sigmoid(x):
    return 0.5 * jnp.tanh(0.5 * x) + 0.5


def _mm(a, b):
    return jnp.dot(a.astype(BF16), b.astype(BF16), preferred_element_type=F32)


def _mm_nt(a, b):
    return lax.dot_general(a.astype(BF16), b.astype(BF16), (((1,), (1,)), ((), ())),
                           preferred_element_type=F32)


def _mm_tn(a, b):
    return lax.dot_general(a.astype(BF16), b.astype(BF16), (((0,), (0,)), ((), ())),
                           preferred_element_type=F32)


def _inproj_kernel(x_ref, g_ref, w_ref, wvd_ref, o_ref, *, tn):
    x = x_ref[...]
    ms = jnp.mean(x * x, axis=-1, keepdims=True)
    h = (x * lax.rsqrt(ms + RMS_EPS) * g_ref[...]).astype(BF16)
    n_main = w_ref.shape[1] - wvd_ref.shape[1]
    for n0 in range(0, n_main, tn):
        n1 = min(n0 + tn, n_main)
        o_ref[:, n0:n1] = jnp.dot(h, w_ref[:, n0:n1], preferred_element_type=F32).astype(o_ref.dtype)
    w_tail = jnp.concatenate([w_ref[:, n_main:], wvd_ref[...]], axis=1)
    o_ref[:, n_main:] = jnp.dot(h, w_tail, preferred_element_type=F32).astype(o_ref.dtype)


def _inproj(x2d, g, w_all, layer, w_vd, *, tm=1024, tn=1280):
    t, d = x2d.shape
    n_w = w_all.shape[2]
    n = n_w + w_vd.shape[1]
    return pl.pallas_call(
        functools.partial(_inproj_kernel, tn=tn),
        grid=(t // tm,),
        in_specs=[
            pl.BlockSpec((tm, d), lambda i: (i, 0)),
            pl.BlockSpec((1, d), lambda i: (0, 0)),
            pl.BlockSpec((None, d, n_w), lambda i: (layer, 0, 0), pipeline_mode=pl.Buffered(1)),
            pl.BlockSpec((d, w_vd.shape[1]), lambda i: (0, 0), pipeline_mode=pl.Buffered(1)),
        ],
        out_specs=pl.BlockSpec((tm, n), lambda i: (i, 0)),
        out_shape=jax.ShapeDtypeStruct((t, n), ACT),
        compiler_params=pltpu.CompilerParams(
            dimension_semantics=("parallel",), vmem_limit_bytes=56 * 1024 * 1024),
        name="inproj",
    )(x2d, g.reshape(1, d), w_all, w_vd)


def _moba_kernel(q_ref, k_ref, v_ref, o_ref, vt_ref, qx_ref, kx_ref, s_ref, p_ref, *, n_blocks, n_pairs):
    hp0 = pl.program_id(1) * n_pairs
    blk = MOBA_BLOCK
    s_len = n_blocks * blk
    q_scale = (HEAD_DIM ** -0.5) * LOG2E
    nt = (((1,), (1,)), ((), ()))
    sub = 32
    tk = MOBA_BLOCK

    lane = lax.broadcasted_iota(jnp.int32, (1, PAIR), 1)
    head_lanes = (lane < HEAD_DIM, lane >= HEAD_DIM)
    nrow = lax.broadcasted_iota(jnp.int32, (n_blocks, s_len), 0)
    pos_i = lax.broadcasted_iota(jnp.int32, (n_blocks, s_len), 1)
    pblk = pos_i // blk
    pos = pos_i.astype(F32)
    past = nrow < pblk
    krow = lax.broadcasted_iota(jnp.int32, (blk, blk), 0)
    qcol = lax.broadcasted_iota(jnp.int32, (blk, blk), 1)
    causal = qcol >= krow

    def split3(x):
        hi = x.astype(BF16).astype(F32)
        mid = (x - hi).astype(BF16).astype(F32)
        return hi, mid, x - hi - mid

    def rows8(r0, r1, r2, r3, r4, r5):
        out = jnp.zeros((n_blocks, s_len), F32)
        for idx, r in enumerate((r0, r1, r2, r3, r4, r5)):
            out = jnp.where(nrow == idx, r, out)
        return out

    ones = jnp.ones((n_blocks, s_len), F32)
    ones_rows = jnp.where(lax.broadcasted_iota(jnp.int32, (V_ROWS - HEAD_DIM, s_len), 0) == 0, 1.0, 0.0).astype(BF16)
    mean_row = lax.broadcasted_iota(jnp.int32, (n_blocks, PAIR), 0)

    for g in range(n_pairs):
        cols = slice(g * PAIR, (g + 1) * PAIR)
        vt = v_ref[:, cols].T
        for h in range(2):
            vt_ref[g, h] = jnp.concatenate([vt[h * HEAD_DIM:(h + 1) * HEAD_DIM], ones_rows], axis=0)

        kmean = jnp.zeros((n_blocks, PAIR), F32)
        for n in range(n_blocks):
            kmean = jnp.where(mean_row == n,
                              jnp.mean(k_ref[n * blk:(n + 1) * blk, cols].astype(F32), axis=0, keepdims=True), kmean)

        km = jnp.concatenate([jnp.where(head_lanes[0], kmean, 0.0), jnp.where(head_lanes[1], kmean, 0.0)], axis=0)
        km_hi = km.astype(BF16)
        km_r = km - km_hi.astype(F32)
        km_mid = km_r.astype(BF16)
        km_lo = (km_r - km_mid.astype(F32)).astype(BF16)
        parts = lax.dot_general(jnp.concatenate([km_lo, km_mid, km_hi], axis=0), q_ref[:, cols], nt,
                                preferred_element_type=F32)
        ng = 2 * n_blocks
        gates = parts[:ng] + parts[ng:2 * ng] + parts[2 * ng:]

        q_extra, k_extra = [], []
        for h in range(2):
            head = lax.convert_element_type((hp0 + g) * 2 + h + 1, F32)
            slope2 = jnp.exp2(jnp.zeros((1, 1), F32) - (8.0 / N_HEADS) * head) * LOG2E

            gm = jnp.where(past, gates[h * n_blocks:(h + 1) * n_blocks], NEG_INF)
            rank = jnp.zeros((n_blocks, s_len), F32)
            for m in range(n_blocks):
                gcol = gm[m:m + 1, :]
                beats = (gcol > gm) | ((gcol == gm) & (nrow > m))
                rank = rank + jnp.where(beats, 1.0, 0.0)
            visible = (past & (rank < MOBA_TOPK)) | (nrow == pblk)

            q_hi, q_mid, q_lo = split3(-slope2 * pos)
            k_hi, k_mid, k_lo = split3(slope2 * pos)
            q_extra.append(jnp.concatenate([rows8(q_hi, q_mid, q_lo, ones, ones, ones),
                                            jnp.where(visible, 0.0, NEG_INF)], axis=0))
            k_extra.append(jnp.concatenate([rows8(ones, ones, ones, k_hi, k_mid, k_lo),
                                            jnp.where(nrow == pblk, 1.0, 0.0)], axis=0))
        gap = jnp.zeros((HEAD_DIM - 2 * n_blocks, s_len), F32)
        qx_ref[g] = jnp.concatenate([q_extra[1], gap, q_extra[0], gap], axis=0).T.astype(BF16)
        k_x = jnp.concatenate([k_extra[1], gap, k_extra[0], gap], axis=0).T.astype(BF16)
        for h in range(2):
            kx_ref[g, h] = jnp.where(head_lanes[h], k_ref[:, cols], k_x)

    units = [(g, i, h) for g in range(n_pairs) for i in range(n_blocks) for h in range(2)]
    st = {}

    def score_tile(u, t):
        g, i, h = u
        d = st[u]
        k0 = t * tk
        s_tile = lax.dot_general(kx_ref[g, h, k0:k0 + tk, :], d["qm"], nt,
                                 preferred_element_type=F32)
        for c in range(0, tk, sub):
            sc = s_tile[c:c + sub]
            if k0 // blk == i:
                r0 = k0 % blk + c
                sc = jnp.where(causal[r0:r0 + sub], sc, NEG_INF)
            s_ref[d["slot"], k0 + c:k0 + c + sub, :] = sc
            for r in range(0, sub, 8):
                d["m8"] = sc[r:r + 8] if d["m8"] is None else jnp.maximum(d["m8"], sc[r:r + 8])

    def softmax_tile(u, t):
        d = st[u]
        for c in range(t * tk, (t + 1) * tk, sub):
            pc = jnp.exp2(s_ref[d["slot"], c:c + sub, :] - d["m"])
            p_ref[d["slot"], c:c + sub, :] = pc.astype(BF16)

    def finish(u):
        g, i, h = u
        d = st[u]
        n_k = (i + 1) * blk
        pv = jnp.dot(vt_ref[g, h, :, :n_k], p_ref[d["slot"], :n_k, :],
                     preferred_element_type=F32)
        d["out"] = pv[:HEAD_DIM] * (1.0 / pv[HEAD_DIM:HEAD_DIM + 1])
        if h == 1:
            out_t = jnp.concatenate([st[(g, i, 0)]["out"], d["out"]], axis=0)
            o_ref[i * blk:(i + 1) * blk, g * PAIR:(g + 1) * PAIR] = out_t.T.astype(o_ref.dtype)

    prev = None
    for idx, u in enumerate(units):
        g, i, h = u
        rows = slice(i * blk, (i + 1) * blk)
        if h == 0:
            q_i = (q_ref[rows, g * PAIR:(g + 1) * PAIR].astype(F32) * q_scale).astype(BF16)
        st[u] = {"qm": jnp.where(head_lanes[h], q_i, qx_ref[g, rows, :]), "slot": idx % 2,
                 "m8": None}
        n_cur = (i + 1) * blk // tk
        n_prev = (prev[1] + 1) * blk // tk if prev is not None else 0
        for step in range(max(n_cur, n_prev)):
            if step < n_cur:
                score_tile(u, step)
            if step < n_prev:
                softmax_tile(prev, step)
        if prev is not None:
            finish(prev)
        st[u]["m"] = jnp.max(st[u]["m8"], axis=0, keepdims=True)
        prev = u
    for step in range((prev[1] + 1) * blk // tk):
        softmax_tile(prev, step)
    finish(prev)


def _moba(proj):
    b, s, _ = proj.shape
    n_blocks = s // MOBA_BLOCK
    assert n_blocks == 8, "the extra-column layout of the MoBA kernel is laid out for 8 key blocks"
    n_pairs = MOBA_PAIRS_PER_STEP
    width = n_pairs * PAIR
    qo, ko, vo = COL_Q // width, COL_K // width, COL_V // width
    return pl.pallas_call(
        functools.partial(_moba_kernel, n_blocks=n_blocks, n_pairs=n_pairs),
        grid=(b, N_PAIRS // n_pairs),
        in_specs=[
            pl.BlockSpec((None, s, width), lambda i, p: (i, 0, qo + p)),
            pl.BlockSpec((None, s, width), lambda i, p: (i, 0, ko + p)),
            pl.BlockSpec((None, s, width), lambda i, p: (i, 0, vo + p)),
        ],
        out_specs=pl.BlockSpec((None, s, width), lambda i, p: (i, 0, p)),
        out_shape=jax.ShapeDtypeStruct((b, s, WIDTH), ACT),
        scratch_shapes=[
            pltpu.VMEM((n_pairs, 2, V_ROWS, s), BF16),
            pltpu.VMEM((n_pairs, s, PAIR), BF16),
            pltpu.VMEM((n_pairs, 2, s, PAIR), BF16),
            pltpu.VMEM((2, s, MOBA_BLOCK), F32),
            pltpu.VMEM((2, s, MOBA_BLOCK), BF16),
        ],
        compiler_params=pltpu.CompilerParams(
            dimension_semantics=("parallel", "parallel"), vmem_limit_bytes=VMEM_LIMIT),
        name="moba",
    )(proj, proj, proj)


def _rwkv_kernel(*refs, has_vmix, ts):
    if has_vmix:
        (r_ref, k_ref, v_ref, rz_ref, lora_ref, vd_ref, vfirst_ref,
         mu_rkv_ref, mu_lora_ref, w0a0_ref, lora_w_ref, kk_ref, ka_ref, rk_ref, lng_ref, lnb_ref,
         vmix_mu_ref, vmix_up_ref, vmix0_ref,
         yb_ref, carry_rkv, carry_lora, carry_vd, state_ref) = refs
    else:
        (r_ref, k_ref, v_ref, rz_ref, lora_ref,
         mu_rkv_ref, mu_lora_ref, w0a0_ref, lora_w_ref, kk_ref, ka_ref, rk_ref, lng_ref, lnb_ref,
         yb_ref, vfirst_ref, carry_rkv, carry_lora, state_ref) = refs
        carry_vd = None

    @pl.when(pl.program_id(1) == 0)
    def _():
        carry_rkv[...] = jnp.zeros_like(carry_rkv)
        carry_lora[...] = jnp.zeros_like(carry_lora)
        if carry_vd is not None:
            carry_vd[...] = jnp.zeros_like(carry_vd)
        state_ref[...] = jnp.zeros_like(state_ref)

    sub = ts
    n_chunks = sub // CHUNK
    n2 = 2 * CHUNK
    pcs = [(p, c) for c in range(n_chunks) for p in range(N_PAIRS)]

    sub_row = lax.broadcasted_iota(jnp.int32, (sub, sub), 0)
    sub_col = lax.broadcasted_iota(jnp.int32, (sub, sub), 1)
    diff_mat = (jnp.where(sub_row == sub_col + 1, 1.0, 0.0)
                - jnp.where(sub_row == sub_col, 1.0, 0.0)).astype(r_ref.dtype)
    ltri = jnp.where((sub_row // CHUNK == sub_col // CHUNK) & (sub_col <= sub_row), 1.0, 0.0).astype(BF16)
    top_row = lax.broadcasted_iota(jnp.int32, (8, 1), 0) == 0
    lane = lax.broadcasted_iota(jnp.int32, (sub, LANES), 1)
    head0_t = lane < HEAD_DIM
    head0 = lax.broadcasted_iota(jnp.int32, (CHUNK, PAIR), 1) < HEAD_DIM
    wrow = lax.broadcasted_iota(jnp.int32, (CHUNK, PAIR), 0)
    wcol = lax.broadcasted_iota(jnp.int32, (CHUNK, PAIR), 1) % CHUNK
    strict = wcol < wrow
    incl = wcol <= wrow
    eye = jnp.where(wcol == wrow, 1.0, 0.0).astype(F32)
    srow = lax.broadcasted_iota(jnp.int32, (n2, n2), 0)
    scol = lax.broadcasted_iota(jnp.int32, (n2, n2), 1)
    same_head = (srow // CHUNK) == (scol // CHUNK)
    mu_rkv = mu_rkv_ref[...]

    def shift(y_act, carry_ref, row_idx, mu):
        y = y_act.astype(F32)
        delta = jnp.dot(diff_mat, y_act, preferred_element_type=F32)
        head = jnp.where(top_row, delta[:8] + carry_ref[row_idx:row_idx + 1, :], delta[:8])
        delta = jnp.concatenate([head, delta[8:]], axis=0)
        carry_ref[row_idx:row_idx + 1, :] = y[sub - 1:sub, :]
        return y + delta * mu

    def headsum(x):
        s0 = jnp.sum(jnp.where(head0_t, x, 0.0), axis=-1, keepdims=True)
        s1 = jnp.sum(jnp.where(head0_t, 0.0, x), axis=-1, keepdims=True)
        return jnp.where(head0_t, s0, s1)

    def stack(x):
        zero = jnp.zeros_like(x)
        return jnp.concatenate([jnp.where(head0, x, zero), jnp.where(head0, zero, x)], axis=0)

    d = {}

    def prep():
        rs = shift(r_ref[...], carry_rkv, 0, mu_rkv[0:1, :])
        ks = shift(k_ref[...], carry_rkv, 1, mu_rkv[1:2, :])
        vs = shift(v_ref[...], carry_rkv, 2, mu_rkv[2:3, :])
        lo = shift(lora_ref[...], carry_lora, 0, mu_lora_ref[...])
        z = jnp.where(lane < DECAY_LORA, jnp.tanh(lo), lo)
        wa = w0a0_ref[...] + _mm(z, lora_w_ref[...])
        lw = (-0.6065306597126334 * LOG2E) * _sigmoid(wa[:, :WIDTH])
        a = _sigmoid(wa[:, WIDTH:])
        if has_vmix:
            vd = shift(vd_ref[...], carry_vd, 0, vmix_mu_ref[...])
            mix = _sigmoid(vmix0_ref[...] + _mm(vd, vmix_up_ref[...]))
            vr = vs + (vfirst_ref[...] - vs) * mix
        else:
            vfirst_ref[...] = vs
            vr = vs
        kx = ks * kk_ref[...]
        kmod = ks * (1.0 + (a - 1.0) * ka_ref[...])
        d["rkr"] = rs * kmod * rk_ref[...]
        d["vr"] = vr
        kk_all = []
        for p in range(N_PAIRS):
            kxp = kx[:, p * PAIR:(p + 1) * PAIR]
            kk_all.append(kxp * jnp.minimum(lax.rsqrt(headsum(kxp * kxp)), 1.0 / L2_EPS))
        kk = jnp.concatenate(kk_all, axis=1)
        na = -kk
        nb = kk * a
        lw_hi = lw.astype(BF16)
        lw_r = lw - lw_hi.astype(F32)
        lw_mid = lw_r.astype(BF16)
        lw_lo = (lw_r - lw_mid.astype(F32)).astype(BF16)
        cum = (jnp.dot(ltri, lw_lo, preferred_element_type=F32) + jnp.dot(ltri, lw_mid, preferred_element_type=F32)
               + jnp.dot(ltri, lw_hi, preferred_element_type=F32))
        a_t = na * jnp.exp2(cum - lw)
        r_t = rs * jnp.exp2(cum)
        e_neg = jnp.exp2(-cum)
        b_t = nb * e_neg
        k_t = kmod * e_neg
        lhs_g, rhs_g, a_s, r_n, b_h, k_h, v_s, gam = {}, {}, {}, {}, {}, {}, {}, {}
        for c in range(n_chunks):
            rc = slice(c * CHUNK, (c + 1) * CHUNK)
            tot = cum[(c + 1) * CHUNK - 1:(c + 1) * CHUNK, :]
            e_rel = jnp.exp2(tot - cum[rc])
            bh_c = nb[rc] * e_rel
            kh_c = kmod[rc] * e_rel
            gam_c = jnp.exp2(tot)
            for p in range(N_PAIRS):
                sl = slice(p * PAIR, (p + 1) * PAIR)
                pc = (p, c)
                a_n = a_t[rc, sl].astype(BF16)
                r_n[pc] = r_t[rc, sl]
                lhs_g[pc] = jnp.concatenate([a_n, r_n[pc].astype(BF16)], axis=0)
                rhs_g[pc] = jnp.concatenate([stack(b_t[rc, sl].astype(BF16)),
                                             stack(k_t[rc, sl].astype(BF16))], axis=0)
                a_s[pc] = stack(a_n)
                b_h[pc] = bh_c[:, sl].astype(BF16)
                k_h[pc] = stack(kh_c[:, sl])
                v_s[pc] = stack(vr[rc, sl].astype(BF16))
                gam[pc] = gam_c[:, sl]
        d.update(lhs_g=lhs_g, rhs_g=rhs_g, a_s=a_s, r_n=r_n, b_h=b_h, k_h=k_h, v_s=v_s, gam=gam)

    def operators():
        a_ab, a_ak, a_rb, a_rk = {}, {}, {}, {}
        for pc in pcs:
            g = _mm_nt(d["lhs_g"][pc], d["rhs_g"][pc])
            a_ab[pc] = jnp.where(strict, g[:CHUNK, :n2], 0.0)
            a_ak[pc] = jnp.where(strict, g[:CHUNK, n2:], 0.0).astype(BF16)
            a_rb[pc] = jnp.where(incl, g[CHUNK:, :n2], 0.0).astype(BF16)
            a_rk[pc] = jnp.where(incl, g[CHUNK:, n2:], 0.0)
        t_inv = {pc: eye + a_ab[pc] for pc in pcs}
        pw = {pc: a_ab[pc].astype(BF16) for pc in pcs}
        pw = {pc: _mm(pw[pc], stack(pw[pc])).astype(BF16) for pc in pcs}
        for _ in range(CHUNK.bit_length() - 3):
            both = {pc: _mm(jnp.concatenate([pw[pc], t_inv[pc].astype(BF16)], axis=0), stack(pw[pc])) for pc in pcs}
            t_inv = {pc: t_inv[pc] + both[pc][CHUNK:] for pc in pcs}
            pw = {pc: both[pc][:CHUNK].astype(BF16) for pc in pcs}
        t_inv = {pc: (t_inv[pc] + _mm(t_inv[pc], stack(pw[pc]))).astype(BF16) for pc in pcs}
        z = {pc: jnp.where(same_head, _mm_tn(t_inv[pc], d["b_h"][pc]), 0.0).astype(BF16) for pc in pcs}
        w = {pc: _mm(a_rb[pc], stack(t_inv[pc])).astype(BF16) for pc in pcs}
        la_ak = {pc: jnp.concatenate([d["a_s"][pc], stack(a_ak[pc])], axis=1) for pc in pcs}
        my = {pc: _mm_tn(la_ak[pc], z[pc]) for pc in pcs}
        wq = {pc: _mm(w[pc], la_ak[pc]) for pc in pcs}
        d["m_t"] = {pc: my[pc][:n2].astype(BF16) for pc in pcs}
        y = {pc: (my[pc][n2:] + d["k_h"][pc]).astype(BF16) for pc in pcs}
        d["q_p"] = {pc: (d["r_n"][pc] + wq[pc][:, :n2]).astype(BF16) for pc in pcs}
        p_p = {pc: (wq[pc][:, n2:] + a_rk[pc]).astype(BF16) for pc in pcs}
        d["g_s"] = {pc: _mm_tn(d["v_s"][pc], y[pc]) for pc in pcs}
        d["o_loc"] = {pc: _mm(p_p[pc], d["v_s"][pc]) for pc in pcs}

    def recurrence():
        states = [state_ref[p] for p in range(N_PAIRS)]
        outs = [[] for _ in range(N_PAIRS)]
        for c in range(n_chunks):
            for p in range(N_PAIRS):
                pc = (p, c)
                sb = states[p].astype(BF16)
                outs[p].append(_mm_nt(d["q_p"][pc], sb) + d["o_loc"][pc])
                states[p] = states[p] * d["gam"][pc] + _mm(sb, d["m_t"][pc]) + d["g_s"][pc]
        for p in range(N_PAIRS):
            state_ref[p] = states[p]
        d["o"] = [jnp.concatenate(o, axis=0) for o in outs]

    def epilogue():
        for p in range(N_PAIRS):
            sl = slice(p * PAIR, (p + 1) * PAIR)
            o_p = d["o"][p]
            bonus = headsum(d["rkr"][:, sl]) * d["vr"][:, sl]
            mu = headsum(o_p) * (1.0 / HEAD_DIM)
            dev = o_p - mu
            var = headsum(dev * dev) * (1.0 / HEAD_DIM)
            on = dev * lax.rsqrt(var + GN_EPS) * lng_ref[:, sl] + lnb_ref[:, sl]
            rz_p = rz_ref[:, sl].astype(F32)
            yb_ref[:, sl] = ((on + bonus) * (rz_p * _sigmoid(rz_p))).astype(yb_ref.dtype)

    prep()
    operators()
    recurrence()
    epilogue()


def _rwkv(proj, vfirst, p, *, ts=256):
    b, s, _ = proj.shape
    has_vmix = vfirst is not None
    proj2d = proj.reshape(b * s, PROJ_W)
    row = lambda width, col: pl.BlockSpec((pl.Element(ts), pl.Element(width)),
                                          lambda i, t: (pl.multiple_of(i * s + t * ts, ts), col))
    const = lambda shape: pl.BlockSpec(shape, lambda i, t: (0,) * len(shape))
    act = pl.BlockSpec((None, ts, WIDTH), lambda i, t: (i, t, 0))

    in_specs = [row(WIDTH, COL_R), row(WIDTH, COL_RK), row(WIDTH, COL_RV), row(WIDTH, COL_RZ),
                row(LANES, COL_LORA)]
    args = [proj2d, proj2d, proj2d, proj2d, proj2d]
    if has_vmix:
        in_specs += [row(LANES, COL_VD), act]
        args += [proj2d, vfirst]
    in_specs += [const((3, WIDTH)), const((1, LANES)), const((1, 2 * WIDTH)), const((LANES, 2 * WIDTH))]
    args += [p["mu_rkv"], p["mu_lora"], p["w0a0"], p["lora_w"]]
    for name in ("k_k", "k_a", "r_k", "ln_g", "ln_b"):
        in_specs.append(const((1, WIDTH)))
        args.append(p[name])
    if has_vmix:
        in_specs += [const((1, LANES)), const((LANES, WIDTH)), const((1, WIDTH))]
        args += [p["vmix_mu"], p["vmix_up"], p["vmix0"]]

    out_shape = [jax.ShapeDtypeStruct((b, s, WIDTH), ACT)]
    out_specs = [act]
    scratch = [pltpu.VMEM((8, WIDTH), F32), pltpu.VMEM((8, LANES), F32)]
    if has_vmix:
        scratch.append(pltpu.VMEM((8, LANES), F32))
    else:
        out_shape.append(jax.ShapeDtypeStruct((b, s, WIDTH), F32))
        out_specs.append(act)
    scratch.append(pltpu.VMEM((N_PAIRS, PAIR, PAIR), F32))

    res = pl.pallas_call(
        functools.partial(_rwkv_kernel, has_vmix=has_vmix, ts=ts),
        grid=(b, s // ts),
        in_specs=in_specs,
        out_specs=out_specs,
        out_shape=out_shape,
        scratch_shapes=scratch,
        compiler_params=pltpu.CompilerParams(
            dimension_semantics=("parallel", "arbitrary"), vmem_limit_bytes=VMEM_LIMIT),
        name="rwkv_vmix" if has_vmix else "rwkv",
    )(*args)
    return (res[0], vfirst) if has_vmix else (res[0], res[1])


def _outproj_kernel(ya_ref, az_ref, yb_ref, gatt_ref, grw_ref, x_ref,
                    wua_ref, wub_ref, wout_ref, g_ref, o_ref):
    az = az_ref[...].astype(F32)
    ya = ya_ref[...].astype(F32) * (az * _sigmoid(az))
    u = (_sigmoid(gatt_ref[...].astype(F32)) * _mm(ya, wua_ref[...])
         + _sigmoid(grw_ref[...].astype(F32)) * _mm(yb_ref[...], wub_ref[...]))
    y = _mm(u, wout_ref[...])
    ms = jnp.mean(y * y, axis=-1, keepdims=True)
    o_ref[...] = x_ref[...] + y * lax.rsqrt(ms + RMS_EPS) * g_ref[...]


def _outproj(ya, proj2d, yb, x2d, w_up_att, w_up_rw, w_out, g, *, tm=1024):
    t, d = x2d.shape
    row = lambda width, col: pl.BlockSpec((pl.Element(tm), pl.Element(width)),
                                          lambda i: (pl.multiple_of(i * tm, tm), col))
    const = lambda shape: pl.BlockSpec(shape, lambda i: (0, 0))
    return pl.pallas_call(
        _outproj_kernel,
        grid=(t // tm,),
        in_specs=[row(WIDTH, 0), row(WIDTH, COL_AZ), row(WIDTH, 0), row(d, COL_GATT), row(d, COL_GRW),
                  row(d, 0), const((WIDTH, d)), const((WIDTH, d)), const((d, d)), const((1, d))],
        out_specs=row(d, 0),
        out_shape=jax.ShapeDtypeStruct((t, d), F32),
        compiler_params=pltpu.CompilerParams(
            dimension_semantics=("parallel",), vmem_limit_bytes=VMEM_LIMIT),
        name="outproj",
    )(ya, proj2d, yb, proj2d, proj2d, x2d, w_up_att, w_up_rw, w_out, g.reshape(1, d))


def _layer_params(l, d, rw_mu, rw_w0, rw_w_up, rw_a0, rw_a_up, rw_k_k, rw_k_a, rw_r_k,
                  rw_ln_g, rw_ln_b, rw_vmix_down, rw_vmix_mu, rw_vmix_up, rw_vmix0):
    pad = LANES - VMIX_LORA
    vd = jnp.pad(rw_vmix_down[l - 1], ((0, 0), (0, pad))) if l > 0 else jnp.zeros((d, LANES), F32)

    mu = rw_mu[l]
    zeros = jnp.zeros((DECAY_LORA, WIDTH), F32)
    lora_w = jnp.concatenate([jnp.concatenate([rw_w_up[l], zeros], axis=1),
                              jnp.concatenate([zeros, rw_a_up[l]], axis=1)], axis=0).astype(BF16)
    p = {
        "w_vd": vd.astype(BF16),
        "mu_rkv": mu[:3 * WIDTH].reshape(3, WIDTH),
        "mu_lora": mu[3 * WIDTH:].reshape(1, LANES),
        "w0a0": jnp.concatenate([rw_w0[l], rw_a0[l]]).reshape(1, 2 * WIDTH),
        "lora_w": lora_w,
        "k_k": rw_k_k[l].reshape(1, WIDTH), "k_a": rw_k_a[l].reshape(1, WIDTH),
        "r_k": rw_r_k[l].reshape(1, WIDTH),
        "ln_g": rw_ln_g[l].reshape(1, WIDTH), "ln_b": rw_ln_b[l].reshape(1, WIDTH),
    }
    if l > 0:
        p["vmix_mu"] = jnp.pad(rw_vmix_mu[l - 1], (0, pad)).reshape(1, LANES)
        p["vmix_up"] = jnp.pad(rw_vmix_up[l - 1], ((0, pad), (0, 0))).astype(BF16)
        p["vmix0"] = rw_vmix0[l - 1].reshape(1, WIDTH)
    return p


def kernel(x, norm_pre, norm_post, w_in, rw_mu, rw_w0, rw_w_up, rw_a0, rw_a_up, rw_k_k, rw_k_a, rw_r_k, rw_ln_g, rw_ln_b, rw_vmix_down, rw_vmix_mu, rw_vmix_up, rw_vmix0, w_up_att, w_up_rw, w_out):
    b, s, d = x.shape
    assert d == D_MODEL and s % MOBA_BLOCK == 0
    depth = w_in.shape[0]
    x2d = x.reshape(b * s, d)
    w_all = w_in.astype(BF16)
    vfirst = None
    for l in range(depth):
        p = _layer_params(l, d, rw_mu, rw_w0, rw_w_up, rw_a0, rw_a_up, rw_k_k, rw_k_a, rw_r_k,
                          rw_ln_g, rw_ln_b, rw_vmix_down, rw_vmix_mu, rw_vmix_up, rw_vmix0)
        proj2d = _inproj(x2d, norm_pre[l], w_all, l, p["w_vd"])
        proj = proj2d.reshape(b, s, PROJ_W)
        ya = _moba(proj)
        yb, vfirst = _rwkv(proj, vfirst, p)
        x2d = _outproj(ya.reshape(b * s, WIDTH), proj2d, yb.reshape(b * s, WIDTH), x2d,
                       w_up_att[l].astype(BF16), w_up_rw[l].astype(BF16), w_out[l].astype(BF16),
                       norm_post[l])
    return x2d.reshape(b, s, d)
```

```python
import functools

import jax
import jax.numpy as jnp
from jax import lax
from jax.experimental import pallas as pl
from jax.experimental.pallas import tpu as pltpu

F32 = jnp.float32
BF16 = jnp.bfloat16
ACT = BF16

D_MODEL = 1024
N_HEADS = 8
HEAD_DIM = 64
WIDTH = N_HEADS * HEAD_DIM
MOBA_BLOCK = 256
MOBA_TOPK = 3
DECAY_LORA = 64
ICLR_LORA = 64
VMIX_LORA = 32
RMS_EPS = 1e-6
GN_EPS = 64e-5
L2_EPS = 1e-12
NEG_INF = -1e30

LANES = 128
PAIR = 2 * HEAD_DIM
N_PAIRS = WIDTH // PAIR
V_ROWS = HEAD_DIM + 16
MOBA_PAIRS_PER_STEP = 4
CHUNK = 64

COL_Q, COL_K, COL_V, COL_AZ = 0, 512, 1024, 1536
COL_R, COL_RK, COL_RV = 2048, 2560, 3072
COL_LORA = 3584
COL_RZ = 3712
COL_GATT, COL_GRW = 4224, 5248
COL_VD = 6272
PROJ_W = 6400

VMEM_LIMIT = 48 * 1024 * 1024
VMEM_LIMIT_INPROJ = 56 * 1024 * 1024
LOG2E = 1.4426950408889634


def _sigmoid(x):
    return 0.5 * jnp.tanh(0.5 * x) + 0.5


def _mm(a, b):
    return jnp.dot(a.astype(BF16), b.astype(BF16), preferred_element_type=F32)


def _mm_nt(a, b):
    return lax.dot_general(a.astype(BF16), b.astype(BF16), (((1,), (1,)), ((), ())),
                           preferred_element_type=F32)


def _mm_tn(a, b):
    return lax.dot_general(a.astype(BF16), b.astype(BF16), (((0,), (0,)), ((), ())),
                           preferred_element_type=F32)


def _inproj_kernel(x_ref, g_ref, w_ref, wvd_ref, o_ref, *, tn):
    x = x_ref[...]
    ms = jnp.mean(x * x, axis=-1, keepdims=True)
    h = (x * lax.rsqrt(ms + RMS_EPS) * g_ref[...]).astype(BF16)
    n_main = w_ref.shape[1] - wvd_ref.shape[1]
    for n0 in range(0, n_main, tn):
        n1 = min(n0 + tn, n_main)
        o_ref[:, n0:n1] = jnp.dot(h, w_ref[:, n0:n1], preferred_element_type=F32).astype(o_ref.dtype)
    w_tail = jnp.concatenate([w_ref[:, n_main:], wvd_ref[...]], axis=1)
    o_ref[:, n_main:] = jnp.dot(h, w_tail, preferred_element_type=F32).astype(o_ref.dtype)


def _inproj(x2d, g, w_all, layer, w_vd, *, tm=1024, tn=1280):
    t, d = x2d.shape
    n_w = w_all.shape[2]
    n = n_w + w_vd.shape[1]
    return pl.pallas_call(
        functools.partial(_inproj_kernel, tn=tn),
        grid=(t // tm,),
        in_specs=[
            pl.BlockSpec((tm, d), lambda i: (i, 0)),
            pl.BlockSpec((1, d), lambda i: (0, 0)),
            pl.BlockSpec((None, d, n_w), lambda i: (layer, 0, 0), pipeline_mode=pl.Buffered(1)),
            pl.BlockSpec((d, w_vd.shape[1]), lambda i: (0, 0), pipeline_mode=pl.Buffered(1)),
        ],
        out_specs=pl.BlockSpec((tm, n), lambda i: (i, 0)),
        out_shape=jax.ShapeDtypeStruct((t, n), ACT),
        compiler_params=pltpu.CompilerParams(
            dimension_semantics=("parallel",), vmem_limit_bytes=VMEM_LIMIT_INPROJ),
        name="inproj",
    )(x2d, g.reshape(1, d), w_all, w_vd)


def _moba_kernel(q_ref, k_ref, v_ref, o_ref, vt_ref, qx_ref, kx_ref, s_ref, p_ref, *, n_blocks, n_pairs):
    hp0 = pl.program_id(1) * n_pairs
    blk = MOBA_BLOCK
    s_len = n_blocks * blk
    q_scale = (HEAD_DIM ** -0.5) * LOG2E
    nt = (((1,), (1,)), ((), ()))
    sub = 32
    tk = MOBA_BLOCK

    lane = lax.broadcasted_iota(jnp.int32, (1, PAIR), 1)
    head_lanes = (lane < HEAD_DIM, lane >= HEAD_DIM)
    nrow = lax.broadcasted_iota(jnp.int32, (n_blocks, s_len), 0)
    pos_i = lax.broadcasted_iota(jnp.int32, (n_blocks, s_len), 1)
    pblk = pos_i // blk
    pos = pos_i.astype(F32)
    past = nrow < pblk
    krow = lax.broadcasted_iota(jnp.int32, (blk, blk), 0)
    qcol = lax.broadcasted_iota(jnp.int32, (blk, blk), 1)
    causal = qcol >= krow

    def split3(x):
        hi = x.astype(BF16).astype(F32)
        mid = (x - hi).astype(BF16).astype(F32)
        return hi, mid, x - hi - mid

    def rows8(r0, r1, r2, r3, r4, r5):
        out = jnp.zeros((n_blocks, s_len), F32)
        for idx, r in enumerate((r0, r1, r2, r3, r4, r5)):
            out = jnp.where(nrow == idx, r, out)
        return out

    ones = jnp.ones((n_blocks, s_len), F32)
    ones_rows = jnp.where(lax.broadcasted_iota(jnp.int32, (V_ROWS - HEAD_DIM, s_len), 0) == 0, 1.0, 0.0).astype(BF16)
    mean_row = lax.broadcasted_iota(jnp.int32, (n_blocks, PAIR), 0)

    for g in range(n_pairs):
        cols = slice(g * PAIR, (g + 1) * PAIR)
        vt = v_ref[:, cols].T
        for h in range(2):
            vt_ref[g, h] = jnp.concatenate([vt[h * HEAD_DIM:(h + 1) * HEAD_DIM], ones_rows], axis=0)

        kmean = jnp.zeros((n_blocks, PAIR), F32)
        for n in range(n_blocks):
            kmean = jnp.where(mean_row == n,
                              jnp.mean(k_ref[n * blk:(n + 1) * blk, cols].astype(F32), axis=0, keepdims=True), kmean)

        km = jnp.concatenate([jnp.where(head_lanes[0], kmean, 0.0), jnp.where(head_lanes[1], kmean, 0.0)], axis=0)
        km_hi = km.astype(BF16)
        km_r = km - km_hi.astype(F32)
        km_mid = km_r.astype(BF16)
        km_lo = (km_r - km_mid.astype(F32)).astype(BF16)
        parts = lax.dot_general(jnp.concatenate([km_lo, km_mid, km_hi], axis=0), q_ref[:, cols], nt,
                                preferred_element_type=F32)
        ng = 2 * n_blocks
        gates = parts[:ng] + parts[ng:2 * ng] + parts[2 * ng:]

        q_extra, k_extra = [], []
        for h in range(2):
            head = lax.convert_element_type((hp0 + g) * 2 + h + 1, F32)
            slope2 = jnp.exp2(jnp.zeros((1, 1), F32) - (8.0 / N_HEADS) * head) * LOG2E

            gm = jnp.where(past, gates[h * n_blocks:(h + 1) * n_blocks], NEG_INF)
            rank = jnp.zeros((n_blocks, s_len), F32)
            for m in range(n_blocks):
                gcol = gm[m:m + 1, :]
                beats = (gcol > gm) | ((gcol == gm) & (nrow > m))
                rank = rank + jnp.where(beats, 1.0, 0.0)
            visible = (past & (rank < MOBA_TOPK)) | (nrow == pblk)

            q_hi, q_mid, q_lo = split3(-slope2 * pos)
            k_hi, k_mid, k_lo = split3(slope2 * pos)
            q_extra.append(jnp.concatenate([rows8(q_hi, q_mid, q_lo, ones, ones, ones),
                                            jnp.where(visible, 0.0, NEG_INF)], axis=0))
            k_extra.append(jnp.concatenate([rows8(ones, ones, ones, k_hi, k_mid, k_lo),
                                            jnp.where(nrow == pblk, 1.0, 0.0)], axis=0))
        gap = jnp.zeros((HEAD_DIM - 2 * n_blocks, s_len), F32)
        qx_ref[g] = jnp.concatenate([q_extra[1], gap, q_extra[0], gap], axis=0).T.astype(BF16)
        k_x = jnp.concatenate([k_extra[1], gap, k_extra[0], gap], axis=0).T.astype(BF16)
        for h in range(2):
            kx_ref[g, h] = jnp.where(head_lanes[h], k_ref[:, cols], k_x)

    units = [(g, i, h) for g in range(n_pairs) for i in range(n_blocks) for h in range(2)]
    st = {}

    def score_tile(u, t):
        g, i, h = u
        d = st[u]
        k0 = t * tk
        s_tile = lax.dot_general(kx_ref[g, h, k0:k0 + tk, :], d["qm"], nt,
                                 preferred_element_type=F32)
        for c in range(0, tk, sub):
            sc = s_tile[c:c + sub]
            if k0 // blk == i:
                r0 = k0 % blk + c
                sc = jnp.where(causal[r0:r0 + sub], sc, NEG_INF)
            s_ref[d["slot"], k0 + c:k0 + c + sub, :] = sc
            for r in range(0, sub, 8):
                d["m8"] = sc[r:r + 8] if d["m8"] is None else jnp.maximum(d["m8"], sc[r:r + 8])

    def softmax_tile(u, t):
        d = st[u]
        for c in range(t * tk, (t + 1) * tk, sub):
            pc = jnp.exp2(s_ref[d["slot"], c:c + sub, :] - d["m"])
            p_ref[d["slot"], c:c + sub, :] = pc.astype(BF16)

    def finish(u):
        g, i, h = u
        d = st[u]
        n_k = (i + 1) * blk
        pv = jnp.dot(vt_ref[g, h, :, :n_k], p_ref[d["slot"], :n_k, :],
                     preferred_element_type=F32)
        d["out"] = pv[:HEAD_DIM] * (1.0 / pv[HEAD_DIM:HEAD_DIM + 1])
        if h == 1:
            out_t = jnp.concatenate([st[(g, i, 0)]["out"], d["out"]], axis=0)
            o_ref[i * blk:(i + 1) * blk, g * PAIR:(g + 1) * PAIR] = out_t.T.astype(o_ref.dtype)

    prev = None
    for idx, u in enumerate(units):
        g, i, h = u
        rows = slice(i * blk, (i + 1) * blk)
        if h == 0:
            q_i = (q_ref[rows, g * PAIR:(g + 1) * PAIR].astype(F32) * q_scale).astype(BF16)
        st[u] = {"qm": jnp.where(head_lanes[h], q_i, qx_ref[g, rows, :]), "slot": idx % 2,
                 "m8": None}
        n_cur = (i + 1) * blk // tk
        n_prev = (prev[1] + 1) * blk // tk if prev is not None else 0
        for step in range(max(n_cur, n_prev)):
            if step < n_cur:
                score_tile(u, step)
            if step < n_prev:
                softmax_tile(prev, step)
        if prev is not None:
            finish(prev)
        st[u]["m"] = jnp.max(st[u]["m8"], axis=0, keepdims=True)
        prev = u
    for step in range((prev[1] + 1) * blk // tk):
        softmax_tile(prev, step)
    finish(prev)


def _moba(proj):
    b, s, _ = proj.shape
    n_blocks = s // MOBA_BLOCK
    assert n_blocks == 8, "the extra-column layout of the MoBA kernel is laid out for 8 key blocks"
    n_pairs = MOBA_PAIRS_PER_STEP
    width = n_pairs * PAIR
    qo, ko, vo = COL_Q // width, COL_K // width, COL_V // width
    return pl.pallas_call(
        functools.partial(_moba_kernel, n_blocks=n_blocks, n_pairs=n_pairs),
        grid=(b, N_PAIRS // n_pairs),
        in_specs=[
            pl.BlockSpec((None, s, width), lambda i, p: (i, 0, qo + p)),
            pl.BlockSpec((None, s, width), lambda i, p: (i, 0, ko + p)),
            pl.BlockSpec((None, s, width), lambda i, p: (i, 0, vo + p)),
        ],
        out_specs=pl.BlockSpec((None, s, width), lambda i, p: (i, 0, p)),
        out_shape=jax.ShapeDtypeStruct((b, s, WIDTH), ACT),
        scratch_shapes=[
            pltpu.VMEM((n_pairs, 2, V_ROWS, s), BF16),
            pltpu.VMEM((n_pairs, s, PAIR), BF16),
            pltpu.VMEM((n_pairs, 2, s, PAIR), BF16),
            pltpu.VMEM((2, s, MOBA_BLOCK), F32),
            pltpu.VMEM((2, s, MOBA_BLOCK), BF16),
        ],
        compiler_params=pltpu.CompilerParams(
            dimension_semantics=("parallel", "parallel"), vmem_limit_bytes=VMEM_LIMIT),
        name="moba",
    )(proj, proj, proj)


def _rwkv_kernel(*refs, has_vmix, ts):
    if has_vmix:
        (r_ref, k_ref, v_ref, rz_ref, lora_ref, vd_ref, vfirst_ref,
         mu_rkv_ref, mu_lora_ref, w0a0_ref, lora_w_ref, kk_ref, ka_ref, rk_ref, lng_ref, lnb_ref,
         vmix_mu_ref, vmix_up_ref, vmix0_ref,
         yb_ref, carry_rkv, carry_lora, carry_vd, state_ref) = refs
    else:
        (r_ref, k_ref, v_ref, rz_ref, lora_ref,
         mu_rkv_ref, mu_lora_ref, w0a0_ref, lora_w_ref, kk_ref, ka_ref, rk_ref, lng_ref, lnb_ref,
         yb_ref, vfirst_ref, carry_rkv, carry_lora, state_ref) = refs
        carry_vd = None

    @pl.when(pl.program_id(1) == 0)
    def _():
        carry_rkv[...] = jnp.zeros_like(carry_rkv)
        carry_lora[...] = jnp.zeros_like(carry_lora)
        if carry_vd is not None:
            carry_vd[...] = jnp.zeros_like(carry_vd)
        state_ref[...] = jnp.zeros_like(state_ref)

    sub = ts
    n_chunks = sub // CHUNK
    n2 = 2 * CHUNK
    pcs = [(p, c) for c in range(n_chunks) for p in range(N_PAIRS)]

    sub_row = lax.broadcasted_iota(jnp.int32, (sub, sub), 0)
    sub_col = lax.broadcasted_iota(jnp.int32, (sub, sub), 1)
    diff_mat = (jnp.where(sub_row == sub_col + 1, 1.0, 0.0)
                - jnp.where(sub_row == sub_col, 1.0, 0.0)).astype(r_ref.dtype)
    ltri = jnp.where((sub_row // CHUNK == sub_col // CHUNK) & (sub_col <= sub_row), 1.0, 0.0).astype(BF16)
    top_row = lax.broadcasted_iota(jnp.int32, (8, 1), 0) == 0
    lane = lax.broadcasted_iota(jnp.int32, (sub, LANES), 1)
    head0_t = lane < HEAD_DIM
    head0 = lax.broadcasted_iota(jnp.int32, (CHUNK, PAIR), 1) < HEAD_DIM
    wrow = lax.broadcasted_iota(jnp.int32, (CHUNK, PAIR), 0)
    wcol = lax.broadcasted_iota(jnp.int32, (CHUNK, PAIR), 1) % CHUNK
    strict = wcol < wrow
    incl = wcol <= wrow
    eye = jnp.where(wcol == wrow, 1.0, 0.0).astype(F32)
    srow = lax.broadcasted_iota(jnp.int32, (n2, n2), 0)
    scol = lax.broadcasted_iota(jnp.int32, (n2, n2), 1)
    same_head = (srow // CHUNK) == (scol // CHUNK)
    mu_rkv = mu_rkv_ref[...]

    def shift(y_act, carry_ref, row_idx, mu):
        y = y_act.astype(F32)
        delta = jnp.dot(diff_mat, y_act, preferred_element_type=F32)
        head = jnp.where(top_row, delta[:8] + carry_ref[row_idx:row_idx + 1, :], delta[:8])
        delta = jnp.concatenate([head, delta[8:]], axis=0)
        carry_ref[row_idx:row_idx + 1, :] = y[sub - 1:sub, :]
        return y + delta * mu

    def headsum(x):
        s0 = jnp.sum(jnp.where(head0_t, x, 0.0), axis=-1, keepdims=True)
        s1 = jnp.sum(jnp.where(head0_t, 0.0, x), axis=-1, keepdims=True)
        return jnp.where(head0_t, s0, s1)

    def stack(x):
        zero = jnp.zeros_like(x)
        return jnp.concatenate([jnp.where(head0, x, zero), jnp.where(head0, zero, x)], axis=0)

    d = {}

    def prep():
        rs = shift(r_ref[...], carry_rkv, 0, mu_rkv[0:1, :])
        ks = shift(k_ref[...], carry_rkv, 1, mu_rkv[1:2, :])
        vs = shift(v_ref[...], carry_rkv, 2, mu_rkv[2:3, :])
        lo = shift(lora_ref[...], carry_lora, 0, mu_lora_ref[...])
        z = jnp.where(lane < DECAY_LORA, jnp.tanh(lo), lo)
        wa = w0a0_ref[...] + _mm(z, lora_w_ref[...])
        lw = (-0.6065306597126334 * LOG2E) * _sigmoid(wa[:, :WIDTH])
        a = _sigmoid(wa[:, WIDTH:])
        if has_vmix:
            vd = shift(vd_ref[...], carry_vd, 0, vmix_mu_ref[...])
            mix = _sigmoid(vmix0_ref[...] + _mm(vd, vmix_up_ref[...]))
            vr = vs + (vfirst_ref[...] - vs) * mix
        else:
            vfirst_ref[...] = vs
            vr = vs
        kx = ks * kk_ref[...]
        kmod = ks * (1.0 + (a - 1.0) * ka_ref[...])
        d["rkr"] = rs * kmod * rk_ref[...]
        d["vr"] = vr
        kk_all = []
        for p in range(N_PAIRS):
            kxp = kx[:, p * PAIR:(p + 1) * PAIR]
            kk_all.append(kxp * jnp.minimum(lax.rsqrt(headsum(kxp * kxp)), 1.0 / L2_EPS))
        kk = jnp.concatenate(kk_all, axis=1)
        na = -kk
        nb = kk * a
        lw_hi = lw.astype(BF16)
        lw_r = lw - lw_hi.astype(F32)
        lw_mid = lw_r.astype(BF16)
        lw_lo = (lw_r - lw_mid.astype(F32)).astype(BF16)
        cum = (jnp.dot(ltri, lw_lo, preferred_element_type=F32) + jnp.dot(ltri, lw_mid, preferred_element_type=F32)
               + jnp.dot(ltri, lw_hi, preferred_element_type=F32))
        a_t = na * jnp.exp2(cum - lw)
        r_t = rs * jnp.exp2(cum)
        e_neg = jnp.exp2(-cum)
        b_t = nb * e_neg
        k_t = kmod * e_neg
        lhs_g, rhs_g, a_s, r_n, b_h, k_h, v_s, gam = {}, {}, {}, {}, {}, {}, {}, {}
        for c in range(n_chunks):
            rc = slice(c * CHUNK, (c + 1) * CHUNK)
            tot = cum[(c + 1) * CHUNK - 1:(c + 1) * CHUNK, :]
            e_rel = jnp.exp2(tot - cum[rc])
            bh_c = nb[rc] * e_rel
            kh_c = kmod[rc] * e_rel
            gam_c = jnp.exp2(tot)
            for p in range(N_PAIRS):
                sl = slice(p * PAIR, (p + 1) * PAIR)
                pc = (p, c)
                a_n = a_t[rc, sl].astype(BF16)
                r_n[pc] = r_t[rc, sl]
                lhs_g[pc] = jnp.concatenate([a_n, r_n[pc].astype(BF16)], axis=0)
                rhs_g[pc] = jnp.concatenate([stack(b_t[rc, sl].astype(BF16)),
                                             stack(k_t[rc, sl].astype(BF16))], axis=0)
                a_s[pc] = stack(a_n)
                b_h[pc] = bh_c[:, sl].astype(BF16)
                k_h[pc] = stack(kh_c[:, sl])
                v_s[pc] = stack(vr[rc, sl].astype(BF16))
                gam[pc] = gam_c[:, sl]
        d.update(lhs_g=lhs_g, rhs_g=rhs_g, a_s=a_s, r_n=r_n, b_h=b_h, k_h=k_h, v_s=v_s, gam=gam)

    def operators():
        a_ab, a_ak, a_rb, a_rk = {}, {}, {}, {}
        for pc in pcs:
            g = _mm_nt(d["lhs_g"][pc], d["rhs_g"][pc])
            a_ab[pc] = jnp.where(strict, g[:CHUNK, :n2], 0.0)
            a_ak[pc] = jnp.where(strict, g[:CHUNK, n2:], 0.0).astype(BF16)
            a_rb[pc] = jnp.where(incl, g[CHUNK:, :n2], 0.0).astype(BF16)
            a_rk[pc] = jnp.where(incl, g[CHUNK:, n2:], 0.0)
        t_inv = {pc: eye + a_ab[pc] for pc in pcs}
        pw = {pc: a_ab[pc].astype(BF16) for pc in pcs}
        pw = {pc: _mm(pw[pc], stack(pw[pc])).astype(BF16) for pc in pcs}
        for _ in range(CHUNK.bit_length() - 3):
            both = {pc: _mm(jnp.concatenate([pw[pc], t_inv[pc].astype(BF16)], axis=0), stack(pw[pc])) for pc in pcs}
            t_inv = {pc: t_inv[pc] + both[pc][CHUNK:] for pc in pcs}
            pw = {pc: both[pc][:CHUNK].astype(BF16) for pc in pcs}
        t_inv = {pc: (t_inv[pc] + _mm(t_inv[pc], stack(pw[pc]))).astype(BF16) for pc in pcs}
        z = {pc: jnp.where(same_head, _mm_tn(t_inv[pc], d["b_h"][pc]), 0.0).astype(BF16) for pc in pcs}
        w = {pc: _mm(a_rb[pc], stack(t_inv[pc])).astype(BF16) for pc in pcs}
        la_ak = {pc: jnp.concatenate([d["a_s"][pc], stack(a_ak[pc])], axis=1) for pc in pcs}
        my = {pc: _mm_tn(la_ak[pc], z[pc]) for pc in pcs}
        wq = {pc: _mm(w[pc], la_ak[pc]) for pc in pcs}
        d["m_t"] = {pc: my[pc][:n2].astype(BF16) for pc in pcs}
        y = {pc: (my[pc][n2:] + d["k_h"][pc]).astype(BF16) for pc in pcs}
        d["q_p"] = {pc: (d["r_n"][pc] + wq[pc][:, :n2]).astype(BF16) for pc in pcs}
        p_p = {pc: (wq[pc][:, n2:] + a_rk[pc]).astype(BF16) for pc in pcs}
        d["g_s"] = {pc: _mm_tn(d["v_s"][pc], y[pc]) for pc in pcs}
        d["o_loc"] = {pc: _mm(p_p[pc], d["v_s"][pc]) for pc in pcs}

    def recurrence():
        states = [state_ref[p] for p in range(N_PAIRS)]
        outs = [[] for _ in range(N_PAIRS)]
        for c in range(n_chunks):
            for p in range(N_PAIRS):
                pc = (p, c)
                sb = states[p].astype(BF16)
                outs[p].append(_mm_nt(d["q_p"][pc], sb) + d["o_loc"][pc])
                states[p] = states[p] * d["gam"][pc] + _mm(sb, d["m_t"][pc]) + d["g_s"][pc]
        for p in range(N_PAIRS):
            state_ref[p] = states[p]
        d["o"] = [jnp.concatenate(o, axis=0) for o in outs]

    def epilogue():
        for p in range(N_PAIRS):
            sl = slice(p * PAIR, (p + 1) * PAIR)
            o_p = d["o"][p]
            bonus = headsum(d["rkr"][:, sl]) * d["vr"][:, sl]
            mu = headsum(o_p) * (1.0 / HEAD_DIM)
            dev = o_p - mu
            var = headsum(dev * dev) * (1.0 / HEAD_DIM)
            on = dev * lax.rsqrt(var + GN_EPS) * lng_ref[:, sl] + lnb_ref[:, sl]
            rz_p = rz_ref[:, sl].astype(F32)
            yb_ref[:, sl] = ((on + bonus) * (rz_p * _sigmoid(rz_p))).astype(yb_ref.dtype)

    prep()
    operators()
    recurrence()
    epilogue()


def _rwkv(proj, vfirst, p, *, ts=256):
    b, s, _ = proj.shape
    has_vmix = vfirst is not None
    proj2d = proj.reshape(b * s, PROJ_W)
    row = lambda width, col: pl.BlockSpec((pl.Element(ts), pl.Element(width)),
                                          lambda i, t: (pl.multiple_of(i * s + t * ts, ts), col))
    const = lambda shape: pl.BlockSpec(shape, lambda i, t: (0,) * len(shape))
    act = pl.BlockSpec((None, ts, WIDTH), lambda i, t: (i, t, 0))

    in_specs = [row(WIDTH, COL_R), row(WIDTH, COL_RK), row(WIDTH, COL_RV), row(WIDTH, COL_RZ),
                row(LANES, COL_LORA)]
    args = [proj2d, proj2d, proj2d, proj2d, proj2d]
    if has_vmix:
        in_specs += [row(LANES, COL_VD), act]
        args += [proj2d, vfirst]
    in_specs += [const((3, WIDTH)), const((1, LANES)), const((1, 2 * WIDTH)), const((LANES, 2 * WIDTH))]
    args += [p["mu_rkv"], p["mu_lora"], p["w0a0"], p["lora_w"]]
    for name in ("k_k", "k_a", "r_k", "ln_g", "ln_b"):
        in_specs.append(const((1, WIDTH)))
        args.append(p[name])
    if has_vmix:
        in_specs += [const((1, LANES)), const((LANES, WIDTH)), const((1, WIDTH))]
        args += [p["vmix_mu"], p["vmix_up"], p["vmix0"]]

    out_shape = [jax.ShapeDtypeStruct((b, s, WIDTH), ACT)]
    out_specs = [act]
    scratch = [pltpu.VMEM((8, WIDTH), F32), pltpu.VMEM((8, LANES), F32)]
    if has_vmix:
        scratch.append(pltpu.VMEM((8, LANES), F32))
    else:
        out_shape.append(jax.ShapeDtypeStruct((b, s, WIDTH), F32))
        out_specs.append(act)
    scratch.append(pltpu.VMEM((N_PAIRS, PAIR, PAIR), F32))

    res = pl.pallas_call(
        functools.partial(_rwkv_kernel, has_vmix=has_vmix, ts=ts),
        grid=(b, s // ts),
        in_specs=in_specs,
        out_specs=out_specs,
        out_shape=out_shape,
        scratch_shapes=scratch,
        compiler_params=pltpu.CompilerParams(
            dimension_semantics=("parallel", "arbitrary"), vmem_limit_bytes=VMEM_LIMIT),
        name="rwkv_vmix" if has_vmix else "rwkv",
    )(*args)
    return (res[0], vfirst) if has_vmix else (res[0], res[1])


def _outproj_kernel(ya_ref, az_ref, yb_ref, gatt_ref, grw_ref, x_ref,
                    wua_ref, wub_ref, wout_ref, g_ref, o_ref):
    az = az_ref[...].astype(F32)
    ya = ya_ref[...].astype(F32) * (az * _sigmoid(az))
    u = (_sigmoid(gatt_ref[...].astype(F32)) * _mm(ya, wua_ref[...])
         + _sigmoid(grw_ref[...].astype(F32)) * _mm(yb_ref[...], wub_ref[...]))
    y = _mm(u, wout_ref[...])
    ms = jnp.mean(y * y, axis=-1, keepdims=True)
    o_ref[...] = x_ref[...] + y * lax.rsqrt(ms + RMS_EPS) * g_ref[...]


def _outproj(ya, proj2d, yb, x2d, w_up_att, w_up_rw, w_out, g, *, tm=1024):
    t, d = x2d.shape
    row = lambda width, col: pl.BlockSpec((pl.Element(tm), pl.Element(width)),
                                          lambda i: (pl.multiple_of(i * tm, tm), col))
    const = lambda shape: pl.BlockSpec(shape, lambda i: (0, 0))
    return pl.pallas_call(
        _outproj_kernel,
        grid=(t // tm,),
        in_specs=[row(WIDTH, 0), row(WIDTH, COL_AZ), row(WIDTH, 0), row(d, COL_GATT), row(d, COL_GRW),
                  row(d, 0), const((WIDTH, d)), const((WIDTH, d)), const((d, d)), const((1, d))],
        out_specs=row(d, 0),
        out_shape=jax.ShapeDtypeStruct((t, d), F32),
        compiler_params=pltpu.CompilerParams(
            dimension_semantics=("parallel",), vmem_limit_bytes=VMEM_LIMIT),
        name="outproj",
    )(ya, proj2d, yb, proj2d, proj2d, x2d, w_up_att, w_up_rw, w_out, g.reshape(1, d))


def _layer_params(l, d, rw_mu, rw_w0, rw_w_up, rw_a0, rw_a_up, rw_k_k, rw_k_a, rw_r_k,
                  rw_ln_g, rw_ln_b, rw_vmix_down, rw_vmix_mu, rw_vmix_up, rw_vmix0):
    pad = LANES - VMIX_LORA
    vd = jnp.pad(rw_vmix_down[l - 1], ((0, 0), (0, pad))) if l > 0 else jnp.zeros((d, LANES), F32)

    mu = rw_mu[l]
    lora_w = jnp.concatenate([jnp.concatenate([rw_w_up[l], jnp.zeros((DECAY_LORA, WIDTH), F32)], axis=1),
                              jnp.concatenate([jnp.zeros((ICLR_LORA, WIDTH), F32), rw_a_up[l]], axis=1)],
                             axis=0).astype(BF16)
    p = {
        "w_vd": vd.astype(BF16),
        "mu_rkv": mu[:3 * WIDTH].reshape(3, WIDTH),
        "mu_lora": mu[3 * WIDTH:].reshape(1, LANES),
        "w0a0": jnp.concatenate([rw_w0[l], rw_a0[l]]).reshape(1, 2 * WIDTH),
        "lora_w": lora_w,
        "k_k": rw_k_k[l].reshape(1, WIDTH), "k_a": rw_k_a[l].reshape(1, WIDTH),
        "r_k": rw_r_k[l].reshape(1, WIDTH),
        "ln_g": rw_ln_g[l].reshape(1, WIDTH), "ln_b": rw_ln_b[l].reshape(1, WIDTH),
    }
    if l > 0:
        p["vmix_mu"] = jnp.pad(rw_vmix_mu[l - 1], (0, pad)).reshape(1, LANES)
        p["vmix_up"] = jnp.pad(rw_vmix_up[l - 1], ((0, pad), (0, 0))).astype(BF16)
        p["vmix0"] = rw_vmix0[l - 1].reshape(1, WIDTH)
    return p


def kernel(x, norm_pre, norm_post, w_in, rw_mu, rw_w0, rw_w_up, rw_a0, rw_a_up, rw_k_k, rw_k_a, rw_r_k, rw_ln_g, rw_ln_b, rw_vmix_down, rw_vmix_mu, rw_vmix_up, rw_vmix0, w_up_att, w_up_rw, w_out):
    b, s, d = x.shape
    assert d == D_MODEL and s % MOBA_BLOCK == 0
    depth = w_in.shape[0]
    x2d = x.reshape(b * s, d)
    w_all = w_in.astype(BF16)
    vfirst = None
    for l in range(depth):
        p = _layer_params(l, d, rw_mu, rw_w0, rw_w_up, rw_a0, rw_a_up, rw_k_k, rw_k_a, rw_r_k,
                          rw_ln_g, rw_ln_b, rw_vmix_down, rw_vmix_mu, rw_vmix_up, rw_vmix0)
        proj2d = _inproj(x2d, norm_pre[l], w_all, l, p["w_vd"])
        proj = proj2d.reshape(b, s, PROJ_W)
        ya = _moba(proj)
        yb, vfirst = _rwkv(proj, vfirst, p)
        x2d = _outproj(ya.reshape(b * s, WIDTH), proj2d, yb.reshape(b * s, WIDTH), x2d,
                       w_up_att[l].astype(BF16), w_up_rw[l].astype(BF16), w_out[l].astype(BF16),
                       norm_post[l])
    return x2d.reshape(b, s, d)
```

```python
import functools

import jax
import jax.numpy as jnp
from jax import lax
from jax.experimental import pallas as pl
from jax.experimental.pallas import tpu as pltpu

F32 = jnp.float32
BF16 = jnp.bfloat16
ACT = BF16

D_MODEL = 1024
N_HEADS = 8
HEAD_DIM = 64
WIDTH = N_HEADS * HEAD_DIM
MOBA_BLOCK = 256
MOBA_TOPK = 3
DECAY_LORA = 64
ICLR_LORA = 64
VMIX_LORA = 32
RMS_EPS = 1e-6
GN_EPS = 64e-5
L2_EPS = 1e-12
NEG_INF = -1e30

LANES = 128
PAIR = 2 * HEAD_DIM
N_PAIRS = WIDTH // PAIR
V_ROWS = HEAD_DIM + 16
MOBA_PAIRS_PER_STEP = 4
CHUNK = 64

COL_Q, COL_K, COL_V, COL_AZ = 0, 512, 1024, 1536
COL_R, COL_RK, COL_RV = 2048, 2560, 3072
COL_LORA = 3584
COL_RZ = 3712
COL_GATT, COL_GRW = 4224, 5248
COL_VD = 6272
PROJ_W = 6400

VMEM_LIMIT = 48 * 1024 * 1024
VMEM_LIMIT_INPROJ = 56 * 1024 * 1024
LOG2E = 1.4426950408889634


def _sigmoid(x):
    return 0.5 * jnp.tanh(0.5 * x) + 0.5


def _mm(a, b):
    return jnp.dot(a.astype(BF16), b.astype(BF16), preferred_element_type=F32)


def _mm_nt(a, b):
    return lax.dot_general(a.astype(BF16), b.astype(BF16), (((1,), (1,)), ((), ())),
                           preferred_element_type=F32)


def _mm_tn(a, b):
    return lax.dot_general(a.astype(BF16), b.astype(BF16), (((0,), (0,)), ((), ())),
                           preferred_element_type=F32)


def _inproj_kernel(x_ref, g_ref, w_ref, wvd_ref, o_ref, *, tn):
    x = x_ref[...]
    ms = jnp.mean(x * x, axis=-1, keepdims=True)
    h = (x * lax.rsqrt(ms + RMS_EPS) * g_ref[...]).astype(BF16)
    n_main = w_ref.shape[1] - wvd_ref.shape[1]
    for n0 in range(0, n_main, tn):
        n1 = min(n0 + tn, n_main)
        o_ref[:, n0:n1] = jnp.dot(h, w_ref[:, n0:n1], preferred_element_type=F32).astype(o_ref.dtype)
    w_tail = jnp.concatenate([w_ref[:, n_main:], wvd_ref[...]], axis=1)
    o_ref[:, n_main:] = jnp.dot(h, w_tail, preferred_element_type=F32).astype(o_ref.dtype)


def _inproj(x2d, g, w_all, layer, w_vd, *, tm=1024, tn=1280):
    t, d = x2d.shape
    n_w = w_all.shape[2]
    n = n_w + w_vd.shape[1]
    return pl.pallas_call(
        functools.partial(_inproj_kernel, tn=tn),
        grid=(t // tm,),
        in_specs=[
            pl.BlockSpec((tm, d), lambda i: (i, 0)),
            pl.BlockSpec((1, d), lambda i: (0, 0)),
            pl.BlockSpec((None, d, n_w), lambda i: (layer, 0, 0), pipeline_mode=pl.Buffered(1)),
            pl.BlockSpec((d, w_vd.shape[1]), lambda i: (0, 0), pipeline_mode=pl.Buffered(1)),
        ],
        out_specs=pl.BlockSpec((tm, n), lambda i: (i, 0)),
        out_shape=jax.ShapeDtypeStruct((t, n), ACT),
        compiler_params=pltpu.CompilerParams(
            dimension_semantics=("parallel",), vmem_limit_bytes=VMEM_LIMIT_INPROJ),
        name="inproj",
    )(x2d, g.reshape(1, d), w_all, w_vd)


def _moba_kernel(q_ref, k_ref, v_ref, o_ref, vt_ref, qx_ref, kx_ref, s_ref, p_ref, *, n_blocks, n_pairs):
    hp0 = pl.program_id(1) * n_pairs
    blk = MOBA_BLOCK
    s_len = n_blocks * blk
    q_scale = (HEAD_DIM ** -0.5) * LOG2E
    nt = (((1,), (1,)), ((), ()))
    sub = 32
    tk = MOBA_BLOCK

    lane = lax.broadcasted_iota(jnp.int32, (1, PAIR), 1)
    head_lanes = (lane < HEAD_DIM, lane >= HEAD_DIM)
    nrow = lax.broadcasted_iota(jnp.int32, (n_blocks, s_len), 0)
    pos_i = lax.broadcasted_iota(jnp.int32, (n_blocks, s_len), 1)
    pblk = pos_i // blk
    pos = pos_i.astype(F32)
    past = nrow < pblk
    krow = lax.broadcasted_iota(jnp.int32, (blk, blk), 0)
    qcol = lax.broadcasted_iota(jnp.int32, (blk, blk), 1)
    causal = qcol >= krow

    def split3(x):
        hi = x.astype(BF16).astype(F32)
        mid = (x - hi).astype(BF16).astype(F32)
        return hi, mid, x - hi - mid

    def rows8(r0, r1, r2, r3, r4, r5):
        out = jnp.zeros((n_blocks, s_len), F32)
        for idx, r in enumerate((r0, r1, r2, r3, r4, r5)):
            out = jnp.where(nrow == idx, r, out)
        return out

    ones = jnp.ones((n_blocks, s_len), F32)
    ones_rows = jnp.where(lax.broadcasted_iota(jnp.int32, (V_ROWS - HEAD_DIM, s_len), 0) == 0, 1.0, 0.0).astype(BF16)
    mean_row = lax.broadcasted_iota(jnp.int32, (n_blocks, PAIR), 0)

    for g in range(n_pairs):
        cols = slice(g * PAIR, (g + 1) * PAIR)
        vt = v_ref[:, cols].T
        for h in range(2):
            vt_ref[g, h] = jnp.concatenate([vt[h * HEAD_DIM:(h + 1) * HEAD_DIM], ones_rows], axis=0)

        kmean = jnp.zeros((n_blocks, PAIR), F32)
        for n in range(n_blocks):
            kmean = jnp.where(mean_row == n,
                              jnp.mean(k_ref[n * blk:(n + 1) * blk, cols].astype(F32), axis=0, keepdims=True), kmean)

        km = jnp.concatenate([jnp.where(head_lanes[0], kmean, 0.0), jnp.where(head_lanes[1], kmean, 0.0)], axis=0)
        km_hi = km.astype(BF16)
        km_r = km - km_hi.astype(F32)
        km_mid = km_r.astype(BF16)
        km_lo = (km_r - km_mid.astype(F32)).astype(BF16)
        parts = lax.dot_general(jnp.concatenate([km_lo, km_mid, km_hi], axis=0), q_ref[:, cols], nt,
                                preferred_element_type=F32)
        ng = 2 * n_blocks
        gates = parts[:ng] + parts[ng:2 * ng] + parts[2 * ng:]

        q_extra, k_extra = [], []
        for h in range(2):
            head = lax.convert_element_type((hp0 + g) * 2 + h + 1, F32)
            slope2 = jnp.exp2(jnp.zeros((1, 1), F32) - (8.0 / N_HEADS) * head) * LOG2E

            gm = jnp.where(past, gates[h * n_blocks:(h + 1) * n_blocks], NEG_INF)
            rank = jnp.zeros((n_blocks, s_len), F32)
            for m in range(n_blocks):
                gcol = gm[m:m + 1, :]
                beats = (gcol > gm) | ((gcol == gm) & (nrow > m))
                rank = rank + jnp.where(beats, 1.0, 0.0)
            visible = (past & (rank < MOBA_TOPK)) | (nrow == pblk)

            q_hi, q_mid, q_lo = split3(-slope2 * pos)
            k_hi, k_mid, k_lo = split3(slope2 * pos)
            q_extra.append(jnp.concatenate([rows8(q_hi, q_mid, q_lo, ones, ones, ones),
                                            jnp.where(visible, 0.0, NEG_INF)], axis=0))
            k_extra.append(jnp.concatenate([rows8(ones, ones, ones, k_hi, k_mid, k_lo),
                                            jnp.where(nrow == pblk, 1.0, 0.0)], axis=0))
        gap = jnp.zeros((HEAD_DIM - 2 * n_blocks, s_len), F32)
        qx_ref[g] = jnp.concatenate([q_extra[1], gap, q_extra[0], gap], axis=0).T.astype(BF16)
        k_x = jnp.concatenate([k_extra[1], gap, k_extra[0], gap], axis=0).T.astype(BF16)
        for h in range(2):
            kx_ref[g, h] = jnp.where(head_lanes[h], k_ref[:, cols], k_x)

    units = [(g, i) for g in range(n_pairs) for i in range(n_blocks)]
    st = {}

    def score_tile(u, t):
        g, i = u
        k0 = t * tk
        for h in range(2):
            d = st[u][h]
            s_tile = lax.dot_general(kx_ref[g, h, k0:k0 + tk, :], d["qm"], nt,
                                     preferred_element_type=F32)
            for c in range(0, tk, sub):
                sc = s_tile[c:c + sub]
                if k0 // blk == i:
                    r0 = k0 % blk + c
                    sc = jnp.where(causal[r0:r0 + sub], sc, NEG_INF)
                s_ref[d["slot"], k0 + c:k0 + c + sub, :] = sc
                for r in range(0, sub, 8):
                    d["m8"] = sc[r:r + 8] if d["m8"] is None else jnp.maximum(d["m8"], sc[r:r + 8])

    def softmax_tile(u, t):
        for h in range(2):
            d = st[u][h]
            for c in range(t * tk, (t + 1) * tk, sub):
                pc = jnp.exp2(s_ref[d["slot"], c:c + sub, :] - d["m"])
                p_ref[d["slot"], c:c + sub, :] = pc.astype(BF16)

    def finish(u):
        g, i = u
        n_k = (i + 1) * blk
        outs = []
        for h in range(2):
            d = st[u][h]
            pv = jnp.dot(vt_ref[g, h, :, :n_k], p_ref[d["slot"], :n_k, :],
                         preferred_element_type=F32)
            outs.append(pv[:HEAD_DIM] * (1.0 / pv[HEAD_DIM:HEAD_DIM + 1]))
        out_t = jnp.concatenate(outs, axis=0)
        o_ref[i * blk:(i + 1) * blk, g * PAIR:(g + 1) * PAIR] = out_t.T.astype(o_ref.dtype)

    prev = None
    for idx, u in enumerate(units):
        g, i = u
        rows = slice(i * blk, (i + 1) * blk)
        q_i = (q_ref[rows, g * PAIR:(g + 1) * PAIR].astype(F32) * q_scale).astype(BF16)
        st[u] = [{"qm": jnp.where(head_lanes[h], q_i, qx_ref[g, rows, :]), "slot": 2 * (idx % 2) + h,
                  "m8": None} for h in range(2)]
        n_cur = (i + 1) * blk // tk
        n_prev = (prev[1] + 1) * blk // tk if prev is not None else 0
        for step in range(max(n_cur, n_prev)):
            if step < n_cur:
                score_tile(u, step)
            if step < n_prev:
                softmax_tile(prev, step)
        if prev is not None:
            finish(prev)
        for h in range(2):
            st[u][h]["m"] = jnp.max(st[u][h]["m8"], axis=0, keepdims=True)
        prev = u
    for step in range((prev[1] + 1) * blk // tk):
        softmax_tile(prev, step)
    finish(prev)


def _moba(proj):
    b, s, _ = proj.shape
    n_blocks = s // MOBA_BLOCK
    assert n_blocks == 8, "the extra-column layout of the MoBA kernel is laid out for 8 key blocks"
    n_pairs = MOBA_PAIRS_PER_STEP
    width = n_pairs * PAIR
    qo, ko, vo = COL_Q // width, COL_K // width, COL_V // width
    return pl.pallas_call(
        functools.partial(_moba_kernel, n_blocks=n_blocks, n_pairs=n_pairs),
        grid=(b, N_PAIRS // n_pairs),
        in_specs=[
            pl.BlockSpec((None, s, width), lambda i, p: (i, 0, qo + p)),
            pl.BlockSpec((None, s, width), lambda i, p: (i, 0, ko + p)),
            pl.BlockSpec((None, s, width), lambda i, p: (i, 0, vo + p)),
        ],
        out_specs=pl.BlockSpec((None, s, width), lambda i, p: (i, 0, p)),
        out_shape=jax.ShapeDtypeStruct((b, s, WIDTH), ACT),
        scratch_shapes=[
            pltpu.VMEM((n_pairs, 2, V_ROWS, s), BF16),
            pltpu.VMEM((n_pairs, s, PAIR), BF16),
            pltpu.VMEM((n_pairs, 2, s, PAIR), BF16),
            pltpu.VMEM((4, s, MOBA_BLOCK), F32),
            pltpu.VMEM((4, s, MOBA_BLOCK), BF16),
        ],
        compiler_params=pltpu.CompilerParams(
            dimension_semantics=("parallel", "parallel"), vmem_limit_bytes=VMEM_LIMIT),
        name="moba",
    )(proj, proj, proj)


def _rwkv_kernel(*refs, has_vmix, ts):
    if has_vmix:
        (r_ref, k_ref, v_ref, rz_ref, lora_ref, vd_ref, vfirst_ref,
         mu_rkv_ref, mu_lora_ref, w0a0_ref, lora_w_ref, kk_ref, ka_ref, rk_ref, lng_ref, lnb_ref,
         vmix_mu_ref, vmix_up_ref, vmix0_ref,
         yb_ref, carry_rkv, carry_lora, carry_vd, state_ref) = refs
    else:
        (r_ref, k_ref, v_ref, rz_ref, lora_ref,
         mu_rkv_ref, mu_lora_ref, w0a0_ref, lora_w_ref, kk_ref, ka_ref, rk_ref, lng_ref, lnb_ref,
         yb_ref, vfirst_ref, carry_rkv, carry_lora, state_ref) = refs
        carry_vd = None

    @pl.when(pl.program_id(1) == 0)
    def _():
        carry_rkv[...] = jnp.zeros_like(carry_rkv)
        carry_lora[...] = jnp.zeros_like(carry_lora)
        if carry_vd is not None:
            carry_vd[...] = jnp.zeros_like(carry_vd)
        state_ref[...] = jnp.zeros_like(state_ref)

    sub = ts
    n_chunks = sub // CHUNK
    n2 = 2 * CHUNK
    pcs = [(p, c) for c in range(n_chunks) for p in range(N_PAIRS)]

    sub_row = lax.broadcasted_iota(jnp.int32, (sub, sub), 0)
    sub_col = lax.broadcasted_iota(jnp.int32, (sub, sub), 1)
    diff_mat = (jnp.where(sub_row == sub_col + 1, 1.0, 0.0)
                - jnp.where(sub_row == sub_col, 1.0, 0.0)).astype(r_ref.dtype)
    ltri = jnp.where((sub_row // CHUNK == sub_col // CHUNK) & (sub_col <= sub_row), 1.0, 0.0).astype(BF16)
    top_row = lax.broadcasted_iota(jnp.int32, (8, 1), 0) == 0
    lane = lax.broadcasted_iota(jnp.int32, (sub, LANES), 1)
    head0_t = lane < HEAD_DIM
    head0 = lax.broadcasted_iota(jnp.int32, (CHUNK, PAIR), 1) < HEAD_DIM
    wrow = lax.broadcasted_iota(jnp.int32, (CHUNK, PAIR), 0)
    wcol = lax.broadcasted_iota(jnp.int32, (CHUNK, PAIR), 1) % CHUNK
    strict = wcol < wrow
    incl = wcol <= wrow
    eye = jnp.where(wcol == wrow, 1.0, 0.0).astype(F32)
    srow = lax.broadcasted_iota(jnp.int32, (n2, n2), 0)
    scol = lax.broadcasted_iota(jnp.int32, (n2, n2), 1)
    same_head = (srow // CHUNK) == (scol // CHUNK)
    mu_rkv = mu_rkv_ref[...]

    def shift(y_act, carry_ref, row_idx, mu):
        y = y_act.astype(F32)
        delta = jnp.dot(diff_mat, y_act, preferred_element_type=F32)
        head = jnp.where(top_row, delta[:8] + carry_ref[row_idx:row_idx + 1, :], delta[:8])
        delta = jnp.concatenate([head, delta[8:]], axis=0)
        carry_ref[row_idx:row_idx + 1, :] = y[sub - 1:sub, :]
        return y + delta * mu

    def headsum(x):
        s0 = jnp.sum(jnp.where(head0_t, x, 0.0), axis=-1, keepdims=True)
        s1 = jnp.sum(jnp.where(head0_t, 0.0, x), axis=-1, keepdims=True)
        return jnp.where(head0_t, s0, s1)

    def stack(x):
        zero = jnp.zeros_like(x)
        return jnp.concatenate([jnp.where(head0, x, zero), jnp.where(head0, zero, x)], axis=0)

    d = {}

    def prep():
        rs = shift(r_ref[...], carry_rkv, 0, mu_rkv[0:1, :])
        ks = shift(k_ref[...], carry_rkv, 1, mu_rkv[1:2, :])
        vs = shift(v_ref[...], carry_rkv, 2, mu_rkv[2:3, :])
        lo = shift(lora_ref[...], carry_lora, 0, mu_lora_ref[...])
        z = jnp.where(lane < DECAY_LORA, jnp.tanh(lo), lo)
        wa = w0a0_ref[...] + _mm(z, lora_w_ref[...])
        lw = (-0.6065306597126334 * LOG2E) * _sigmoid(wa[:, :WIDTH])
        a = _sigmoid(wa[:, WIDTH:])
        if has_vmix:
            vd = shift(vd_ref[...], carry_vd, 0, vmix_mu_ref[...])
            mix = _sigmoid(vmix0_ref[...] + _mm(vd, vmix_up_ref[...]))
            vr = vs + (vfirst_ref[...] - vs) * mix
        else:
            vfirst_ref[...] = vs
            vr = vs
        kx = ks * kk_ref[...]
        kmod = ks * (1.0 + (a - 1.0) * ka_ref[...])
        d["rkr"] = rs * kmod * rk_ref[...]
        d["vr"] = vr
        kk_all = []
        for p in range(N_PAIRS):
            kxp = kx[:, p * PAIR:(p + 1) * PAIR]
            kk_all.append(kxp * jnp.minimum(lax.rsqrt(headsum(kxp * kxp)), 1.0 / L2_EPS))
        kk = jnp.concatenate(kk_all, axis=1)
        na = -kk
        nb = kk * a
        lw_hi = lw.astype(BF16)
        lw_r = lw - lw_hi.astype(F32)
        lw_mid = lw_r.astype(BF16)
        lw_lo = (lw_r - lw_mid.astype(F32)).astype(BF16)
        cum = (jnp.dot(ltri, lw_lo, preferred_element_type=F32) + jnp.dot(ltri, lw_mid, preferred_element_type=F32)
               + jnp.dot(ltri, lw_hi, preferred_element_type=F32))
        a_t = na * jnp.exp2(cum - lw)
        r_t = rs * jnp.exp2(cum)
        e_neg = jnp.exp2(-cum)
        b_t = nb * e_neg
        k_t = kmod * e_neg
        lhs_g, rhs_g, a_s, r_n, b_h, k_h, v_s, gam = {}, {}, {}, {}, {}, {}, {}, {}
        for c in range(n_chunks):
            rc = slice(c * CHUNK, (c + 1) * CHUNK)
            tot = cum[(c + 1) * CHUNK - 1:(c + 1) * CHUNK, :]
            e_rel = jnp.exp2(tot - cum[rc])
            bh_c = nb[rc] * e_rel
            kh_c = kmod[rc] * e_rel
            gam_c = jnp.exp2(tot)
            for p in range(N_PAIRS):
                sl = slice(p * PAIR, (p + 1) * PAIR)
                pc = (p, c)
                a_n = a_t[rc, sl].astype(BF16)
                r_n[pc] = r_t[rc, sl]
                lhs_g[pc] = jnp.concatenate([a_n, r_n[pc].astype(BF16)], axis=0)
                rhs_g[pc] = jnp.concatenate([stack(b_t[rc, sl].astype(BF16)),
                                             stack(k_t[rc, sl].astype(BF16))], axis=0)
                a_s[pc] = stack(a_n)
                b_h[pc] = bh_c[:, sl].astype(BF16)
                k_h[pc] = stack(kh_c[:, sl])
                v_s[pc] = stack(vr[rc, sl].astype(BF16))
                gam[pc] = gam_c[:, sl]
        d.update(lhs_g=lhs_g, rhs_g=rhs_g, a_s=a_s, r_n=r_n, b_h=b_h, k_h=k_h, v_s=v_s, gam=gam)

    def operators():
        a_ab, a_ak, a_rb, a_rk = {}, {}, {}, {}
        for pc in pcs:
            g = _mm_nt(d["lhs_g"][pc], d["rhs_g"][pc])
            a_ab[pc] = jnp.where(strict, g[:CHUNK, :n2], 0.0)
            a_ak[pc] = jnp.where(strict, g[:CHUNK, n2:], 0.0).astype(BF16)
            a_rb[pc] = jnp.where(incl, g[CHUNK:, :n2], 0.0).astype(BF16)
            a_rk[pc] = jnp.where(incl, g[CHUNK:, n2:], 0.0)
        t_inv = {pc: eye + a_ab[pc] for pc in pcs}
        pw = {pc: a_ab[pc].astype(BF16) for pc in pcs}
        pw = {pc: _mm(pw[pc], stack(pw[pc])).astype(BF16) for pc in pcs}
        for _ in range(CHUNK.bit_length() - 3):
            both = {pc: _mm(jnp.concatenate([pw[pc], t_inv[pc].astype(BF16)], axis=0), stack(pw[pc])) for pc in pcs}
            t_inv = {pc: t_inv[pc] + both[pc][CHUNK:] for pc in pcs}
            pw = {pc: both[pc][:CHUNK].astype(BF16) for pc in pcs}
        t_inv = {pc: (t_inv[pc] + _mm(t_inv[pc], stack(pw[pc]))).astype(BF16) for pc in pcs}
        z = {pc: jnp.where(same_head, _mm_tn(t_inv[pc], d["b_h"][pc]), 0.0).astype(BF16) for pc in pcs}
        w = {pc: _mm(a_rb[pc], stack(t_inv[pc])).astype(BF16) for pc in pcs}
        la_ak = {pc: jnp.concatenate([d["a_s"][pc], stack(a_ak[pc])], axis=1) for pc in pcs}
        my = {pc: _mm_tn(la_ak[pc], z[pc]) for pc in pcs}
        wq = {pc: _mm(w[pc], la_ak[pc]) for pc in pcs}
        d["m_t"] = {pc: my[pc][:n2].astype(BF16) for pc in pcs}
        y = {pc: (my[pc][n2:] + d["k_h"][pc]).astype(BF16) for pc in pcs}
        d["q_p"] = {pc: (d["r_n"][pc] + wq[pc][:, :n2]).astype(BF16) for pc in pcs}
        p_p = {pc: (wq[pc][:, n2:] + a_rk[pc]).astype(BF16) for pc in pcs}
        d["g_s"] = {pc: _mm_tn(d["v_s"][pc], y[pc]) for pc in pcs}
        d["o_loc"] = {pc: _mm(p_p[pc], d["v_s"][pc]) for pc in pcs}

    def recurrence():
        states = [state_ref[p] for p in range(N_PAIRS)]
        outs = [[] for _ in range(N_PAIRS)]
        for c in range(n_chunks):
            for p in range(N_PAIRS):
                pc = (p, c)
                sb = states[p].astype(BF16)
                outs[p].append(_mm_nt(d["q_p"][pc], sb) + d["o_loc"][pc])
                states[p] = states[p] * d["gam"][pc] + _mm(sb, d["m_t"][pc]) + d["g_s"][pc]
        for p in range(N_PAIRS):
            state_ref[p] = states[p]
        d["o"] = [jnp.concatenate(o, axis=0) for o in outs]

    def epilogue():
        for p in range(N_PAIRS):
            sl = slice(p * PAIR, (p + 1) * PAIR)
            o_p = d["o"][p]
            bonus = headsum(d["rkr"][:, sl]) * d["vr"][:, sl]
            mu = headsum(o_p) * (1.0 / HEAD_DIM)
            dev = o_p - mu
            var = headsum(dev * dev) * (1.0 / HEAD_DIM)
            on = dev * lax.rsqrt(var + GN_EPS) * lng_ref[:, sl] + lnb_ref[:, sl]
            rz_p = rz_ref[:, sl].astype(F32)
            yb_ref[:, sl] = ((on + bonus) * (rz_p * _sigmoid(rz_p))).astype(yb_ref.dtype)

    prep()
    operators()
    recurrence()
    epilogue()


def _rwkv(proj, vfirst, p, *, ts=256):
    b, s, _ = proj.shape
    has_vmix = vfirst is not None
    proj2d = proj.reshape(b * s, PROJ_W)
    row = lambda width, col: pl.BlockSpec((pl.Element(ts), pl.Element(width)),
                                          lambda i, t: (pl.multiple_of(i * s + t * ts, ts), col))
    const = lambda shape: pl.BlockSpec(shape, lambda i, t: (0,) * len(shape))
    act = pl.BlockSpec((None, ts, WIDTH), lambda i, t: (i, t, 0))

    in_specs = [row(WIDTH, COL_R), row(WIDTH, COL_RK), row(WIDTH, COL_RV), row(WIDTH, COL_RZ),
                row(LANES, COL_LORA)]
    args = [proj2d, proj2d, proj2d, proj2d, proj2d]
    if has_vmix:
        in_specs += [row(LANES, COL_VD), act]
        args += [proj2d, vfirst]
    in_specs += [const((3, WIDTH)), const((1, LANES)), const((1, 2 * WIDTH)), const((LANES, 2 * WIDTH))]
    args += [p["mu_rkv"], p["mu_lora"], p["w0a0"], p["lora_w"]]
    for name in ("k_k", "k_a", "r_k", "ln_g", "ln_b"):
        in_specs.append(const((1, WIDTH)))
        args.append(p[name])
    if has_vmix:
        in_specs += [const((1, LANES)), const((LANES, WIDTH)), const((1, WIDTH))]
        args += [p["vmix_mu"], p["vmix_up"], p["vmix0"]]

    out_shape = [jax.ShapeDtypeStruct((b, s, WIDTH), ACT)]
    out_specs = [act]
    scratch = [pltpu.VMEM((8, WIDTH), F32), pltpu.VMEM((8, LANES), F32)]
    if has_vmix:
        scratch.append(pltpu.VMEM((8, LANES), F32))
    else:
        out_shape.append(jax.ShapeDtypeStruct((b, s, WIDTH), F32))
        out_specs.append(act)
    scratch.append(pltpu.VMEM((N_PAIRS, PAIR, PAIR), F32))

    res = pl.pallas_call(
        functools.partial(_rwkv_kernel, has_vmix=has_vmix, ts=ts),
        grid=(b, s // ts),
        in_specs=in_specs,
        out_specs=out_specs,
        out_shape=out_shape,
        scratch_shapes=scratch,
        compiler_params=pltpu.CompilerParams(
            dimension_semantics=("parallel", "arbitrary"), vmem_limit_bytes=VMEM_LIMIT),
        name="rwkv_vmix" if has_vmix else "rwkv",
    )(*args)
    return (res[0], vfirst) if has_vmix else (res[0], res[1])


def _outproj_kernel(ya_ref, az_ref, yb_ref, gatt_ref, grw_ref, x_ref,
                    wua_ref, wub_ref, wout_ref, g_ref, o_ref):
    az = az_ref[...].astype(F32)
    ya = ya_ref[...].astype(F32) * (az * _sigmoid(az))
    u = (_sigmoid(gatt_ref[...].astype(F32)) * _mm(ya, wua_ref[...])
         + _sigmoid(grw_ref[...].astype(F32)) * _mm(yb_ref[...], wub_ref[...]))
    y = _mm(u, wout_ref[...])
    ms = jnp.mean(y * y, axis=-1, keepdims=True)
    o_ref[...] = x_ref[...] + y * lax.rsqrt(ms + RMS_EPS) * g_ref[...]


def _outproj(ya, proj2d, yb, x2d, w_up_att, w_up_rw, w_out, g, *, tm=1024):
    t, d = x2d.shape
    row = lambda width, col: pl.BlockSpec((pl.Element(tm), pl.Element(width)),
                                          lambda i: (pl.multiple_of(i * tm, tm), col))
    const = lambda shape: pl.BlockSpec(shape, lambda i: (0, 0))
    return pl.pallas_call(
        _outproj_kernel,
        grid=(t // tm,),
        in_specs=[row(WIDTH, 0), row(WIDTH, COL_AZ), row(WIDTH, 0), row(d, COL_GATT), row(d, COL_GRW),
                  row(d, 0), const((WIDTH, d)), const((WIDTH, d)), const((d, d)), const((1, d))],
        out_specs=row(d, 0),
        out_shape=jax.ShapeDtypeStruct((t, d), F32),
        compiler_params=pltpu.CompilerParams(
            dimension_semantics=("parallel",), vmem_limit_bytes=VMEM_LIMIT),
        name="outproj",
    )(ya, proj2d, yb, proj2d, proj2d, x2d, w_up_att, w_up_rw, w_out, g.reshape(1, d))


def _layer_params(l, d, rw_mu, rw_w0, rw_w_up, rw_a0, rw_a_up, rw_k_k, rw_k_a, rw_r_k,
                  rw_ln_g, rw_ln_b, rw_vmix_down, rw_vmix_mu, rw_vmix_up, rw_vmix0):
    pad = LANES - VMIX_LORA
    vd = jnp.pad(rw_vmix_down[l - 1], ((0, 0), (0, pad))) if l > 0 else jnp.zeros((d, LANES), F32)

    mu = rw_mu[l]
    lora_w = jnp.concatenate([jnp.concatenate([rw_w_up[l], jnp.zeros((DECAY_LORA, WIDTH), F32)], axis=1),
                              jnp.concatenate([jnp.zeros((ICLR_LORA, WIDTH), F32), rw_a_up[l]], axis=1)],
                             axis=0).astype(BF16)
    p = {
        "w_vd": vd.astype(BF16),
        "mu_rkv": mu[:3 * WIDTH].reshape(3, WIDTH),
        "mu_lora": mu[3 * WIDTH:].reshape(1, LANES),
        "w0a0": jnp.concatenate([rw_w0[l], rw_a0[l]]).reshape(1, 2 * WIDTH),
        "lora_w": lora_w,
        "k_k": rw_k_k[l].reshape(1, WIDTH), "k_a": rw_k_a[l].reshape(1, WIDTH),
        "r_k": rw_r_k[l].reshape(1, WIDTH),
        "ln_g": rw_ln_g[l].reshape(1, WIDTH), "ln_b": rw_ln_b[l].reshape(1, WIDTH),
    }
    if l > 0:
        p["vmix_mu"] = jnp.pad(rw_vmix_mu[l - 1], (0, pad)).reshape(1, LANES)
        p["vmix_up"] = jnp.pad(rw_vmix_up[l - 1], ((0, pad), (0, 0))).astype(BF16)
        p["vmix0"] = rw_vmix0[l - 1].reshape(1, WIDTH)
    return p


def kernel(x, norm_pre, norm_post, w_in, rw_mu, rw_w0, rw_w_up, rw_a0, rw_a_up, rw_k_k, rw_k_a, rw_r_k, rw_ln_g, rw_ln_b, rw_vmix_down, rw_vmix_mu, rw_vmix_up, rw_vmix0, w_up_att, w_up_rw, w_out):
    b, s, d = x.shape
    assert d == D_MODEL and s % MOBA_BLOCK == 0
    depth = w_in.shape[0]
    x2d = x.reshape(b * s, d)
    w_all = w_in.astype(BF16)
    vfirst = None
    for l in range(depth):
        p = _layer_params(l, d, rw_mu, rw_w0, rw_w_up, rw_a0, rw_a_up, rw_k_k, rw_k_a, rw_r_k,
                          rw_ln_g, rw_ln_b, rw_vmix_down, rw_vmix_mu, rw_vmix_up, rw_vmix0)
        proj2d = _inproj(x2d, norm_pre[l], w_all, l, p["w_vd"])
        proj = proj2d.reshape(b, s, PROJ_W)
        ya = _moba(proj)
        yb, vfirst = _rwkv(proj, vfirst, p)
        x2d = _outproj(ya.reshape(b * s, WIDTH), proj2d, yb.reshape(b * s, WIDTH), x2d,
                       w_up_att[l].astype(BF16), w_up_rw[l].astype(BF16), w_out[l].astype(BF16),
                       norm_post[l])
    return x2d.reshape(b, s, d)
```

```python
import functools

import jax
import jax.numpy as jnp
from jax import lax
from jax.experimental import pallas as pl
from jax.experimental.pallas import tpu as pltpu

F32 = jnp.float32
BF16 = jnp.bfloat16
ACT = BF16

D_MODEL = 1024
N_HEADS = 8
HEAD_DIM = 64
WIDTH = N_HEADS * HEAD_DIM
MOBA_BLOCK = 256
MOBA_TOPK = 3
DECAY_LORA = 64
ICLR_LORA = 64
VMIX_LORA = 32
RMS_EPS = 1e-6
GN_EPS = 64e-5
L2_EPS = 1e-12
NEG_INF = -1e30

LANES = 128
PAIR = 2 * HEAD_DIM
N_PAIRS = WIDTH // PAIR
V_ROWS = HEAD_DIM + 16
MOBA_PAIRS_PER_STEP = 4
CHUNK = 64

COL_Q, COL_K, COL_V, COL_AZ = 0, 512, 1024, 1536
COL_R, COL_RK, COL_RV = 2048, 2560, 3072
COL_LORA = 3584
COL_RZ = 3712
COL_GATT, COL_GRW = 4224, 5248
COL_VD = 6272
PROJ_W = 6400

VMEM_LIMIT = 48 * 1024 * 1024
VMEM_LIMIT_INPROJ = 56 * 1024 * 1024
LOG2E = 1.4426950408889634


def _sigmoid(x):
    return 0.5 * jnp.tanh(0.5 * x) + 0.5


def _mm(a, b):
    return jnp.dot(a.astype(BF16), b.astype(BF16), preferred_element_type=F32)


def _mm_nt(a, b):
    return lax.dot_general(a.astype(BF16), b.astype(BF16), (((1,), (1,)), ((), ())),
                           preferred_element_type=F32)


def _mm_tn(a, b):
    return lax.dot_general(a.astype(BF16), b.astype(BF16), (((0,), (0,)), ((), ())),
                           preferred_element_type=F32)


def _inproj_kernel(x_ref, g_ref, w_ref, wvd_ref, o_ref, *, tn):
    x = x_ref[...]
    ms = jnp.mean(x * x, axis=-1, keepdims=True)
    h = (x * lax.rsqrt(ms + RMS_EPS) * g_ref[...]).astype(BF16)
    n_main = w_ref.shape[1] - wvd_ref.shape[1]
    for n0 in range(0, n_main, tn):
        n1 = min(n0 + tn, n_main)
        o_ref[:, n0:n1] = jnp.dot(h, w_ref[:, n0:n1], preferred_element_type=F32).astype(o_ref.dtype)
    w_tail = jnp.concatenate([w_ref[:, n_main:], wvd_ref[...]], axis=1)
    o_ref[:, n_main:] = jnp.dot(h, w_tail, preferred_element_type=F32).astype(o_ref.dtype)


def _inproj(x2d, g, w_all, layer, w_vd, *, tm=1024, tn=1280):
    t, d = x2d.shape
    n_w = w_all.shape[2]
    n = n_w + w_vd.shape[1]
    return pl.pallas_call(
        functools.partial(_inproj_kernel, tn=tn),
        grid=(t // tm,),
        in_specs=[
            pl.BlockSpec((tm, d), lambda i: (i, 0)),
            pl.BlockSpec((1, d), lambda i: (0, 0)),
            pl.BlockSpec((None, d, n_w), lambda i: (layer, 0, 0), pipeline_mode=pl.Buffered(1)),
            pl.BlockSpec((d, w_vd.shape[1]), lambda i: (0, 0), pipeline_mode=pl.Buffered(1)),
        ],
        out_specs=pl.BlockSpec((tm, n), lambda i: (i, 0)),
        out_shape=jax.ShapeDtypeStruct((t, n), ACT),
        compiler_params=pltpu.CompilerParams(
            dimension_semantics=("parallel",), vmem_limit_bytes=VMEM_LIMIT_INPROJ),
        name="inproj",
    )(x2d, g.reshape(1, d), w_all, w_vd)


def _moba_kernel(q_ref, k_ref, v_ref, o_ref, vt_ref, qx_ref, kx_ref, s_ref, p_ref, *, n_blocks, n_pairs):
    hp0 = pl.program_id(1) * n_pairs
    blk = MOBA_BLOCK
    s_len = n_blocks * blk
    q_scale = (HEAD_DIM ** -0.5) * LOG2E
    nt = (((1,), (1,)), ((), ()))
    sub = 32
    tk = MOBA_BLOCK

    lane = lax.broadcasted_iota(jnp.int32, (1, PAIR), 1)
    head_lanes = (lane < HEAD_DIM, lane >= HEAD_DIM)
    nrow = lax.broadcasted_iota(jnp.int32, (n_blocks, s_len), 0)
    pos_i = lax.broadcasted_iota(jnp.int32, (n_blocks, s_len), 1)
    pblk = pos_i // blk
    pos = pos_i.astype(F32)
    past = nrow < pblk
    krow = lax.broadcasted_iota(jnp.int32, (blk, blk), 0)
    qcol = lax.broadcasted_iota(jnp.int32, (blk, blk), 1)
    causal = qcol >= krow

    def split3(x):
        hi = x.astype(BF16).astype(F32)
        mid = (x - hi).astype(BF16).astype(F32)
        return hi, mid, x - hi - mid

    def rows8(r0, r1, r2, r3, r4, r5):
        out = jnp.zeros((n_blocks, s_len), F32)
        for idx, r in enumerate((r0, r1, r2, r3, r4, r5)):
            out = jnp.where(nrow == idx, r, out)
        return out

    ones = jnp.ones((n_blocks, s_len), F32)
    ones_rows = jnp.where(lax.broadcasted_iota(jnp.int32, (V_ROWS - HEAD_DIM, s_len), 0) == 0, 1.0, 0.0).astype(BF16)
    mean_row = lax.broadcasted_iota(jnp.int32, (n_blocks, PAIR), 0)

    for g in range(n_pairs):
        cols = slice(g * PAIR, (g + 1) * PAIR)
        vt = v_ref[:, cols].T
        for h in range(2):
            vt_ref[g, h] = jnp.concatenate([vt[h * HEAD_DIM:(h + 1) * HEAD_DIM], ones_rows], axis=0)

        kmean = jnp.zeros((n_blocks, PAIR), F32)
        for n in range(n_blocks):
            kmean = jnp.where(mean_row == n,
                              jnp.mean(k_ref[n * blk:(n + 1) * blk, cols].astype(F32), axis=0, keepdims=True), kmean)

        km = jnp.concatenate([jnp.where(head_lanes[0], kmean, 0.0), jnp.where(head_lanes[1], kmean, 0.0)], axis=0)
        km_hi = km.astype(BF16)
        km_r = km - km_hi.astype(F32)
        km_mid = km_r.astype(BF16)
        km_lo = (km_r - km_mid.astype(F32)).astype(BF16)
        parts = lax.dot_general(jnp.concatenate([km_lo, km_mid, km_hi], axis=0), q_ref[:, cols], nt,
                                preferred_element_type=F32)
        ng = 2 * n_blocks
        gates = parts[:ng] + parts[ng:2 * ng] + parts[2 * ng:]

        q_extra, k_extra = [], []
        for h in range(2):
            head = lax.convert_element_type((hp0 + g) * 2 + h + 1, F32)
            slope2 = jnp.exp2(jnp.zeros((1, 1), F32) - (8.0 / N_HEADS) * head) * LOG2E

            gm = jnp.where(past, gates[h * n_blocks:(h + 1) * n_blocks], NEG_INF)
            rank = jnp.zeros((n_blocks, s_len), F32)
            for m in range(n_blocks):
                gcol = gm[m:m + 1, :]
                beats = (gcol > gm) | ((gcol == gm) & (nrow > m))
                rank = rank + jnp.where(beats, 1.0, 0.0)
            visible = (past & (rank < MOBA_TOPK)) | (nrow == pblk)

            q_hi, q_mid, q_lo = split3(-slope2 * pos)
            k_hi, k_mid, k_lo = split3(slope2 * pos)
            q_extra.append(jnp.concatenate([rows8(q_hi, q_mid, q_lo, ones, ones, ones),
                                            jnp.where(visible, 0.0, NEG_INF)], axis=0))
            k_extra.append(jnp.concatenate([rows8(ones, ones, ones, k_hi, k_mid, k_lo),
                                            jnp.where(nrow == pblk, 1.0, 0.0)], axis=0))
        gap = jnp.zeros((HEAD_DIM - 2 * n_blocks, s_len), F32)
        qx_ref[g] = jnp.concatenate([q_extra[1], gap, q_extra[0], gap], axis=0).T.astype(BF16)
        k_x = jnp.concatenate([k_extra[1], gap, k_extra[0], gap], axis=0).T.astype(BF16)
        for h in range(2):
            kx_ref[g, h] = jnp.where(head_lanes[h], k_ref[:, cols], k_x)

    units = [(g, i, h) for g in range(n_pairs) for i in range(n_blocks) for h in range(2)]
    st = {}

    def score_tile(u, t):
        g, i, h = u
        d = st[u]
        k0 = t * tk
        s_tile = lax.dot_general(kx_ref[g, h, k0:k0 + tk, :], d["qm"], nt,
                                 preferred_element_type=F32)
        for c in range(0, tk, sub):
            sc = s_tile[c:c + sub]
            if k0 // blk == i:
                r0 = k0 % blk + c
                sc = jnp.where(causal[r0:r0 + sub], sc, NEG_INF)
            s_ref[d["slot"], k0 + c:k0 + c + sub, :] = sc
            for r in range(0, sub, 8):
                d["m8"] = sc[r:r + 8] if d["m8"] is None else jnp.maximum(d["m8"], sc[r:r + 8])

    def softmax_tile(u, t):
        d = st[u]
        for c in range(t * tk, (t + 1) * tk, sub):
            pc = jnp.exp2(s_ref[d["slot"], c:c + sub, :] - d["m"])
            p_ref[d["slot"], c:c + sub, :] = pc.astype(BF16)

    def finish(u):
        g, i, h = u
        d = st[u]
        n_k = (i + 1) * blk
        pv = jnp.dot(vt_ref[g, h, :, :n_k], p_ref[d["slot"], :n_k, :],
                     preferred_element_type=F32)
        d["out"] = pv[:HEAD_DIM] * (1.0 / pv[HEAD_DIM:HEAD_DIM + 1])
        if h == 1:
            out_t = jnp.concatenate([st[(g, i, 0)]["out"], d["out"]], axis=0)
            o_ref[i * blk:(i + 1) * blk, g * PAIR:(g + 1) * PAIR] = out_t.T.astype(o_ref.dtype)

    prev = None
    for idx, u in enumerate(units):
        g, i, h = u
        rows = slice(i * blk, (i + 1) * blk)
        if h == 0:
            q_i = (q_ref[rows, g * PAIR:(g + 1) * PAIR].astype(F32) * q_scale).astype(BF16)
        st[u] = {"qm": jnp.where(head_lanes[h], q_i, qx_ref[g, rows, :]), "slot": idx % 2,
                 "m8": None}
        n_cur = (i + 1) * blk // tk
        n_prev = (prev[1] + 1) * blk // tk if prev is not None else 0
        for step in range(max(n_cur, n_prev)):
            if step < n_cur:
                score_tile(u, step)
            if step < n_prev:
                softmax_tile(prev, step)
        if prev is not None:
            finish(prev)
        st[u]["m"] = jnp.max(st[u]["m8"], axis=0, keepdims=True)
        prev = u
    for step in range((prev[1] + 1) * blk // tk):
        softmax_tile(prev, step)
    finish(prev)


def _moba(proj):
    b, s, _ = proj.shape
    n_blocks = s // MOBA_BLOCK
    assert n_blocks == 8, "the extra-column layout of the MoBA kernel is laid out for 8 key blocks"
    n_pairs = MOBA_PAIRS_PER_STEP
    width = n_pairs * PAIR
    qo, ko, vo = COL_Q // width, COL_K // width, COL_V // width
    return pl.pallas_call(
        functools.partial(_moba_kernel, n_blocks=n_blocks, n_pairs=n_pairs),
        grid=(b, N_PAIRS // n_pairs),
        in_specs=[
            pl.BlockSpec((None, s, width), lambda i, p: (i, 0, qo + p)),
            pl.BlockSpec((None, s, width), lambda i, p: (i, 0, ko + p)),
            pl.BlockSpec((None, s, width), lambda i, p: (i, 0, vo + p)),
        ],
        out_specs=pl.BlockSpec((None, s, width), lambda i, p: (i, 0, p)),
        out_shape=jax.ShapeDtypeStruct((b, s, WIDTH), ACT),
        scratch_shapes=[
            pltpu.VMEM((n_pairs, 2, V_ROWS, s), BF16),
            pltpu.VMEM((n_pairs, s, PAIR), BF16),
            pltpu.VMEM((n_pairs, 2, s, PAIR), BF16),
            pltpu.VMEM((2, s, MOBA_BLOCK), F32),
            pltpu.VMEM((2, s, MOBA_BLOCK), BF16),
        ],
        compiler_params=pltpu.CompilerParams(
            dimension_semantics=("parallel", "parallel"), vmem_limit_bytes=VMEM_LIMIT),
        name="moba",
    )(proj, proj, proj)


def _rwkv_kernel(*refs, has_vmix, ts):
    if has_vmix:
        (r_ref, k_ref, v_ref, rz_ref, lora_ref, vd_ref, vfirst_ref,
         mu_rkv_ref, mu_lora_ref, w0a0_ref, lora_w_ref, kk_ref, ka_ref, rk_ref, lng_ref, lnb_ref,
         vmix_mu_ref, vmix_up_ref, vmix0_ref,
         yb_ref, carry_rkv, carry_lora, carry_vd, state_ref) = refs
    else:
        (r_ref, k_ref, v_ref, rz_ref, lora_ref,
         mu_rkv_ref, mu_lora_ref, w0a0_ref, lora_w_ref, kk_ref, ka_ref, rk_ref, lng_ref, lnb_ref,
         yb_ref, vfirst_ref, carry_rkv, carry_lora, state_ref) = refs
        carry_vd = None

    @pl.when(pl.program_id(1) == 0)
    def _():
        carry_rkv[...] = jnp.zeros_like(carry_rkv)
        carry_lora[...] = jnp.zeros_like(carry_lora)
        if carry_vd is not None:
            carry_vd[...] = jnp.zeros_like(carry_vd)
        state_ref[...] = jnp.zeros_like(state_ref)

    sub = ts
    n_chunks = sub // CHUNK
    n2 = 2 * CHUNK
    pcs = [(p, c) for c in range(n_chunks) for p in range(N_PAIRS)]

    sub_row = lax.broadcasted_iota(jnp.int32, (sub, sub), 0)
    sub_col = lax.broadcasted_iota(jnp.int32, (sub, sub), 1)
    diff_mat = (jnp.where(sub_row == sub_col + 1, 1.0, 0.0)
                - jnp.where(sub_row == sub_col, 1.0, 0.0)).astype(r_ref.dtype)
    ltri = jnp.where((sub_row // CHUNK == sub_col // CHUNK) & (sub_col <= sub_row), 1.0, 0.0).astype(BF16)
    top_row = lax.broadcasted_iota(jnp.int32, (8, 1), 0) == 0
    lane = lax.broadcasted_iota(jnp.int32, (sub, LANES), 1)
    head0_t = lane < HEAD_DIM
    head0 = lax.broadcasted_iota(jnp.int32, (CHUNK, PAIR), 1) < HEAD_DIM
    wrow = lax.broadcasted_iota(jnp.int32, (CHUNK, PAIR), 0)
    wcol = lax.broadcasted_iota(jnp.int32, (CHUNK, PAIR), 1) % CHUNK
    strict = wcol < wrow
    incl = wcol <= wrow
    eye = jnp.where(wcol == wrow, 1.0, 0.0).astype(F32)
    srow = lax.broadcasted_iota(jnp.int32, (n2, n2), 0)
    scol = lax.broadcasted_iota(jnp.int32, (n2, n2), 1)
    same_head = (srow // CHUNK) == (scol // CHUNK)
    mu_rkv = mu_rkv_ref[...]

    def shift(y_act, carry_ref, row_idx, mu):
        y = y_act.astype(F32)
        delta = jnp.dot(diff_mat, y_act, preferred_element_type=F32)
        head = jnp.where(top_row, delta[:8] + carry_ref[row_idx:row_idx + 1, :], delta[:8])
        delta = jnp.concatenate([head, delta[8:]], axis=0)
        carry_ref[row_idx:row_idx + 1, :] = y[sub - 1:sub, :]
        return y + delta * mu

    def headsum(x):
        s0 = jnp.sum(jnp.where(head0_t, x, 0.0), axis=-1, keepdims=True)
        s1 = jnp.sum(jnp.where(head0_t, 0.0, x), axis=-1, keepdims=True)
        return jnp.where(head0_t, s0, s1)

    def stack(x):
        zero = jnp.zeros_like(x)
        return jnp.concatenate([jnp.where(head0, x, zero), jnp.where(head0, zero, x)], axis=0)

    d = {}

    def prep():
        rs = shift(r_ref[...], carry_rkv, 0, mu_rkv[0:1, :])
        ks = shift(k_ref[...], carry_rkv, 1, mu_rkv[1:2, :])
        vs = shift(v_ref[...], carry_rkv, 2, mu_rkv[2:3, :])
        lo = shift(lora_ref[...], carry_lora, 0, mu_lora_ref[...])
        z = jnp.where(lane < DECAY_LORA, jnp.tanh(lo), lo)
        wa = w0a0_ref[...] + _mm(z, lora_w_ref[...])
        lw = (-0.6065306597126334 * LOG2E) * _sigmoid(wa[:, :WIDTH])
        a = _sigmoid(wa[:, WIDTH:])
        if has_vmix:
            vd = shift(vd_ref[...], carry_vd, 0, vmix_mu_ref[...])
            mix = _sigmoid(vmix0_ref[...] + _mm(vd, vmix_up_ref[...]))
            vr = vs + (vfirst_ref[...] - vs) * mix
        else:
            vfirst_ref[...] = vs
            vr = vs
        kx = ks * kk_ref[...]
        kmod = ks * (1.0 + (a - 1.0) * ka_ref[...])
        d["rkr"] = rs * kmod * rk_ref[...]
        d["vr"] = vr
        kk_all = []
        for p in range(N_PAIRS):
            kxp = kx[:, p * PAIR:(p + 1) * PAIR]
            kk_all.append(kxp * jnp.minimum(lax.rsqrt(headsum(kxp * kxp)), 1.0 / L2_EPS))
        kk = jnp.concatenate(kk_all, axis=1)
        na = -kk
        nb = kk * a
        lw_hi = lw.astype(BF16)
        lw_r = lw - lw_hi.astype(F32)
        lw_mid = lw_r.astype(BF16)
        lw_lo = (lw_r - lw_mid.astype(F32)).astype(BF16)
        cum = (jnp.dot(ltri, lw_lo, preferred_element_type=F32) + jnp.dot(ltri, lw_mid, preferred_element_type=F32)
               + jnp.dot(ltri, lw_hi, preferred_element_type=F32))
        a_t = na * jnp.exp2(cum - lw)
        r_t = rs * jnp.exp2(cum)
        e_neg = jnp.exp2(-cum)
        b_t = nb * e_neg
        k_t = kmod * e_neg
        lhs_g, rhs_g, a_s, r_n, b_h, k_h, v_s, gam = {}, {}, {}, {}, {}, {}, {}, {}
        for c in range(n_chunks):
            rc = slice(c * CHUNK, (c + 1) * CHUNK)
            tot = cum[(c + 1) * CHUNK - 1:(c + 1) * CHUNK, :]
            e_rel = jnp.exp2(tot - cum[rc])
            bh_c = nb[rc] * e_rel
            kh_c = kmod[rc] * e_rel
            gam_c = jnp.exp2(tot)
            for p in range(N_PAIRS):
                sl = slice(p * PAIR, (p + 1) * PAIR)
                pc = (p, c)
                a_n = a_t[rc, sl].astype(BF16)
                r_n[pc] = r_t[rc, sl]
                lhs_g[pc] = jnp.concatenate([a_n, r_n[pc].astype(BF16)], axis=0)
                rhs_g[pc] = jnp.concatenate([stack(b_t[rc, sl].astype(BF16)),
                                             stack(k_t[rc, sl].astype(BF16))], axis=0)
                a_s[pc] = stack(a_n)
                b_h[pc] = bh_c[:, sl].astype(BF16)
                k_h[pc] = stack(kh_c[:, sl])
                v_s[pc] = stack(vr[rc, sl].astype(BF16))
                gam[pc] = gam_c[:, sl]
        d.update(lhs_g=lhs_g, rhs_g=rhs_g, a_s=a_s, r_n=r_n, b_h=b_h, k_h=k_h, v_s=v_s, gam=gam)

    def operators():
        a_ab, a_ak, a_rb, a_rk = {}, {}, {}, {}
        for pc in pcs:
            g = _mm_nt(d["lhs_g"][pc], d["rhs_g"][pc])
            a_ab[pc] = jnp.where(strict, g[:CHUNK, :n2], 0.0)
            a_ak[pc] = jnp.where(strict, g[:CHUNK, n2:], 0.0).astype(BF16)
            a_rb[pc] = jnp.where(incl, g[CHUNK:, :n2], 0.0).astype(BF16)
            a_rk[pc] = jnp.where(incl, g[CHUNK:, n2:], 0.0)
        t_inv = {pc: eye + a_ab[pc] for pc in pcs}
        pw = {pc: a_ab[pc].astype(BF16) for pc in pcs}
        pw = {pc: _mm(pw[pc], stack(pw[pc])).astype(BF16) for pc in pcs}
        for _ in range(CHUNK.bit_length() - 3):
            both = {pc: _mm(jnp.concatenate([pw[pc], t_inv[pc].astype(BF16)], axis=0), stack(pw[pc])) for pc in pcs}
            t_inv = {pc: t_inv[pc] + both[pc][CHUNK:] for pc in pcs}
            pw = {pc: both[pc][:CHUNK].astype(BF16) for pc in pcs}
        t_inv = {pc: (t_inv[pc] + _mm(t_inv[pc], stack(pw[pc]))).astype(BF16) for pc in pcs}
        z = {pc: jnp.where(same_head, _mm_tn(t_inv[pc], d["b_h"][pc]), 0.0).astype(BF16) for pc in pcs}
        w = {pc: _mm(a_rb[pc], stack(t_inv[pc])).astype(BF16) for pc in pcs}
        la_ak = {pc: jnp.concatenate([d["a_s"][pc], stack(a_ak[pc])], axis=1) for pc in pcs}
        my = {pc: _mm_tn(la_ak[pc], z[pc]) for pc in pcs}
        wq = {pc: _mm(w[pc], la_ak[pc]) for pc in pcs}
        d["m_t"] = {pc: my[pc][:n2].astype(BF16) for pc in pcs}
        y = {pc: (my[pc][n2:] + d["k_h"][pc]).astype(BF16) for pc in pcs}
        d["q_p"] = {pc: (d["r_n"][pc] + wq[pc][:, :n2]).astype(BF16) for pc in pcs}
        p_p = {pc: (wq[pc][:, n2:] + a_rk[pc]).astype(BF16) for pc in pcs}
        d["g_s"] = {pc: _mm_tn(d["v_s"][pc], y[pc]) for pc in pcs}
        d["o_loc"] = {pc: _mm(p_p[pc], d["v_s"][pc]) for pc in pcs}

    def recurrence():
        states = [state_ref[p] for p in range(N_PAIRS)]
        outs = [[] for _ in range(N_PAIRS)]
        for c in range(n_chunks):
            for p in range(N_PAIRS):
                pc = (p, c)
                sb = states[p].astype(BF16)
                outs[p].append(_mm_nt(d["q_p"][pc], sb) + d["o_loc"][pc])
                states[p] = states[p] * d["gam"][pc] + _mm(sb, d["m_t"][pc]) + d["g_s"][pc]
        for p in range(N_PAIRS):
            state_ref[p] = states[p]
        d["o"] = [jnp.concatenate(o, axis=0) for o in outs]

    def epilogue():
        for p in range(N_PAIRS):
            sl = slice(p * PAIR, (p + 1) * PAIR)
            o_p = d["o"][p]
            bonus = headsum(d["rkr"][:, sl]) * d["vr"][:, sl]
            mu = headsum(o_p) * (1.0 / HEAD_DIM)
            dev = o_p - mu
            var = headsum(dev * dev) * (1.0 / HEAD_DIM)
            on = dev * lax.rsqrt(var + GN_EPS) * lng_ref[:, sl] + lnb_ref[:, sl]
            rz_p = rz_ref[:, sl].astype(F32)
            yb_ref[:, sl] = ((on + bonus) * (rz_p * _sigmoid(rz_p))).astype(yb_ref.dtype)

    prep()
    operators()
    recurrence()
    epilogue()


def _rwkv(proj, vfirst, p, *, ts=256):
    b, s, _ = proj.shape
    has_vmix = vfirst is not None
    proj2d = proj.reshape(b * s, PROJ_W)
    row = lambda width, col: pl.BlockSpec((pl.Element(ts), pl.Element(width)),
                                          lambda i, t: (pl.multiple_of(i * s + t * ts, ts), col))
    const = lambda shape: pl.BlockSpec(shape, lambda i, t: (0,) * len(shape))
    act = pl.BlockSpec((None, ts, WIDTH), lambda i, t: (i, t, 0))

    in_specs = [row(WIDTH, COL_R), row(WIDTH, COL_RK), row(WIDTH, COL_RV), row(WIDTH, COL_RZ),
                row(LANES, COL_LORA)]
    args = [proj2d, proj2d, proj2d, proj2d, proj2d]
    if has_vmix:
        in_specs += [row(LANES, COL_VD), act]
        args += [proj2d, vfirst]
    in_specs += [const((3, WIDTH)), const((1, LANES)), const((1, 2 * WIDTH)), const((LANES, 2 * WIDTH))]
    args += [p["mu_rkv"], p["mu_lora"], p["w0a0"], p["lora_w"]]
    for name in ("k_k", "k_a", "r_k", "ln_g", "ln_b"):
        in_specs.append(const((1, WIDTH)))
        args.append(p[name])
    if has_vmix:
        in_specs += [const((1, LANES)), const((LANES, WIDTH)), const((1, WIDTH))]
        args += [p["vmix_mu"], p["vmix_up"], p["vmix0"]]

    out_shape = [jax.ShapeDtypeStruct((b, s, WIDTH), ACT)]
    out_specs = [act]
    scratch = [pltpu.VMEM((8, WIDTH), F32), pltpu.VMEM((8, LANES), F32)]
    if has_vmix:
        scratch.append(pltpu.VMEM((8, LANES), F32))
    else:
        out_shape.append(jax.ShapeDtypeStruct((b, s, WIDTH), F32))
        out_specs.append(act)
    scratch.append(pltpu.VMEM((N_PAIRS, PAIR, PAIR), F32))

    res = pl.pallas_call(
        functools.partial(_rwkv_kernel, has_vmix=has_vmix, ts=ts),
        grid=(b, s // ts),
        in_specs=in_specs,
        out_specs=out_specs,
        out_shape=out_shape,
        scratch_shapes=scratch,
        compiler_params=pltpu.CompilerParams(
            dimension_semantics=("parallel", "arbitrary"), vmem_limit_bytes=VMEM_LIMIT),
        name="rwkv_vmix" if has_vmix else "rwkv",
    )(*args)
    return (res[0], vfirst) if has_vmix else (res[0], res[1])


def _outproj_kernel(ya_ref, az_ref, yb_ref, gatt_ref, grw_ref, x_ref,
                    wua_ref, wub_ref, wout_ref, g_ref, o_ref):
    az = az_ref[...].astype(F32)
    ya = ya_ref[...].astype(F32) * (az * _sigmoid(az))
    u = (_sigmoid(gatt_ref[...].astype(F32)) * _mm(ya, wua_ref[...])
         + _sigmoid(grw_ref[...].astype(F32)) * _mm(yb_ref[...], wub_ref[...]))
    y = _mm(u, wout_ref[...])
    ms = jnp.mean(y * y, axis=-1, keepdims=True)
    o_ref[...] = x_ref[...] + y * lax.rsqrt(ms + RMS_EPS) * g_ref[...]


def _outproj(ya, proj2d, yb, x2d, w_up_att, w_up_rw, w_out, g, *, tm=1024, n_buf=3):
    t, d = x2d.shape

    def outer(ya_hbm, proj_hbm, yb_hbm, x_hbm, wua_ref, wub_ref, wout_ref, g_ref, o_hbm):
        def body(ya_ref, az_ref, yb_ref, gatt_ref, grw_ref, x_ref, o_ref):
            _outproj_kernel(ya_ref, az_ref, yb_ref, gatt_ref, grw_ref, x_ref,
                            wua_ref, wub_ref, wout_ref, g_ref, o_ref)

        tile = lambda width: pl.BlockSpec((tm, width), lambda i: (i, 0), pipeline_mode=pl.Buffered(n_buf))
        window = lambda col, width: proj_hbm.at[:, pl.ds(col, width)]
        pltpu.emit_pipeline(
            body,
            grid=(t // tm,),
            in_specs=[tile(WIDTH), tile(WIDTH), tile(WIDTH), tile(d), tile(d), tile(d)],
            out_specs=[pl.BlockSpec((tm, d), lambda i: (i, 0))],
        )(ya_hbm, window(COL_AZ, WIDTH), yb_hbm, window(COL_GATT, d), window(COL_GRW, d), x_hbm, o_hbm)

    hbm = pl.BlockSpec(memory_space=pl.ANY)
    vmem = pl.BlockSpec(memory_space=pltpu.VMEM)
    return pl.pallas_call(
        outer,
        in_specs=[hbm, hbm, hbm, hbm, vmem, vmem, vmem, vmem],
        out_specs=hbm,
        out_shape=jax.ShapeDtypeStruct((t, d), F32),
        compiler_params=pltpu.CompilerParams(vmem_limit_bytes=VMEM_LIMIT_INPROJ),
        name="outproj",
    )(ya, proj2d, yb, x2d, w_up_att, w_up_rw, w_out, g.reshape(1, d))


def _layer_params(l, d, rw_mu, rw_w0, rw_w_up, rw_a0, rw_a_up, rw_k_k, rw_k_a, rw_r_k,
                  rw_ln_g, rw_ln_b, rw_vmix_down, rw_vmix_mu, rw_vmix_up, rw_vmix0):
    pad = LANES - VMIX_LORA
    vd = jnp.pad(rw_vmix_down[l - 1], ((0, 0), (0, pad))) if l > 0 else jnp.zeros((d, LANES), F32)

    mu = rw_mu[l]
    lora_w = jnp.concatenate([jnp.concatenate([rw_w_up[l], jnp.zeros((DECAY_LORA, WIDTH), F32)], axis=1),
                              jnp.concatenate([jnp.zeros((ICLR_LORA, WIDTH), F32), rw_a_up[l]], axis=1)],
                             axis=0).astype(BF16)
    p = {
        "w_vd": vd.astype(BF16),
        "mu_rkv": mu[:3 * WIDTH].reshape(3, WIDTH),
        "mu_lora": mu[3 * WIDTH:].reshape(1, LANES),
        "w0a0": jnp.concatenate([rw_w0[l], rw_a0[l]]).reshape(1, 2 * WIDTH),
        "lora_w": lora_w,
        "k_k": rw_k_k[l].reshape(1, WIDTH), "k_a": rw_k_a[l].reshape(1, WIDTH),
        "r_k": rw_r_k[l].reshape(1, WIDTH),
        "ln_g": rw_ln_g[l].reshape(1, WIDTH), "ln_b": rw_ln_b[l].reshape(1, WIDTH),
    }
    if l > 0:
        p["vmix_mu"] = jnp.pad(rw_vmix_mu[l - 1], (0, pad)).reshape(1, LANES)
        p["vmix_up"] = jnp.pad(rw_vmix_up[l - 1], ((0, pad), (0, 0))).astype(BF16)
        p["vmix0"] = rw_vmix0[l - 1].reshape(1, WIDTH)
    return p


def kernel(x, norm_pre, norm_post, w_in, rw_mu, rw_w0, rw_w_up, rw_a0, rw_a_up, rw_k_k, rw_k_a, rw_r_k, rw_ln_g, rw_ln_b, rw_vmix_down, rw_vmix_mu, rw_vmix_up, rw_vmix0, w_up_att, w_up_rw, w_out):
    b, s, d = x.shape
    assert d == D_MODEL and s % MOBA_BLOCK == 0
    depth = w_in.shape[0]
    x2d = x.reshape(b * s, d)
    w_all = w_in.astype(BF16)
    vfirst = None
    for l in range(depth):
        p = _layer_params(l, d, rw_mu, rw_w0, rw_w_up, rw_a0, rw_a_up, rw_k_k, rw_k_a, rw_r_k,
                          rw_ln_g, rw_ln_b, rw_vmix_down, rw_vmix_mu, rw_vmix_up, rw_vmix0)
        proj2d = _inproj(x2d, norm_pre[l], w_all, l, p["w_vd"])
        proj = proj2d.reshape(b, s, PROJ_W)
        ya = _moba(proj)
        yb, vfirst = _rwkv(proj, vfirst, p)
        x2d = _outproj(ya.reshape(b * s, WIDTH), proj2d, yb.reshape(b * s, WIDTH), x2d,
                       w_up_att[l].astype(BF16), w_up_rw[l].astype(BF16), w_out[l].astype(BF16),
                       norm_post[l])
    return x2d.reshape(b, s, d)
```

```python
import functools

import jax
import jax.numpy as jnp
from jax import lax
from jax.experimental import pallas as pl
from jax.experimental.pallas import tpu as pltpu

F32 = jnp.float32
BF16 = jnp.bfloat16
ACT = BF16

D_MODEL = 1024
N_HEADS = 8
HEAD_DIM = 64
WIDTH = N_HEADS * HEAD_DIM
MOBA_BLOCK = 256
MOBA_TOPK = 3
DECAY_LORA = 64
ICLR_LORA = 64
VMIX_LORA = 32
RMS_EPS = 1e-6
GN_EPS = 64e-5
L2_EPS = 1e-12
NEG_INF = -1e30

LANES = 128
PAIR = 2 * HEAD_DIM
N_PAIRS = WIDTH // PAIR
V_ROWS = HEAD_DIM + 16
MOBA_PAIRS_PER_STEP = 4
CHUNK = 64

COL_Q, COL_K, COL_V, COL_AZ = 0, 512, 1024, 1536
COL_R, COL_RK, COL_RV = 2048, 2560, 3072
COL_LORA = 3584
COL_RZ = 3712
COL_GATT, COL_GRW = 4224, 5248
COL_VD = 6272
PROJ_W = 6400

VMEM_LIMIT = 48 * 1024 * 1024
VMEM_LIMIT_INPROJ = 56 * 1024 * 1024
LOG2E = 1.4426950408889634


def _sigmoid(x):
    return 0.5 * jnp.tanh(0.5 * x) + 0.5


def _mm(a, b):
    return jnp.dot(a.astype(BF16), b.astype(BF16), preferred_element_type=F32)


def _mm_nt(a, b):
    return lax.dot_general(a.astype(BF16), b.astype(BF16), (((1,), (1,)), ((), ())),
                           preferred_element_type=F32)


def _mm_tn(a, b):
    return lax.dot_general(a.astype(BF16), b.astype(BF16), (((0,), (0,)), ((), ())),
                           preferred_element_type=F32)


def _inproj_kernel(x_ref, g_ref, w_hbm, wvd_ref, o_ref, w_ref, wbuf_ref, sem_ref, *, tn, layer):
    @pl.when(pl.program_id(0) == 0)
    def _():
        n_slots, rc = wbuf_ref.shape[0], wbuf_ref.shape[1]
        n_chunks = w_ref.shape[0] // rc

        def fetch(c):
            return pltpu.make_async_copy(w_hbm.at[layer, pl.ds(c * rc, rc), :],
                                         wbuf_ref.at[c % n_slots], sem_ref.at[c % n_slots])

        for c in range(n_slots - 1):
            fetch(c).start()
        for c in range(n_chunks):
            if c + n_slots - 1 < n_chunks:
                fetch(c + n_slots - 1).start()
            fetch(c).wait()
            w_ref[c * rc:(c + 1) * rc, :] = wbuf_ref[c % n_slots].astype(BF16)

    x = x_ref[...]
    ms = jnp.mean(x * x, axis=-1, keepdims=True)
    h = (x * lax.rsqrt(ms + RMS_EPS) * g_ref[...]).astype(BF16)
    n_main = w_ref.shape[1] - wvd_ref.shape[1]
    for n0 in range(0, n_main, tn):
        n1 = min(n0 + tn, n_main)
        o_ref[:, n0:n1] = jnp.dot(h, w_ref[:, n0:n1], preferred_element_type=F32).astype(o_ref.dtype)
    w_tail = jnp.concatenate([w_ref[:, n_main:], wvd_ref[...]], axis=1)
    o_ref[:, n_main:] = jnp.dot(h, w_tail, preferred_element_type=F32).astype(o_ref.dtype)


def _inproj(x2d, g, w_all, layer, w_vd, *, tm=1024, tn=1280, rc=32, n_slots=4):
    t, d = x2d.shape
    n_w = w_all.shape[2]
    n = n_w + w_vd.shape[1]
    assert d % rc == 0
    return pl.pallas_call(
        functools.partial(_inproj_kernel, tn=tn, layer=layer),
        grid=(t // tm,),
        in_specs=[
            pl.BlockSpec((tm, d), lambda i: (i, 0)),
            pl.BlockSpec((1, d), lambda i: (0, 0)),
            pl.BlockSpec(memory_space=pl.ANY),
            pl.BlockSpec((d, w_vd.shape[1]), lambda i: (0, 0), pipeline_mode=pl.Buffered(1)),
        ],
        out_specs=pl.BlockSpec((tm, n), lambda i: (i, 0)),
        out_shape=jax.ShapeDtypeStruct((t, n), ACT),
        scratch_shapes=[pltpu.VMEM((d, n_w), BF16), pltpu.VMEM((n_slots, rc, n_w), w_all.dtype),
                        pltpu.SemaphoreType.DMA((n_slots,))],
        compiler_params=pltpu.CompilerParams(
            dimension_semantics=("arbitrary",), vmem_limit_bytes=VMEM_LIMIT_INPROJ),
        name="inproj",
    )(x2d, g.reshape(1, d), w_all, w_vd)


def _moba_kernel(q_ref, k_ref, v_ref, o_ref, vt_ref, qx_ref, kx_ref, s_ref, p_ref, *, n_blocks, n_pairs):
    hp0 = pl.program_id(1) * n_pairs
    blk = MOBA_BLOCK
    s_len = n_blocks * blk
    q_scale = (HEAD_DIM ** -0.5) * LOG2E
    nt = (((1,), (1,)), ((), ()))
    sub = 32
    tk = MOBA_BLOCK

    lane = lax.broadcasted_iota(jnp.int32, (1, PAIR), 1)
    head_lanes = (lane < HEAD_DIM, lane >= HEAD_DIM)
    nrow = lax.broadcasted_iota(jnp.int32, (n_blocks, s_len), 0)
    pos_i = lax.broadcasted_iota(jnp.int32, (n_blocks, s_len), 1)
    pblk = pos_i // blk
    pos = pos_i.astype(F32)
    past = nrow < pblk
    krow = lax.broadcasted_iota(jnp.int32, (blk, blk), 0)
    qcol = lax.broadcasted_iota(jnp.int32, (blk, blk), 1)
    causal = qcol >= krow

    def split3(x):
        hi = x.astype(BF16).astype(F32)
        mid = (x - hi).astype(BF16).astype(F32)
        return hi, mid, x - hi - mid

    def rows8(r0, r1, r2, r3, r4, r5):
        out = jnp.zeros((n_blocks, s_len), F32)
        for idx, r in enumerate((r0, r1, r2, r3, r4, r5)):
            out = jnp.where(nrow == idx, r, out)
        return out

    ones = jnp.ones((n_blocks, s_len), F32)
    ones_rows = jnp.where(lax.broadcasted_iota(jnp.int32, (V_ROWS - HEAD_DIM, s_len), 0) == 0, 1.0, 0.0).astype(BF16)
    mean_row = lax.broadcasted_iota(jnp.int32, (n_blocks, PAIR), 0)

    for g in range(n_pairs):
        cols = slice(g * PAIR, (g + 1) * PAIR)
        vt = v_ref[:, cols].T
        for h in range(2):
            vt_ref[g, h] = jnp.concatenate([vt[h * HEAD_DIM:(h + 1) * HEAD_DIM], ones_rows], axis=0)

        kmean = jnp.zeros((n_blocks, PAIR), F32)
        for n in range(n_blocks):
            kmean = jnp.where(mean_row == n,
                              jnp.mean(k_ref[n * blk:(n + 1) * blk, cols].astype(F32), axis=0, keepdims=True), kmean)

        km = jnp.concatenate([jnp.where(head_lanes[0], kmean, 0.0), jnp.where(head_lanes[1], kmean, 0.0)], axis=0)
        km_hi = km.astype(BF16)
        km_r = km - km_hi.astype(F32)
        km_mid = km_r.astype(BF16)
        km_lo = (km_r - km_mid.astype(F32)).astype(BF16)
        parts = lax.dot_general(jnp.concatenate([km_lo, km_mid, km_hi], axis=0), q_ref[:, cols], nt,
                                preferred_element_type=F32)
        ng = 2 * n_blocks
        gates = parts[:ng] + parts[ng:2 * ng] + parts[2 * ng:]

        q_extra, k_extra = [], []
        for h in range(2):
            head = lax.convert_element_type((hp0 + g) * 2 + h + 1, F32)
            slope2 = jnp.exp2(jnp.zeros((1, 1), F32) - (8.0 / N_HEADS) * head) * LOG2E

            gm = jnp.where(past, gates[h * n_blocks:(h + 1) * n_blocks], NEG_INF)
            rank = jnp.zeros((n_blocks, s_len), F32)
            for m in range(n_blocks):
                gcol = gm[m:m + 1, :]
                beats = (gcol > gm) | ((gcol == gm) & (nrow > m))
                rank = rank + jnp.where(beats, 1.0, 0.0)
            visible = (past & (rank < MOBA_TOPK)) | (nrow == pblk)

            q_hi, q_mid, q_lo = split3(-slope2 * pos)
            k_hi, k_mid, k_lo = split3(slope2 * pos)
            q_extra.append(jnp.concatenate([rows8(q_hi, q_mid, q_lo, ones, ones, ones),
                                            jnp.where(visible, 0.0, NEG_INF)], axis=0))
            k_extra.append(jnp.concatenate([rows8(ones, ones, ones, k_hi, k_mid, k_lo),
                                            jnp.where(nrow == pblk, 1.0, 0.0)], axis=0))
        gap = jnp.zeros((HEAD_DIM - 2 * n_blocks, s_len), F32)
        qx_ref[g] = jnp.concatenate([q_extra[1], gap, q_extra[0], gap], axis=0).T.astype(BF16)
        k_x = jnp.concatenate([k_extra[1], gap, k_extra[0], gap], axis=0).T.astype(BF16)
        for h in range(2):
            kx_ref[g, h] = jnp.where(head_lanes[h], k_ref[:, cols], k_x)

    units = [(g, i, h) for g in range(n_pairs) for i in range(n_blocks) for h in range(2)]
    st = {}

    def score_tile(u, t):
        g, i, h = u
        d = st[u]
        k0 = t * tk
        s_tile = lax.dot_general(kx_ref[g, h, k0:k0 + tk, :], d["qm"], nt,
                                 preferred_element_type=F32)
        for c in range(0, tk, sub):
            sc = s_tile[c:c + sub]
            if k0 // blk == i:
                r0 = k0 % blk + c
                sc = jnp.where(causal[r0:r0 + sub], sc, NEG_INF)
            s_ref[d["slot"], k0 + c:k0 + c + sub, :] = sc
            for r in range(0, sub, 8):
                d["m8"] = sc[r:r + 8] if d["m8"] is None else jnp.maximum(d["m8"], sc[r:r + 8])

    def softmax_tile(u, t):
        d = st[u]
        for c in range(t * tk, (t + 1) * tk, sub):
            pc = jnp.exp2(s_ref[d["slot"], c:c + sub, :] - d["m"])
            p_ref[d["slot"], c:c + sub, :] = pc.astype(BF16)

    def finish(u):
        g, i, h = u
        d = st[u]
        n_k = (i + 1) * blk
        pv = jnp.dot(vt_ref[g, h, :, :n_k], p_ref[d["slot"], :n_k, :],
                     preferred_element_type=F32)
        d["out"] = pv[:HEAD_DIM] * (1.0 / pv[HEAD_DIM:HEAD_DIM + 1])
        if h == 1:
            out_t = jnp.concatenate([st[(g, i, 0)]["out"], d["out"]], axis=0)
            o_ref[i * blk:(i + 1) * blk, g * PAIR:(g + 1) * PAIR] = out_t.T.astype(o_ref.dtype)

    prev = None
    for idx, u in enumerate(units):
        g, i, h = u
        rows = slice(i * blk, (i + 1) * blk)
        if h == 0:
            q_i = (q_ref[rows, g * PAIR:(g + 1) * PAIR].astype(F32) * q_scale).astype(BF16)
        st[u] = {"qm": jnp.where(head_lanes[h], q_i, qx_ref[g, rows, :]), "slot": idx % 2,
                 "m8": None}
        n_cur = (i + 1) * blk // tk
        n_prev = (prev[1] + 1) * blk // tk if prev is not None else 0
        for step in range(max(n_cur, n_prev)):
            if step < n_cur:
                score_tile(u, step)
            if step < n_prev:
                softmax_tile(prev, step)
        if prev is not None:
            finish(prev)
        st[u]["m"] = jnp.max(st[u]["m8"], axis=0, keepdims=True)
        prev = u
    for step in range((prev[1] + 1) * blk // tk):
        softmax_tile(prev, step)
    finish(prev)


def _moba(proj):
    b, s, _ = proj.shape
    n_blocks = s // MOBA_BLOCK
    assert n_blocks == 8, "the extra-column layout of the MoBA kernel is laid out for 8 key blocks"
    n_pairs = MOBA_PAIRS_PER_STEP
    width = n_pairs * PAIR
    qo, ko, vo = COL_Q // width, COL_K // width, COL_V // width
    return pl.pallas_call(
        functools.partial(_moba_kernel, n_blocks=n_blocks, n_pairs=n_pairs),
        grid=(b, N_PAIRS // n_pairs),
        in_specs=[
            pl.BlockSpec((None, s, width), lambda i, p: (i, 0, qo + p)),
            pl.BlockSpec((None, s, width), lambda i, p: (i, 0, ko + p)),
            pl.BlockSpec((None, s, width), lambda i, p: (i, 0, vo + p)),
        ],
        out_specs=pl.BlockSpec((None, s, width), lambda i, p: (i, 0, p)),
        out_shape=jax.ShapeDtypeStruct((b, s, WIDTH), ACT),
        scratch_shapes=[
            pltpu.VMEM((n_pairs, 2, V_ROWS, s), BF16),
            pltpu.VMEM((n_pairs, s, PAIR), BF16),
            pltpu.VMEM((n_pairs, 2, s, PAIR), BF16),
            pltpu.VMEM((2, s, MOBA_BLOCK), F32),
            pltpu.VMEM((2, s, MOBA_BLOCK), BF16),
        ],
        compiler_params=pltpu.CompilerParams(
            dimension_semantics=("parallel", "parallel"), vmem_limit_bytes=VMEM_LIMIT),
        name="moba",
    )(proj, proj, proj)


def _rwkv_kernel(*refs, has_vmix, ts):
    if has_vmix:
        (r_ref, k_ref, v_ref, rz_ref, lora_ref, vd_ref, vfirst_ref,
         mu_rkv_ref, mu_lora_ref, w0a0_ref, lora_w_ref, kk_ref, ka_ref, rk_ref, lng_ref, lnb_ref,
         vmix_mu_ref, vmix_up_ref, vmix0_ref,
         yb_ref, carry_rkv, carry_lora, carry_vd, state_ref) = refs
    else:
        (r_ref, k_ref, v_ref, rz_ref, lora_ref,
         mu_rkv_ref, mu_lora_ref, w0a0_ref, lora_w_ref, kk_ref, ka_ref, rk_ref, lng_ref, lnb_ref,
         yb_ref, vfirst_ref, carry_rkv, carry_lora, state_ref) = refs
        carry_vd = None

    @pl.when(pl.program_id(1) == 0)
    def _():
        carry_rkv[...] = jnp.zeros_like(carry_rkv)
        carry_lora[...] = jnp.zeros_like(carry_lora)
        if carry_vd is not None:
            carry_vd[...] = jnp.zeros_like(carry_vd)
        state_ref[...] = jnp.zeros_like(state_ref)

    sub = ts
    n_chunks = sub // CHUNK
    n2 = 2 * CHUNK
    pcs = [(p, c) for c in range(n_chunks) for p in range(N_PAIRS)]

    sub_row = lax.broadcasted_iota(jnp.int32, (sub, sub), 0)
    sub_col = lax.broadcasted_iota(jnp.int32, (sub, sub), 1)
    diff_mat = (jnp.where(sub_row == sub_col + 1, 1.0, 0.0)
                - jnp.where(sub_row == sub_col, 1.0, 0.0)).astype(r_ref.dtype)
    ltri = jnp.where((sub_row // CHUNK == sub_col // CHUNK) & (sub_col <= sub_row), 1.0, 0.0).astype(BF16)
    top_row = lax.broadcasted_iota(jnp.int32, (8, 1), 0) == 0
    lane = lax.broadcasted_iota(jnp.int32, (sub, LANES), 1)
    head0_t = lane < HEAD_DIM
    head0 = lax.broadcasted_iota(jnp.int32, (CHUNK, PAIR), 1) < HEAD_DIM
    wrow = lax.broadcasted_iota(jnp.int32, (CHUNK, PAIR), 0)
    wcol = lax.broadcasted_iota(jnp.int32, (CHUNK, PAIR), 1) % CHUNK
    strict = wcol < wrow
    incl = wcol <= wrow
    eye = jnp.where(wcol == wrow, 1.0, 0.0).astype(F32)
    srow = lax.broadcasted_iota(jnp.int32, (n2, n2), 0)
    scol = lax.broadcasted_iota(jnp.int32, (n2, n2), 1)
    same_head = (srow // CHUNK) == (scol // CHUNK)
    mu_rkv = mu_rkv_ref[...]

    def shift(y_act, carry_ref, row_idx, mu):
        y = y_act.astype(F32)
        delta = jnp.dot(diff_mat, y_act, preferred_element_type=F32)
        head = jnp.where(top_row, delta[:8] + carry_ref[row_idx:row_idx + 1, :], delta[:8])
        delta = jnp.concatenate([head, delta[8:]], axis=0)
        carry_ref[row_idx:row_idx + 1, :] = y[sub - 1:sub, :]
        return y + delta * mu

    def headsum(x):
        s0 = jnp.sum(jnp.where(head0_t, x, 0.0), axis=-1, keepdims=True)
        s1 = jnp.sum(jnp.where(head0_t, 0.0, x), axis=-1, keepdims=True)
        return jnp.where(head0_t, s0, s1)

    def stack(x):
        zero = jnp.zeros_like(x)
        return jnp.concatenate([jnp.where(head0, x, zero), jnp.where(head0, zero, x)], axis=0)

    d = {}

    def prep():
        rs = shift(r_ref[...], carry_rkv, 0, mu_rkv[0:1, :])
        ks = shift(k_ref[...], carry_rkv, 1, mu_rkv[1:2, :])
        vs = shift(v_ref[...], carry_rkv, 2, mu_rkv[2:3, :])
        lo = shift(lora_ref[...], carry_lora, 0, mu_lora_ref[...])
        z = jnp.where(lane < DECAY_LORA, jnp.tanh(lo), lo)
        wa = w0a0_ref[...] + _mm(z, lora_w_ref[...])
        lw = (-0.6065306597126334 * LOG2E) * _sigmoid(wa[:, :WIDTH])
        a = _sigmoid(wa[:, WIDTH:])
        if has_vmix:
            vd = shift(vd_ref[...], carry_vd, 0, vmix_mu_ref[...])
            mix = _sigmoid(vmix0_ref[...] + _mm(vd, vmix_up_ref[...]))
            vr = vs + (vfirst_ref[...] - vs) * mix
        else:
            vfirst_ref[...] = vs
            vr = vs
        kx = ks * kk_ref[...]
        kmod = ks * (1.0 + (a - 1.0) * ka_ref[...])
        d["rkr"] = rs * kmod * rk_ref[...]
        d["vr"] = vr
        kk_all = []
        for p in range(N_PAIRS):
            kxp = kx[:, p * PAIR:(p + 1) * PAIR]
            kk_all.append(kxp * jnp.minimum(lax.rsqrt(headsum(kxp * kxp)), 1.0 / L2_EPS))
        kk = jnp.concatenate(kk_all, axis=1)
        na = -kk
        nb = kk * a
        lw_hi = lw.astype(BF16)
        lw_r = lw - lw_hi.astype(F32)
        lw_mid = lw_r.astype(BF16)
        lw_lo = (lw_r - lw_mid.astype(F32)).astype(BF16)
        cum = (jnp.dot(ltri, lw_lo, preferred_element_type=F32) + jnp.dot(ltri, lw_mid, preferred_element_type=F32)
               + jnp.dot(ltri, lw_hi, preferred_element_type=F32))
        a_t = na * jnp.exp2(cum - lw)
        r_t = rs * jnp.exp2(cum)
        e_neg = jnp.exp2(-cum)
        b_t = nb * e_neg
        k_t = kmod * e_neg
        lhs_g, rhs_g, a_s, r_n, b_h, k_h, v_s, gam = {}, {}, {}, {}, {}, {}, {}, {}
        for c in range(n_chunks):
            rc = slice(c * CHUNK, (c + 1) * CHUNK)
            tot = cum[(c + 1) * CHUNK - 1:(c + 1) * CHUNK, :]
            e_rel = jnp.exp2(tot - cum[rc])
            bh_c = nb[rc] * e_rel
            kh_c = kmod[rc] * e_rel
            gam_c = jnp.exp2(tot)
            for p in range(N_PAIRS):
                sl = slice(p * PAIR, (p + 1) * PAIR)
                pc = (p, c)
                a_n = a_t[rc, sl].astype(BF16)
                r_n[pc] = r_t[rc, sl]
                lhs_g[pc] = jnp.concatenate([a_n, r_n[pc].astype(BF16)], axis=0)
                rhs_g[pc] = jnp.concatenate([stack(b_t[rc, sl].astype(BF16)),
                                             stack(k_t[rc, sl].astype(BF16))], axis=0)
                a_s[pc] = stack(a_n)
                b_h[pc] = bh_c[:, sl].astype(BF16)
                k_h[pc] = stack(kh_c[:, sl])
                v_s[pc] = stack(vr[rc, sl].astype(BF16))
                gam[pc] = gam_c[:, sl]
        d.update(lhs_g=lhs_g, rhs_g=rhs_g, a_s=a_s, r_n=r_n, b_h=b_h, k_h=k_h, v_s=v_s, gam=gam)

    def operators():
        a_ab, a_ak, a_rb, a_rk = {}, {}, {}, {}
        for pc in pcs:
            g = _mm_nt(d["lhs_g"][pc], d["rhs_g"][pc])
            a_ab[pc] = jnp.where(strict, g[:CHUNK, :n2], 0.0)
            a_ak[pc] = jnp.where(strict, g[:CHUNK, n2:], 0.0).astype(BF16)
            a_rb[pc] = jnp.where(incl, g[CHUNK:, :n2], 0.0).astype(BF16)
            a_rk[pc] = jnp.where(incl, g[CHUNK:, n2:], 0.0)
        t_inv = {pc: eye + a_ab[pc] for pc in pcs}
        pw = {pc: a_ab[pc].astype(BF16) for pc in pcs}
        pw = {pc: _mm(pw[pc], stack(pw[pc])).astype(BF16) for pc in pcs}
        for _ in range(CHUNK.bit_length() - 3):
            both = {pc: _mm(jnp.concatenate([pw[pc], t_inv[pc].astype(BF16)], axis=0), stack(pw[pc])) for pc in pcs}
            t_inv = {pc: t_inv[pc] + both[pc][CHUNK:] for pc in pcs}
            pw = {pc: both[pc][:CHUNK].astype(BF16) for pc in pcs}
        t_inv = {pc: (t_inv[pc] + _mm(t_inv[pc], stack(pw[pc]))).astype(BF16) for pc in pcs}
        z = {pc: jnp.where(same_head, _mm_tn(t_inv[pc], d["b_h"][pc]), 0.0).astype(BF16) for pc in pcs}
        w = {pc: _mm(a_rb[pc], stack(t_inv[pc])).astype(BF16) for pc in pcs}
        la_ak = {pc: jnp.concatenate([d["a_s"][pc], stack(a_ak[pc])], axis=1) for pc in pcs}
        my = {pc: _mm_tn(la_ak[pc], z[pc]) for pc in pcs}
        wq = {pc: _mm(w[pc], la_ak[pc]) for pc in pcs}
        d["m_t"] = {pc: my[pc][:n2].astype(BF16) for pc in pcs}
        y = {pc: (my[pc][n2:] + d["k_h"][pc]).astype(BF16) for pc in pcs}
        d["q_p"] = {pc: (d["r_n"][pc] + wq[pc][:, :n2]).astype(BF16) for pc in pcs}
        p_p = {pc: (wq[pc][:, n2:] + a_rk[pc]).astype(BF16) for pc in pcs}
        d["g_s"] = {pc: _mm_tn(d["v_s"][pc], y[pc]) for pc in pcs}
        d["o_loc"] = {pc: _mm(p_p[pc], d["v_s"][pc]) for pc in pcs}

    def recurrence():
        states = [state_ref[p] for p in range(N_PAIRS)]
        outs = [[] for _ in range(N_PAIRS)]
        for c in range(n_chunks):
            for p in range(N_PAIRS):
                pc = (p, c)
                sb = states[p].astype(BF16)
                outs[p].append(_mm_nt(d["q_p"][pc], sb) + d["o_loc"][pc])
                states[p] = states[p] * d["gam"][pc] + _mm(sb, d["m_t"][pc]) + d["g_s"][pc]
        for p in range(N_PAIRS):
            state_ref[p] = states[p]
        d["o"] = [jnp.concatenate(o, axis=0) for o in outs]

    def epilogue():
        for p in range(N_PAIRS):
            sl = slice(p * PAIR, (p + 1) * PAIR)
            o_p = d["o"][p]
            bonus = headsum(d["rkr"][:, sl]) * d["vr"][:, sl]
            mu = headsum(o_p) * (1.0 / HEAD_DIM)
            dev = o_p - mu
            var = headsum(dev * dev) * (1.0 / HEAD_DIM)
            on = dev * lax.rsqrt(var + GN_EPS) * lng_ref[:, sl] + lnb_ref[:, sl]
            rz_p = rz_ref[:, sl].astype(F32)
            yb_ref[:, sl] = ((on + bonus) * (rz_p * _sigmoid(rz_p))).astype(yb_ref.dtype)

    prep()
    operators()
    recurrence()
    epilogue()


def _rwkv(proj, vfirst, p, *, ts=256):
    b, s, _ = proj.shape
    has_vmix = vfirst is not None
    proj2d = proj.reshape(b * s, PROJ_W)
    row = lambda width, col: pl.BlockSpec((pl.Element(ts), pl.Element(width)),
                                          lambda i, t: (pl.multiple_of(i * s + t * ts, ts), col))
    const = lambda shape: pl.BlockSpec(shape, lambda i, t: (0,) * len(shape))
    act = pl.BlockSpec((None, ts, WIDTH), lambda i, t: (i, t, 0))

    in_specs = [row(WIDTH, COL_R), row(WIDTH, COL_RK), row(WIDTH, COL_RV), row(WIDTH, COL_RZ),
                row(LANES, COL_LORA)]
    args = [proj2d, proj2d, proj2d, proj2d, proj2d]
    if has_vmix:
        in_specs += [row(LANES, COL_VD), act]
        args += [proj2d, vfirst]
    in_specs += [const((3, WIDTH)), const((1, LANES)), const((1, 2 * WIDTH)), const((LANES, 2 * WIDTH))]
    args += [p["mu_rkv"], p["mu_lora"], p["w0a0"], p["lora_w"]]
    for name in ("k_k", "k_a", "r_k", "ln_g", "ln_b"):
        in_specs.append(const((1, WIDTH)))
        args.append(p[name])
    if has_vmix:
        in_specs += [const((1, LANES)), const((LANES, WIDTH)), const((1, WIDTH))]
        args += [p["vmix_mu"], p["vmix_up"], p["vmix0"]]

    out_shape = [jax.ShapeDtypeStruct((b, s, WIDTH), ACT)]
    out_specs = [act]
    scratch = [pltpu.VMEM((8, WIDTH), F32), pltpu.VMEM((8, LANES), F32)]
    if has_vmix:
        scratch.append(pltpu.VMEM((8, LANES), F32))
    else:
        out_shape.append(jax.ShapeDtypeStruct((b, s, WIDTH), F32))
        out_specs.append(act)
    scratch.append(pltpu.VMEM((N_PAIRS, PAIR, PAIR), F32))

    res = pl.pallas_call(
        functools.partial(_rwkv_kernel, has_vmix=has_vmix, ts=ts),
        grid=(b, s // ts),
        in_specs=in_specs,
        out_specs=out_specs,
        out_shape=out_shape,
        scratch_shapes=scratch,
        compiler_params=pltpu.CompilerParams(
            dimension_semantics=("parallel", "arbitrary"), vmem_limit_bytes=VMEM_LIMIT),
        name="rwkv_vmix" if has_vmix else "rwkv",
    )(*args)
    return (res[0], vfirst) if has_vmix else (res[0], res[1])


def _outproj_kernel(ya_ref, az_ref, yb_ref, gatt_ref, grw_ref, x_ref,
                    wua_ref, wub_ref, wout_ref, g_ref, o_ref):
    az = az_ref[...].astype(F32)
    ya = ya_ref[...].astype(F32) * (az * _sigmoid(az))
    u = (_sigmoid(gatt_ref[...].astype(F32)) * _mm(ya, wua_ref[...])
         + _sigmoid(grw_ref[...].astype(F32)) * _mm(yb_ref[...], wub_ref[...]))
    y = _mm(u, wout_ref[...])
    ms = jnp.mean(y * y, axis=-1, keepdims=True)
    o_ref[...] = x_ref[...] + y * lax.rsqrt(ms + RMS_EPS) * g_ref[...]


def _outproj(ya, proj2d, yb, x2d, w_up_att, w_up_rw, w_out, g, *, tm=1024):
    t, d = x2d.shape
    row = lambda width, col: pl.BlockSpec((pl.Element(tm), pl.Element(width)),
                                          lambda i: (pl.multiple_of(i * tm, tm), col))
    const = lambda shape: pl.BlockSpec(shape, lambda i: (0, 0))
    return pl.pallas_call(
        _outproj_kernel,
        grid=(t // tm,),
        in_specs=[row(WIDTH, 0), row(WIDTH, COL_AZ), row(WIDTH, 0), row(d, COL_GATT), row(d, COL_GRW),
                  row(d, 0), const((WIDTH, d)), const((WIDTH, d)), const((d, d)), const((1, d))],
        out_specs=row(d, 0),
        out_shape=jax.ShapeDtypeStruct((t, d), F32),
        compiler_params=pltpu.CompilerParams(
            dimension_semantics=("parallel",), vmem_limit_bytes=VMEM_LIMIT),
        name="outproj",
    )(ya, proj2d, yb, proj2d, proj2d, x2d, w_up_att, w_up_rw, w_out, g.reshape(1, d))


def _layer_params(l, d, rw_mu, rw_w0, rw_w_up, rw_a0, rw_a_up, rw_k_k, rw_k_a, rw_r_k,
                  rw_ln_g, rw_ln_b, rw_vmix_down, rw_vmix_mu, rw_vmix_up, rw_vmix0):
    pad = LANES - VMIX_LORA
    vd = jnp.pad(rw_vmix_down[l - 1], ((0, 0), (0, pad))) if l > 0 else jnp.zeros((d, LANES), F32)

    mu = rw_mu[l]
    lora_w = jnp.concatenate([jnp.concatenate([rw_w_up[l], jnp.zeros((DECAY_LORA, WIDTH), F32)], axis=1),
                              jnp.concatenate([jnp.zeros((ICLR_LORA, WIDTH), F32), rw_a_up[l]], axis=1)],
                             axis=0).astype(BF16)
    p = {
        "w_vd": vd.astype(BF16),
        "mu_rkv": mu[:3 * WIDTH].reshape(3, WIDTH),
        "mu_lora": mu[3 * WIDTH:].reshape(1, LANES),
        "w0a0": jnp.concatenate([rw_w0[l], rw_a0[l]]).reshape(1, 2 * WIDTH),
        "lora_w": lora_w,
        "k_k": rw_k_k[l].reshape(1, WIDTH), "k_a": rw_k_a[l].reshape(1, WIDTH),
        "r_k": rw_r_k[l].reshape(1, WIDTH),
        "ln_g": rw_ln_g[l].reshape(1, WIDTH), "ln_b": rw_ln_b[l].reshape(1, WIDTH),
    }
    if l > 0:
        p["vmix_mu"] = jnp.pad(rw_vmix_mu[l - 1], (0, pad)).reshape(1, LANES)
        p["vmix_up"] = jnp.pad(rw_vmix_up[l - 1], ((0, pad), (0, 0))).astype(BF16)
        p["vmix0"] = rw_vmix0[l - 1].reshape(1, WIDTH)
    return p


def kernel(x, norm_pre, norm_post, w_in, rw_mu, rw_w0, rw_w_up, rw_a0, rw_a_up, rw_k_k, rw_k_a, rw_r_k, rw_ln_g, rw_ln_b, rw_vmix_down, rw_vmix_mu, rw_vmix_up, rw_vmix0, w_up_att, w_up_rw, w_out):
    b, s, d = x.shape
    assert d == D_MODEL and s % MOBA_BLOCK == 0
    depth = w_in.shape[0]
    x2d = x.reshape(b * s, d)
    vfirst = None
    for l in range(depth):
        p = _layer_params(l, d, rw_mu, rw_w0, rw_w_up, rw_a0, rw_a_up, rw_k_k, rw_k_a, rw_r_k,
                          rw_ln_g, rw_ln_b, rw_vmix_down, rw_vmix_mu, rw_vmix_up, rw_vmix0)
        proj2d = _inproj(x2d, norm_pre[l], w_in, l, p["w_vd"])
        proj = proj2d.reshape(b, s, PROJ_W)
        ya = _moba(proj)
        yb, vfirst = _rwkv(proj, vfirst, p)
        x2d = _outproj(ya.reshape(b * s, WIDTH), proj2d, yb.reshape(b * s, WIDTH), x2d,
                       w_up_att[l].astype(BF16), w_up_rw[l].astype(BF16), w_out[l].astype(BF16),
                       norm_post[l])
    return x2d.reshape(b, s, d)
```

```python
import functools

import jax
import jax.numpy as jnp
from jax import lax
from jax.experimental import pallas as pl
from jax.experimental.pallas import tpu as pltpu

F32 = jnp.float32
BF16 = jnp.bfloat16
ACT = BF16

D_MODEL = 1024
N_HEADS = 8
HEAD_DIM = 64
WIDTH = N_HEADS * HEAD_DIM
MOBA_BLOCK = 256
MOBA_TOPK = 3
DECAY_LORA = 64
ICLR_LORA = 64
VMIX_LORA = 32
RMS_EPS = 1e-6
GN_EPS = 64e-5
L2_EPS = 1e-12
NEG_INF = -1e30

LANES = 128
PAIR = 2 * HEAD_DIM
N_PAIRS = WIDTH // PAIR
V_ROWS = HEAD_DIM + 16
MOBA_PAIRS_PER_STEP = 4
CHUNK = 64

COL_Q, COL_K, COL_V, COL_AZ = 0, 512, 1024, 1536
COL_R, COL_RK, COL_RV = 2048, 2560, 3072
COL_LORA = 3584
COL_RZ = 3712
COL_GATT, COL_GRW = 4224, 5248
COL_VD = 6272
PROJ_W = 6400

VMEM_LIMIT = 48 * 1024 * 1024
VMEM_LIMIT_INPROJ = 56 * 1024 * 1024
LOG2E = 1.4426950408889634


def _sigmoid(x):
    return 0.5 * jnp.tanh(0.5 * x) + 0.5


def _mm(a, b):
    return jnp.dot(a.astype(BF16), b.astype(BF16), preferred_element_type=F32)


def _mm_nt(a, b):
    return lax.dot_general(a.astype(BF16), b.astype(BF16), (((1,), (1,)), ((), ())),
                           preferred_element_type=F32)


def _mm_tn(a, b):
    return lax.dot_general(a.astype(BF16), b.astype(BF16), (((0,), (0,)), ((), ())),
                           preferred_element_type=F32)


def _inproj_kernel(x_ref, g_ref, w_hbm, wvd_ref, o_ref, w_ref, wbuf_ref, sem_ref, *, tn, layer):
    x = x_ref[...]
    ms = jnp.mean(x * x, axis=-1, keepdims=True)
    h = (x * lax.rsqrt(ms + RMS_EPS) * g_ref[...]).astype(BF16)
    n_w = w_ref.shape[1]
    n_main = n_w - wvd_ref.shape[1]

    def matmuls(need):
        for n0 in range(0, n_main, tn):
            n1 = min(n0 + tn, n_main)
            need(n1)
            o_ref[:, n0:n1] = jnp.dot(h, w_ref[:, n0:n1], preferred_element_type=F32).astype(o_ref.dtype)
        need(n_w)
        w_tail = jnp.concatenate([w_ref[:, n_main:], wvd_ref[...]], axis=1)
        o_ref[:, n_main:] = jnp.dot(h, w_tail, preferred_element_type=F32).astype(o_ref.dtype)

    @pl.when(pl.program_id(0) == 0)
    def _():
        n_slots, wc = wbuf_ref.shape[0], wbuf_ref.shape[2]
        n_chunks = -(-n_w // wc)
        width = lambda c: min(wc, n_w - c * wc)

        def fetch(c):
            return pltpu.make_async_copy(w_hbm.at[layer, :, pl.ds(c * wc, width(c))],
                                         wbuf_ref.at[c % n_slots, :, pl.ds(0, width(c))],
                                         sem_ref.at[c % n_slots])

        for c in range(min(n_slots - 1, n_chunks)):
            fetch(c).start()
        done = [0]

        def need(n1):
            while done[0] * wc < n1:
                c = done[0]
                if c + n_slots - 1 < n_chunks:
                    fetch(c + n_slots - 1).start()
                fetch(c).wait()
                w_ref[:, c * wc:c * wc + width(c)] = wbuf_ref[c % n_slots, :, :width(c)].astype(BF16)
                done[0] += 1

        matmuls(need)

    @pl.when(pl.program_id(0) != 0)
    def _():
        matmuls(lambda n1: None)


def _inproj(x2d, g, w_all, layer, w_vd, *, tm=1024, tn=1280, wc=256, n_slots=3):
    t, d = x2d.shape
    n_w = w_all.shape[2]
    n = n_w + w_vd.shape[1]
    return pl.pallas_call(
        functools.partial(_inproj_kernel, tn=tn, layer=layer),
        grid=(t // tm,),
        in_specs=[
            pl.BlockSpec((tm, d), lambda i: (i, 0)),
            pl.BlockSpec((1, d), lambda i: (0, 0)),
            pl.BlockSpec(memory_space=pl.ANY),
            pl.BlockSpec((d, w_vd.shape[1]), lambda i: (0, 0), pipeline_mode=pl.Buffered(1)),
        ],
        out_specs=pl.BlockSpec((tm, n), lambda i: (i, 0)),
        out_shape=jax.ShapeDtypeStruct((t, n), ACT),
        scratch_shapes=[pltpu.VMEM((d, n_w), BF16), pltpu.VMEM((n_slots, d, wc), w_all.dtype),
                        pltpu.SemaphoreType.DMA((n_slots,))],
        compiler_params=pltpu.CompilerParams(
            dimension_semantics=("arbitrary",), vmem_limit_bytes=VMEM_LIMIT_INPROJ),
        name="inproj",
    )(x2d, g.reshape(1, d), w_all, w_vd)


def _moba_kernel(q_ref, k_ref, v_ref, o_ref, vt_ref, qx_ref, kx_ref, s_ref, p_ref, *, n_blocks, n_pairs):
    hp0 = pl.program_id(1) * n_pairs
    blk = MOBA_BLOCK
    s_len = n_blocks * blk
    q_scale = (HEAD_DIM ** -0.5) * LOG2E
    nt = (((1,), (1,)), ((), ()))
    sub = 32
    tk = MOBA_BLOCK

    lane = lax.broadcasted_iota(jnp.int32, (1, PAIR), 1)
    head_lanes = (lane < HEAD_DIM, lane >= HEAD_DIM)
    nrow = lax.broadcasted_iota(jnp.int32, (n_blocks, s_len), 0)
    pos_i = lax.broadcasted_iota(jnp.int32, (n_blocks, s_len), 1)
    pblk = pos_i // blk
    pos = pos_i.astype(F32)
    past = nrow < pblk
    krow = lax.broadcasted_iota(jnp.int32, (blk, blk), 0)
    qcol = lax.broadcasted_iota(jnp.int32, (blk, blk), 1)
    causal = qcol >= krow

    def split3(x):
        hi = x.astype(BF16).astype(F32)
        mid = (x - hi).astype(BF16).astype(F32)
        return hi, mid, x - hi - mid

    def rows8(r0, r1, r2, r3, r4, r5):
        out = jnp.zeros((n_blocks, s_len), F32)
        for idx, r in enumerate((r0, r1, r2, r3, r4, r5)):
            out = jnp.where(nrow == idx, r, out)
        return out

    ones = jnp.ones((n_blocks, s_len), F32)
    ones_rows = jnp.where(lax.broadcasted_iota(jnp.int32, (V_ROWS - HEAD_DIM, s_len), 0) == 0, 1.0, 0.0).astype(BF16)
    mean_row = lax.broadcasted_iota(jnp.int32, (n_blocks, PAIR), 0)

    for g in range(n_pairs):
        cols = slice(g * PAIR, (g + 1) * PAIR)
        vt = v_ref[:, cols].T
        for h in range(2):
            vt_ref[g, h] = jnp.concatenate([vt[h * HEAD_DIM:(h + 1) * HEAD_DIM], ones_rows], axis=0)

        kmean = jnp.zeros((n_blocks, PAIR), F32)
        for n in range(n_blocks):
            kmean = jnp.where(mean_row == n,
                              jnp.mean(k_ref[n * blk:(n + 1) * blk, cols].astype(F32), axis=0, keepdims=True), kmean)

        km = jnp.concatenate([jnp.where(head_lanes[0], kmean, 0.0), jnp.where(head_lanes[1], kmean, 0.0)], axis=0)
        km_hi = km.astype(BF16)
        km_r = km - km_hi.astype(F32)
        km_mid = km_r.astype(BF16)
        km_lo = (km_r - km_mid.astype(F32)).astype(BF16)
        parts = lax.dot_general(jnp.concatenate([km_lo, km_mid, km_hi], axis=0), q_ref[:, cols], nt,
                                preferred_element_type=F32)
        ng = 2 * n_blocks
        gates = parts[:ng] + parts[ng:2 * ng] + parts[2 * ng:]

        q_extra, k_extra = [], []
        for h in range(2):
            head = lax.convert_element_type((hp0 + g) * 2 + h + 1, F32)
            slope2 = jnp.exp2(jnp.zeros((1, 1), F32) - (8.0 / N_HEADS) * head) * LOG2E

            gm = jnp.where(past, gates[h * n_blocks:(h + 1) * n_blocks], NEG_INF)
            rank = jnp.zeros((n_blocks, s_len), F32)
            for m in range(n_blocks):
                gcol = gm[m:m + 1, :]
                beats = (gcol > gm) | ((gcol == gm) & (nrow > m))
                rank = rank + jnp.where(beats, 1.0, 0.0)
            visible = (past & (rank < MOBA_TOPK)) | (nrow == pblk)

            q_hi, q_mid, q_lo = split3(-slope2 * pos)
            k_hi, k_mid, k_lo = split3(slope2 * pos)
            q_extra.append(jnp.concatenate([rows8(q_hi, q_mid, q_lo, ones, ones, ones),
                                            jnp.where(visible, 0.0, NEG_INF)], axis=0))
            k_extra.append(jnp.concatenate([rows8(ones, ones, ones, k_hi, k_mid, k_lo),
                                            jnp.where(nrow == pblk, 1.0, 0.0)], axis=0))
        gap = jnp.zeros((HEAD_DIM - 2 * n_blocks, s_len), F32)
        qx_ref[g] = jnp.concatenate([q_extra[1], gap, q_extra[0], gap], axis=0).T.astype(BF16)
        k_x = jnp.concatenate([k_extra[1], gap, k_extra[0], gap], axis=0).T.astype(BF16)
        for h in range(2):
            kx_ref[g, h] = jnp.where(head_lanes[h], k_ref[:, cols], k_x)

    units = [(g, i, h) for g in range(n_pairs) for i in range(n_blocks) for h in range(2)]
    st = {}

    def score_tile(u, t):
        g, i, h = u
        d = st[u]
        k0 = t * tk
        s_tile = lax.dot_general(kx_ref[g, h, k0:k0 + tk, :], d["qm"], nt,
                                 preferred_element_type=F32)
        for c in range(0, tk, sub):
            sc = s_tile[c:c + sub]
            if k0 // blk == i:
                r0 = k0 % blk + c
                sc = jnp.where(causal[r0:r0 + sub], sc, NEG_INF)
            s_ref[d["slot"], k0 + c:k0 + c + sub, :] = sc
            for r in range(0, sub, 8):
                d["m8"] = sc[r:r + 8] if d["m8"] is None else jnp.maximum(d["m8"], sc[r:r + 8])

    def softmax_tile(u, t):
        d = st[u]
        for c in range(t * tk, (t + 1) * tk, sub):
            pc = jnp.exp2(s_ref[d["slot"], c:c + sub, :] - d["m"])
            p_ref[d["slot"], c:c + sub, :] = pc.astype(BF16)

    def finish(u):
        g, i, h = u
        d = st[u]
        n_k = (i + 1) * blk
        pv = jnp.dot(vt_ref[g, h, :, :n_k], p_ref[d["slot"], :n_k, :],
                     preferred_element_type=F32)
        d["out"] = pv[:HEAD_DIM] * (1.0 / pv[HEAD_DIM:HEAD_DIM + 1])
        if h == 1:
            out_t = jnp.concatenate([st[(g, i, 0)]["out"], d["out"]], axis=0)
            o_ref[i * blk:(i + 1) * blk, g * PAIR:(g + 1) * PAIR] = out_t.T.astype(o_ref.dtype)

    prev = None
    for idx, u in enumerate(units):
        g, i, h = u
        rows = slice(i * blk, (i + 1) * blk)
        if h == 0:
            q_i = (q_ref[rows, g * PAIR:(g + 1) * PAIR].astype(F32) * q_scale).astype(BF16)
        st[u] = {"qm": jnp.where(head_lanes[h], q_i, qx_ref[g, rows, :]), "slot": idx % 2,
                 "m8": None}
        n_cur = (i + 1) * blk // tk
        n_prev = (prev[1] + 1) * blk // tk if prev is not None else 0
        for step in range(max(n_cur, n_prev)):
            if step < n_cur:
                score_tile(u, step)
            if step < n_prev:
                softmax_tile(prev, step)
        if prev is not None:
            finish(prev)
        st[u]["m"] = jnp.max(st[u]["m8"], axis=0, keepdims=True)
        prev = u
    for step in range((prev[1] + 1) * blk // tk):
        softmax_tile(prev, step)
    finish(prev)


def _moba(proj):
    b, s, _ = proj.shape
    n_blocks = s // MOBA_BLOCK
    assert n_blocks == 8, "the extra-column layout of the MoBA kernel is laid out for 8 key blocks"
    n_pairs = MOBA_PAIRS_PER_STEP
    width = n_pairs * PAIR
    qo, ko, vo = COL_Q // width, COL_K // width, COL_V // width
    return pl.pallas_call(
        functools.partial(_moba_kernel, n_blocks=n_blocks, n_pairs=n_pairs),
        grid=(b, N_PAIRS // n_pairs),
        in_specs=[
            pl.BlockSpec((None, s, width), lambda i, p: (i, 0, qo + p)),
            pl.BlockSpec((None, s, width), lambda i, p: (i, 0, ko + p)),
            pl.BlockSpec((None, s, width), lambda i, p: (i, 0, vo + p)),
        ],
        out_specs=pl.BlockSpec((None, s, width), lambda i, p: (i, 0, p)),
        out_shape=jax.ShapeDtypeStruct((b, s, WIDTH), ACT),
        scratch_shapes=[
            pltpu.VMEM((n_pairs, 2, V_ROWS, s), BF16),
            pltpu.VMEM((n_pairs, s, PAIR), BF16),
            pltpu.VMEM((n_pairs, 2, s, PAIR), BF16),
            pltpu.VMEM((2, s, MOBA_BLOCK), F32),
            pltpu.VMEM((2, s, MOBA_BLOCK), BF16),
        ],
        compiler_params=pltpu.CompilerParams(
            dimension_semantics=("parallel", "parallel"), vmem_limit_bytes=VMEM_LIMIT),
        name="moba",
    )(proj, proj, proj)


def _rwkv_kernel(*refs, has_vmix, ts):
    if has_vmix:
        (r_ref, k_ref, v_ref, rz_ref, lora_ref, vd_ref, vfirst_ref,
         mu_rkv_ref, mu_lora_ref, w0a0_ref, lora_w_ref, kk_ref, ka_ref, rk_ref, lng_ref, lnb_ref,
         vmix_mu_ref, vmix_up_ref, vmix0_ref,
         yb_ref, carry_rkv, carry_lora, carry_vd, state_ref) = refs
    else:
        (r_ref, k_ref, v_ref, rz_ref, lora_ref,
         mu_rkv_ref, mu_lora_ref, w0a0_ref, lora_w_ref, kk_ref, ka_ref, rk_ref, lng_ref, lnb_ref,
         yb_ref, vfirst_ref, carry_rkv, carry_lora, state_ref) = refs
        carry_vd = None

    @pl.when(pl.program_id(1) == 0)
    def _():
        carry_rkv[...] = jnp.zeros_like(carry_rkv)
        carry_lora[...] = jnp.zeros_like(carry_lora)
        if carry_vd is not None:
            carry_vd[...] = jnp.zeros_like(carry_vd)
        state_ref[...] = jnp.zeros_like(state_ref)

    sub = ts
    n_chunks = sub // CHUNK
    n2 = 2 * CHUNK
    pcs = [(p, c) for c in range(n_chunks) for p in range(N_PAIRS)]

    sub_row = lax.broadcasted_iota(jnp.int32, (sub, sub), 0)
    sub_col = lax.broadcasted_iota(jnp.int32, (sub, sub), 1)
    diff_mat = (jnp.where(sub_row == sub_col + 1, 1.0, 0.0)
                - jnp.where(sub_row == sub_col, 1.0, 0.0)).astype(r_ref.dtype)
    ltri = jnp.where((sub_row // CHUNK == sub_col // CHUNK) & (sub_col <= sub_row), 1.0, 0.0).astype(BF16)
    top_row = lax.broadcasted_iota(jnp.int32, (8, 1), 0) == 0
    lane = lax.broadcasted_iota(jnp.int32, (sub, LANES), 1)
    head0_t = lane < HEAD_DIM
    head0 = lax.broadcasted_iota(jnp.int32, (CHUNK, PAIR), 1) < HEAD_DIM
    wrow = lax.broadcasted_iota(jnp.int32, (CHUNK, PAIR), 0)
    wcol = lax.broadcasted_iota(jnp.int32, (CHUNK, PAIR), 1) % CHUNK
    strict = wcol < wrow
    incl = wcol <= wrow
    eye = jnp.where(wcol == wrow, 1.0, 0.0).astype(F32)
    srow = lax.broadcasted_iota(jnp.int32, (n2, n2), 0)
    scol = lax.broadcasted_iota(jnp.int32, (n2, n2), 1)
    same_head = (srow // CHUNK) == (scol // CHUNK)
    mu_rkv = mu_rkv_ref[...]

    def shift(y_act, carry_ref, row_idx, mu):
        y = y_act.astype(F32)
        delta = jnp.dot(diff_mat, y_act, preferred_element_type=F32)
        head = jnp.where(top_row, delta[:8] + carry_ref[row_idx:row_idx + 1, :], delta[:8])
        delta = jnp.concatenate([head, delta[8:]], axis=0)
        carry_ref[row_idx:row_idx + 1, :] = y[sub - 1:sub, :]
        return y + delta * mu

    def headsum(x):
        s0 = jnp.sum(jnp.where(head0_t, x, 0.0), axis=-1, keepdims=True)
        s1 = jnp.sum(jnp.where(head0_t, 0.0, x), axis=-1, keepdims=True)
        return jnp.where(head0_t, s0, s1)

    def stack(x):
        zero = jnp.zeros_like(x)
        return jnp.concatenate([jnp.where(head0, x, zero), jnp.where(head0, zero, x)], axis=0)

    d = {}

    def prep():
        rs = shift(r_ref[...], carry_rkv, 0, mu_rkv[0:1, :])
        ks = shift(k_ref[...], carry_rkv, 1, mu_rkv[1:2, :])
        vs = shift(v_ref[...], carry_rkv, 2, mu_rkv[2:3, :])
        lo = shift(lora_ref[...], carry_lora, 0, mu_lora_ref[...])
        z = jnp.where(lane < DECAY_LORA, jnp.tanh(lo), lo)
        wa = w0a0_ref[...] + _mm(z, lora_w_ref[...])
        lw = (-0.6065306597126334 * LOG2E) * _sigmoid(wa[:, :WIDTH])
        a = _sigmoid(wa[:, WIDTH:])
        if has_vmix:
            vd = shift(vd_ref[...], carry_vd, 0, vmix_mu_ref[...])
            mix = _sigmoid(vmix0_ref[...] + _mm(vd, vmix_up_ref[...]))
            vr = vs + (vfirst_ref[...] - vs) * mix
        else:
            vfirst_ref[...] = vs
            vr = vs
        kx = ks * kk_ref[...]
        kmod = ks * (1.0 + (a - 1.0) * ka_ref[...])
        d["rkr"] = rs * kmod * rk_ref[...]
        d["vr"] = vr
        kk_all = []
        for p in range(N_PAIRS):
            kxp = kx[:, p * PAIR:(p + 1) * PAIR]
            kk_all.append(kxp * jnp.minimum(lax.rsqrt(headsum(kxp * kxp)), 1.0 / L2_EPS))
        kk = jnp.concatenate(kk_all, axis=1)
        na = -kk
        nb = kk * a
        lw_hi = lw.astype(BF16)
        lw_r = lw - lw_hi.astype(F32)
        lw_mid = lw_r.astype(BF16)
        lw_lo = (lw_r - lw_mid.astype(F32)).astype(BF16)
        cum = (jnp.dot(ltri, lw_lo, preferred_element_type=F32) + jnp.dot(ltri, lw_mid, preferred_element_type=F32)
               + jnp.dot(ltri, lw_hi, preferred_element_type=F32))
        a_t = na * jnp.exp2(cum - lw)
        r_t = rs * jnp.exp2(cum)
        e_neg = jnp.exp2(-cum)
        b_t = nb * e_neg
        k_t = kmod * e_neg
        lhs_g, rhs_g, a_s, r_n, b_h, k_h, v_s, gam = {}, {}, {}, {}, {}, {}, {}, {}
        for c in range(n_chunks):
            rc = slice(c * CHUNK, (c + 1) * CHUNK)
            tot = cum[(c + 1) * CHUNK - 1:(c + 1) * CHUNK, :]
            e_rel = jnp.exp2(tot - cum[rc])
            bh_c = nb[rc] * e_rel
            kh_c = kmod[rc] * e_rel
            gam_c = jnp.exp2(tot)
            for p in range(N_PAIRS):
                sl = slice(p * PAIR, (p + 1) * PAIR)
                pc = (p, c)
                a_n = a_t[rc, sl].astype(BF16)
                r_n[pc] = r_t[rc, sl]
                lhs_g[pc] = jnp.concatenate([a_n, r_n[pc].astype(BF16)], axis=0)
                rhs_g[pc] = jnp.concatenate([stack(b_t[rc, sl].astype(BF16)),
                                             stack(k_t[rc, sl].astype(BF16))], axis=0)
                a_s[pc] = stack(a_n)
                b_h[pc] = bh_c[:, sl].astype(BF16)
                k_h[pc] = stack(kh_c[:, sl])
                v_s[pc] = stack(vr[rc, sl].astype(BF16))
                gam[pc] = gam_c[:, sl]
        d.update(lhs_g=lhs_g, rhs_g=rhs_g, a_s=a_s, r_n=r_n, b_h=b_h, k_h=k_h, v_s=v_s, gam=gam)

    def operators():
        a_ab, a_ak, a_rb, a_rk = {}, {}, {}, {}
        for pc in pcs:
            g = _mm_nt(d["lhs_g"][pc], d["rhs_g"][pc])
            a_ab[pc] = jnp.where(strict, g[:CHUNK, :n2], 0.0)
            a_ak[pc] = jnp.where(strict, g[:CHUNK, n2:], 0.0).astype(BF16)
            a_rb[pc] = jnp.where(incl, g[CHUNK:, :n2], 0.0).astype(BF16)
            a_rk[pc] = jnp.where(incl, g[CHUNK:, n2:], 0.0)
        t_inv = {pc: eye + a_ab[pc] for pc in pcs}
        pw = {pc: a_ab[pc].astype(BF16) for pc in pcs}
        pw = {pc: _mm(pw[pc], stack(pw[pc])).astype(BF16) for pc in pcs}
        for _ in range(CHUNK.bit_length() - 3):
            both = {pc: _mm(jnp.concatenate([pw[pc], t_inv[pc].astype(BF16)], axis=0), stack(pw[pc])) for pc in pcs}
            t_inv = {pc: t_inv[pc] + both[pc][CHUNK:] for pc in pcs}
            pw = {pc: both[pc][:CHUNK].astype(BF16) for pc in pcs}
        t_inv = {pc: (t_inv[pc] + _mm(t_inv[pc], stack(pw[pc]))).astype(BF16) for pc in pcs}
        z = {pc: jnp.where(same_head, _mm_tn(t_inv[pc], d["b_h"][pc]), 0.0).astype(BF16) for pc in pcs}
        w = {pc: _mm(a_rb[pc], stack(t_inv[pc])).astype(BF16) for pc in pcs}
        la_ak = {pc: jnp.concatenate([d["a_s"][pc], stack(a_ak[pc])], axis=1) for pc in pcs}
        my = {pc: _mm_tn(la_ak[pc], z[pc]) for pc in pcs}
        wq = {pc: _mm(w[pc], la_ak[pc]) for pc in pcs}
        d["m_t"] = {pc: my[pc][:n2].astype(BF16) for pc in pcs}
        y = {pc: (my[pc][n2:] + d["k_h"][pc]).astype(BF16) for pc in pcs}
        d["q_p"] = {pc: (d["r_n"][pc] + wq[pc][:, :n2]).astype(BF16) for pc in pcs}
        p_p = {pc: (wq[pc][:, n2:] + a_rk[pc]).astype(BF16) for pc in pcs}
        d["g_s"] = {pc: _mm_tn(d["v_s"][pc], y[pc]) for pc in pcs}
        d["o_loc"] = {pc: _mm(p_p[pc], d["v_s"][pc]) for pc in pcs}

    def recurrence():
        states = [state_ref[p] for p in range(N_PAIRS)]
        outs = [[] for _ in range(N_PAIRS)]
        for c in range(n_chunks):
            for p in range(N_PAIRS):
                pc = (p, c)
                sb = states[p].astype(BF16)
                outs[p].append(_mm_nt(d["q_p"][pc], sb) + d["o_loc"][pc])
                states[p] = states[p] * d["gam"][pc] + _mm(sb, d["m_t"][pc]) + d["g_s"][pc]
        for p in range(N_PAIRS):
            state_ref[p] = states[p]
        d["o"] = [jnp.concatenate(o, axis=0) for o in outs]

    def epilogue():
        for p in range(N_PAIRS):
            sl = slice(p * PAIR, (p + 1) * PAIR)
            o_p = d["o"][p]
            bonus = headsum(d["rkr"][:, sl]) * d["vr"][:, sl]
            mu = headsum(o_p) * (1.0 / HEAD_DIM)
            dev = o_p - mu
            var = headsum(dev * dev) * (1.0 / HEAD_DIM)
            on = dev * lax.rsqrt(var + GN_EPS) * lng_ref[:, sl] + lnb_ref[:, sl]
            rz_p = rz_ref[:, sl].astype(F32)
            yb_ref[:, sl] = ((on + bonus) * (rz_p * _sigmoid(rz_p))).astype(yb_ref.dtype)

    prep()
    operators()
    recurrence()
    epilogue()


def _rwkv(proj, vfirst, p, *, ts=256):
    b, s, _ = proj.shape
    has_vmix = vfirst is not None
    proj2d = proj.reshape(b * s, PROJ_W)
    row = lambda width, col: pl.BlockSpec((pl.Element(ts), pl.Element(width)),
                                          lambda i, t: (pl.multiple_of(i * s + t * ts, ts), col))
    const = lambda shape: pl.BlockSpec(shape, lambda i, t: (0,) * len(shape))
    act = pl.BlockSpec((None, ts, WIDTH), lambda i, t: (i, t, 0))

    in_specs = [row(WIDTH, COL_R), row(WIDTH, COL_RK), row(WIDTH, COL_RV), row(WIDTH, COL_RZ),
                row(LANES, COL_LORA)]
    args = [proj2d, proj2d, proj2d, proj2d, proj2d]
    if has_vmix:
        in_specs += [row(LANES, COL_VD), act]
        args += [proj2d, vfirst]
    in_specs += [const((3, WIDTH)), const((1, LANES)), const((1, 2 * WIDTH)), const((LANES, 2 * WIDTH))]
    args += [p["mu_rkv"], p["mu_lora"], p["w0a0"], p["lora_w"]]
    for name in ("k_k", "k_a", "r_k", "ln_g", "ln_b"):
        in_specs.append(const((1, WIDTH)))
        args.append(p[name])
    if has_vmix:
        in_specs += [const((1, LANES)), const((LANES, WIDTH)), const((1, WIDTH))]
        args += [p["vmix_mu"], p["vmix_up"], p["vmix0"]]

    out_shape = [jax.ShapeDtypeStruct((b, s, WIDTH), ACT)]
    out_specs = [act]
    scratch = [pltpu.VMEM((8, WIDTH), F32), pltpu.VMEM((8, LANES), F32)]
    if has_vmix:
        scratch.append(pltpu.VMEM((8, LANES), F32))
    else:
        out_shape.append(jax.ShapeDtypeStruct((b, s, WIDTH), F32))
        out_specs.append(act)
    scratch.append(pltpu.VMEM((N_PAIRS, PAIR, PAIR), F32))

    res = pl.pallas_call(
        functools.partial(_rwkv_kernel, has_vmix=has_vmix, ts=ts),
        grid=(b, s // ts),
        in_specs=in_specs,
        out_specs=out_specs,
        out_shape=out_shape,
        scratch_shapes=scratch,
        compiler_params=pltpu.CompilerParams(
            dimension_semantics=("parallel", "arbitrary"), vmem_limit_bytes=VMEM_LIMIT),
        name="rwkv_vmix" if has_vmix else "rwkv",
    )(*args)
    return (res[0], vfirst) if has_vmix else (res[0], res[1])


def _outproj_kernel(ya_ref, az_ref, yb_ref, gatt_ref, grw_ref, x_ref,
                    wua_ref, wub_ref, wout_ref, g_ref, o_ref):
    az = az_ref[...].astype(F32)
    ya = ya_ref[...].astype(F32) * (az * _sigmoid(az))
    u = (_sigmoid(gatt_ref[...].astype(F32)) * _mm(ya, wua_ref[...])
         + _sigmoid(grw_ref[...].astype(F32)) * _mm(yb_ref[...], wub_ref[...]))
    y = _mm(u, wout_ref[...])
    ms = jnp.mean(y * y, axis=-1, keepdims=True)
    o_ref[...] = x_ref[...] + y * lax.rsqrt(ms + RMS_EPS) * g_ref[...]


def _outproj(ya, proj2d, yb, x2d, w_up_att, w_up_rw, w_out, g, *, tm=1024):
    t, d = x2d.shape
    row = lambda width, col: pl.BlockSpec((pl.Element(tm), pl.Element(width)),
                                          lambda i: (pl.multiple_of(i * tm, tm), col))
    const = lambda shape: pl.BlockSpec(shape, lambda i: (0, 0))
    return pl.pallas_call(
        _outproj_kernel,
        grid=(t // tm,),
        in_specs=[row(WIDTH, 0), row(WIDTH, COL_AZ), row(WIDTH, 0), row(d, COL_GATT), row(d, COL_GRW),
                  row(d, 0), const((WIDTH, d)), const((WIDTH, d)), const((d, d)), const((1, d))],
        out_specs=row(d, 0),
        out_shape=jax.ShapeDtypeStruct((t, d), F32),
        compiler_params=pltpu.CompilerParams(
            dimension_semantics=("parallel",), vmem_limit_bytes=VMEM_LIMIT),
        name="outproj",
    )(ya, proj2d, yb, proj2d, proj2d, x2d, w_up_att, w_up_rw, w_out, g.reshape(1, d))


def _layer_params(l, d, rw_mu, rw_w0, rw_w_up, rw_a0, rw_a_up, rw_k_k, rw_k_a, rw_r_k,
                  rw_ln_g, rw_ln_b, rw_vmix_down, rw_vmix_mu, rw_vmix_up, rw_vmix0):
    pad = LANES - VMIX_LORA
    vd = jnp.pad(rw_vmix_down[l - 1], ((0, 0), (0, pad))) if l > 0 else jnp.zeros((d, LANES), F32)

    mu = rw_mu[l]
    lora_w = jnp.concatenate([jnp.concatenate([rw_w_up[l], jnp.zeros((DECAY_LORA, WIDTH), F32)], axis=1),
                              jnp.concatenate([jnp.zeros((ICLR_LORA, WIDTH), F32), rw_a_up[l]], axis=1)],
                             axis=0).astype(BF16)
    p = {
        "w_vd": vd.astype(BF16),
        "mu_rkv": mu[:3 * WIDTH].reshape(3, WIDTH),
        "mu_lora": mu[3 * WIDTH:].reshape(1, LANES),
        "w0a0": jnp.concatenate([rw_w0[l], rw_a0[l]]).reshape(1, 2 * WIDTH),
        "lora_w": lora_w,
        "k_k": rw_k_k[l].reshape(1, WIDTH), "k_a": rw_k_a[l].reshape(1, WIDTH),
        "r_k": rw_r_k[l].reshape(1, WIDTH),
        "ln_g": rw_ln_g[l].reshape(1, WIDTH), "ln_b": rw_ln_b[l].reshape(1, WIDTH),
    }
    if l > 0:
        p["vmix_mu"] = jnp.pad(rw_vmix_mu[l - 1], (0, pad)).reshape(1, LANES)
        p["vmix_up"] = jnp.pad(rw_vmix_up[l - 1], ((0, pad), (0, 0))).astype(BF16)
        p["vmix0"] = rw_vmix0[l - 1].reshape(1, WIDTH)
    return p


def kernel(x, norm_pre, norm_post, w_in, rw_mu, rw_w0, rw_w_up, rw_a0, rw_a_up, rw_k_k, rw_k_a, rw_r_k, rw_ln_g, rw_ln_b, rw_vmix_down, rw_vmix_mu, rw_vmix_up, rw_vmix0, w_up_att, w_up_rw, w_out):
    b, s, d = x.shape
    assert d == D_MODEL and s % MOBA_BLOCK == 0
    depth = w_in.shape[0]
    x2d = x.reshape(b * s, d)
    vfirst = None
    for l in range(depth):
        p = _layer_params(l, d, rw_mu, rw_w0, rw_w_up, rw_a0, rw_a_up, rw_k_k, rw_k_a, rw_r_k,
                          rw_ln_g, rw_ln_b, rw_vmix_down, rw_vmix_mu, rw_vmix_up, rw_vmix0)
        proj2d = _inproj(x2d, norm_pre[l], w_in, l, p["w_vd"])
        proj = proj2d.reshape(b, s, PROJ_W)
        ya = _moba(proj)
        yb, vfirst = _rwkv(proj, vfirst, p)
        x2d = _outproj(ya.reshape(b * s, WIDTH), proj2d, yb.reshape(b * s, WIDTH), x2d,
                       w_up_att[l].astype(BF16), w_up_rw[l].astype(BF16), w_out[l].astype(BF16),
                       norm_post[l])
    return x2d.reshape(b, s, d)
```

```python
import functools

import jax
import jax.numpy as jnp
from jax import lax
from jax.experimental import pallas as pl
from jax.experimental.pallas import tpu as pltpu

F32 = jnp.float32
BF16 = jnp.bfloat16
ACT = BF16

D_MODEL = 1024
N_HEADS = 8
HEAD_DIM = 64
WIDTH = N_HEADS * HEAD_DIM
MOBA_BLOCK = 256
MOBA_TOPK = 3
DECAY_LORA = 64
ICLR_LORA = 64
VMIX_LORA = 32
RMS_EPS = 1e-6
GN_EPS = 64e-5
L2_EPS = 1e-12
NEG_INF = -1e30

LANES = 128
PAIR = 2 * HEAD_DIM
N_PAIRS = WIDTH // PAIR
V_ROWS = HEAD_DIM + 16
MOBA_PAIRS_PER_STEP = 4
CHUNK = 64

COL_Q, COL_K, COL_V, COL_AZ = 0, 512, 1024, 1536
COL_R, COL_RK, COL_RV = 2048, 2560, 3072
COL_LORA = 3584
COL_RZ = 3712
COL_GATT, COL_GRW = 4224, 5248
COL_VD = 6272
PROJ_W = 6400

VMEM_LIMIT = 48 * 1024 * 1024
VMEM_LIMIT_INPROJ = 60 * 1024 * 1024
LOG2E = 1.4426950408889634


def _sigmoid(x):
    return 0.5 * jnp.tanh(0.5 * x) + 0.5


def _mm(a, b):
    return jnp.dot(a.astype(BF16), b.astype(BF16), preferred_element_type=F32)


def _mm_nt(a, b):
    return lax.dot_general(a.astype(BF16), b.astype(BF16), (((1,), (1,)), ((), ())),
                           preferred_element_type=F32)


def _mm_tn(a, b):
    return lax.dot_general(a.astype(BF16), b.astype(BF16), (((0,), (0,)), ((), ())),
                           preferred_element_type=F32)


def _inproj_kernel(x_ref, g_ref, w_hbm, wvd_ref, o_ref, w_ref, wbuf_ref, sem_ref, *, tn, layer):
    @pl.when(pl.program_id(0) == 0)
    def _():
        n_slots, rc = wbuf_ref.shape[0], wbuf_ref.shape[1]
        n_chunks = w_ref.shape[0] // rc

        def fetch(c):
            return pltpu.make_async_copy(w_hbm.at[layer, pl.ds(c * rc, rc), :],
                                         wbuf_ref.at[c % n_slots], sem_ref.at[c % n_slots])

        for c in range(n_slots - 1):
            fetch(c).start()
        for c in range(n_chunks):
            if c + n_slots - 1 < n_chunks:
                fetch(c + n_slots - 1).start()
            fetch(c).wait()
            w_ref[c * rc:(c + 1) * rc, :] = wbuf_ref[c % n_slots].astype(BF16)

    x = x_ref[...]
    ms = jnp.mean(x * x, axis=-1, keepdims=True)
    h = (x * lax.rsqrt(ms + RMS_EPS) * g_ref[...]).astype(BF16)
    n_main = w_ref.shape[1] - wvd_ref.shape[1]
    for n0 in range(0, n_main, tn):
        n1 = min(n0 + tn, n_main)
        o_ref[:, n0:n1] = jnp.dot(h, w_ref[:, n0:n1], preferred_element_type=F32).astype(o_ref.dtype)
    w_tail = jnp.concatenate([w_ref[:, n_main:], wvd_ref[...]], axis=1)
    o_ref[:, n_main:] = jnp.dot(h, w_tail, preferred_element_type=F32).astype(o_ref.dtype)


def _inproj(x2d, g, w_all, layer, w_vd, *, tm=1024, tn=1280, rc=64, n_slots=4):
    t, d = x2d.shape
    n_w = w_all.shape[2]
    n = n_w + w_vd.shape[1]
    assert d % rc == 0
    return pl.pallas_call(
        functools.partial(_inproj_kernel, tn=tn, layer=layer),
        grid=(t // tm,),
        in_specs=[
            pl.BlockSpec((tm, d), lambda i: (i, 0)),
            pl.BlockSpec((1, d), lambda i: (0, 0)),
            pl.BlockSpec(memory_space=pl.ANY),
            pl.BlockSpec((d, w_vd.shape[1]), lambda i: (0, 0), pipeline_mode=pl.Buffered(1)),
        ],
        out_specs=pl.BlockSpec((tm, n), lambda i: (i, 0)),
        out_shape=jax.ShapeDtypeStruct((t, n), ACT),
        scratch_shapes=[pltpu.VMEM((d, n_w), BF16), pltpu.VMEM((n_slots, rc, n_w), w_all.dtype),
                        pltpu.SemaphoreType.DMA((n_slots,))],
        compiler_params=pltpu.CompilerParams(
            dimension_semantics=("arbitrary",), vmem_limit_bytes=VMEM_LIMIT_INPROJ),
        name="inproj",
    )(x2d, g.reshape(1, d), w_all, w_vd)


def _moba_kernel(q_ref, k_ref, v_ref, o_ref, vt_ref, qx_ref, kx_ref, s_ref, p_ref, *, n_blocks, n_pairs):
    hp0 = pl.program_id(1) * n_pairs
    blk = MOBA_BLOCK
    s_len = n_blocks * blk
    q_scale = (HEAD_DIM ** -0.5) * LOG2E
    nt = (((1,), (1,)), ((), ()))
    sub = 32
    tk = MOBA_BLOCK

    lane = lax.broadcasted_iota(jnp.int32, (1, PAIR), 1)
    head_lanes = (lane < HEAD_DIM, lane >= HEAD_DIM)
    nrow = lax.broadcasted_iota(jnp.int32, (n_blocks, s_len), 0)
    pos_i = lax.broadcasted_iota(jnp.int32, (n_blocks, s_len), 1)
    pblk = pos_i // blk
    pos = pos_i.astype(F32)
    past = nrow < pblk
    krow = lax.broadcasted_iota(jnp.int32, (blk, blk), 0)
    qcol = lax.broadcasted_iota(jnp.int32, (blk, blk), 1)
    causal = qcol >= krow

    def split3(x):
        hi = x.astype(BF16).astype(F32)
        mid = (x - hi).astype(BF16).astype(F32)
        return hi, mid, x - hi - mid

    def rows8(r0, r1, r2, r3, r4, r5):
        out = jnp.zeros((n_blocks, s_len), F32)
        for idx, r in enumerate((r0, r1, r2, r3, r4, r5)):
            out = jnp.where(nrow == idx, r, out)
        return out

    ones = jnp.ones((n_blocks, s_len), F32)
    ones_rows = jnp.where(lax.broadcasted_iota(jnp.int32, (V_ROWS - HEAD_DIM, s_len), 0) == 0, 1.0, 0.0).astype(BF16)
    mean_row = lax.broadcasted_iota(jnp.int32, (n_blocks, PAIR), 0)

    for g in range(n_pairs):
        cols = slice(g * PAIR, (g + 1) * PAIR)
        vt = v_ref[:, cols].T
        for h in range(2):
            vt_ref[g, h] = jnp.concatenate([vt[h * HEAD_DIM:(h + 1) * HEAD_DIM], ones_rows], axis=0)

        kmean = jnp.zeros((n_blocks, PAIR), F32)
        for n in range(n_blocks):
            kmean = jnp.where(mean_row == n,
                              jnp.mean(k_ref[n * blk:(n + 1) * blk, cols].astype(F32), axis=0, keepdims=True), kmean)

        km = jnp.concatenate([jnp.where(head_lanes[0], kmean, 0.0), jnp.where(head_lanes[1], kmean, 0.0)], axis=0)
        km_hi = km.astype(BF16)
        km_r = km - km_hi.astype(F32)
        km_mid = km_r.astype(BF16)
        km_lo = (km_r - km_mid.astype(F32)).astype(BF16)
        parts = lax.dot_general(jnp.concatenate([km_lo, km_mid, km_hi], axis=0), q_ref[:, cols], nt,
                                preferred_element_type=F32)
        ng = 2 * n_blocks
        gates = parts[:ng] + parts[ng:2 * ng] + parts[2 * ng:]

        q_extra, k_extra = [], []
        for h in range(2):
            head = lax.convert_element_type((hp0 + g) * 2 + h + 1, F32)
            slope2 = jnp.exp2(jnp.zeros((1, 1), F32) - (8.0 / N_HEADS) * head) * LOG2E

            gm = jnp.where(past, gates[h * n_blocks:(h + 1) * n_blocks], NEG_INF)
            rank = jnp.zeros((n_blocks, s_len), F32)
            for m in range(n_blocks):
                gcol = gm[m:m + 1, :]
                beats = (gcol > gm) | ((gcol == gm) & (nrow > m))
                rank = rank + jnp.where(beats, 1.0, 0.0)
            visible = (past & (rank < MOBA_TOPK)) | (nrow == pblk)

            q_hi, q_mid, q_lo = split3(-slope2 * pos)
            k_hi, k_mid, k_lo = split3(slope2 * pos)
            q_extra.append(jnp.concatenate([rows8(q_hi, q_mid, q_lo, ones, ones, ones),
                                            jnp.where(visible, 0.0, NEG_INF)], axis=0))
            k_extra.append(jnp.concatenate([rows8(ones, ones, ones, k_hi, k_mid, k_lo),
                                            jnp.where(nrow == pblk, 1.0, 0.0)], axis=0))
        gap = jnp.zeros((HEAD_DIM - 2 * n_blocks, s_len), F32)
        qx_ref[g] = jnp.concatenate([q_extra[1], gap, q_extra[0], gap], axis=0).T.astype(BF16)
        k_x = jnp.concatenate([k_extra[1], gap, k_extra[0], gap], axis=0).T.astype(BF16)
        for h in range(2):
            kx_ref[g, h] = jnp.where(head_lanes[h], k_ref[:, cols], k_x)

    units = [(g, i, h) for g in range(n_pairs) for i in range(n_blocks) for h in range(2)]
    st = {}

    def score_tile(u, t):
        g, i, h = u
        d = st[u]
        k0 = t * tk
        s_tile = lax.dot_general(kx_ref[g, h, k0:k0 + tk, :], d["qm"], nt,
                                 preferred_element_type=F32)
        for c in range(0, tk, sub):
            sc = s_tile[c:c + sub]
            if k0 // blk == i:
                r0 = k0 % blk + c
                sc = jnp.where(causal[r0:r0 + sub], sc, NEG_INF)
            s_ref[d["slot"], k0 + c:k0 + c + sub, :] = sc
            for r in range(0, sub, 8):
                d["m8"] = sc[r:r + 8] if d["m8"] is None else jnp.maximum(d["m8"], sc[r:r + 8])

    def softmax_tile(u, t):
        d = st[u]
        for c in range(t * tk, (t + 1) * tk, sub):
            pc = jnp.exp2(s_ref[d["slot"], c:c + sub, :] - d["m"])
            p_ref[d["slot"], c:c + sub, :] = pc.astype(BF16)

    def finish(u):
        g, i, h = u
        d = st[u]
        n_k = (i + 1) * blk
        pv = jnp.dot(vt_ref[g, h, :, :n_k], p_ref[d["slot"], :n_k, :],
                     preferred_element_type=F32)
        d["out"] = pv[:HEAD_DIM] * (1.0 / pv[HEAD_DIM:HEAD_DIM + 1])
        if h == 1:
            out_t = jnp.concatenate([st[(g, i, 0)]["out"], d["out"]], axis=0)
            o_ref[i * blk:(i + 1) * blk, g * PAIR:(g + 1) * PAIR] = out_t.T.astype(o_ref.dtype)

    prev = None
    for idx, u in enumerate(units):
        g, i, h = u
        rows = slice(i * blk, (i + 1) * blk)
        if h == 0:
            q_i = (q_ref[rows, g * PAIR:(g + 1) * PAIR].astype(F32) * q_scale).astype(BF16)
        st[u] = {"qm": jnp.where(head_lanes[h], q_i, qx_ref[g, rows, :]), "slot": idx % 2,
                 "m8": None}
        n_cur = (i + 1) * blk // tk
        n_prev = (prev[1] + 1) * blk // tk if prev is not None else 0
        for step in range(max(n_cur, n_prev)):
            if step < n_cur:
                score_tile(u, step)
            if step < n_prev:
                softmax_tile(prev, step)
        if prev is not None:
            finish(prev)
        st[u]["m"] = jnp.max(st[u]["m8"], axis=0, keepdims=True)
        prev = u
    for step in range((prev[1] + 1) * blk // tk):
        softmax_tile(prev, step)
    finish(prev)


def _moba(proj):
    b, s, _ = proj.shape
    n_blocks = s // MOBA_BLOCK
    assert n_blocks == 8, "the extra-column layout of the MoBA kernel is laid out for 8 key blocks"
    n_pairs = MOBA_PAIRS_PER_STEP
    width = n_pairs * PAIR
    qo, ko, vo = COL_Q // width, COL_K // width, COL_V // width
    return pl.pallas_call(
        functools.partial(_moba_kernel, n_blocks=n_blocks, n_pairs=n_pairs),
        grid=(b, N_PAIRS // n_pairs),
        in_specs=[
            pl.BlockSpec((None, s, width), lambda i, p: (i, 0, qo + p)),
            pl.BlockSpec((None, s, width), lambda i, p: (i, 0, ko + p)),
            pl.BlockSpec((None, s, width), lambda i, p: (i, 0, vo + p)),
        ],
        out_specs=pl.BlockSpec((None, s, width), lambda i, p: (i, 0, p)),
        out_shape=jax.ShapeDtypeStruct((b, s, WIDTH), ACT),
        scratch_shapes=[
            pltpu.VMEM((n_pairs, 2, V_ROWS, s), BF16),
            pltpu.VMEM((n_pairs, s, PAIR), BF16),
            pltpu.VMEM((n_pairs, 2, s, PAIR), BF16),
            pltpu.VMEM((2, s, MOBA_BLOCK), F32),
            pltpu.VMEM((2, s, MOBA_BLOCK), BF16),
        ],
        compiler_params=pltpu.CompilerParams(
            dimension_semantics=("parallel", "parallel"), vmem_limit_bytes=VMEM_LIMIT),
        name="moba",
    )(proj, proj, proj)


def _rwkv_kernel(*refs, has_vmix, ts):
    if has_vmix:
        (r_ref, k_ref, v_ref, rz_ref, lora_ref, vd_ref, vfirst_ref,
         mu_rkv_ref, mu_lora_ref, w0a0_ref, lora_w_ref, kk_ref, ka_ref, rk_ref, lng_ref, lnb_ref,
         vmix_mu_ref, vmix_up_ref, vmix0_ref,
         yb_ref, carry_rkv, carry_lora, carry_vd, state_ref) = refs
    else:
        (r_ref, k_ref, v_ref, rz_ref, lora_ref,
         mu_rkv_ref, mu_lora_ref, w0a0_ref, lora_w_ref, kk_ref, ka_ref, rk_ref, lng_ref, lnb_ref,
         yb_ref, vfirst_ref, carry_rkv, carry_lora, state_ref) = refs
        carry_vd = None

    @pl.when(pl.program_id(1) == 0)
    def _():
        carry_rkv[...] = jnp.zeros_like(carry_rkv)
        carry_lora[...] = jnp.zeros_like(carry_lora)
        if carry_vd is not None:
            carry_vd[...] = jnp.zeros_like(carry_vd)
        state_ref[...] = jnp.zeros_like(state_ref)

    sub = ts
    n_chunks = sub // CHUNK
    n2 = 2 * CHUNK
    pcs = [(p, c) for c in range(n_chunks) for p in range(N_PAIRS)]

    sub_row = lax.broadcasted_iota(jnp.int32, (sub, sub), 0)
    sub_col = lax.broadcasted_iota(jnp.int32, (sub, sub), 1)
    diff_mat = (jnp.where(sub_row == sub_col + 1, 1.0, 0.0)
                - jnp.where(sub_row == sub_col, 1.0, 0.0)).astype(r_ref.dtype)
    ltri = jnp.where((sub_row // CHUNK == sub_col // CHUNK) & (sub_col <= sub_row), 1.0, 0.0).astype(BF16)
    top_row = lax.broadcasted_iota(jnp.int32, (8, 1), 0) == 0
    lane = lax.broadcasted_iota(jnp.int32, (sub, LANES), 1)
    head0_t = lane < HEAD_DIM
    head0 = lax.broadcasted_iota(jnp.int32, (CHUNK, PAIR), 1) < HEAD_DIM
    wrow = lax.broadcasted_iota(jnp.int32, (CHUNK, PAIR), 0)
    wcol = lax.broadcasted_iota(jnp.int32, (CHUNK, PAIR), 1) % CHUNK
    strict = wcol < wrow
    incl = wcol <= wrow
    eye = jnp.where(wcol == wrow, 1.0, 0.0).astype(F32)
    srow = lax.broadcasted_iota(jnp.int32, (n2, n2), 0)
    scol = lax.broadcasted_iota(jnp.int32, (n2, n2), 1)
    same_head = (srow // CHUNK) == (scol // CHUNK)
    mu_rkv = mu_rkv_ref[...]

    def shift(y_act, carry_ref, row_idx, mu):
        y = y_act.astype(F32)
        delta = jnp.dot(diff_mat, y_act, preferred_element_type=F32)
        head = jnp.where(top_row, delta[:8] + carry_ref[row_idx:row_idx + 1, :], delta[:8])
        delta = jnp.concatenate([head, delta[8:]], axis=0)
        carry_ref[row_idx:row_idx + 1, :] = y[sub - 1:sub, :]
        return y + delta * mu

    def headsum(x):
        s0 = jnp.sum(jnp.where(head0_t, x, 0.0), axis=-1, keepdims=True)
        s1 = jnp.sum(jnp.where(head0_t, 0.0, x), axis=-1, keepdims=True)
        return jnp.where(head0_t, s0, s1)

    def stack(x):
        zero = jnp.zeros_like(x)
        return jnp.concatenate([jnp.where(head0, x, zero), jnp.where(head0, zero, x)], axis=0)

    d = {}

    def prep():
        rs = shift(r_ref[...], carry_rkv, 0, mu_rkv[0:1, :])
        ks = shift(k_ref[...], carry_rkv, 1, mu_rkv[1:2, :])
        vs = shift(v_ref[...], carry_rkv, 2, mu_rkv[2:3, :])
        lo = shift(lora_ref[...], carry_lora, 0, mu_lora_ref[...])
        z = jnp.where(lane < DECAY_LORA, jnp.tanh(lo), lo)
        wa = w0a0_ref[...] + _mm(z, lora_w_ref[...])
        lw = (-0.6065306597126334 * LOG2E) * _sigmoid(wa[:, :WIDTH])
        a = _sigmoid(wa[:, WIDTH:])
        if has_vmix:
            vd = shift(vd_ref[...], carry_vd, 0, vmix_mu_ref[...])
            mix = _sigmoid(vmix0_ref[...] + _mm(vd, vmix_up_ref[...]))
            vr = vs + (vfirst_ref[...] - vs) * mix
        else:
            vfirst_ref[...] = vs
            vr = vs
        kx = ks * kk_ref[...]
        kmod = ks * (1.0 + (a - 1.0) * ka_ref[...])
        d["rkr"] = rs * kmod * rk_ref[...]
        d["vr"] = vr
        kk_all = []
        for p in range(N_PAIRS):
            kxp = kx[:, p * PAIR:(p + 1) * PAIR]
            kk_all.append(kxp * jnp.minimum(lax.rsqrt(headsum(kxp * kxp)), 1.0 / L2_EPS))
        kk = jnp.concatenate(kk_all, axis=1)
        na = -kk
        nb = kk * a
        lw_hi = lw.astype(BF16)
        lw_r = lw - lw_hi.astype(F32)
        lw_mid = lw_r.astype(BF16)
        lw_lo = (lw_r - lw_mid.astype(F32)).astype(BF16)
        cum = (jnp.dot(ltri, lw_lo, preferred_element_type=F32) + jnp.dot(ltri, lw_mid, preferred_element_type=F32)
               + jnp.dot(ltri, lw_hi, preferred_element_type=F32))
        a_t = na * jnp.exp2(cum - lw)
        r_t = rs * jnp.exp2(cum)
        e_neg = jnp.exp2(-cum)
        b_t = nb * e_neg
        k_t = kmod * e_neg
        lhs_g, rhs_g, a_s, r_n, b_h, k_h, v_s, gam = {}, {}, {}, {}, {}, {}, {}, {}
        for c in range(n_chunks):
            rc = slice(c * CHUNK, (c + 1) * CHUNK)
            tot = cum[(c + 1) * CHUNK - 1:(c + 1) * CHUNK, :]
            e_rel = jnp.exp2(tot - cum[rc])
            bh_c = nb[rc] * e_rel
            kh_c = kmod[rc] * e_rel
            gam_c = jnp.exp2(tot)
            for p in range(N_PAIRS):
                sl = slice(p * PAIR, (p + 1) * PAIR)
                pc = (p, c)
                a_n = a_t[rc, sl].astype(BF16)
                r_n[pc] = r_t[rc, sl]
                lhs_g[pc] = jnp.concatenate([a_n, r_n[pc].astype(BF16)], axis=0)
                rhs_g[pc] = jnp.concatenate([stack(b_t[rc, sl].astype(BF16)),
                                             stack(k_t[rc, sl].astype(BF16))], axis=0)
                a_s[pc] = stack(a_n)
                b_h[pc] = bh_c[:, sl].astype(BF16)
                k_h[pc] = stack(kh_c[:, sl])
                v_s[pc] = stack(vr[rc, sl].astype(BF16))
                gam[pc] = gam_c[:, sl]
        d.update(lhs_g=lhs_g, rhs_g=rhs_g, a_s=a_s, r_n=r_n, b_h=b_h, k_h=k_h, v_s=v_s, gam=gam)

    def operators():
        a_ab, a_ak, a_rb, a_rk = {}, {}, {}, {}
        for pc in pcs:
            g = _mm_nt(d["lhs_g"][pc], d["rhs_g"][pc])
            a_ab[pc] = jnp.where(strict, g[:CHUNK, :n2], 0.0)
            a_ak[pc] = jnp.where(strict, g[:CHUNK, n2:], 0.0).astype(BF16)
            a_rb[pc] = jnp.where(incl, g[CHUNK:, :n2], 0.0).astype(BF16)
            a_rk[pc] = jnp.where(incl, g[CHUNK:, n2:], 0.0)
        t_inv = {pc: eye + a_ab[pc] for pc in pcs}
        pw = {pc: a_ab[pc].astype(BF16) for pc in pcs}
        pw = {pc: _mm(pw[pc], stack(pw[pc])).astype(BF16) for pc in pcs}
        for _ in range(CHUNK.bit_length() - 3):
            both = {pc: _mm(jnp.concatenate([pw[pc], t_inv[pc].astype(BF16)], axis=0), stack(pw[pc])) for pc in pcs}
            t_inv = {pc: t_inv[pc] + both[pc][CHUNK:] for pc in pcs}
            pw = {pc: both[pc][:CHUNK].astype(BF16) for pc in pcs}
        t_inv = {pc: (t_inv[pc] + _mm(t_inv[pc], stack(pw[pc]))).astype(BF16) for pc in pcs}
        z = {pc: jnp.where(same_head, _mm_tn(t_inv[pc], d["b_h"][pc]), 0.0).astype(BF16) for pc in pcs}
        w = {pc: _mm(a_rb[pc], stack(t_inv[pc])).astype(BF16) for pc in pcs}
        la_ak = {pc: jnp.concatenate([d["a_s"][pc], stack(a_ak[pc])], axis=1) for pc in pcs}
        my = {pc: _mm_tn(la_ak[pc], z[pc]) for pc in pcs}
        wq = {pc: _mm(w[pc], la_ak[pc]) for pc in pcs}
        d["m_t"] = {pc: my[pc][:n2].astype(BF16) for pc in pcs}
        y = {pc: (my[pc][n2:] + d["k_h"][pc]).astype(BF16) for pc in pcs}
        d["q_p"] = {pc: (d["r_n"][pc] + wq[pc][:, :n2]).astype(BF16) for pc in pcs}
        p_p = {pc: (wq[pc][:, n2:] + a_rk[pc]).astype(BF16) for pc in pcs}
        d["g_s"] = {pc: _mm_tn(d["v_s"][pc], y[pc]) for pc in pcs}
        d["o_loc"] = {pc: _mm(p_p[pc], d["v_s"][pc]) for pc in pcs}

    def recurrence():
        states = [state_ref[p] for p in range(N_PAIRS)]
        outs = [[] for _ in range(N_PAIRS)]
        for c in range(n_chunks):
            for p in range(N_PAIRS):
                pc = (p, c)
                sb = states[p].astype(BF16)
                outs[p].append(_mm_nt(d["q_p"][pc], sb) + d["o_loc"][pc])
                states[p] = states[p] * d["gam"][pc] + _mm(sb, d["m_t"][pc]) + d["g_s"][pc]
        for p in range(N_PAIRS):
            state_ref[p] = states[p]
        d["o"] = [jnp.concatenate(o, axis=0) for o in outs]

    def epilogue():
        for p in range(N_PAIRS):
            sl = slice(p * PAIR, (p + 1) * PAIR)
            o_p = d["o"][p]
            bonus = headsum(d["rkr"][:, sl]) * d["vr"][:, sl]
            mu = headsum(o_p) * (1.0 / HEAD_DIM)
            dev = o_p - mu
            var = headsum(dev * dev) * (1.0 / HEAD_DIM)
            on = dev * lax.rsqrt(var + GN_EPS) * lng_ref[:, sl] + lnb_ref[:, sl]
            rz_p = rz_ref[:, sl].astype(F32)
            yb_ref[:, sl] = ((on + bonus) * (rz_p * _sigmoid(rz_p))).astype(yb_ref.dtype)

    prep()
    operators()
    recurrence()
    epilogue()


def _rwkv(proj, vfirst, p, *, ts=256):
    b, s, _ = proj.shape
    has_vmix = vfirst is not None
    proj2d = proj.reshape(b * s, PROJ_W)
    row = lambda width, col: pl.BlockSpec((pl.Element(ts), pl.Element(width)),
                                          lambda i, t: (pl.multiple_of(i * s + t * ts, ts), col))
    const = lambda shape: pl.BlockSpec(shape, lambda i, t: (0,) * len(shape))
    act = pl.BlockSpec((None, ts, WIDTH), lambda i, t: (i, t, 0))

    in_specs = [row(WIDTH, COL_R), row(WIDTH, COL_RK), row(WIDTH, COL_RV), row(WIDTH, COL_RZ),
                row(LANES, COL_LORA)]
    args = [proj2d, proj2d, proj2d, proj2d, proj2d]
    if has_vmix:
        in_specs += [row(LANES, COL_VD), act]
        args += [proj2d, vfirst]
    in_specs += [const((3, WIDTH)), const((1, LANES)), const((1, 2 * WIDTH)), const((LANES, 2 * WIDTH))]
    args += [p["mu_rkv"], p["mu_lora"], p["w0a0"], p["lora_w"]]
    for name in ("k_k", "k_a", "r_k", "ln_g", "ln_b"):
        in_specs.append(const((1, WIDTH)))
        args.append(p[name])
    if has_vmix:
        in_specs += [const((1, LANES)), const((LANES, WIDTH)), const((1, WIDTH))]
        args += [p["vmix_mu"], p["vmix_up"], p["vmix0"]]

    out_shape = [jax.ShapeDtypeStruct((b, s, WIDTH), ACT)]
    out_specs = [act]
    scratch = [pltpu.VMEM((8, WIDTH), F32), pltpu.VMEM((8, LANES), F32)]
    if has_vmix:
        scratch.append(pltpu.VMEM((8, LANES), F32))
    else:
        out_shape.append(jax.ShapeDtypeStruct((b, s, WIDTH), F32))
        out_specs.append(act)
    scratch.append(pltpu.VMEM((N_PAIRS, PAIR, PAIR), F32))

    res = pl.pallas_call(
        functools.partial(_rwkv_kernel, has_vmix=has_vmix, ts=ts),
        grid=(b, s // ts),
        in_specs=in_specs,
        out_specs=out_specs,
        out_shape=out_shape,
        scratch_shapes=scratch,
        compiler_params=pltpu.CompilerParams(
            dimension_semantics=("parallel", "arbitrary"), vmem_limit_bytes=VMEM_LIMIT),
        name="rwkv_vmix" if has_vmix else "rwkv",
    )(*args)
    return (res[0], vfirst) if has_vmix else (res[0], res[1])


def _outproj_kernel(ya_ref, az_ref, yb_ref, gatt_ref, grw_ref, x_ref,
                    wua_ref, wub_ref, wout_ref, g_ref, o_ref):
    az = az_ref[...].astype(F32)
    ya = ya_ref[...].astype(F32) * (az * _sigmoid(az))
    u = (_sigmoid(gatt_ref[...].astype(F32)) * _mm(ya, wua_ref[...])
         + _sigmoid(grw_ref[...].astype(F32)) * _mm(yb_ref[...], wub_ref[...]))
    y = _mm(u, wout_ref[...])
    ms = jnp.mean(y * y, axis=-1, keepdims=True)
    o_ref[...] = x_ref[...] + y * lax.rsqrt(ms + RMS_EPS) * g_ref[...]


def _outproj(ya, proj2d, yb, x2d, w_up_att, w_up_rw, w_out, g, *, tm=1024):
    t, d = x2d.shape
    row = lambda width, col: pl.BlockSpec((pl.Element(tm), pl.Element(width)),
                                          lambda i: (pl.multiple_of(i * tm, tm), col))
    const = lambda shape: pl.BlockSpec(shape, lambda i: (0, 0))
    return pl.pallas_call(
        _outproj_kernel,
        grid=(t // tm,),
        in_specs=[row(WIDTH, 0), row(WIDTH, COL_AZ), row(WIDTH, 0), row(d, COL_GATT), row(d, COL_GRW),
                  row(d, 0), const((WIDTH, d)), const((WIDTH, d)), const((d, d)), const((1, d))],
        out_specs=row(d, 0),
        out_shape=jax.ShapeDtypeStruct((t, d), F32),
        compiler_params=pltpu.CompilerParams(
            dimension_semantics=("parallel",), vmem_limit_bytes=VMEM_LIMIT),
        name="outproj",
    )(ya, proj2d, yb, proj2d, proj2d, x2d, w_up_att, w_up_rw, w_out, g.reshape(1, d))


def _layer_params(l, d, rw_mu, rw_w0, rw_w_up, rw_a0, rw_a_up, rw_k_k, rw_k_a, rw_r_k,
                  rw_ln_g, rw_ln_b, rw_vmix_down, rw_vmix_mu, rw_vmix_up, rw_vmix0):
    pad = LANES - VMIX_LORA
    vd = jnp.pad(rw_vmix_down[l - 1], ((0, 0), (0, pad))) if l > 0 else jnp.zeros((d, LANES), F32)

    mu = rw_mu[l]
    lora_w = jnp.concatenate([jnp.concatenate([rw_w_up[l], jnp.zeros((DECAY_LORA, WIDTH), F32)], axis=1),
                              jnp.concatenate([jnp.zeros((ICLR_LORA, WIDTH), F32), rw_a_up[l]], axis=1)],
                             axis=0).astype(BF16)
    p = {
        "w_vd": vd.astype(BF16),
        "mu_rkv": mu[:3 * WIDTH].reshape(3, WIDTH),
        "mu_lora": mu[3 * WIDTH:].reshape(1, LANES),
        "w0a0": jnp.concatenate([rw_w0[l], rw_a0[l]]).reshape(1, 2 * WIDTH),
        "lora_w": lora_w,
        "k_k": rw_k_k[l].reshape(1, WIDTH), "k_a": rw_k_a[l].reshape(1, WIDTH),
        "r_k": rw_r_k[l].reshape(1, WIDTH),
        "ln_g": rw_ln_g[l].reshape(1, WIDTH), "ln_b": rw_ln_b[l].reshape(1, WIDTH),
    }
    if l > 0:
        p["vmix_mu"] = jnp.pad(rw_vmix_mu[l - 1], (0, pad)).reshape(1, LANES)
        p["vmix_up"] = jnp.pad(rw_vmix_up[l - 1], ((0, pad), (0, 0))).astype(BF16)
        p["vmix0"] = rw_vmix0[l - 1].reshape(1, WIDTH)
    return p


def kernel(x, norm_pre, norm_post, w_in, rw_mu, rw_w0, rw_w_up, rw_a0, rw_a_up, rw_k_k, rw_k_a, rw_r_k, rw_ln_g, rw_ln_b, rw_vmix_down, rw_vmix_mu, rw_vmix_up, rw_vmix0, w_up_att, w_up_rw, w_out):
    b, s, d = x.shape
    assert d == D_MODEL and s % MOBA_BLOCK == 0
    depth = w_in.shape[0]
    x2d = x.reshape(b * s, d)
    vfirst = None
    for l in range(depth):
        p = _layer_params(l, d, rw_mu, rw_w0, rw_w_up, rw_a0, rw_a_up, rw_k_k, rw_k_a, rw_r_k,
                          rw_ln_g, rw_ln_b, rw_vmix_down, rw_vmix_mu, rw_vmix_up, rw_vmix0)
        proj2d = _inproj(x2d, norm_pre[l], w_in, l, p["w_vd"])
        proj = proj2d.reshape(b, s, PROJ_W)
        ya = _moba(proj)
        yb, vfirst = _rwkv(proj, vfirst, p)
        x2d = _outproj(ya.reshape(b * s, WIDTH), proj2d, yb.reshape(b * s, WIDTH), x2d,
                       w_up_att[l].astype(BF16), w_up_rw[l].astype(BF16), w_out[l].astype(BF16),
                       norm_post[l])
    return x2d.reshape(b, s, d)
```
